```python
import math
import jax, jax.numpy as jnp
from jax import lax
import numpy as np


D_MODEL = 1024
BATCH = 16
SEQ = 2048
DEPTH = 4

GRID_W = 64
CTX_LEN = 256
EPS = 1e-6
ROPE_THETA = 10000.0
Q_BLOCK = 128

D_MIX = D_MODEL
DA_HEADS = 4
DA_QK = 32
DA_V = 2 * DA_QK
DA_WIDTH = DA_HEADS * DA_V
DA_IN = 2 * DA_HEADS * 2 * DA_QK + DA_HEADS * DA_V
SSD_HEADS = 8
SSD_P = 64
SSD_INNER = SSD_HEADS * SSD_P
SSD_GROUPS = 2
SSD_STATE = 64
SSD_CONV = 3
SSD_CHUNK = 128
SSD_CONV_DIM = SSD_INNER + 2 * SSD_GROUPS * SSD_STATE
SSD_IN = SSD_INNER + SSD_CONV_DIM + 2 * SSD_HEADS
MLA_HEADS = 4
MLA_Q_RANK = 256
MLA_KV_RANK = 128
MLA_NOPE = 64
MLA_ROPE = 32
MLA_V = 64
MLA_WIDTH = MLA_HEADS * MLA_V
MLA_IN = MLA_Q_RANK + MLA_KV_RANK + MLA_ROPE
MLA_SCALE = (MLA_NOPE + MLA_ROPE) ** -0.5
IN_WIDTH = DA_IN + SSD_IN + MLA_IN
N_EXPERTS = 16
N_GROUPS = 4
TOP_K = 2
D_EXPERT = 512

kernel_name = 'hybrid_dit_diffattn_ssd_mla_moe'


def rms_norm(x, g):
    xf = x.astype(jnp.float32)
    y = xf * lax.rsqrt(jnp.mean(xf * xf, axis=-1, keepdims=True) + EPS)
    return (y * g.astype(jnp.float32)).astype(x.dtype)


def ada_norm(x, g, shift, scale):
    return rms_norm(x, g) * (1 + scale) + shift


def axial_rope_tables(length, dim):
    rows = length // GRID_W
    row = jnp.repeat(jnp.arange(rows), GRID_W).astype(jnp.float32)
    col = jnp.tile(jnp.arange(GRID_W), rows).astype(jnp.float32)
    n_freq = dim // 4
    inv_freq = ROPE_THETA ** (-jnp.arange(n_freq, dtype=jnp.float32) / n_freq)
    ang = jnp.concatenate([row[:, None] * inv_freq, col[:, None] * inv_freq], axis=-1)
    return jnp.cos(ang), jnp.sin(ang)


def apply_rope(x, cos, sin):
    half = x.shape[-1] // 2
    shape = (1, x.shape[1]) + (1,) * (x.ndim - 3) + (half,)
    cos = cos.reshape(shape)
    sin = sin.reshape(shape)
    xf = x.astype(jnp.float32)
    x1, x2 = xf[..., :half], xf[..., half:]
    return jnp.concatenate([x1 * cos - x2 * sin, x2 * cos + x1 * sin], axis=-1).astype(x.dtype)


def sweep_query_blocks(fn, *qs):
    b, l = qs[0].shape[:2]
    nb = l // Q_BLOCK
    blocks = tuple(jnp.moveaxis(q.reshape((b, nb, Q_BLOCK) + q.shape[2:]), 1, 0) for q in qs)
    out = lax.map(lambda qb: fn(*qb), blocks)
    return jnp.moveaxis(out, 0, 1).reshape((b, l) + out.shape[3:])


def diff_attend(q, k, v, lam):
    s = jnp.einsum('bqhmd,bkhmd->bhmqk', q, k).astype(jnp.float32) * (DA_QK ** -0.5)
    p = jax.nn.softmax(s, axis=-1)
    a = p[:, :, 0] - lam * p[:, :, 1]
    return jnp.einsum('bhqk,bkhv->bqhv', a.astype(v.dtype), v)


def mla_attend(q_nope, q_rope, k_nope, k_rope, v):
    s = (jnp.einsum('bqhd,bkhd->bhqk', q_nope, k_nope)
         + jnp.einsum('bqhr,bkr->bhqk', q_rope, k_rope)).astype(jnp.float32) * MLA_SCALE
    p = jax.nn.softmax(s, axis=-1)
    return jnp.einsum('bhqk,bkhv->bqhv', p.astype(v.dtype), v)


def diff_attention_mixer(p_da, n_ctx, cos, sin, lam_vec, subln_g, lam_init):
    b, t, _ = p_da.shape
    q, k, v = jnp.split(p_da, 3, axis=-1)
    q = q.reshape(b, t, DA_HEADS, 2, DA_QK)
    k = k.reshape(b, t, DA_HEADS, 2, DA_QK)
    v = v.reshape(b, t, DA_HEADS, DA_V)
    lv = lam_vec.astype(jnp.float32)
    lam = jnp.exp(jnp.sum(lv[0] * lv[1])) - jnp.exp(jnp.sum(lv[2] * lv[3])) + lam_init
    q_c, q_x = q[:, :n_ctx], apply_rope(q[:, n_ctx:], cos, sin)
    k_c = k[:, :n_ctx]
    k_all = jnp.concatenate([k_c, apply_rope(k[:, n_ctx:], cos, sin)], axis=1)
    o_c = diff_attend(q_c, k_c, v[:, :n_ctx], lam)
    o_x = sweep_query_blocks(lambda qb: diff_attend(qb, k_all, v, lam), q_x)
    o = rms_norm(jnp.concatenate([o_c, o_x], axis=1), subln_g) * (1.0 - lam_init)
    return o.reshape(b, t, DA_WIDTH)


def depthwise_conv(u, w, bias):
    pad = SSD_CONV // 2
    y = lax.conv_general_dilated(u, w[:, None, :].astype(u.dtype), (1,), [(pad, pad)],
                                 dimension_numbers=('NWC', 'WIO', 'NWC'),
                                 feature_group_count=u.shape[-1])
    return y + bias


def ssd_chunked_scan(xh, dt, a, bm, cm, h0):
    b, l, h, p = xh.shape
    n = bm.shape[-1]
    q = SSD_CHUNK
    nc = l // q
    x_c = xh.reshape(b, nc, q, h, p)
    dt_c = dt.reshape(b, nc, q, h)
    b_c = bm.reshape(b, nc, q, h, n)
    c_c = cm.reshape(b, nc, q, h, n)
    a_cum = jnp.cumsum(dt_c * a, axis=2)
    seg = a_cum[:, :, :, None, :] - a_cum[:, :, None, :, :]
    lower = jnp.tril(jnp.ones((q, q), dtype=bool))[None, None, :, :, None]
    decay = jnp.exp(jnp.where(lower, seg, -jnp.inf))
    scores = jnp.einsum('bcihn,bcjhn->bcijh', c_c, b_c) * decay * dt_c[:, :, None, :, :]
    y_diag = jnp.einsum('bcijh,bcjhp->bcihp', scores, x_c)
    decay_to_end = jnp.exp(a_cum[:, :, -1:, :] - a_cum)
    states = jnp.einsum('bcjhn,bcjh,bcjhp->bchpn', b_c, decay_to_end * dt_c, x_c)
    chunk_decay = jnp.exp(a_cum[:, :, -1, :])

    def step(h_prev, inp):
        st, dec = inp
        return h_prev * dec[:, :, None, None] + st, h_prev

    h_final, h_in = lax.scan(step, h0, (jnp.moveaxis(states, 1, 0), jnp.moveaxis(chunk_decay, 1, 0)))
    h_in = jnp.moveaxis(h_in, 0, 1)
    y_off = jnp.einsum('bcihn,bchpn,bcih->bcihp', c_c, h_in, jnp.exp(a_cum))
    return (y_diag + y_off).reshape(b, l, h, p), h_final


def ssd_bidirectional(xbc, dt, a, d_skip, h0_fwd, h0_bwd):
    b, l, _ = xbc.shape
    u = xbc.astype(jnp.float32)
    rep = SSD_HEADS // SSD_GROUPS
    xh = u[..., :SSD_INNER].reshape(b, l, SSD_HEADS, SSD_P)
    gn = SSD_GROUPS * SSD_STATE
    bm = jnp.repeat(u[..., SSD_INNER:SSD_INNER + gn].reshape(b, l, SSD_GROUPS, SSD_STATE), rep, axis=2)
    cm = jnp.repeat(u[..., SSD_INNER + gn:].reshape(b, l, SSD_GROUPS, SSD_STATE), rep, axis=2)
    flip = lambda arr: jnp.flip(arr, axis=1)
    y_f, h_f = ssd_chunked_scan(xh, dt[:, :, 0], a[0], bm, cm, h0_fwd)
    y_b, h_b = ssd_chunked_scan(flip(xh), flip(dt[:, :, 1]), a[1], flip(bm), flip(cm), h0_bwd)
    y = y_f + flip(y_b) + d_skip.astype(jnp.float32)[:, None] * xh
    return y, h_f, h_b


def ssd_mixer(p_ssd, n_ctx, conv_w, conv_b, dt_bias, a_log, d_skip, norm_g):
    b, t, _ = p_ssd.shape
    z = p_ssd[..., :SSD_INNER]
    xbc = p_ssd[..., SSD_INNER:SSD_INNER + SSD_CONV_DIM]
    dt_raw = p_ssd[..., SSD_INNER + SSD_CONV_DIM:]
    a = -jnp.exp(a_log.astype(jnp.float32))
    dt = jax.nn.softplus(dt_raw.astype(jnp.float32).reshape(b, t, 2, SSD_HEADS) + dt_bias.astype(jnp.float32))
    xbc_c = jax.nn.silu(depthwise_conv(xbc[:, :n_ctx], conv_w, conv_b))
    xbc_x = jax.nn.silu(depthwise_conv(xbc[:, n_ctx:], conv_w, conv_b))
    h_zero = jnp.zeros((b, SSD_HEADS, SSD_P, SSD_STATE), jnp.float32)
    y_c, h_f, h_b = ssd_bidirectional(xbc_c, dt[:, :n_ctx], a, d_skip, h_zero, h_zero)
    y_x, _, _ = ssd_bidirectional(xbc_x, dt[:, n_ctx:], a, d_skip, h_f, h_b)
    y = jnp.concatenate([y_c, y_x], axis=1).reshape(b, t, SSD_INNER).astype(p_ssd.dtype)
    return rms_norm(y * jax.nn.silu(z), norm_g)


def mla_mixer(p_mla, n_ctx, cos, sin, q_norm_g, kv_norm_g, w_uq, w_ukv):
    b, t, _ = p_mla.shape
    cq = p_mla[..., :MLA_Q_RANK]
    ckv = p_mla[..., MLA_Q_RANK:MLA_Q_RANK + MLA_KV_RANK]
    kr = p_mla[..., MLA_Q_RANK + MLA_KV_RANK:]
    q = (rms_norm(cq, q_norm_g) @ w_uq).reshape(b, t, MLA_HEADS, MLA_NOPE + MLA_ROPE)
    kv = (rms_norm(ckv, kv_norm_g) @ w_ukv).reshape(b, t, MLA_HEADS, MLA_NOPE + MLA_V)
    q_nope, q_rope = q[..., :MLA_NOPE], q[..., MLA_NOPE:]
    k_nope, v = kv[..., :MLA_NOPE], kv[..., MLA_NOPE:]
    q_rope_x = apply_rope(q_rope[:, n_ctx:], cos, sin)
    k_rope = jnp.concatenate([kr[:, :n_ctx], apply_rope(kr[:, n_ctx:], cos, sin)], axis=1)
    o_c = mla_attend(q_nope[:, :n_ctx], q_rope[:, :n_ctx], k_nope[:, :n_ctx], k_rope[:, :n_ctx], v[:, :n_ctx])
    o_x = sweep_query_blocks(lambda qn, qr: mla_attend(qn, qr, k_nope, k_rope, v), q_nope[:, n_ctx:], q_rope_x)
    return jnp.concatenate([o_c, o_x], axis=1).reshape(b, t, MLA_WIDTH)


def moe_ffn(h, router_w, router_bias, w_gate, w_up, w_down):
    b, t, _ = h.shape
    per_group = N_EXPERTS // N_GROUPS
    aff = jax.nn.sigmoid((h @ router_w).astype(jnp.float32))
    sel = aff + router_bias.astype(jnp.float32)
    group_score = jnp.sum(lax.top_k(sel.reshape(b, t, N_GROUPS, per_group), 2)[0], axis=-1)
    best = jnp.argmax(group_score, axis=-1)
    in_group = jnp.repeat(jax.nn.one_hot(best, N_GROUPS, dtype=jnp.float32), per_group, axis=-1) > 0
    _, idx = lax.top_k(jnp.where(in_group, sel, -jnp.inf), TOP_K)
    w = jnp.take_along_axis(aff, idx, axis=-1)
    w = w / jnp.sum(w, axis=-1, keepdims=True)
    gates = jnp.einsum('btk,btke->bte', w, jax.nn.one_hot(idx, N_EXPERTS, dtype=jnp.float32)).astype(h.dtype)
    y = jnp.zeros_like(h)
    for e in range(N_EXPERTS):
        he = jax.nn.silu(h @ w_gate[e]) * (h @ w_up[e])
        y = y + gates[..., e:e + 1] * (he @ w_down[e])
    return y


def setup_inputs(seed: int = 0) -> dict:
    key = jax.random.key(seed)
    ks = jax.random.split(key, 32)
    f32 = jnp.float32
    nrm = lambda k, shape, s: jax.random.normal(k, shape, f32) * s
    gain = lambda k, shape: 1.0 + 0.02 * jax.random.normal(k, shape, f32)
    dt = jnp.exp(jax.random.uniform(ks[14], (DEPTH, 2, SSD_HEADS), f32, math.log(1e-3), math.log(1e-1)))
    return {
        'x': nrm(ks[0], (BATCH, SEQ, D_MODEL), 1.0),
        'c': nrm(ks[1], (BATCH, D_MODEL), 1.0),
        'ctx': nrm(ks[2], (BATCH, CTX_LEN, D_MODEL), 1.0),
        'c_ctx': nrm(ks[3], (D_MODEL,), 1.0),
        'norm_mix_g': gain(ks[4], (DEPTH, D_MODEL)),
        'norm_ffn_g': gain(ks[5], (DEPTH, D_MODEL)),
        'w_mod': nrm(ks[6], (DEPTH, D_MODEL, 6 * D_MODEL), 0.5 * D_MODEL ** -0.5),
        'b_mod': nrm(ks[7], (DEPTH, 6 * D_MODEL), 0.02),
        'w_in': nrm(ks[8], (DEPTH, D_MODEL, IN_WIDTH), D_MODEL ** -0.5),
        'w_out': nrm(ks[9], (DEPTH, D_MIX, D_MODEL), D_MIX ** -0.5),
        'da_lambda': nrm(ks[10], (DEPTH, 4, DA_QK), 0.1),
        'da_subln_g': gain(ks[11], (DEPTH, DA_V)),
        'ssd_conv_w': nrm(ks[12], (DEPTH, SSD_CONV, SSD_CONV_DIM), SSD_CONV ** -0.5),
        'ssd_conv_b': nrm(ks[13], (DEPTH, SSD_CONV_DIM), 0.02),
        'ssd_dt_bias': dt + jnp.log(-jnp.expm1(-dt)),
        'ssd_a_log': jnp.log(jax.random.uniform(ks[15], (DEPTH, 2, SSD_HEADS), f32, 1.0, 16.0)),
        'ssd_d': 1.0 + 0.1 * jax.random.normal(ks[16], (DEPTH, SSD_HEADS), f32),
        'ssd_norm_g': gain(ks[17], (DEPTH, SSD_INNER)),
        'mla_q_norm_g': gain(ks[18], (DEPTH, MLA_Q_RANK)),
        'mla_kv_norm_g': gain(ks[19], (DEPTH, MLA_KV_RANK)),
        'mla_w_uq': nrm(ks[20], (DEPTH, MLA_Q_RANK, MLA_HEADS * (MLA_NOPE + MLA_ROPE)), MLA_Q_RANK ** -0.5),
        'mla_w_ukv': nrm(ks[21], (DEPTH, MLA_KV_RANK, MLA_HEADS * (MLA_NOPE + MLA_V)), MLA_KV_RANK ** -0.5),
        'router_w': nrm(ks[22], (D_MODEL, N_EXPERTS), D_MODEL ** -0.5),
        'router_bias': nrm(ks[23], (N_EXPERTS,), 0.01),
        'exp_w_gate': nrm(ks[24], (DEPTH, N_EXPERTS, D_MODEL, D_EXPERT), D_MODEL ** -0.5),
        'exp_w_up': nrm(ks[25], (DEPTH, N_EXPERTS, D_MODEL, D_EXPERT), D_MODEL ** -0.5),
        'exp_w_down': nrm(ks[26], (DEPTH, N_EXPERTS, D_EXPERT, D_MODEL), D_EXPERT ** -0.5),
        'final_norm_g': gain(ks[27], (D_MODEL,)),
    }


def reference(x, c, ctx, c_ctx, norm_mix_g, norm_ffn_g, w_mod, b_mod, w_in, w_out, da_lambda, da_subln_g,
              ssd_conv_w, ssd_conv_b, ssd_dt_bias, ssd_a_log, ssd_d, ssd_norm_g, mla_q_norm_g, mla_kv_norm_g,
              mla_w_uq, mla_w_ukv, router_w, router_bias, exp_w_gate, exp_w_up, exp_w_down, final_norm_g):
    n_ctx = ctx.shape[1]
    n_lat = x.shape[1]
    da_cos, da_sin = axial_rope_tables(n_lat, DA_QK)
    mla_cos, mla_sin = axial_rope_tables(n_lat, MLA_ROPE)
    c_act = jax.nn.silu(c)
    cc_act = jax.nn.silu(c_ctx)
    hx, hc = x, ctx
    for i in range(DEPTH):
        lam_init = 0.8 - 0.6 * math.exp(-0.3 * i)
        mod_x = jnp.split((c_act @ w_mod[i] + b_mod[i])[:, None, :], 6, axis=-1)
        mod_c = jnp.split(cc_act @ w_mod[i] + b_mod[i], 6, axis=-1)
        u = jnp.concatenate([ada_norm(hc, norm_mix_g[i], mod_c[0], mod_c[1]),
                             ada_norm(hx, norm_mix_g[i], mod_x[0], mod_x[1])], axis=1)
        proj = u @ w_in[i]
        o_da = diff_attention_mixer(proj[..., :DA_IN], n_ctx, da_cos, da_sin, da_lambda[i], da_subln_g[i], lam_init)
        o_ssd = ssd_mixer(proj[..., DA_IN:DA_IN + SSD_IN], n_ctx, ssd_conv_w[i], ssd_conv_b[i],
                          ssd_dt_bias[i], ssd_a_log[i], ssd_d[i], ssd_norm_g[i])
        o_mla = mla_mixer(proj[..., DA_IN + SSD_IN:], n_ctx, mla_cos, mla_sin, mla_q_norm_g[i], mla_kv_norm_g[i],
                          mla_w_uq[i], mla_w_ukv[i])
        mix = jnp.concatenate([o_da, o_ssd, o_mla], axis=-1) @ w_out[i]
        hc = hc + mod_c[2] * mix[:, :n_ctx]
        hx = hx + mod_x[2] * mix[:, n_ctx:]
        u_x = ada_norm(hx, norm_ffn_g[i], mod_x[3], mod_x[4])
        if i < DEPTH - 1:
            u_c = ada_norm(hc, norm_ffn_g[i], mod_c[3], mod_c[4])
            y = moe_ffn(jnp.concatenate([u_c, u_x], axis=1), router_w, router_bias,
                        exp_w_gate[i], exp_w_up[i], exp_w_down[i])
            hc = hc + mod_c[5] * y[:, :n_ctx]
            y_x = y[:, n_ctx:]
        else:
            y_x = moe_ffn(u_x, router_w, router_bias, exp_w_gate[i], exp_w_up[i], exp_w_down[i])
        hx = hx + mod_x[5] * y_x
    return rms_norm(hx, final_norm_g)
```

```python
import functools
import math

import jax
import jax.numpy as jnp
from jax import lax
from jax.experimental import pallas as pl
from jax.experimental.pallas import tpu as pltpu

F32 = jnp.float32
BF16 = jnp.bfloat16

DEPTH = 4
GRID_W = 64
EPS = 1e-6
ROPE_THETA = 10000.0
DA_HEADS, DA_QK = 4, 32
DA_V = 2 * DA_QK
DA_WIDTH = DA_HEADS * DA_V
DA_QW = DA_HEADS * 2 * DA_QK
DA_IN = 2 * DA_QW + DA_WIDTH
SSD_HEADS, SSD_P, SSD_GROUPS, SSD_STATE, SSD_CHUNK = 8, 64, 2, 64, 128
SSD_INNER = SSD_HEADS * SSD_P
SSD_GN = SSD_GROUPS * SSD_STATE
SSD_CONV_DIM = SSD_INNER + 2 * SSD_GN
SSD_IN = SSD_INNER + SSD_CONV_DIM + 2 * SSD_HEADS
MLA_HEADS, MLA_Q_RANK, MLA_KV_RANK, MLA_NOPE, MLA_ROPE, MLA_V = 4, 256, 128, 64, 32, 64
MLA_WIDTH = MLA_HEADS * MLA_V
MLA_IN = MLA_Q_RANK + MLA_KV_RANK + MLA_ROPE
MLA_SCALE = (MLA_NOPE + MLA_ROPE) ** -0.5
N_EXPERTS, N_GROUPS, D_EXPERT = 16, 4, 512
PER_GROUP = N_EXPERTS // N_GROUPS

LANES = 128
SUBLANES_BF16 = 16
VMEM_LIMIT_BYTES = 56 * 1024 * 1024

MLA_HEAD_PAD = LANES
MLA_QPAD = MLA_HEADS * MLA_HEAD_PAD
KR_LANE0 = MLA_NOPE
DT_LANE0 = MLA_NOPE + MLA_ROPE
IN_COLS = DA_IN + SSD_INNER + SSD_CONV_DIM + MLA_Q_RANK + MLA_KV_RANK + LANES
C_Z = DA_IN
C_XBC = C_Z + SSD_INNER
C_CQ = C_XBC + SSD_CONV_DIM
C_CKV = C_CQ + MLA_Q_RANK
C_MISC = C_CKV + MLA_KV_RANK
LOG2E = math.log2(math.e)
DA_QSCALE = DA_QK ** -0.5 * LOG2E
MLA_QSCALE = MLA_SCALE * LOG2E

MOE_BLK = SUBLANES_BF16
MOE_STEP_BLKS = 32
MOE_EXTRA = LANES


def _sigmoid(x):
    return 1.0 / (1.0 + jnp.exp(-x))


def _silu(x):
    return x * _sigmoid(x)


def _rms(x, g, axis=-1):
    return x * lax.rsqrt(jnp.mean(x * x, axis=axis, keepdims=True) + EPS) * g


def _ada_norm(hv, g, shift, scale):
    return _rms(hv, g) * (1.0 + scale) + shift


def _rope(x, cos, sin_signed, half):
    w = x.shape[-1]
    lane = lax.broadcasted_iota(jnp.int32, x.shape, x.ndim - 1)
    first = (lane % (2 * half)) < half
    partner = jnp.where(first, pltpu.roll(x, w - half, x.ndim - 1), pltpu.roll(x, half, x.ndim - 1))
    return x * cos + partner * sin_signed


def _dot(a, b):
    return jnp.dot(a, b, preferred_element_type=F32)


def _dot_hi(a, b):
    return jnp.dot(a, b, preferred_element_type=F32, precision=lax.Precision.HIGHEST)


def _dot_split3(x, m01):
    x1 = x.astype(BF16)
    r1 = x - x1.astype(F32)
    x2 = r1.astype(BF16)
    x3 = (r1 - x2.astype(F32)).astype(BF16)
    return _dot(x1, m01) + _dot(x2, m01) + _dot(x3, m01)


def _cparams(*sem):
    return pltpu.CompilerParams(dimension_semantics=sem, vmem_limit_bytes=VMEM_LIMIT_BYTES)


def _mod_kernel(c_ref, w_ref, b_ref, o_ref):
    o_ref[...] = _dot_hi(_silu(c_ref[...]), w_ref[...]) + b_ref[...]


def _modulation(c_rows, w_mod, b_mod):
    depth, d, n = w_mod.shape
    r = c_rows.shape[0]
    tn = 1536
    return pl.pallas_call(
        _mod_kernel,
        grid=(depth, n // tn),
        in_specs=[pl.BlockSpec((r, d), lambda l, j: (0, 0)),
                  pl.BlockSpec((None, d, tn), lambda l, j: (l, 0, j)),
                  pl.BlockSpec((None, 1, tn), lambda l, j: (l, 0, j))],
        out_specs=pl.BlockSpec((None, r, tn), lambda l, j: (l, 0, j)),
        out_shape=jax.ShapeDtypeStruct((depth, r, n), F32),
        compiler_params=_cparams("arbitrary", "arbitrary"),
        name="modulation",
    )(c_rows, w_mod, b_mod.reshape(depth, 1, n))


def _inproj_kernel(h_ref, mod_ref, ng_ref, w_ref, cda_ref, sda_ref, cq_ref, ck_ref, sm_ref,
                   qg_ref, kvg_ref, wuq_ref, wukv_ref,
                   qda_ref, kda_ref, vda_ref, z_ref, xbc_ref, dtm_ref, qm_ref, km_ref, vm_ref,
                   *, n_ctx_tiles, ctx_row, d):
    ti, b = pl.program_id(0), pl.program_id(1)
    row = jnp.where(ti < n_ctx_tiles, ctx_row, b)
    shift = mod_ref[pl.ds(row, 1), pl.ds(0, d)]
    scale = mod_ref[pl.ds(row, 1), pl.ds(d, d)]
    u = _ada_norm(h_ref[...], ng_ref[...], shift, scale).astype(BF16)
    acc = _dot(u, w_ref[...])
    cda, sda = cda_ref[...], sda_ref[...]
    qda_ref[...] = (_rope(acc[:, 0:DA_QW], cda, sda, DA_QK // 2) * DA_QSCALE).astype(BF16)
    kda_ref[...] = _rope(acc[:, DA_QW:2 * DA_QW], cda, sda, DA_QK // 2).astype(BF16)
    vda_ref[...] = acc[:, 2 * DA_QW:DA_IN].astype(BF16)
    z_ref[...] = acc[:, C_Z:C_XBC].astype(BF16)
    xbc_ref[...] = acc[:, C_XBC:C_CQ]
    misc = acc[:, C_MISC:IN_COLS]
    dtm_ref[...] = misc
    cqn = _rms(acc[:, C_CQ:C_CKV], qg_ref[...]).astype(BF16)
    qm = _dot(cqn, wuq_ref[...])
    sm = sm_ref[...]
    cos_q = jnp.concatenate([cq_ref[...]] * MLA_HEADS, axis=1)
    sin_q = jnp.concatenate([sm] * MLA_HEADS, axis=1)
    qm_ref[...] = (_rope(qm, cos_q, sin_q, MLA_ROPE // 2) * MLA_QSCALE).astype(BF16)
    ckvn = _rms(acc[:, C_CKV:C_MISC], kvg_ref[...]).astype(BF16)
    kv = _dot(ckvn, wukv_ref[...])
    kr = _rope(misc, ck_ref[...], sm, MLA_ROPE // 2)
    km_ref[...] = (kv[:, :MLA_QPAD] + jnp.concatenate([kr] * MLA_HEADS, axis=1)).astype(BF16)
    vm_ref[...] = kv[:, MLA_QPAD:].astype(BF16)


def _inproj(h, mod_l, ng, w_in_p, tabs, qg, kvg, wuq_p, wukv_p, *, tm, n_ctx):
    bsz, t, d = h.shape
    nt = t // tm
    tok = lambda w: pl.BlockSpec((None, tm, w), lambda ti, b: (b, ti, 0))
    tab = lambda w: pl.BlockSpec((tm, w), lambda ti, b: (ti, 0))
    full = lambda a: pl.BlockSpec(a.shape, lambda ti, b: (0,) * a.ndim)
    cda, sda, cq, ck, sm = tabs
    outs = [(DA_QW, BF16), (DA_QW, BF16), (DA_WIDTH, BF16), (SSD_INNER, BF16), (SSD_CONV_DIM, F32),
            (LANES, F32), (MLA_QPAD, BF16), (MLA_QPAD, BF16), (MLA_WIDTH, BF16)]
    kern = functools.partial(_inproj_kernel, n_ctx_tiles=n_ctx // tm, ctx_row=bsz, d=d)
    return pl.pallas_call(
        kern,
        grid=(nt, bsz),
        in_specs=[tok(d), full(mod_l), full(ng), full(w_in_p), tab(DA_QW), tab(DA_QW), tab(LANES), tab(LANES),
                  tab(LANES), full(qg), full(kvg), full(wuq_p), full(wukv_p)],
        out_specs=[tok(w) for w, _ in outs],
        out_shape=[jax.ShapeDtypeStruct((bsz, t, w), dt) for w, dt in outs],
        compiler_params=_cparams("arbitrary", "arbitrary"),
        name="inproj",
    )(h, mod_l, ng, w_in_p, cda, sda, cq, ck, sm, qg, kvg, wuq_p, wukv_p)


def _da_attn_kernel(lam_ref, g_ref, q_ref, kt_ref, v_ref, o_ref, acc_ref, *, n_ctx, n_ctx_tiles, lam_init, tq):
    lv = lam_ref[...]
    lam = (jnp.exp(jnp.sum(lv[0:1] * lv[1:2], axis=-1, keepdims=True))
           - jnp.exp(jnp.sum(lv[2:3] * lv[3:4], axis=-1, keepdims=True)) + lam_init)
    lane_row = lax.broadcasted_iota(jnp.int32, (1, DA_QW), 1)
    lane = lax.broadcasted_iota(jnp.int32, (tq, DA_WIDTH), 1)

    def attend(nk):
        q = q_ref[...]
        kt = kt_ref[:, :nk]
        v = v_ref[:nk, :]
        acc_ref[...] = jnp.zeros_like(acc_ref)

        def body(j, carry):
            qm = q * jnp.where(lane_row // DA_QK == j, 1.0, 0.0).astype(BF16)
            s = _dot(qm, kt)
            e = jnp.exp2(s - jnp.max(s, axis=-1, keepdims=True))
            l = jnp.sum(e, axis=-1, keepdims=True)
            pv = _dot(e.astype(BF16), v)
            coef = jnp.where(j % 2 == 0, 1.0, -lam)
            acc_ref[...] += jnp.where(lane // DA_V == j // 2, pv * (coef / l), 0.0)
            return carry

        lax.fori_loop(0, 2 * DA_HEADS, body, 0)
        o = acc_ref[...]
        sq = o * o
        inv = jnp.zeros_like(o)
        for hh in range(DA_HEADS):
            hm = lane // DA_V == hh
            ms = jnp.sum(jnp.where(hm, sq, 0.0), axis=-1, keepdims=True) * (1.0 / DA_V)
            inv = jnp.where(hm, lax.rsqrt(ms + EPS), inv)
        o_ref[...] = (o * inv * g_ref[...] * (1.0 - lam_init)).astype(BF16)

    qi = pl.program_id(1)

    @pl.when(qi < n_ctx_tiles)
    def _():
        attend(n_ctx)

    @pl.when(qi >= n_ctx_tiles)
    def _():
        attend(kt_ref.shape[1])


def _da_attention(q, kt, v, lam_vec, g_tiled, *, lam_init, tq, n_ctx):
    bsz, t, _ = q.shape
    kern = functools.partial(_da_attn_kernel, n_ctx=n_ctx, n_ctx_tiles=n_ctx // tq, lam_init=lam_init, tq=tq)
    return pl.pallas_call(
        kern,
        grid=(bsz, t // tq),
        in_specs=[pl.BlockSpec(lam_vec.shape, lambda b, i: (0, 0)),
                  pl.BlockSpec(g_tiled.shape, lambda b, i: (0, 0)),
                  pl.BlockSpec((None, tq, DA_QW), lambda b, i: (b, i, 0)),
                  pl.BlockSpec((None, DA_QW, t), lambda b, i: (b, 0, 0)),
                  pl.BlockSpec((None, t, DA_WIDTH), lambda b, i: (b, 0, 0))],
        out_specs=pl.BlockSpec((None, tq, DA_WIDTH), lambda b, i: (b, i, 0)),
        out_shape=jax.ShapeDtypeStruct((bsz, t, DA_WIDTH), BF16),
        scratch_shapes=[pltpu.VMEM((tq, DA_WIDTH), F32)],
        compiler_params=_cparams("arbitrary", "arbitrary"),
        name="da_attention",
    )(lam_vec, g_tiled, q, kt, v)


def _mla_attn_kernel(q_ref, kt_ref, v_ref, o_ref, acc_ref, *, n_ctx, n_ctx_tiles, tq):
    lane = lax.broadcasted_iota(jnp.int32, (tq, MLA_WIDTH), 1)

    def attend(nk):
        v = v_ref[:nk, :]
        acc_ref[...] = jnp.zeros_like(acc_ref)

        def body(hh, carry):
            s = _dot(q_ref[hh], kt_ref[hh, :, :nk])
            e = jnp.exp2(s - jnp.max(s, axis=-1, keepdims=True))
            l = jnp.sum(e, axis=-1, keepdims=True)
            pv = _dot(e.astype(BF16), v)
            acc_ref[...] += jnp.where(lane // MLA_V == hh, pv * (1.0 / l), 0.0)
            return carry

        lax.fori_loop(0, MLA_HEADS, body, 0)
        o_ref[...] = acc_ref[...].astype(BF16)

    qi = pl.program_id(1)

    @pl.when(qi < n_ctx_tiles)
    def _():
        attend(n_ctx)

    @pl.when(qi >= n_ctx_tiles)
    def _():
        attend(kt_ref.shape[2])


def _mla_attention(q4, kt4, v, *, tq, n_ctx):
    bsz, _, t, _ = q4.shape
    kern = functools.partial(_mla_attn_kernel, n_ctx=n_ctx, n_ctx_tiles=n_ctx // tq, tq=tq)
    return pl.pallas_call(
        kern,
        grid=(bsz, t // tq),
        in_specs=[pl.BlockSpec((None, MLA_HEADS, tq, MLA_HEAD_PAD), lambda b, i: (b, 0, i, 0)),
                  pl.BlockSpec((None, MLA_HEADS, MLA_HEAD_PAD, t), lambda b, i: (b, 0, 0, 0)),
                  pl.BlockSpec((None, t, MLA_WIDTH), lambda b, i: (b, 0, 0))],
        out_specs=pl.BlockSpec((None, tq, MLA_WIDTH), lambda b, i: (b, i, 0)),
        out_shape=jax.ShapeDtypeStruct((bsz, t, MLA_WIDTH), BF16),
        scratch_shapes=[pltpu.VMEM((tq, MLA_WIDTH), F32)],
        compiler_params=_cparams("arbitrary", "arbitrary"),
        name="mla_attention",
    )(q4, kt4, v)


def _ssd_kernel(z_ref, xbc_ref, dt_ref, cw_ref, cb_ref, dtb_ref, alog_ref, dsk_ref, ng_ref,
                o_ref, xc_scr, y_scr, hf_scr, hb_scr, *, nc, nc0):
    q = SSD_CHUNK
    hp = SSD_HEADS * SSD_P
    lane_x = lax.broadcasted_iota(jnp.int32, (SSD_CONV_DIM, q), 1)

    def conv_body(c, carry):
        xc = xbc_ref[c]
        keep_prev = jnp.where((c == 0) | (c == nc0), 0.0, 1.0)
        keep_next = jnp.where((c == nc0 - 1) | (c == nc - 1), 0.0, 1.0)
        xp = xbc_ref[jnp.maximum(c - 1, 0)] * keep_prev
        xn = xbc_ref[jnp.minimum(c + 1, nc - 1)] * keep_next
        prev = pltpu.roll(jnp.where(lane_x == q - 1, xp, xc), 1, 1)
        nxt = pltpu.roll(jnp.where(lane_x == 0, xn, xc), q - 1, 1)
        a = _silu(cw_ref[0] * prev + cw_ref[1] * xc + cw_ref[2] * nxt + cb_ref[...])
        xc_scr[c] = a
        y_scr[c] = dsk_ref[...] * a[:hp]
        return carry

    lax.fori_loop(0, nc, conv_body, 0)
    hf_scr[...] = jnp.zeros_like(hf_scr)
    hb_scr[...] = jnp.zeros_like(hb_scr)

    sub = lax.broadcasted_iota(jnp.int32, (q, q), 0)
    lan = lax.broadcasted_iota(jnp.int32, (q, q), 1)
    lane_n = lax.broadcasted_iota(jnp.int32, (1, SSD_GN), 1)
    pad_rows = jnp.zeros((q - SSD_HEADS, q), F32)

    def run_dir(c, d, h_scr):
        fwd = d == 0
        tri = (sub <= lan) if fwd else (sub >= lan)
        tri01 = jnp.where(tri, 1.0, 0.0).astype(BF16)
        xc = xc_scr[c]
        xt, bt, ct = xc[:hp], xc[hp:hp + SSD_GN], xc[hp + SSD_GN:]
        hs = slice(SSD_HEADS * d, SSD_HEADS * (d + 1))
        dtl = dt_ref[c][hs] + dtb_ref[hs]
        dt = jnp.maximum(dtl, 0.0) + jnp.log(1.0 + jnp.exp(-jnp.abs(dtl)))
        dta = dt * (-jnp.exp(alog_ref[hs]))
        cum_pad = _dot_split3(jnp.concatenate([dta, pad_rows], axis=0), tri01)
        cum_row = cum_pad[:SSD_HEADS]
        cum_col = cum_pad.T
        btok = bt.T
        ct16 = ct.astype(BF16)
        hm = h_scr[...]
        gts, yoffs = [], []
        for g in range(SSD_GROUPS):
            gm = lane_n // SSD_STATE == g
            gts.append(_dot(jnp.where(gm, btok, 0.0).astype(BF16), ct16))
            rows = slice(g * hp // SSD_GROUPS, (g + 1) * hp // SSD_GROUPS)
            yoffs.append(_dot(hm[rows].astype(BF16), ct16))
        end = q - 1 if fwd else 0
        ys, xws, decs = [], [], []
        per_g = SSD_HEADS // SSD_GROUPS
        for hh in range(SSD_HEADS):
            g = hh // per_g
            ar = cum_row[hh:hh + 1, :]
            ac = cum_col[:, hh:hh + 1]
            dec = jnp.where(tri, jnp.exp(ar - ac), 0.0)
            sct = (gts[g] * dec).astype(BF16)
            xdt = xt[hh * SSD_P:(hh + 1) * SSD_P] * dt[hh:hh + 1, :]
            ydiag = _dot(xdt.astype(BF16), sct)
            yoff = yoffs[g][(hh % per_g) * SSD_P:(hh % per_g + 1) * SSD_P] * jnp.exp(ar)
            ys.append(ydiag + yoff)
            a_end = ar[:, end:end + 1]
            xws.append(xdt * jnp.exp(a_end - ar))
            decs.append(jnp.exp(a_end))
        y_scr[c] += jnp.concatenate(ys, axis=0)
        btok16 = btok.astype(BF16)
        new_rows = []
        for g in range(SSD_GROUPS):
            gm = lane_n // SSD_STATE == g
            xw = jnp.concatenate(xws[g * per_g:(g + 1) * per_g], axis=0).astype(BF16)
            inc = jnp.where(gm, _dot(xw, btok16), 0.0)
            for k in range(per_g):
                hh = g * per_g + k
                new_rows.append(hm[hh * SSD_P:(hh + 1) * SSD_P] * decs[hh] + inc[k * SSD_P:(k + 1) * SSD_P])
        h_scr[...] = jnp.concatenate(new_rows, axis=0)

    def step(s, carry):
        run_dir(s, 0, hf_scr)
        cb = jnp.where(s < nc0, nc0 - 1 - s, nc - 1 - (s - nc0))
        run_dir(cb, 1, hb_scr)
        return carry

    lax.fori_loop(0, nc, step, 0)

    def fin_body(c, carry):
        gated = y_scr[c] * _silu(z_ref[c].astype(F32))
        o_ref[c] = _rms(gated, ng_ref[...], axis=0).astype(BF16)
        return carry

    lax.fori_loop(0, nc, fin_body, 0)


def _ssd_mixer(z_t, xbc_t, dt_t, cw, cb, dtb, alog, dsk, ng, *, nc0):
    bsz, nc, _, q = xbc_t.shape
    hp = SSD_HEADS * SSD_P
    per_b = lambda f: pl.BlockSpec((None, nc, f, q), lambda b: (b, 0, 0, 0))
    full = lambda a: pl.BlockSpec(a.shape, lambda b: (0,) * a.ndim)
    kern = functools.partial(_ssd_kernel, nc=nc, nc0=nc0)
    return pl.pallas_call(
        kern,
        grid=(bsz,),
        in_specs=[per_b(hp), per_b(SSD_CONV_DIM), per_b(2 * SSD_HEADS),
                  full(cw), full(cb), full(dtb), full(alog), full(dsk), full(ng)],
        out_specs=per_b(hp),
        out_shape=jax.ShapeDtypeStruct((bsz, nc, hp, q), BF16),
        scratch_shapes=[pltpu.VMEM((nc, SSD_CONV_DIM, q), F32), pltpu.VMEM((nc, hp, q), F32),
                        pltpu.VMEM((hp, SSD_GN), F32), pltpu.VMEM((hp, SSD_GN), F32)],
        compiler_params=_cparams("arbitrary"),
        name="ssd_mixer",
    )(z_t, xbc_t, dt_t, cw, cb, dtb, alog, dsk, ng)


def _route(logits_t, bias_col):
    aff = _sigmoid(logits_t)
    sel = aff + bias_col
    rows = [sel[e:e + 1, :] for e in range(N_EXPERTS)]
    gscore = []
    for g in range(N_GROUPS):
        a, b, c, d = rows[PER_GROUP * g:PER_GROUP * (g + 1)]
        hi1, lo1, hi2, lo2 = jnp.maximum(a, b), jnp.minimum(a, b), jnp.maximum(c, d), jnp.minimum(c, d)
        gscore.append(jnp.maximum(hi1, hi2) + jnp.maximum(jnp.minimum(hi1, hi2), jnp.maximum(lo1, lo2)))
    best = jnp.zeros_like(gscore[0], dtype=jnp.int32)
    cur = gscore[0]
    for g in range(1, N_GROUPS):
        better = gscore[g] > cur
        best = jnp.where(better, g, best)
        cur = jnp.where(better, gscore[g], cur)
    eidx = lax.broadcasted_iota(jnp.int32, sel.shape, 0)
    masked = jnp.where(eidx // PER_GROUP == best, sel, -jnp.inf)
    m1 = jnp.max(masked, axis=0, keepdims=True)
    idx1 = jnp.min(jnp.where(masked == m1, eidx, N_EXPERTS), axis=0, keepdims=True)
    masked2 = jnp.where(eidx == idx1, -jnp.inf, masked)
    m2 = jnp.max(masked2, axis=0, keepdims=True)
    idx2 = jnp.min(jnp.where(masked2 == m2, eidx, N_EXPERTS), axis=0, keepdims=True)
    oh1, oh2 = eidx == idx1, eidx == idx2
    w1 = jnp.sum(jnp.where(oh1, aff, 0.0), axis=0, keepdims=True)
    w2 = jnp.sum(jnp.where(oh2, aff, 0.0), axis=0, keepdims=True)
    den = w1 + w2
    return oh1, oh2, w1 / den, w2 / den


def _outproj_kernel(oda_ref, ossd_ref, omla_ref, h_ref, mod_ref, ng_ref, wo_ref, rwt_ref, rb_ref,
                    h1_ref, xs_ref, meta_ref, cnt_ref, *, n_ctx_tiles, ctx_row, d, tm, s_loc):
    ti, b = pl.program_id(0), pl.program_id(1)
    row = jnp.where(ti < n_ctx_tiles, ctx_row, b)
    gate = mod_ref[pl.ds(row, 1), pl.ds(2 * d, d)]
    shift = mod_ref[pl.ds(row, 1), pl.ds(3 * d, d)]
    scale = mod_ref[pl.ds(row, 1), pl.ds(4 * d, d)]
    mix = (_dot(oda_ref[...], wo_ref[0:DA_WIDTH]) + _dot(ossd_ref[...], wo_ref[DA_WIDTH:DA_WIDTH + SSD_INNER])
           + _dot(omla_ref[...], wo_ref[DA_WIDTH + SSD_INNER:]))
    h1 = h_ref[...] + gate * mix
    h1_ref[...] = h1
    u = _ada_norm(h1, ng_ref[...], shift, scale)
    u16 = u.astype(BF16)
    logits_t = lax.dot_general(rwt_ref[...], u, (((1,), (1,)), ((), ())), preferred_element_type=F32,
                               precision=lax.Precision.HIGHEST)
    oh1, oh2, w1, w2 = _route(logits_t, rb_ref[...])
    cnt = jnp.where(oh1, 1.0, 0.0) + jnp.where(oh2, 1.0, 0.0)
    r_i = lax.broadcasted_iota(jnp.int32, (tm, tm), 0)
    c_i = lax.broadcasted_iota(jnp.int32, (tm, tm), 1)
    before = jnp.where(r_i < c_i, 1.0, 0.0).astype(BF16)
    rank = _dot(cnt.astype(BF16), before)
    tot = jnp.sum(cnt, axis=1, keepdims=True)
    ptot = jnp.floor((tot + (MOE_BLK - 1)) * (1.0 / MOE_BLK)) * MOE_BLK
    e_r = lax.broadcasted_iota(jnp.int32, (N_EXPERTS, N_EXPERTS), 0)
    e_c = lax.broadcasted_iota(jnp.int32, (N_EXPERTS, N_EXPERTS), 1)
    lower = jnp.where(e_c < e_r, 1.0, 0.0).astype(BF16)
    ptot_b = jnp.broadcast_to(ptot, (N_EXPERTS, LANES))
    off = _dot(lower, ptot_b.astype(BF16))[:, 0:1]
    slot = off + rank
    dest1 = jnp.sum(jnp.where(oh1, slot, 0.0), axis=0, keepdims=True)
    dest2 = jnp.sum(jnp.where(oh2, slot, 0.0), axis=0, keepdims=True)
    r_s = lax.broadcasted_iota(jnp.int32, (s_loc, tm), 0).astype(F32)
    p1, p2 = r_s == dest1, r_s == dest2
    perm = jnp.where(p1 | p2, 1.0, 0.0).astype(BF16)
    xs_ref[:, 0:d] = _dot(perm, u16).astype(BF16)
    wslot = jnp.sum(jnp.where(p1, w1, 0.0) + jnp.where(p2, w2, 0.0), axis=1, keepdims=True)
    w_hi = wslot.astype(BF16).astype(F32)
    lane_e = lax.broadcasted_iota(jnp.int32, (s_loc, MOE_EXTRA), 1)
    xs_ref[:, d:] = jnp.where(lane_e == 0, w_hi, jnp.where(lane_e == 1, wslot - w_hi, 0.0)).astype(BF16)
    row_m = lax.broadcasted_iota(jnp.int32, (LANES, tm), 0)
    meta_rows = jnp.where(row_m == 0, dest1, jnp.where(row_m == 1, dest2, 0.0))
    meta_ref[...] = meta_rows.T
    cnt_ref[...] = ptot_b


def _outproj_dispatch(oda, ossd, omla, h, mod_l, ng, wo, rwt, rb, *, tm, n_ctx, s_loc):
    bsz, t, d = h.shape
    nt = t // tm
    tok = lambda w: pl.BlockSpec((None, tm, w), lambda ti, b: (b, ti, 0))
    full = lambda a: pl.BlockSpec(a.shape, lambda ti, b: (0,) * a.ndim)
    tile = lambda r, w: pl.BlockSpec((None, r, w), lambda ti, b: (b * nt + ti, 0, 0))
    kern = functools.partial(_outproj_kernel, n_ctx_tiles=n_ctx // tm, ctx_row=bsz, d=d, tm=tm, s_loc=s_loc)
    return pl.pallas_call(
        kern,
        grid=(nt, bsz),
        in_specs=[tok(DA_WIDTH), tok(SSD_INNER), tok(MLA_WIDTH), tok(d), full(mod_l), full(ng), full(wo),
                  full(rwt), full(rb)],
        out_specs=[tok(d), tile(s_loc, d + MOE_EXTRA), tile(tm, LANES), tile(N_EXPERTS, LANES)],
        out_shape=[jax.ShapeDtypeStruct((bsz, t, d), F32),
                   jax.ShapeDtypeStruct((bsz * nt, s_loc, d + MOE_EXTRA), BF16),
                   jax.ShapeDtypeStruct((bsz * nt, tm, LANES), F32),
                   jax.ShapeDtypeStruct((bsz * nt, N_EXPERTS, LANES), F32)],
        compiler_params=_cparams("arbitrary", "arbitrary"),
        name="outproj_dispatch",
    )(oda, ossd, omla, h, mod_l, ng, wo, rwt, rb)


def _expert_kernel(se_ref, bi_ref, sv_ref, *refs, d):
    x_refs = refs[:MOE_STEP_BLKS]
    wg_ref, wu_ref, wd_ref, y_ref = refs[MOE_STEP_BLKS:]
    s = pl.program_id(0)

    @pl.when(sv_ref[s] > 0)
    def _():
        x = jnp.concatenate([r[...] for r in x_refs], axis=0)
        xm = x[:, :d]
        wrow = x[:, d:d + 1].astype(F32) + x[:, d + 1:d + 2].astype(F32)
        he = _silu(_dot(xm, wg_ref[...])) * _dot(xm, wu_ref[...])
        y_ref[...] = (_dot(he.astype(BF16), wd_ref[...]) * wrow).astype(BF16)

    @pl.when(sv_ref[s] == 0)
    def _():
        y_ref[...] = jnp.zeros_like(y_ref)


def _expert_ffn(xs2d, wg, wu, wd, step_e, blk_ids, step_valid, *, n_steps):
    d = wg.shape[1]
    rows = MOE_STEP_BLKS * MOE_BLK
    x_specs = [pl.BlockSpec((MOE_BLK, d + MOE_EXTRA), lambda s, se, bi, sv, j=j: (bi[s * MOE_STEP_BLKS + j], 0))
               for j in range(MOE_STEP_BLKS)]
    wspec = lambda a: pl.BlockSpec((None,) + a.shape[1:], lambda s, se, bi, sv: (se[s], 0, 0))
    grid_spec = pltpu.PrefetchScalarGridSpec(
        num_scalar_prefetch=3,
        grid=(n_steps,),
        in_specs=x_specs + [wspec(wg), wspec(wu), wspec(wd)],
        out_specs=pl.BlockSpec((rows, d), lambda s, se, bi, sv: (s, 0)),
    )
    return pl.pallas_call(
        functools.partial(_expert_kernel, d=d),
        grid_spec=grid_spec,
        out_shape=jax.ShapeDtypeStruct((n_steps * rows, d), BF16),
        compiler_params=_cparams("arbitrary"),
        name="expert_ffn",
    )(step_e, blk_ids, step_valid, *([xs2d] * MOE_STEP_BLKS), wg, wu, wd)


def _combine_kernel(inv_ref, *refs, n_lb, n_ctx_tiles, ctx_row, d, tm, s_loc, final):
    y_refs = refs[:n_lb]
    meta_ref, h1_ref, mod_ref, fg_ref, o_ref = refs[n_lb:]
    ti, b = pl.program_id(0), pl.program_id(1)
    row = jnp.where(ti < n_ctx_tiles, ctx_row, b)
    gate = mod_ref[pl.ds(row, 1), pl.ds(5 * d, d)]
    ys = jnp.concatenate([r[...] for r in y_refs], axis=0)
    meta = meta_ref[...]
    lane_s = lax.broadcasted_iota(jnp.int32, (tm, s_loc), 1).astype(F32)
    pt = jnp.where((lane_s == meta[:, 0:1]) | (lane_s == meta[:, 1:2]), 1.0, 0.0).astype(BF16)
    h2 = h1_ref[...] + gate * _dot(pt, ys)
    o_ref[...] = _rms(h2, fg_ref[...]) if final else h2


def _combine(ys_em, inv, meta, h1, mod_l, fg, *, tm, n_ctx, s_loc, final):
    bsz, t, d = h1.shape
    nt = t // tm
    n_lb = s_loc // MOE_BLK
    y_specs = [pl.BlockSpec((MOE_BLK, d), lambda ti, b, inv, j=j: (inv[(b * nt + ti) * n_lb + j], 0))
               for j in range(n_lb)]
    full = lambda a: pl.BlockSpec(a.shape, lambda ti, b, inv: (0,) * a.ndim)
    grid_spec = pltpu.PrefetchScalarGridSpec(
        num_scalar_prefetch=1,
        grid=(nt, bsz),
        in_specs=y_specs + [pl.BlockSpec((None, tm, LANES), lambda ti, b, inv: (b * nt + ti, 0, 0)),
                            pl.BlockSpec((None, tm, d), lambda ti, b, inv: (b, ti, 0)),
                            full(mod_l), full(fg)],
        out_specs=pl.BlockSpec((None, tm, d), lambda ti, b, inv: (b, ti, 0)),
    )
    kern = functools.partial(_combine_kernel, n_lb=n_lb, n_ctx_tiles=n_ctx // tm, ctx_row=bsz, d=d, tm=tm,
                             s_loc=s_loc, final=final)
    return pl.pallas_call(
        kern,
        grid_spec=grid_spec,
        out_shape=jax.ShapeDtypeStruct((bsz, t, d), F32),
        compiler_params=_cparams("arbitrary", "arbitrary"),
        name="moe_combine",
    )(inv, *([ys_em] * n_lb), meta, h1, mod_l, fg)


def _rope_tables(length, dim):
    rows = length // GRID_W
    row = jnp.repeat(jnp.arange(rows), GRID_W).astype(F32)
    col = jnp.tile(jnp.arange(GRID_W), rows).astype(F32)
    n_freq = dim // 4
    inv_freq = ROPE_THETA ** (-jnp.arange(n_freq, dtype=F32) / n_freq)
    ang = jnp.concatenate([row[:, None] * inv_freq, col[:, None] * inv_freq], axis=-1)
    return jnp.cos(ang), jnp.sin(ang)


def _table_set(n_ctx, n_lat):
    def lanes(cos, sin, lane0, width, reps, outside_cos):
        c = jnp.concatenate([cos, cos], axis=-1)
        s = jnp.concatenate([-sin, sin], axis=-1)
        grp_c = jnp.full((n_lat, width), outside_cos, F32).at[:, lane0:lane0 + c.shape[1]].set(c)
        grp_s = jnp.zeros((n_lat, width), F32).at[:, lane0:lane0 + s.shape[1]].set(s)
        ctx_c = jnp.full((n_ctx, width), outside_cos, F32).at[:, lane0:lane0 + c.shape[1]].set(1.0)
        ctx_s = jnp.zeros((n_ctx, width), F32)
        return (jnp.tile(jnp.concatenate([ctx_c, grp_c], axis=0), (1, reps)),
                jnp.tile(jnp.concatenate([ctx_s, grp_s], axis=0), (1, reps)))

    dcos, dsin = _rope_tables(n_lat, DA_QK)
    cda, sda = lanes(dcos, dsin, 0, DA_QK, DA_QW // DA_QK, 1.0)
    mcos, msin = _rope_tables(n_lat, MLA_ROPE)
    cq, sm = lanes(mcos, msin, KR_LANE0, LANES, 1, 1.0)
    ck, _ = lanes(mcos, msin, KR_LANE0, LANES, 1, 0.0)
    return cda, sda, cq, ck, sm


def _pack_w_in(w_in):
    depth, d, _ = w_in.shape
    o_ssd = DA_IN
    o_mla = DA_IN + SSD_IN
    z = w_in[..., o_ssd:o_ssd + SSD_INNER]
    xbc = w_in[..., o_ssd + SSD_INNER:o_ssd + SSD_INNER + SSD_CONV_DIM]
    dt = w_in[..., o_ssd + SSD_INNER + SSD_CONV_DIM:o_mla]
    cq = w_in[..., o_mla:o_mla + MLA_Q_RANK]
    ckv = w_in[..., o_mla + MLA_Q_RANK:o_mla + MLA_Q_RANK + MLA_KV_RANK]
    kr = w_in[..., o_mla + MLA_Q_RANK + MLA_KV_RANK:]
    zeros = lambda n: jnp.zeros((depth, d, n), w_in.dtype)
    misc = jnp.concatenate([zeros(KR_LANE0), kr, dt, zeros(LANES - DT_LANE0 - 2 * SSD_HEADS)], axis=-1)
    return jnp.concatenate([w_in[..., :DA_IN], z, xbc, cq, ckv, misc], axis=-1).astype(BF16)


def _pack_w_uq(w_uq):
    depth, r, _ = w_uq.shape
    w = w_uq.reshape(depth, r, MLA_HEADS, MLA_NOPE + MLA_ROPE)
    w = jnp.pad(w, ((0, 0), (0, 0), (0, 0), (0, MLA_HEAD_PAD - MLA_NOPE - MLA_ROPE)))
    return w.reshape(depth, r, MLA_QPAD).astype(BF16)


def _pack_w_ukv(w_ukv):
    depth, r, _ = w_ukv.shape
    w = w_ukv.reshape(depth, r, MLA_HEADS, MLA_NOPE + MLA_V)
    kn = jnp.pad(w[..., :MLA_NOPE], ((0, 0), (0, 0), (0, 0), (0, MLA_HEAD_PAD - MLA_NOPE)))
    return jnp.concatenate([kn.reshape(depth, r, MLA_QPAD), w[..., MLA_NOPE:].reshape(depth, r, MLA_WIDTH)],
                           axis=-1).astype(BF16)


def _dispatch_tables(pcnt, n_lb, n_steps):
    ntt = pcnt.shape[0]
    nb = (pcnt / MOE_BLK).astype(jnp.int32)
    lo = jnp.cumsum(nb, axis=1) - nb
    n_e = jnp.sum(nb, axis=0)
    p_e = (n_e + MOE_STEP_BLKS - 1) // MOE_STEP_BLKS * MOE_STEP_BLKS
    ends = jnp.cumsum(p_e)
    base = (ends - p_e)[None, :] + jnp.cumsum(nb, axis=0) - nb
    lb = jnp.arange(n_lb, dtype=jnp.int32)
    e_of = jnp.sum(lb[None, :, None] >= (lo + nb)[:, None, :], axis=-1)
    used = e_of < N_EXPERTS
    e_c = jnp.minimum(e_of, N_EXPERTS - 1)
    pos = jnp.take_along_axis(base, e_c, axis=1) + lb[None, :] - jnp.take_along_axis(lo, e_c, axis=1)
    inv = jnp.where(used, pos, 0).astype(jnp.int32)
    n_pos = n_steps * MOE_STEP_BLKS
    src = jnp.arange(ntt, dtype=jnp.int32)[:, None] * n_lb + lb[None, :]
    fwd = jnp.zeros((n_pos,), jnp.int32).at[jnp.where(used, pos, n_pos).reshape(-1)].set(src.reshape(-1), mode="drop")
    starts = jnp.arange(n_steps, dtype=jnp.int32) * MOE_STEP_BLKS
    step_e = jnp.minimum(jnp.searchsorted(ends, starts, side="right"), N_EXPERTS - 1).astype(jnp.int32)
    step_valid = (starts < ends[-1]).astype(jnp.int32)
    return inv.reshape(-1), fwd, step_e, step_valid


def kernel(x, c, ctx, c_ctx, norm_mix_g, norm_ffn_g, w_mod, b_mod, w_in, w_out, da_lambda, da_subln_g, ssd_conv_w, ssd_conv_b, ssd_dt_bias, ssd_a_log, ssd_d, ssd_norm_g, mla_q_norm_g, mla_kv_norm_g, mla_w_uq, mla_w_ukv, router_w, router_bias, exp_w_gate, exp_w_up, exp_w_down, final_norm_g):
    bsz, n_lat, d = x.shape
    n_ctx = ctx.shape[1]
    t = n_ctx + n_lat
    tm = min(256, n_ctx)
    assert n_ctx % tm == 0 and n_lat % tm == 0 and n_ctx % SSD_CHUNK == 0 and n_lat % SSD_CHUNK == 0
    nt = t // tm
    s_loc = 2 * tm + 2 * LANES
    n_lb = s_loc // MOE_BLK
    n_steps = -(-(bsz * nt * n_lb + N_EXPERTS * (MOE_STEP_BLKS - 1)) // MOE_STEP_BLKS)
    nc, nc0 = t // SSD_CHUNK, n_ctx // SSD_CHUNK
    q = SSD_CHUNK

    r_pad = -(-(bsz + 1) // 8) * 8
    c_rows = jnp.concatenate([c, c_ctx[None, :], jnp.zeros((r_pad - bsz - 1, d), F32)], axis=0)
    mod = _modulation(c_rows, w_mod, b_mod)

    w_in_p = _pack_w_in(w_in)
    w_uq_p = _pack_w_uq(mla_w_uq)
    w_ukv_p = _pack_w_ukv(mla_w_ukv)
    w_out16 = w_out.astype(BF16)
    wg16, wu16, wd16 = exp_w_gate.astype(BF16), exp_w_up.astype(BF16), exp_w_down.astype(BF16)
    tabs = _table_set(n_ctx, n_lat)
    rwt = router_w.T
    rb = router_bias.reshape(N_EXPERTS, 1)
    lane_b = lambda v: jnp.broadcast_to(v[..., None], v.shape + (q,))
    dsk_rows = jnp.repeat(ssd_d, SSD_P, axis=-1)

    h = jnp.concatenate([ctx, x], axis=1)
    for i in range(DEPTH):
        lam_init = 0.8 - 0.6 * math.exp(-0.3 * i)
        qda, kda, vda, z, xbc, dtm, qm, km, vm = _inproj(
            h, mod[i], norm_mix_g[i][None, :], w_in_p[i], tabs, mla_q_norm_g[i][None, :],
            mla_kv_norm_g[i][None, :], w_uq_p[i], w_ukv_p[i], tm=tm, n_ctx=n_ctx)
        o_da = _da_attention(qda, jnp.swapaxes(kda, 1, 2), vda, da_lambda[i],
                             jnp.tile(da_subln_g[i], DA_HEADS)[None, :], lam_init=lam_init, tq=tm, n_ctx=n_ctx)
        q4 = qm.reshape(bsz, t, MLA_HEADS, MLA_HEAD_PAD).transpose(0, 2, 1, 3)
        kt4 = km.reshape(bsz, t, MLA_HEADS, MLA_HEAD_PAD).transpose(0, 2, 3, 1)
        o_mla = _mla_attention(q4, kt4, vm, tq=tm, n_ctx=n_ctx)
        to_chunks = lambda a: a.reshape(bsz, nc, q, a.shape[-1]).transpose(0, 1, 3, 2)
        o_ssd_t = _ssd_mixer(
            to_chunks(z), to_chunks(xbc), to_chunks(dtm[..., DT_LANE0:DT_LANE0 + 2 * SSD_HEADS]),
            lane_b(ssd_conv_w[i]), lane_b(ssd_conv_b[i]), lane_b(ssd_dt_bias[i].reshape(-1)),
            lane_b(ssd_a_log[i].reshape(-1)), lane_b(dsk_rows[i]), lane_b(ssd_norm_g[i]), nc0=nc0)
        o_ssd = o_ssd_t.transpose(0, 1, 3, 2).reshape(bsz, t, SSD_INNER)
        h1, xs, meta, cnt = _outproj_dispatch(o_da, o_ssd, o_mla, h, mod[i], norm_ffn_g[i][None, :], w_out16[i],
                                              rwt, rb, tm=tm, n_ctx=n_ctx, s_loc=s_loc)
        inv, fwd, step_e, step_valid = _dispatch_tables(cnt[:, :, 0], n_lb, n_steps)
        ys_em = _expert_ffn(xs.reshape(bsz * nt * s_loc, d + MOE_EXTRA), wg16[i], wu16[i], wd16[i],
                            step_e, fwd, step_valid, n_steps=n_steps)
        h = _combine(ys_em, inv, meta, h1, mod[i], final_norm_g[None, :], tm=tm, n_ctx=n_ctx, s_loc=s_loc,
                     final=(i == DEPTH - 1))
    return h[:, n_ctx:]
```

```python
import functools
import math

import jax
import jax.numpy as jnp
from jax import lax
from jax.experimental import pallas as pl
from jax.experimental.pallas import tpu as pltpu

F32 = jnp.float32
BF16 = jnp.bfloat16

DEPTH = 4
GRID_W = 64
EPS = 1e-6
ROPE_THETA = 10000.0
DA_HEADS, DA_QK = 4, 32
DA_V = 2 * DA_QK
DA_WIDTH = DA_HEADS * DA_V
DA_QW = DA_HEADS * 2 * DA_QK
DA_IN = 2 * DA_QW + DA_WIDTH
SSD_HEADS, SSD_P, SSD_GROUPS, SSD_STATE, SSD_CHUNK = 8, 64, 2, 64, 128
SSD_INNER = SSD_HEADS * SSD_P
SSD_GN = SSD_GROUPS * SSD_STATE
SSD_CONV_DIM = SSD_INNER + 2 * SSD_GN
SSD_IN = SSD_INNER + SSD_CONV_DIM + 2 * SSD_HEADS
MLA_HEADS, MLA_Q_RANK, MLA_KV_RANK, MLA_NOPE, MLA_ROPE, MLA_V = 4, 256, 128, 64, 32, 64
MLA_WIDTH = MLA_HEADS * MLA_V
MLA_IN = MLA_Q_RANK + MLA_KV_RANK + MLA_ROPE
MLA_SCALE = (MLA_NOPE + MLA_ROPE) ** -0.5
N_EXPERTS, N_GROUPS, D_EXPERT = 16, 4, 512
PER_GROUP = N_EXPERTS // N_GROUPS

LANES = 128
SUBLANES_BF16 = 16
VMEM_LIMIT_BYTES = 56 * 1024 * 1024

MLA_HEAD_PAD = LANES
MLA_QPAD = MLA_HEADS * MLA_HEAD_PAD
KR_LANE0 = MLA_NOPE
DT_LANE0 = MLA_NOPE + MLA_ROPE
IN_COLS = DA_IN + SSD_INNER + SSD_CONV_DIM + MLA_Q_RANK + MLA_KV_RANK + LANES
C_Z = DA_IN
C_XBC = C_Z + SSD_INNER
C_CQ = C_XBC + SSD_CONV_DIM
C_CKV = C_CQ + MLA_Q_RANK
C_MISC = C_CKV + MLA_KV_RANK
LOG2E = math.log2(math.e)
DA_QSCALE = DA_QK ** -0.5 * LOG2E
MLA_QSCALE = MLA_SCALE * LOG2E

KEY_CHUNK = 256
MOE_BLK = SUBLANES_BF16
MOE_STEP_BLKS = 32
MOE_EXTRA = LANES


def _sigmoid(x):
    return 1.0 / (1.0 + jnp.exp(-x))


def _silu(x):
    return x * _sigmoid(x)


def _rms(x, g, axis=-1):
    return x * lax.rsqrt(jnp.mean(x * x, axis=axis, keepdims=True) + EPS) * g


def _ada_norm(hv, g, shift, scale):
    return _rms(hv, g) * (1.0 + scale) + shift


def _rope(x, cos, sin_signed, half):
    w = x.shape[-1]
    lane = lax.broadcasted_iota(jnp.int32, x.shape, x.ndim - 1)
    first = (lane % (2 * half)) < half
    partner = jnp.where(first, pltpu.roll(x, w - half, x.ndim - 1), pltpu.roll(x, half, x.ndim - 1))
    return x * cos + partner * sin_signed


def _dot(a, b):
    return jnp.dot(a, b, preferred_element_type=F32)


def _dot_hi(a, b):
    return jnp.dot(a, b, preferred_element_type=F32, precision=lax.Precision.HIGHEST)


def _dot_split3(x, m01):
    x1 = x.astype(BF16)
    r1 = x - x1.astype(F32)
    x2 = r1.astype(BF16)
    x3 = (r1 - x2.astype(F32)).astype(BF16)
    return _dot(x1, m01) + _dot(x2, m01) + _dot(x3, m01)


def _layer_spec(a, i):
    return pl.BlockSpec((None,) + a.shape[1:], lambda *_: (i,) + (0,) * (a.ndim - 1))


def _cparams(*sem):
    return pltpu.CompilerParams(dimension_semantics=sem, vmem_limit_bytes=VMEM_LIMIT_BYTES)


def _mod_kernel(c_ref, w_ref, b_ref, o_ref):
    o_ref[...] = _dot_hi(_silu(c_ref[...]), w_ref[...]) + b_ref[...]


def _modulation(c_rows, w_mod, b_mod):
    depth, d, n = w_mod.shape
    r = c_rows.shape[0]
    tn = 1536
    return pl.pallas_call(
        _mod_kernel,
        grid=(depth, n // tn),
        in_specs=[pl.BlockSpec((r, d), lambda l, j: (0, 0)),
                  pl.BlockSpec((None, d, tn), lambda l, j: (l, 0, j)),
                  pl.BlockSpec((None, 1, tn), lambda l, j: (l, 0, j))],
        out_specs=pl.BlockSpec((None, r, tn), lambda l, j: (l, 0, j)),
        out_shape=jax.ShapeDtypeStruct((depth, r, n), F32),
        compiler_params=_cparams("arbitrary", "arbitrary"),
        name="modulation",
    )(c_rows, w_mod, b_mod.reshape(depth, 1, n))


def _inproj_kernel(h_ref, mod_ref, ng_ref, w_ref, cda_ref, sda_ref, cq_ref, ck_ref, sm_ref,
                   qg_ref, kvg_ref, wuq_ref, wukv_ref,
                   qda_ref, kda_ref, vda_ref, z_ref, xbc_ref, dtm_ref, qm_ref, km_ref, vm_ref,
                   *, n_ctx_tiles, ctx_row, d):
    ti, b = pl.program_id(0), pl.program_id(1)
    row = jnp.where(ti < n_ctx_tiles, ctx_row, b)
    shift = mod_ref[pl.ds(row, 1), pl.ds(0, d)]
    scale = mod_ref[pl.ds(row, 1), pl.ds(d, d)]
    u = _ada_norm(h_ref[...], ng_ref[...], shift, scale).astype(BF16)
    acc = _dot(u, w_ref[...])
    cda, sda = cda_ref[...], sda_ref[...]
    qda_ref[...] = (_rope(acc[:, 0:DA_QW], cda, sda, DA_QK // 2) * DA_QSCALE).astype(BF16)
    kda_ref[...] = _rope(acc[:, DA_QW:2 * DA_QW], cda, sda, DA_QK // 2).astype(BF16)
    vda_ref[...] = acc[:, 2 * DA_QW:DA_IN].astype(BF16)
    z_ref[...] = acc[:, C_Z:C_XBC].astype(BF16)
    xbc_ref[...] = acc[:, C_XBC:C_CQ]
    misc = acc[:, C_MISC:IN_COLS]
    dtm_ref[...] = misc
    cqn = _rms(acc[:, C_CQ:C_CKV], qg_ref[...]).astype(BF16)
    qm = _dot(cqn, wuq_ref[...])
    sm = sm_ref[...]
    cos_q = jnp.concatenate([cq_ref[...]] * MLA_HEADS, axis=1)
    sin_q = jnp.concatenate([sm] * MLA_HEADS, axis=1)
    qm_ref[...] = (_rope(qm, cos_q, sin_q, MLA_ROPE // 2) * MLA_QSCALE).astype(BF16)
    ckvn = _rms(acc[:, C_CKV:C_MISC], kvg_ref[...]).astype(BF16)
    kv = _dot(ckvn, wukv_ref[...])
    kr = _rope(misc, ck_ref[...], sm, MLA_ROPE // 2)
    km_ref[...] = (kv[:, :MLA_QPAD] + jnp.concatenate([kr] * MLA_HEADS, axis=1)).astype(BF16)
    vm_ref[...] = kv[:, MLA_QPAD:].astype(BF16)


def _inproj(h, mod, ng, w_in_p, tabs, qg, kvg, wuq_p, wukv_p, *, layer, tm, n_ctx):
    bsz, t, d = h.shape
    nt = t // tm
    tok = lambda w: pl.BlockSpec((None, tm, w), lambda ti, b: (b, ti, 0))
    tab = lambda w: pl.BlockSpec((tm, w), lambda ti, b: (ti, 0))
    lay = lambda a: _layer_spec(a, layer)
    cda, sda, cq, ck, sm = tabs
    outs = [(DA_QW, BF16), (DA_QW, BF16), (DA_WIDTH, BF16), (SSD_INNER, BF16), (SSD_CONV_DIM, F32),
            (LANES, F32), (MLA_QPAD, BF16), (MLA_QPAD, BF16), (MLA_WIDTH, BF16)]
    kern = functools.partial(_inproj_kernel, n_ctx_tiles=n_ctx // tm, ctx_row=bsz, d=d)
    return pl.pallas_call(
        kern,
        grid=(nt, bsz),
        in_specs=[tok(d), lay(mod), lay(ng), lay(w_in_p), tab(DA_QW), tab(DA_QW), tab(LANES), tab(LANES),
                  tab(LANES), lay(qg), lay(kvg), lay(wuq_p), lay(wukv_p)],
        out_specs=[tok(w) for w, _ in outs],
        out_shape=[jax.ShapeDtypeStruct((bsz, t, w), dt) for w, dt in outs],
        compiler_params=_cparams("arbitrary", "arbitrary"),
        name="inproj",
    )(h, mod, ng, w_in_p, cda, sda, cq, ck, sm, qg, kvg, wuq_p, wukv_p)


def _scores_pass(qm, kt_at, nk, s_scr, m_scr):
    kc = KEY_CHUNK if nk % KEY_CHUNK == 0 else LANES
    mrun = None
    for c0 in range(0, nk, kc):
        s_c = _dot(qm, kt_at(c0, c0 + kc))
        s_scr[:, c0:c0 + kc] = s_c
        for l0 in range(0, kc, LANES):
            part = s_c[:, l0:l0 + LANES]
            mrun = part if mrun is None else jnp.maximum(mrun, part)
    m_scr[...] = mrun


def _pv_pass(v_ref, nk, s_scr, m_scr):
    kc = KEY_CHUNK if nk % KEY_CHUNK == 0 else LANES
    m = jnp.max(m_scr[...], axis=-1, keepdims=True)
    lrun = pv = None
    for c0 in range(0, nk, kc):
        e = jnp.exp2(s_scr[:, c0:c0 + kc] - m)
        for l0 in range(0, kc, LANES):
            part = e[:, l0:l0 + LANES]
            lrun = part if lrun is None else lrun + part
        inc = _dot(e.astype(BF16), v_ref[c0:c0 + kc, :])
        pv = inc if pv is None else pv + inc
    return pv, jnp.sum(lrun, axis=-1, keepdims=True)


def _attend_heads(q_of, kt_of, v_ref, nk, scr, n_sub, combine):
    _scores_pass(q_of(0), kt_of(0), nk, *scr[0])
    out = None
    for j in range(n_sub):
        if j + 1 < n_sub:
            _scores_pass(q_of(j + 1), kt_of(j + 1), nk, *scr[(j + 1) % 2])
        pv, l = _pv_pass(v_ref, nk, *scr[j % 2])
        out = combine(j, pv, l, out)
    return out


def _da_attn_kernel(lam_ref, g_ref, q_ref, kt_ref, v_ref, o_ref, s0_scr, m0_scr, s1_scr, m1_scr, *, n_ctx,
                    n_ctx_tiles, lam_init, tq):
    lv = lam_ref[...]
    lam = (jnp.exp(jnp.sum(lv[0:1] * lv[1:2], axis=-1, keepdims=True))
           - jnp.exp(jnp.sum(lv[2:3] * lv[3:4], axis=-1, keepdims=True)) + lam_init)
    lane_row = lax.broadcasted_iota(jnp.int32, (1, DA_QW), 1)
    lane = lax.broadcasted_iota(jnp.int32, (tq, DA_WIDTH), 1)
    scr = ((s0_scr, m0_scr), (s1_scr, m1_scr))

    def attend(nk):
        q = q_ref[...]

        def q_of(j):
            return q * jnp.where(lane_row // DA_QK == j, 1.0, 0.0).astype(BF16)

        def combine(j, pv, l, out):
            coef = 1.0 if j % 2 == 0 else -lam
            term = jnp.where(lane // DA_V == j // 2, pv * (coef / l), 0.0)
            return term if out is None else out + term

        o = _attend_heads(q_of, lambda j: (lambda c0, c1: kt_ref[:, c0:c1]), v_ref, nk, scr, 2 * DA_HEADS, combine)
        sq = o * o
        inv = jnp.zeros_like(o)
        for hh in range(DA_HEADS):
            hm = lane // DA_V == hh
            ms = jnp.sum(jnp.where(hm, sq, 0.0), axis=-1, keepdims=True) * (1.0 / DA_V)
            inv = jnp.where(hm, lax.rsqrt(ms + EPS), inv)
        o_ref[...] = (o * inv * g_ref[...] * (1.0 - lam_init)).astype(BF16)

    qi = pl.program_id(1)

    @pl.when(qi < n_ctx_tiles)
    def _():
        attend(n_ctx)

    @pl.when(qi >= n_ctx_tiles)
    def _():
        attend(kt_ref.shape[1])


def _da_attention(q, kt, v, lam_vec, g_tiled, *, layer, lam_init, tq, n_ctx):
    bsz, t, _ = q.shape
    kern = functools.partial(_da_attn_kernel, n_ctx=n_ctx, n_ctx_tiles=n_ctx // tq, lam_init=lam_init, tq=tq)
    return pl.pallas_call(
        kern,
        grid=(bsz, t // tq),
        in_specs=[_layer_spec(lam_vec, layer), _layer_spec(g_tiled, layer),
                  pl.BlockSpec((None, tq, DA_QW), lambda b, i: (b, i, 0)),
                  pl.BlockSpec((None, DA_QW, t), lambda b, i: (b, 0, 0)),
                  pl.BlockSpec((None, t, DA_WIDTH), lambda b, i: (b, 0, 0))],
        out_specs=pl.BlockSpec((None, tq, DA_WIDTH), lambda b, i: (b, i, 0)),
        out_shape=jax.ShapeDtypeStruct((bsz, t, DA_WIDTH), BF16),
        scratch_shapes=[pltpu.VMEM((tq, t), F32), pltpu.VMEM((tq, LANES), F32)] * 2,
        compiler_params=_cparams("arbitrary", "arbitrary"),
        name="da_attention",
    )(lam_vec, g_tiled, q, kt, v)


def _mla_attn_kernel(q_ref, kt_ref, v_ref, o_ref, s0_scr, m0_scr, s1_scr, m1_scr, *, n_ctx, n_ctx_tiles, tq):
    lane = lax.broadcasted_iota(jnp.int32, (tq, MLA_WIDTH), 1)
    scr = ((s0_scr, m0_scr), (s1_scr, m1_scr))

    def attend(nk):
        def combine(hh, pv, l, out):
            term = jnp.where(lane // MLA_V == hh, pv * (1.0 / l), 0.0)
            return term if out is None else out + term

        o = _attend_heads(lambda hh: q_ref[hh], lambda hh: (lambda c0, c1: kt_ref[hh, :, c0:c1]), v_ref, nk, scr,
                          MLA_HEADS, combine)
        o_ref[...] = o.astype(BF16)

    qi = pl.program_id(1)

    @pl.when(qi < n_ctx_tiles)
    def _():
        attend(n_ctx)

    @pl.when(qi >= n_ctx_tiles)
    def _():
        attend(kt_ref.shape[2])


def _mla_attention(q4, kt4, v, *, tq, n_ctx):
    bsz, _, t, _ = q4.shape
    kern = functools.partial(_mla_attn_kernel, n_ctx=n_ctx, n_ctx_tiles=n_ctx // tq, tq=tq)
    return pl.pallas_call(
        kern,
        grid=(bsz, t // tq),
        in_specs=[pl.BlockSpec((None, MLA_HEADS, tq, MLA_HEAD_PAD), lambda b, i: (b, 0, i, 0)),
                  pl.BlockSpec((None, MLA_HEADS, MLA_HEAD_PAD, t), lambda b, i: (b, 0, 0, 0)),
                  pl.BlockSpec((None, t, MLA_WIDTH), lambda b, i: (b, 0, 0))],
        out_specs=pl.BlockSpec((None, tq, MLA_WIDTH), lambda b, i: (b, i, 0)),
        out_shape=jax.ShapeDtypeStruct((bsz, t, MLA_WIDTH), BF16),
        scratch_shapes=[pltpu.VMEM((tq, t), F32), pltpu.VMEM((tq, LANES), F32)] * 2,
        compiler_params=_cparams("arbitrary", "arbitrary"),
        name="mla_attention",
    )(q4, kt4, v)


def _ssd_kernel(z_ref, xbc_ref, dt_ref, cw_ref, cb_ref, dtb_ref, alog_ref, dsk_ref, ng_ref,
                o_ref, xc_scr, y_scr, hf_scr, hb_scr, *, nc, nc0):
    q = SSD_CHUNK
    hp = SSD_HEADS * SSD_P
    lane_x = lax.broadcasted_iota(jnp.int32, (SSD_CONV_DIM, q), 1)

    def conv_body(c, carry):
        xc = xbc_ref[c]
        keep_prev = jnp.where((c == 0) | (c == nc0), 0.0, 1.0)
        keep_next = jnp.where((c == nc0 - 1) | (c == nc - 1), 0.0, 1.0)
        xp = xbc_ref[jnp.maximum(c - 1, 0)] * keep_prev
        xn = xbc_ref[jnp.minimum(c + 1, nc - 1)] * keep_next
        prev = pltpu.roll(jnp.where(lane_x == q - 1, xp, xc), 1, 1)
        nxt = pltpu.roll(jnp.where(lane_x == 0, xn, xc), q - 1, 1)
        a = _silu(cw_ref[0] * prev + cw_ref[1] * xc + cw_ref[2] * nxt + cb_ref[...])
        xc_scr[c] = a
        y_scr[c] = dsk_ref[...] * a[:hp]
        return carry

    lax.fori_loop(0, nc, conv_body, 0)
    hf_scr[...] = jnp.zeros_like(hf_scr)
    hb_scr[...] = jnp.zeros_like(hb_scr)

    sub = lax.broadcasted_iota(jnp.int32, (q, q), 0)
    lan = lax.broadcasted_iota(jnp.int32, (q, q), 1)
    lane_n = lax.broadcasted_iota(jnp.int32, (1, SSD_GN), 1)
    pad_rows = jnp.zeros((q - SSD_HEADS, q), F32)

    def run_dir(c, d, h_scr):
        fwd = d == 0
        tri = (sub <= lan) if fwd else (sub >= lan)
        tri01 = jnp.where(tri, 1.0, 0.0).astype(BF16)
        xc = xc_scr[c]
        xt, bt, ct = xc[:hp], xc[hp:hp + SSD_GN], xc[hp + SSD_GN:]
        hs = slice(SSD_HEADS * d, SSD_HEADS * (d + 1))
        dtl = dt_ref[c][hs] + dtb_ref[hs]
        dt = jnp.maximum(dtl, 0.0) + jnp.log(1.0 + jnp.exp(-jnp.abs(dtl)))
        dta = dt * (-jnp.exp(alog_ref[hs]))
        cum_pad = _dot_split3(jnp.concatenate([dta, pad_rows], axis=0), tri01)
        cum_row = cum_pad[:SSD_HEADS]
        cum_col = cum_pad.T
        btok = bt.T
        ct16 = ct.astype(BF16)
        hm = h_scr[...]
        gts, yoffs = [], []
        for g in range(SSD_GROUPS):
            gm = lane_n // SSD_STATE == g
            gts.append(_dot(jnp.where(gm, btok, 0.0).astype(BF16), ct16))
            rows = slice(g * hp // SSD_GROUPS, (g + 1) * hp // SSD_GROUPS)
            yoffs.append(_dot(hm[rows].astype(BF16), ct16))
        end = q - 1 if fwd else 0
        ys, xws, decs = [], [], []
        per_g = SSD_HEADS // SSD_GROUPS
        for hh in range(SSD_HEADS):
            g = hh // per_g
            ar = cum_row[hh:hh + 1, :]
            ac = cum_col[:, hh:hh + 1]
            dec = jnp.where(tri, jnp.exp(ar - ac), 0.0)
            sct = (gts[g] * dec).astype(BF16)
            xdt = xt[hh * SSD_P:(hh + 1) * SSD_P] * dt[hh:hh + 1, :]
            ydiag = _dot(xdt.astype(BF16), sct)
            yoff = yoffs[g][(hh % per_g) * SSD_P:(hh % per_g + 1) * SSD_P] * jnp.exp(ar)
            ys.append(ydiag + yoff)
            a_end = ar[:, end:end + 1]
            xws.append(xdt * jnp.exp(a_end - ar))
            decs.append(jnp.exp(a_end))
        y_scr[c] += jnp.concatenate(ys, axis=0)
        btok16 = btok.astype(BF16)
        new_rows = []
        for g in range(SSD_GROUPS):
            gm = lane_n // SSD_STATE == g
            xw = jnp.concatenate(xws[g * per_g:(g + 1) * per_g], axis=0).astype(BF16)
            inc = jnp.where(gm, _dot(xw, btok16), 0.0)
            for k in range(per_g):
                hh = g * per_g + k
                new_rows.append(hm[hh * SSD_P:(hh + 1) * SSD_P] * decs[hh] + inc[k * SSD_P:(k + 1) * SSD_P])
        h_scr[...] = jnp.concatenate(new_rows, axis=0)

    def step(s, carry):
        run_dir(s, 0, hf_scr)
        cb = jnp.where(s < nc0, nc0 - 1 - s, nc - 1 - (s - nc0))
        run_dir(cb, 1, hb_scr)
        return carry

    lax.fori_loop(0, nc, step, 0)

    def fin_body(c, carry):
        gated = y_scr[c] * _silu(z_ref[c].astype(F32))
        o_ref[c] = _rms(gated, ng_ref[...], axis=0).astype(BF16)
        return carry

    lax.fori_loop(0, nc, fin_body, 0)


def _ssd_mixer(z_t, xbc_t, dt_t, cw, cb, dtb, alog, dsk, ng, *, layer, nc0):
    bsz, nc, _, q = xbc_t.shape
    hp = SSD_HEADS * SSD_P
    per_b = lambda f: pl.BlockSpec((None, nc, f, q), lambda b: (b, 0, 0, 0))
    full = lambda a: _layer_spec(a, layer)
    kern = functools.partial(_ssd_kernel, nc=nc, nc0=nc0)
    return pl.pallas_call(
        kern,
        grid=(bsz,),
        in_specs=[per_b(hp), per_b(SSD_CONV_DIM), per_b(2 * SSD_HEADS),
                  full(cw), full(cb), full(dtb), full(alog), full(dsk), full(ng)],
        out_specs=per_b(hp),
        out_shape=jax.ShapeDtypeStruct((bsz, nc, hp, q), BF16),
        scratch_shapes=[pltpu.VMEM((nc, SSD_CONV_DIM, q), F32), pltpu.VMEM((nc, hp, q), F32),
                        pltpu.VMEM((hp, SSD_GN), F32), pltpu.VMEM((hp, SSD_GN), F32)],
        compiler_params=_cparams("arbitrary"),
        name="ssd_mixer",
    )(z_t, xbc_t, dt_t, cw, cb, dtb, alog, dsk, ng)


def _route(logits_t, bias_col):
    aff = _sigmoid(logits_t)
    sel = aff + bias_col
    rows = [sel[e:e + 1, :] for e in range(N_EXPERTS)]
    gscore = []
    for g in range(N_GROUPS):
        a, b, c, d = rows[PER_GROUP * g:PER_GROUP * (g + 1)]
        hi1, lo1, hi2, lo2 = jnp.maximum(a, b), jnp.minimum(a, b), jnp.maximum(c, d), jnp.minimum(c, d)
        gscore.append(jnp.maximum(hi1, hi2) + jnp.maximum(jnp.minimum(hi1, hi2), jnp.maximum(lo1, lo2)))
    best = jnp.zeros_like(gscore[0], dtype=jnp.int32)
    cur = gscore[0]
    for g in range(1, N_GROUPS):
        better = gscore[g] > cur
        best = jnp.where(better, g, best)
        cur = jnp.where(better, gscore[g], cur)
    eidx = lax.broadcasted_iota(jnp.int32, sel.shape, 0)
    masked = jnp.where(eidx // PER_GROUP == best, sel, -jnp.inf)
    m1 = jnp.max(masked, axis=0, keepdims=True)
    idx1 = jnp.min(jnp.where(masked == m1, eidx, N_EXPERTS), axis=0, keepdims=True)
    masked2 = jnp.where(eidx == idx1, -jnp.inf, masked)
    m2 = jnp.max(masked2, axis=0, keepdims=True)
    idx2 = jnp.min(jnp.where(masked2 == m2, eidx, N_EXPERTS), axis=0, keepdims=True)
    oh1, oh2 = eidx == idx1, eidx == idx2
    w1 = jnp.sum(jnp.where(oh1, aff, 0.0), axis=0, keepdims=True)
    w2 = jnp.sum(jnp.where(oh2, aff, 0.0), axis=0, keepdims=True)
    den = w1 + w2
    return oh1, oh2, w1 / den, w2 / den


def _outproj_kernel(oda_ref, ossd_ref, omla_ref, h_ref, mod_ref, ng_ref, wo_ref, rwt_ref, rb_ref,
                    h1_ref, xs_ref, meta_ref, cnt_ref, *, n_ctx_tiles, ctx_row, d, tm, s_loc):
    ti, b = pl.program_id(0), pl.program_id(1)
    row = jnp.where(ti < n_ctx_tiles, ctx_row, b)
    gate = mod_ref[pl.ds(row, 1), pl.ds(2 * d, d)]
    shift = mod_ref[pl.ds(row, 1), pl.ds(3 * d, d)]
    scale = mod_ref[pl.ds(row, 1), pl.ds(4 * d, d)]
    mix = (_dot(oda_ref[...], wo_ref[0:DA_WIDTH]) + _dot(ossd_ref[...], wo_ref[DA_WIDTH:DA_WIDTH + SSD_INNER])
           + _dot(omla_ref[...], wo_ref[DA_WIDTH + SSD_INNER:]))
    h1 = h_ref[...] + gate * mix
    h1_ref[...] = h1
    u = _ada_norm(h1, ng_ref[...], shift, scale)
    u16 = u.astype(BF16)
    logits_t = lax.dot_general(rwt_ref[...], u, (((1,), (1,)), ((), ())), preferred_element_type=F32,
                               precision=lax.Precision.HIGHEST)
    oh1, oh2, w1, w2 = _route(logits_t, rb_ref[...])
    cnt = jnp.where(oh1, 1.0, 0.0) + jnp.where(oh2, 1.0, 0.0)
    r_i = lax.broadcasted_iota(jnp.int32, (tm, tm), 0)
    c_i = lax.broadcasted_iota(jnp.int32, (tm, tm), 1)
    before = jnp.where(r_i < c_i, 1.0, 0.0).astype(BF16)
    rank = _dot(cnt.astype(BF16), before)
    tot = jnp.sum(cnt, axis=1, keepdims=True)
    ptot = jnp.floor((tot + (MOE_BLK - 1)) * (1.0 / MOE_BLK)) * MOE_BLK
    e_r = lax.broadcasted_iota(jnp.int32, (N_EXPERTS, N_EXPERTS), 0)
    e_c = lax.broadcasted_iota(jnp.int32, (N_EXPERTS, N_EXPERTS), 1)
    lower = jnp.where(e_c < e_r, 1.0, 0.0).astype(BF16)
    ptot_b = jnp.broadcast_to(ptot, (N_EXPERTS, LANES))
    off = _dot(lower, ptot_b.astype(BF16))[:, 0:1]
    slot = off + rank
    dest1 = jnp.sum(jnp.where(oh1, slot, 0.0), axis=0, keepdims=True)
    dest2 = jnp.sum(jnp.where(oh2, slot, 0.0), axis=0, keepdims=True)
    r_s = lax.broadcasted_iota(jnp.int32, (s_loc, tm), 0).astype(F32)
    p1, p2 = r_s == dest1, r_s == dest2
    perm = jnp.where(p1 | p2, 1.0, 0.0).astype(BF16)
    xs_ref[:, 0:d] = _dot(perm, u16).astype(BF16)
    wslot = jnp.sum(jnp.where(p1, w1, 0.0) + jnp.where(p2, w2, 0.0), axis=1, keepdims=True)
    w_hi = wslot.astype(BF16).astype(F32)
    lane_e = lax.broadcasted_iota(jnp.int32, (s_loc, MOE_EXTRA), 1)
    xs_ref[:, d:] = jnp.where(lane_e == 0, w_hi, jnp.where(lane_e == 1, wslot - w_hi, 0.0)).astype(BF16)
    row_m = lax.broadcasted_iota(jnp.int32, (LANES, tm), 0)
    meta_rows = jnp.where(row_m == 0, dest1, jnp.where(row_m == 1, dest2, 0.0))
    meta_ref[...] = meta_rows.T
    cnt_ref[...] = ptot_b


def _outproj_dispatch(oda, ossd, omla, h, mod, ng, wo, rwt, rb, *, layer, tm, n_ctx, s_loc):
    bsz, t, d = h.shape
    nt = t // tm
    tok = lambda w: pl.BlockSpec((None, tm, w), lambda ti, b: (b, ti, 0))
    full = lambda a: pl.BlockSpec(a.shape, lambda ti, b: (0,) * a.ndim)
    tile = lambda r, w: pl.BlockSpec((None, r, w), lambda ti, b: (b * nt + ti, 0, 0))
    lay = lambda a: _layer_spec(a, layer)
    kern = functools.partial(_outproj_kernel, n_ctx_tiles=n_ctx // tm, ctx_row=bsz, d=d, tm=tm, s_loc=s_loc)
    return pl.pallas_call(
        kern,
        grid=(nt, bsz),
        in_specs=[tok(DA_WIDTH), tok(SSD_INNER), tok(MLA_WIDTH), tok(d), lay(mod), lay(ng), lay(wo),
                  full(rwt), full(rb)],
        out_specs=[tok(d), tile(s_loc, d + MOE_EXTRA), tile(tm, LANES), tile(N_EXPERTS, LANES)],
        out_shape=[jax.ShapeDtypeStruct((bsz, t, d), F32),
                   jax.ShapeDtypeStruct((bsz * nt, s_loc, d + MOE_EXTRA), BF16),
                   jax.ShapeDtypeStruct((bsz * nt, tm, LANES), F32),
                   jax.ShapeDtypeStruct((bsz * nt, N_EXPERTS, LANES), F32)],
        compiler_params=_cparams("arbitrary", "arbitrary"),
        name="outproj_dispatch",
    )(oda, ossd, omla, h, mod, ng, wo, rwt, rb)


def _expert_kernel(se_ref, bi_ref, sv_ref, *refs, d):
    x_refs = refs[:MOE_STEP_BLKS]
    wg_ref, wu_ref, wd_ref, y_ref = refs[MOE_STEP_BLKS:]
    s = pl.program_id(0)

    @pl.when(sv_ref[s] > 0)
    def _():
        x = jnp.concatenate([r[...] for r in x_refs], axis=0)
        xm = x[:, :d]
        wrow = x[:, d:d + 1].astype(F32) + x[:, d + 1:d + 2].astype(F32)
        he = _silu(_dot(xm, wg_ref[...])) * _dot(xm, wu_ref[...])
        y_ref[...] = (_dot(he.astype(BF16), wd_ref[...]) * wrow).astype(BF16)

    @pl.when(sv_ref[s] == 0)
    def _():
        y_ref[...] = jnp.zeros_like(y_ref)


def _expert_ffn(xs2d, wg, wu, wd, step_e, blk_ids, step_valid, *, layer, n_steps):
    d = wg.shape[2]
    rows = MOE_STEP_BLKS * MOE_BLK
    x_specs = [pl.BlockSpec((MOE_BLK, d + MOE_EXTRA), lambda s, se, bi, sv, j=j: (bi[s * MOE_STEP_BLKS + j], 0))
               for j in range(MOE_STEP_BLKS)]
    wspec = lambda a: pl.BlockSpec((None, None) + a.shape[2:], lambda s, se, bi, sv: (layer, se[s], 0, 0))
    grid_spec = pltpu.PrefetchScalarGridSpec(
        num_scalar_prefetch=3,
        grid=(n_steps,),
        in_specs=x_specs + [wspec(wg), wspec(wu), wspec(wd)],
        out_specs=pl.BlockSpec((rows, d), lambda s, se, bi, sv: (s, 0)),
    )
    return pl.pallas_call(
        functools.partial(_expert_kernel, d=d),
        grid_spec=grid_spec,
        out_shape=jax.ShapeDtypeStruct((n_steps * rows, d), BF16),
        compiler_params=_cparams("arbitrary"),
        name="expert_ffn",
    )(step_e, blk_ids, step_valid, *([xs2d] * MOE_STEP_BLKS), wg, wu, wd)


def _combine_kernel(inv_ref, *refs, n_lb, n_ctx_tiles, ctx_row, d, tm, s_loc, final):
    y_refs = refs[:n_lb]
    meta_ref, h1_ref, mod_ref, fg_ref, o_ref = refs[n_lb:]
    ti, b = pl.program_id(0), pl.program_id(1)
    row = jnp.where(ti < n_ctx_tiles, ctx_row, b)
    gate = mod_ref[pl.ds(row, 1), pl.ds(5 * d, d)]
    ys = jnp.concatenate([r[...] for r in y_refs], axis=0)
    meta = meta_ref[...]
    lane_s = lax.broadcasted_iota(jnp.int32, (tm, s_loc), 1).astype(F32)
    pt = jnp.where((lane_s == meta[:, 0:1]) | (lane_s == meta[:, 1:2]), 1.0, 0.0).astype(BF16)
    h2 = h1_ref[...] + gate * _dot(pt, ys)
    o_ref[...] = _rms(h2, fg_ref[...]) if final else h2


def _combine(ys_em, inv, meta, h1, mod, fg, *, layer, tm, n_ctx, s_loc, final):
    bsz, t, d = h1.shape
    nt = t // tm
    n_lb = s_loc // MOE_BLK
    y_specs = [pl.BlockSpec((MOE_BLK, d), lambda ti, b, inv, j=j: (inv[(b * nt + ti) * n_lb + j], 0))
               for j in range(n_lb)]
    full = lambda a: pl.BlockSpec(a.shape, lambda ti, b, inv: (0,) * a.ndim)
    grid_spec = pltpu.PrefetchScalarGridSpec(
        num_scalar_prefetch=1,
        grid=(nt, bsz),
        in_specs=y_specs + [pl.BlockSpec((None, tm, LANES), lambda ti, b, inv: (b * nt + ti, 0, 0)),
                            pl.BlockSpec((None, tm, d), lambda ti, b, inv: (b, ti, 0)),
                            _layer_spec(mod, layer), full(fg)],
        out_specs=pl.BlockSpec((None, tm, d), lambda ti, b, inv: (b, ti, 0)),
    )
    kern = functools.partial(_combine_kernel, n_lb=n_lb, n_ctx_tiles=n_ctx // tm, ctx_row=bsz, d=d, tm=tm,
                             s_loc=s_loc, final=final)
    return pl.pallas_call(
        kern,
        grid_spec=grid_spec,
        out_shape=jax.ShapeDtypeStruct((bsz, t, d), F32),
        compiler_params=_cparams("arbitrary", "arbitrary"),
        name="moe_combine",
    )(inv, *([ys_em] * n_lb), meta, h1, mod, fg)


def _rope_tables(length, dim):
    rows = length // GRID_W
    row = jnp.repeat(jnp.arange(rows), GRID_W).astype(F32)
    col = jnp.tile(jnp.arange(GRID_W), rows).astype(F32)
    n_freq = dim // 4
    inv_freq = ROPE_THETA ** (-jnp.arange(n_freq, dtype=F32) / n_freq)
    ang = jnp.concatenate([row[:, None] * inv_freq, col[:, None] * inv_freq], axis=-1)
    return jnp.cos(ang), jnp.sin(ang)


def _table_set(n_ctx, n_lat):
    def lanes(cos, sin, lane0, width, reps, outside_cos):
        c = jnp.concatenate([cos, cos], axis=-1)
        s = jnp.concatenate([-sin, sin], axis=-1)
        grp_c = jnp.full((n_lat, width), outside_cos, F32).at[:, lane0:lane0 + c.shape[1]].set(c)
        grp_s = jnp.zeros((n_lat, width), F32).at[:, lane0:lane0 + s.shape[1]].set(s)
        ctx_c = jnp.full((n_ctx, width), outside_cos, F32).at[:, lane0:lane0 + c.shape[1]].set(1.0)
        ctx_s = jnp.zeros((n_ctx, width), F32)
        return (jnp.tile(jnp.concatenate([ctx_c, grp_c], axis=0), (1, reps)),
                jnp.tile(jnp.concatenate([ctx_s, grp_s], axis=0), (1, reps)))

    dcos, dsin = _rope_tables(n_lat, DA_QK)
    cda, sda = lanes(dcos, dsin, 0, DA_QK, DA_QW // DA_QK, 1.0)
    mcos, msin = _rope_tables(n_lat, MLA_ROPE)
    cq, sm = lanes(mcos, msin, KR_LANE0, LANES, 1, 1.0)
    ck, _ = lanes(mcos, msin, KR_LANE0, LANES, 1, 0.0)
    return cda, sda, cq, ck, sm


def _pack_w_in(w_in):
    depth, d, _ = w_in.shape
    o_ssd = DA_IN
    o_mla = DA_IN + SSD_IN
    z = w_in[..., o_ssd:o_ssd + SSD_INNER]
    xbc = w_in[..., o_ssd + SSD_INNER:o_ssd + SSD_INNER + SSD_CONV_DIM]
    dt = w_in[..., o_ssd + SSD_INNER + SSD_CONV_DIM:o_mla]
    cq = w_in[..., o_mla:o_mla + MLA_Q_RANK]
    ckv = w_in[..., o_mla + MLA_Q_RANK:o_mla + MLA_Q_RANK + MLA_KV_RANK]
    kr = w_in[..., o_mla + MLA_Q_RANK + MLA_KV_RANK:]
    zeros = lambda n: jnp.zeros((depth, d, n), w_in.dtype)
    misc = jnp.concatenate([zeros(KR_LANE0), kr, dt, zeros(LANES - DT_LANE0 - 2 * SSD_HEADS)], axis=-1)
    return jnp.concatenate([w_in[..., :DA_IN], z, xbc, cq, ckv, misc], axis=-1).astype(BF16)


def _pack_w_uq(w_uq):
    depth, r, _ = w_uq.shape
    w = w_uq.reshape(depth, r, MLA_HEADS, MLA_NOPE + MLA_ROPE)
    w = jnp.pad(w, ((0, 0), (0, 0), (0, 0), (0, MLA_HEAD_PAD - MLA_NOPE - MLA_ROPE)))
    return w.reshape(depth, r, MLA_QPAD).astype(BF16)


def _pack_w_ukv(w_ukv):
    depth, r, _ = w_ukv.shape
    w = w_ukv.reshape(depth, r, MLA_HEADS, MLA_NOPE + MLA_V)
    kn = jnp.pad(w[..., :MLA_NOPE], ((0, 0), (0, 0), (0, 0), (0, MLA_HEAD_PAD - MLA_NOPE)))
    return jnp.concatenate([kn.reshape(depth, r, MLA_QPAD), w[..., MLA_NOPE:].reshape(depth, r, MLA_WIDTH)],
                           axis=-1).astype(BF16)


def _dispatch_tables(pcnt, n_lb, n_steps):
    ntt = pcnt.shape[0]
    nb = (pcnt / MOE_BLK).astype(jnp.int32)
    lo = jnp.cumsum(nb, axis=1) - nb
    n_e = jnp.sum(nb, axis=0)
    p_e = (n_e + MOE_STEP_BLKS - 1) // MOE_STEP_BLKS * MOE_STEP_BLKS
    ends = jnp.cumsum(p_e)
    base = (ends - p_e)[None, :] + jnp.cumsum(nb, axis=0) - nb
    lb = jnp.arange(n_lb, dtype=jnp.int32)
    e_of = jnp.sum(lb[None, :, None] >= (lo + nb)[:, None, :], axis=-1)
    used = e_of < N_EXPERTS
    e_c = jnp.minimum(e_of, N_EXPERTS - 1)
    pos = jnp.take_along_axis(base, e_c, axis=1) + lb[None, :] - jnp.take_along_axis(lo, e_c, axis=1)
    inv = jnp.where(used, pos, 0).astype(jnp.int32)
    n_pos = n_steps * MOE_STEP_BLKS
    src = jnp.arange(ntt, dtype=jnp.int32)[:, None] * n_lb + lb[None, :]
    fwd = jnp.zeros((n_pos,), jnp.int32).at[jnp.where(used, pos, n_pos).reshape(-1)].set(src.reshape(-1), mode="drop")
    starts = jnp.arange(n_steps, dtype=jnp.int32) * MOE_STEP_BLKS
    step_e = jnp.minimum(jnp.sum(starts[:, None] >= ends[None, :], axis=1), N_EXPERTS - 1).astype(jnp.int32)
    step_valid = (starts < ends[-1]).astype(jnp.int32)
    return inv.reshape(-1), fwd, step_e, step_valid


def kernel(x, c, ctx, c_ctx, norm_mix_g, norm_ffn_g, w_mod, b_mod, w_in, w_out, da_lambda, da_subln_g, ssd_conv_w, ssd_conv_b, ssd_dt_bias, ssd_a_log, ssd_d, ssd_norm_g, mla_q_norm_g, mla_kv_norm_g, mla_w_uq, mla_w_ukv, router_w, router_bias, exp_w_gate, exp_w_up, exp_w_down, final_norm_g):
    bsz, n_lat, d = x.shape
    n_ctx = ctx.shape[1]
    t = n_ctx + n_lat
    tm = min(256, n_ctx)
    assert n_ctx % tm == 0 and n_lat % tm == 0 and n_ctx % SSD_CHUNK == 0 and n_lat % SSD_CHUNK == 0
    nt = t // tm
    s_loc = 2 * tm + 2 * LANES
    n_lb = s_loc // MOE_BLK
    n_steps = -(-(bsz * nt * n_lb + N_EXPERTS * (MOE_STEP_BLKS - 1)) // MOE_STEP_BLKS)
    nc, nc0 = t // SSD_CHUNK, n_ctx // SSD_CHUNK
    q = SSD_CHUNK

    r_pad = -(-(bsz + 1) // 8) * 8
    c_rows = jnp.concatenate([c, c_ctx[None, :], jnp.zeros((r_pad - bsz - 1, d), F32)], axis=0)
    mod = _modulation(c_rows, w_mod, b_mod)

    w_in_p = _pack_w_in(w_in)
    w_uq_p = _pack_w_uq(mla_w_uq)
    w_ukv_p = _pack_w_ukv(mla_w_ukv)
    w_out16 = w_out.astype(BF16)
    wg16, wu16, wd16 = exp_w_gate.astype(BF16), exp_w_up.astype(BF16), exp_w_down.astype(BF16)
    tabs = _table_set(n_ctx, n_lat)
    rwt = router_w.T
    rb = router_bias.reshape(N_EXPERTS, 1)
    lane_b = lambda v: jnp.broadcast_to(v[..., None], v.shape + (q,))
    dsk_rows = jnp.repeat(ssd_d, SSD_P, axis=-1)

    ng_mix, ng_ffn = norm_mix_g[:, None, :], norm_ffn_g[:, None, :]
    qg, kvg = mla_q_norm_g[:, None, :], mla_kv_norm_g[:, None, :]
    sub_g = jnp.tile(da_subln_g, (1, DA_HEADS))[:, None, :]
    ssd_par = (lane_b(ssd_conv_w), lane_b(ssd_conv_b), lane_b(ssd_dt_bias.reshape(DEPTH, -1)),
               lane_b(ssd_a_log.reshape(DEPTH, -1)), lane_b(dsk_rows), lane_b(ssd_norm_g))
    to_chunks = lambda a: a.reshape(bsz, nc, q, a.shape[-1]).transpose(0, 1, 3, 2)

    h = jnp.concatenate([ctx, x], axis=1)
    for i in range(DEPTH):
        lam_init = 0.8 - 0.6 * math.exp(-0.3 * i)
        qda, kda, vda, z, xbc, dtm, qm, km, vm = _inproj(
            h, mod, ng_mix, w_in_p, tabs, qg, kvg, w_uq_p, w_ukv_p, layer=i, tm=tm, n_ctx=n_ctx)
        o_da = _da_attention(qda, jnp.swapaxes(kda, 1, 2), vda, da_lambda, sub_g, layer=i, lam_init=lam_init,
                             tq=tm, n_ctx=n_ctx)
        q4 = qm.reshape(bsz, t, MLA_HEADS, MLA_HEAD_PAD).transpose(0, 2, 1, 3)
        kt4 = km.reshape(bsz, t, MLA_HEADS, MLA_HEAD_PAD).transpose(0, 2, 3, 1)
        o_mla = _mla_attention(q4, kt4, vm, tq=tm, n_ctx=n_ctx)
        o_ssd_t = _ssd_mixer(to_chunks(z), to_chunks(xbc), to_chunks(dtm[..., DT_LANE0:DT_LANE0 + 2 * SSD_HEADS]),
                             *ssd_par, layer=i, nc0=nc0)
        o_ssd = o_ssd_t.transpose(0, 1, 3, 2).reshape(bsz, t, SSD_INNER)
        h1, xs, meta, cnt = _outproj_dispatch(o_da, o_ssd, o_mla, h, mod, ng_ffn, w_out16, rwt, rb, layer=i, tm=tm,
                                              n_ctx=n_ctx, s_loc=s_loc)
        inv, fwd, step_e, step_valid = _dispatch_tables(cnt[:, :, 0], n_lb, n_steps)
        ys_em = _expert_ffn(xs.reshape(bsz * nt * s_loc, d + MOE_EXTRA), wg16, wu16, wd16, step_e, fwd, step_valid,
                            layer=i, n_steps=n_steps)
        h = _combine(ys_em, inv, meta, h1, mod, final_norm_g[None, :], layer=i, tm=tm, n_ctx=n_ctx, s_loc=s_loc,
                     final=(i == DEPTH - 1))
    return h[:, n_ctx:]
```

```python
import functools
import math

import jax
import jax.numpy as jnp
from jax import lax
from jax.experimental import pallas as pl
from jax.experimental.pallas import tpu as pltpu

F32 = jnp.float32
BF16 = jnp.bfloat16

DEPTH = 4
GRID_W = 64
EPS = 1e-6
ROPE_THETA = 10000.0
DA_HEADS, DA_QK = 4, 32
DA_V = 2 * DA_QK
DA_WIDTH = DA_HEADS * DA_V
DA_QW = DA_HEADS * 2 * DA_QK
DA_IN = 2 * DA_QW + DA_WIDTH
SSD_HEADS, SSD_P, SSD_GROUPS, SSD_STATE, SSD_CHUNK = 8, 64, 2, 64, 128
SSD_INNER = SSD_HEADS * SSD_P
SSD_GN = SSD_GROUPS * SSD_STATE
SSD_CONV_DIM = SSD_INNER + 2 * SSD_GN
SSD_IN = SSD_INNER + SSD_CONV_DIM + 2 * SSD_HEADS
MLA_HEADS, MLA_Q_RANK, MLA_KV_RANK, MLA_NOPE, MLA_ROPE, MLA_V = 4, 256, 128, 64, 32, 64
MLA_WIDTH = MLA_HEADS * MLA_V
MLA_IN = MLA_Q_RANK + MLA_KV_RANK + MLA_ROPE
MLA_SCALE = (MLA_NOPE + MLA_ROPE) ** -0.5
N_EXPERTS, N_GROUPS, D_EXPERT = 16, 4, 512
PER_GROUP = N_EXPERTS // N_GROUPS

LANES = 128
SUBLANES_BF16 = 16
VMEM_LIMIT_BYTES = 56 * 1024 * 1024

MLA_HEAD_PAD = LANES
MLA_QPAD = MLA_HEADS * MLA_HEAD_PAD
KR_LANE0 = MLA_NOPE
DT_LANE0 = MLA_NOPE + MLA_ROPE
IN_COLS = DA_IN + SSD_INNER + SSD_CONV_DIM + MLA_Q_RANK + MLA_KV_RANK + LANES
C_Z = DA_IN
C_XBC = C_Z + SSD_INNER
C_CQ = C_XBC + SSD_CONV_DIM
C_CKV = C_CQ + MLA_Q_RANK
C_MISC = C_CKV + MLA_KV_RANK
LOG2E = math.log2(math.e)
DA_QSCALE = DA_QK ** -0.5 * LOG2E
MLA_QSCALE = MLA_SCALE * LOG2E

KEY_CHUNK = 256
MOE_BLK = SUBLANES_BF16
MOE_STEP_BLKS = 32
MOE_EXTRA = LANES


def _sigmoid(x):
    return 1.0 / (1.0 + jnp.exp(-x))


def _silu(x):
    return x * _sigmoid(x)


def _rms(x, g, axis=-1):
    return x * lax.rsqrt(jnp.mean(x * x, axis=axis, keepdims=True) + EPS) * g


def _ada_norm(hv, g, shift, scale):
    return _rms(hv, g) * (1.0 + scale) + shift


def _rope(x, cos, sin_signed, half):
    w = x.shape[-1]
    lane = lax.broadcasted_iota(jnp.int32, x.shape, x.ndim - 1)
    first = (lane % (2 * half)) < half
    partner = jnp.where(first, pltpu.roll(x, w - half, x.ndim - 1), pltpu.roll(x, half, x.ndim - 1))
    return x * cos + partner * sin_signed


def _dot(a, b):
    return jnp.dot(a, b, preferred_element_type=F32)


def _dot_hi(a, b):
    return jnp.dot(a, b, preferred_element_type=F32, precision=lax.Precision.HIGHEST)


def _dot_split3(x, m01):
    x1 = x.astype(BF16)
    r1 = x - x1.astype(F32)
    x2 = r1.astype(BF16)
    x3 = (r1 - x2.astype(F32)).astype(BF16)
    return _dot(x1, m01) + _dot(x2, m01) + _dot(x3, m01)


def _layer_spec(a, i):
    return pl.BlockSpec((None,) + a.shape[1:], lambda *_: (i,) + (0,) * (a.ndim - 1))


def _cparams(*sem):
    return pltpu.CompilerParams(dimension_semantics=sem, vmem_limit_bytes=VMEM_LIMIT_BYTES)


def _mod_kernel(c_ref, w_ref, b_ref, o_ref):
    o_ref[...] = _dot_hi(_silu(c_ref[...]), w_ref[...]) + b_ref[...]


def _modulation(c_rows, w_mod, b_mod):
    depth, d, n = w_mod.shape
    r = c_rows.shape[0]
    tn = 1536
    return pl.pallas_call(
        _mod_kernel,
        grid=(depth, n // tn),
        in_specs=[pl.BlockSpec((r, d), lambda l, j: (0, 0)),
                  pl.BlockSpec((None, d, tn), lambda l, j: (l, 0, j)),
                  pl.BlockSpec((None, 1, tn), lambda l, j: (l, 0, j))],
        out_specs=pl.BlockSpec((None, r, tn), lambda l, j: (l, 0, j)),
        out_shape=jax.ShapeDtypeStruct((depth, r, n), F32),
        compiler_params=_cparams("arbitrary", "arbitrary"),
        name="modulation",
    )(c_rows, w_mod, b_mod.reshape(depth, 1, n))


def _inproj_kernel(h_ref, mod_ref, ng_ref, w_ref, cda_ref, sda_ref, cq_ref, ck_ref, sm_ref,
                   qg_ref, kvg_ref, wuq_ref, wukv_ref,
                   qda_ref, kdat_ref, vda_ref, zt_ref, xbct_ref, dtt_ref, q4_ref, kt4_ref, vm_ref,
                   *, n_ctx_tiles, ctx_row, d, nb, tm):
    ti, bp = pl.program_id(0), pl.program_id(1)
    cda, sda, sm = cda_ref[...], sda_ref[...], sm_ref[...]
    cos_q = jnp.concatenate([cq_ref[...]] * MLA_HEADS, axis=1)
    sin_q = jnp.concatenate([sm] * MLA_HEADS, axis=1)
    q = SSD_CHUNK
    accs = []
    for bb in range(nb):
        row = jnp.where(ti < n_ctx_tiles, ctx_row, bp * nb + bb)
        shift = mod_ref[pl.ds(row, 1), pl.ds(0, d)]
        scale = mod_ref[pl.ds(row, 1), pl.ds(d, d)]
        u = _ada_norm(h_ref[bb], ng_ref[...], shift, scale).astype(BF16)
        accs.append(_dot(u, w_ref[...]))
    for bb in range(nb):
        acc = accs[bb]
        qda_ref[bb] = (_rope(acc[:, 0:DA_QW], cda, sda, DA_QK // 2) * DA_QSCALE).astype(BF16)
        kdat_ref[bb] = _rope(acc[:, DA_QW:2 * DA_QW], cda, sda, DA_QK // 2).T.astype(BF16)
        vda_ref[bb] = acc[:, 2 * DA_QW:DA_IN].astype(BF16)
        z_t = acc[:, C_Z:C_XBC].T
        xbc_t = acc[:, C_XBC:C_CQ].T
        misc = acc[:, C_MISC:IN_COLS]
        dt_t = misc.T[DT_LANE0:DT_LANE0 + 2 * SSD_HEADS]
        for c in range(tm // q):
            zt_ref[bb, c] = z_t[:, c * q:(c + 1) * q].astype(BF16)
            xbct_ref[bb, c] = xbc_t[:, c * q:(c + 1) * q]
            dtt_ref[bb, c] = dt_t[:, c * q:(c + 1) * q]
        cqn = _rms(acc[:, C_CQ:C_CKV], qg_ref[...]).astype(BF16)
        qm = _rope(_dot(cqn, wuq_ref[...]), cos_q, sin_q, MLA_ROPE // 2) * MLA_QSCALE
        ckvn = _rms(acc[:, C_CKV:C_MISC], kvg_ref[...]).astype(BF16)
        kv = _dot(ckvn, wukv_ref[...])
        kr = _rope(misc, ck_ref[...], sm, MLA_ROPE // 2)
        km_t = (kv[:, :MLA_QPAD] + jnp.concatenate([kr] * MLA_HEADS, axis=1)).T
        for hh in range(MLA_HEADS):
            q4_ref[bb, hh] = qm[:, hh * MLA_HEAD_PAD:(hh + 1) * MLA_HEAD_PAD].astype(BF16)
            kt4_ref[bb, hh] = km_t[hh * MLA_HEAD_PAD:(hh + 1) * MLA_HEAD_PAD].astype(BF16)
        vm_ref[bb] = kv[:, MLA_QPAD:].astype(BF16)


def _inproj(h, mod, ng, w_in_p, tabs, qg, kvg, wuq_p, wukv_p, *, layer, tm, n_ctx, nb):
    bsz, t, d = h.shape
    nt, q = t // tm, SSD_CHUNK
    cpt = tm // q
    tok = lambda w: pl.BlockSpec((nb, tm, w), lambda ti, b: (b, ti, 0))
    tab = lambda w: pl.BlockSpec((tm, w), lambda ti, b: (ti, 0))
    chunked = lambda f: pl.BlockSpec((nb, cpt, f, q), lambda ti, b: (b, ti, 0, 0))
    lay = lambda a: _layer_spec(a, layer)
    cda, sda, cq, ck, sm = tabs
    sds = jax.ShapeDtypeStruct
    out_specs = [tok(DA_QW), pl.BlockSpec((nb, DA_QW, tm), lambda ti, b: (b, 0, ti)), tok(DA_WIDTH),
                 chunked(SSD_INNER), chunked(SSD_CONV_DIM), chunked(2 * SSD_HEADS),
                 pl.BlockSpec((nb, MLA_HEADS, tm, MLA_HEAD_PAD), lambda ti, b: (b, 0, ti, 0)),
                 pl.BlockSpec((nb, MLA_HEADS, MLA_HEAD_PAD, tm), lambda ti, b: (b, 0, 0, ti)), tok(MLA_WIDTH)]
    out_shape = [sds((bsz, t, DA_QW), BF16), sds((bsz, DA_QW, t), BF16), sds((bsz, t, DA_WIDTH), BF16),
                 sds((bsz, t // q, SSD_INNER, q), BF16), sds((bsz, t // q, SSD_CONV_DIM, q), F32),
                 sds((bsz, t // q, 2 * SSD_HEADS, q), F32),
                 sds((bsz, MLA_HEADS, t, MLA_HEAD_PAD), BF16), sds((bsz, MLA_HEADS, MLA_HEAD_PAD, t), BF16),
                 sds((bsz, t, MLA_WIDTH), BF16)]
    kern = functools.partial(_inproj_kernel, n_ctx_tiles=n_ctx // tm, ctx_row=bsz, d=d, nb=nb, tm=tm)
    return pl.pallas_call(
        kern,
        grid=(nt, bsz // nb),
        in_specs=[tok(d), lay(mod), lay(ng), lay(w_in_p), tab(DA_QW), tab(DA_QW), tab(LANES), tab(LANES),
                  tab(LANES), lay(qg), lay(kvg), lay(wuq_p), lay(wukv_p)],
        out_specs=out_specs,
        out_shape=out_shape,
        compiler_params=_cparams("arbitrary", "arbitrary"),
        name="inproj",
    )(h, mod, ng, w_in_p, cda, sda, cq, ck, sm, qg, kvg, wuq_p, wukv_p)


def _scores_pass(qm, kt_at, nk, s_scr, m_scr):
    kc = KEY_CHUNK if nk % KEY_CHUNK == 0 else LANES
    mrun = None
    for c0 in range(0, nk, kc):
        s_c = _dot(qm, kt_at(c0, c0 + kc))
        s_scr[:, c0:c0 + kc] = s_c
        for l0 in range(0, kc, LANES):
            part = s_c[:, l0:l0 + LANES]
            mrun = part if mrun is None else jnp.maximum(mrun, part)
    m_scr[...] = mrun


def _pv_pass(v_ref, nk, s_scr, m_scr):
    kc = KEY_CHUNK if nk % KEY_CHUNK == 0 else LANES
    m = jnp.max(m_scr[...], axis=-1, keepdims=True)
    lrun = pv = None
    for c0 in range(0, nk, kc):
        e = jnp.exp2(s_scr[:, c0:c0 + kc] - m)
        for l0 in range(0, kc, LANES):
            part = e[:, l0:l0 + LANES]
            lrun = part if lrun is None else lrun + part
        inc = _dot(e.astype(BF16), v_ref[c0:c0 + kc, :])
        pv = inc if pv is None else pv + inc
    return pv, jnp.sum(lrun, axis=-1, keepdims=True)


def _attend_heads(q_of, kt_of, v_ref, nk, scr, n_sub, combine):
    _scores_pass(q_of(0), kt_of(0), nk, *scr[0])
    out = None
    for j in range(n_sub):
        if j + 1 < n_sub:
            _scores_pass(q_of(j + 1), kt_of(j + 1), nk, *scr[(j + 1) % 2])
        pv, l = _pv_pass(v_ref, nk, *scr[j % 2])
        out = combine(j, pv, l, out)
    return out


def _da_attn_kernel(lam_ref, g_ref, q_ref, kt_ref, v_ref, o_ref, s0_scr, m0_scr, s1_scr, m1_scr, *, n_ctx,
                    n_ctx_tiles, lam_init, tq):
    lv = lam_ref[...]
    lam = (jnp.exp(jnp.sum(lv[0:1] * lv[1:2], axis=-1, keepdims=True))
           - jnp.exp(jnp.sum(lv[2:3] * lv[3:4], axis=-1, keepdims=True)) + lam_init)
    lane_row = lax.broadcasted_iota(jnp.int32, (1, DA_QW), 1)
    lane = lax.broadcasted_iota(jnp.int32, (tq, DA_WIDTH), 1)
    scr = ((s0_scr, m0_scr), (s1_scr, m1_scr))

    def attend(nk):
        q = q_ref[...]

        def q_of(j):
            return q * jnp.where(lane_row // DA_QK == j, 1.0, 0.0).astype(BF16)

        def combine(j, pv, l, out):
            coef = 1.0 if j % 2 == 0 else -lam
            term = jnp.where(lane // DA_V == j // 2, pv * (coef / l), 0.0)
            return term if out is None else out + term

        o = _attend_heads(q_of, lambda j: (lambda c0, c1: kt_ref[:, c0:c1]), v_ref, nk, scr, 2 * DA_HEADS, combine)
        sq = o * o
        inv = jnp.zeros_like(o)
        for hh in range(DA_HEADS):
            hm = lane // DA_V == hh
            ms = jnp.sum(jnp.where(hm, sq, 0.0), axis=-1, keepdims=True) * (1.0 / DA_V)
            inv = jnp.where(hm, lax.rsqrt(ms + EPS), inv)
        o_ref[...] = (o * inv * g_ref[...] * (1.0 - lam_init)).astype(BF16)

    qi = pl.program_id(1)

    @pl.when(qi < n_ctx_tiles)
    def _():
        attend(n_ctx)

    @pl.when(qi >= n_ctx_tiles)
    def _():
        attend(kt_ref.shape[1])


def _da_attention(q, kt, v, lam_vec, g_tiled, *, layer, lam_init, tq, n_ctx):
    bsz, t, _ = q.shape
    kern = functools.partial(_da_attn_kernel, n_ctx=n_ctx, n_ctx_tiles=n_ctx // tq, lam_init=lam_init, tq=tq)
    return pl.pallas_call(
        kern,
        grid=(bsz, t // tq),
        in_specs=[_layer_spec(lam_vec, layer), _layer_spec(g_tiled, layer),
                  pl.BlockSpec((None, tq, DA_QW), lambda b, i: (b, i, 0)),
                  pl.BlockSpec((None, DA_QW, t), lambda b, i: (b, 0, 0)),
                  pl.BlockSpec((None, t, DA_WIDTH), lambda b, i: (b, 0, 0))],
        out_specs=pl.BlockSpec((None, tq, DA_WIDTH), lambda b, i: (b, i, 0)),
        out_shape=jax.ShapeDtypeStruct((bsz, t, DA_WIDTH), BF16),
        scratch_shapes=[pltpu.VMEM((tq, t), F32), pltpu.VMEM((tq, LANES), F32)] * 2,
        compiler_params=_cparams("arbitrary", "arbitrary"),
        name="da_attention",
    )(lam_vec, g_tiled, q, kt, v)


def _mla_attn_kernel(q_ref, kt_ref, v_ref, o_ref, s0_scr, m0_scr, s1_scr, m1_scr, *, n_ctx, n_ctx_tiles, tq):
    lane = lax.broadcasted_iota(jnp.int32, (tq, MLA_WIDTH), 1)
    scr = ((s0_scr, m0_scr), (s1_scr, m1_scr))

    def attend(nk):
        def combine(hh, pv, l, out):
            term = jnp.where(lane // MLA_V == hh, pv * (1.0 / l), 0.0)
            return term if out is None else out + term

        o = _attend_heads(lambda hh: q_ref[hh], lambda hh: (lambda c0, c1: kt_ref[hh, :, c0:c1]), v_ref, nk, scr,
                          MLA_HEADS, combine)
        o_ref[...] = o.astype(BF16)

    qi = pl.program_id(1)

    @pl.when(qi < n_ctx_tiles)
    def _():
        attend(n_ctx)

    @pl.when(qi >= n_ctx_tiles)
    def _():
        attend(kt_ref.shape[2])


def _mla_attention(q4, kt4, v, *, tq, n_ctx):
    bsz, _, t, _ = q4.shape
    kern = functools.partial(_mla_attn_kernel, n_ctx=n_ctx, n_ctx_tiles=n_ctx // tq, tq=tq)
    return pl.pallas_call(
        kern,
        grid=(bsz, t // tq),
        in_specs=[pl.BlockSpec((None, MLA_HEADS, tq, MLA_HEAD_PAD), lambda b, i: (b, 0, i, 0)),
                  pl.BlockSpec((None, MLA_HEADS, MLA_HEAD_PAD, t), lambda b, i: (b, 0, 0, 0)),
                  pl.BlockSpec((None, t, MLA_WIDTH), lambda b, i: (b, 0, 0))],
        out_specs=pl.BlockSpec((None, tq, MLA_WIDTH), lambda b, i: (b, i, 0)),
        out_shape=jax.ShapeDtypeStruct((bsz, t, MLA_WIDTH), BF16),
        scratch_shapes=[pltpu.VMEM((tq, t), F32), pltpu.VMEM((tq, LANES), F32)] * 2,
        compiler_params=_cparams("arbitrary", "arbitrary"),
        name="mla_attention",
    )(q4, kt4, v)


def _ssd_kernel(z_ref, xbc_ref, dt_ref, cw_ref, cb_ref, dtb_ref, alog_ref, dsk_ref, ng_ref,
                o_ref, xc_scr, y_scr, hf_scr, hb_scr, *, nc, nc0):
    q = SSD_CHUNK
    hp = SSD_HEADS * SSD_P
    lane_x = lax.broadcasted_iota(jnp.int32, (SSD_CONV_DIM, q), 1)

    def conv_body(c, carry):
        xc = xbc_ref[c]
        keep_prev = jnp.where((c == 0) | (c == nc0), 0.0, 1.0)
        keep_next = jnp.where((c == nc0 - 1) | (c == nc - 1), 0.0, 1.0)
        xp = xbc_ref[jnp.maximum(c - 1, 0)] * keep_prev
        xn = xbc_ref[jnp.minimum(c + 1, nc - 1)] * keep_next
        prev = pltpu.roll(jnp.where(lane_x == q - 1, xp, xc), 1, 1)
        nxt = pltpu.roll(jnp.where(lane_x == 0, xn, xc), q - 1, 1)
        a = _silu(cw_ref[0] * prev + cw_ref[1] * xc + cw_ref[2] * nxt + cb_ref[...])
        xc_scr[c] = a
        y_scr[c] = dsk_ref[...] * a[:hp]
        return carry

    lax.fori_loop(0, nc, conv_body, 0)
    hf_scr[...] = jnp.zeros_like(hf_scr)
    hb_scr[...] = jnp.zeros_like(hb_scr)

    sub = lax.broadcasted_iota(jnp.int32, (q, q), 0)
    lan = lax.broadcasted_iota(jnp.int32, (q, q), 1)
    lane_n = lax.broadcasted_iota(jnp.int32, (1, SSD_GN), 1)
    pad_rows = jnp.zeros((q - SSD_HEADS, q), F32)

    def run_dir(c, d, h_scr):
        fwd = d == 0
        tri = (sub <= lan) if fwd else (sub >= lan)
        tri01 = jnp.where(tri, 1.0, 0.0).astype(BF16)
        xc = xc_scr[c]
        xt, bt, ct = xc[:hp], xc[hp:hp + SSD_GN], xc[hp + SSD_GN:]
        hs = slice(SSD_HEADS * d, SSD_HEADS * (d + 1))
        dtl = dt_ref[c][hs] + dtb_ref[hs]
        dt = jnp.maximum(dtl, 0.0) + jnp.log(1.0 + jnp.exp(-jnp.abs(dtl)))
        dta = dt * (-jnp.exp(alog_ref[hs]))
        cum_pad = _dot_split3(jnp.concatenate([dta, pad_rows], axis=0), tri01)
        cum_row = cum_pad[:SSD_HEADS]
        cum_col = cum_pad.T
        btok = bt.T
        ct16 = ct.astype(BF16)
        hm = h_scr[...]
        gts, yoffs = [], []
        for g in range(SSD_GROUPS):
            gm = lane_n // SSD_STATE == g
            gts.append(_dot(jnp.where(gm, btok, 0.0).astype(BF16), ct16))
            rows = slice(g * hp // SSD_GROUPS, (g + 1) * hp // SSD_GROUPS)
            yoffs.append(_dot(hm[rows].astype(BF16), ct16))
        end = q - 1 if fwd else 0
        ys, xws, decs = [], [], []
        per_g = SSD_HEADS // SSD_GROUPS
        for hh in range(SSD_HEADS):
            g = hh // per_g
            ar = cum_row[hh:hh + 1, :]
            ac = cum_col[:, hh:hh + 1]
            dec = jnp.where(tri, jnp.exp(ar - ac), 0.0)
            sct = (gts[g] * dec).astype(BF16)
            xdt = xt[hh * SSD_P:(hh + 1) * SSD_P] * dt[hh:hh + 1, :]
            ydiag = _dot(xdt.astype(BF16), sct)
            yoff = yoffs[g][(hh % per_g) * SSD_P:(hh % per_g + 1) * SSD_P] * jnp.exp(ar)
            ys.append(ydiag + yoff)
            a_end = ar[:, end:end + 1]
            xws.append(xdt * jnp.exp(a_end - ar))
            decs.append(jnp.exp(a_end))
        y_scr[c] += jnp.concatenate(ys, axis=0)
        btok16 = btok.astype(BF16)
        new_rows = []
        for g in range(SSD_GROUPS):
            gm = lane_n // SSD_STATE == g
            xw = jnp.concatenate(xws[g * per_g:(g + 1) * per_g], axis=0).astype(BF16)
            inc = jnp.where(gm, _dot(xw, btok16), 0.0)
            for k in range(per_g):
                hh = g * per_g + k
                new_rows.append(hm[hh * SSD_P:(hh + 1) * SSD_P] * decs[hh] + inc[k * SSD_P:(k + 1) * SSD_P])
        h_scr[...] = jnp.concatenate(new_rows, axis=0)

    def step(s, carry):
        run_dir(s, 0, hf_scr)
        cb = jnp.where(s < nc0, nc0 - 1 - s, nc - 1 - (s - nc0))
        run_dir(cb, 1, hb_scr)
        return carry

    lax.fori_loop(0, nc, step, 0)

    def fin_body(c, carry):
        gated = y_scr[c] * _silu(z_ref[c].astype(F32))
        o_ref[c] = _rms(gated, ng_ref[...], axis=0).astype(BF16)
        return carry

    lax.fori_loop(0, nc, fin_body, 0)


def _ssd_mixer(z_t, xbc_t, dt_t, cw, cb, dtb, alog, dsk, ng, *, layer, nc0):
    bsz, nc, _, q = xbc_t.shape
    hp = SSD_HEADS * SSD_P
    per_b = lambda f: pl.BlockSpec((None, nc, f, q), lambda b: (b, 0, 0, 0))
    full = lambda a: _layer_spec(a, layer)
    kern = functools.partial(_ssd_kernel, nc=nc, nc0=nc0)
    return pl.pallas_call(
        kern,
        grid=(bsz,),
        in_specs=[per_b(hp), per_b(SSD_CONV_DIM), per_b(2 * SSD_HEADS),
                  full(cw), full(cb), full(dtb), full(alog), full(dsk), full(ng)],
        out_specs=per_b(hp),
        out_shape=jax.ShapeDtypeStruct((bsz, nc, hp, q), BF16),
        scratch_shapes=[pltpu.VMEM((nc, SSD_CONV_DIM, q), F32), pltpu.VMEM((nc, hp, q), F32),
                        pltpu.VMEM((hp, SSD_GN), F32), pltpu.VMEM((hp, SSD_GN), F32)],
        compiler_params=_cparams("arbitrary"),
        name="ssd_mixer",
    )(z_t, xbc_t, dt_t, cw, cb, dtb, alog, dsk, ng)


def _route(logits_t, bias_col):
    aff = _sigmoid(logits_t)
    sel = aff + bias_col
    rows = [sel[e:e + 1, :] for e in range(N_EXPERTS)]
    gscore = []
    for g in range(N_GROUPS):
        a, b, c, d = rows[PER_GROUP * g:PER_GROUP * (g + 1)]
        hi1, lo1, hi2, lo2 = jnp.maximum(a, b), jnp.minimum(a, b), jnp.maximum(c, d), jnp.minimum(c, d)
        gscore.append(jnp.maximum(hi1, hi2) + jnp.maximum(jnp.minimum(hi1, hi2), jnp.maximum(lo1, lo2)))
    best = jnp.zeros_like(gscore[0], dtype=jnp.int32)
    cur = gscore[0]
    for g in range(1, N_GROUPS):
        better = gscore[g] > cur
        best = jnp.where(better, g, best)
        cur = jnp.where(better, gscore[g], cur)
    eidx = lax.broadcasted_iota(jnp.int32, sel.shape, 0)
    masked = jnp.where(eidx // PER_GROUP == best, sel, -jnp.inf)
    m1 = jnp.max(masked, axis=0, keepdims=True)
    idx1 = jnp.min(jnp.where(masked == m1, eidx, N_EXPERTS), axis=0, keepdims=True)
    masked2 = jnp.where(eidx == idx1, -jnp.inf, masked)
    m2 = jnp.max(masked2, axis=0, keepdims=True)
    idx2 = jnp.min(jnp.where(masked2 == m2, eidx, N_EXPERTS), axis=0, keepdims=True)
    oh1, oh2 = eidx == idx1, eidx == idx2
    w1 = jnp.sum(jnp.where(oh1, aff, 0.0), axis=0, keepdims=True)
    w2 = jnp.sum(jnp.where(oh2, aff, 0.0), axis=0, keepdims=True)
    den = w1 + w2
    return oh1, oh2, w1 / den, w2 / den


def _split_hi_lo(x):
    hi = x.astype(BF16)
    return hi, (x - hi.astype(F32)).astype(BF16)


def _outproj_kernel(oda_ref, ossdt_ref, omla_ref, h_ref, mod_ref, ng_ref, wo_ref, rw_ref, rb_ref,
                    h1_ref, xs_ref, meta_ref, cnt_ref, *, n_ctx_tiles, ctx_row, d, tm, s_loc, nb):
    ti, bp = pl.program_id(0), pl.program_id(1)
    tiles = range(nb)
    r_i = lax.broadcasted_iota(jnp.int32, (tm, tm), 0)
    c_i = lax.broadcasted_iota(jnp.int32, (tm, tm), 1)
    before = jnp.where(r_i < c_i, 1.0, 0.0).astype(BF16)
    row_e = lax.broadcasted_iota(jnp.int32, (N_EXPERTS, LANES), 0)
    r_s = lax.broadcasted_iota(jnp.int32, (s_loc, tm), 0).astype(F32)
    lane_e = lax.broadcasted_iota(jnp.int32, (s_loc, MOE_EXTRA), 1)
    row_m = lax.broadcasted_iota(jnp.int32, (LANES, tm), 0)
    rows = [jnp.where(ti < n_ctx_tiles, ctx_row, bp * nb + bb) for bb in tiles]
    mod_at = lambda bb, k: mod_ref[pl.ds(rows[bb], 1), pl.ds(k * d, d)]
    ossd = [jnp.concatenate([ossdt_ref[bb, c].astype(F32).T for c in range(tm // SSD_CHUNK)], axis=0).astype(BF16)
            for bb in tiles]
    mix = [_dot(oda_ref[bb], wo_ref[0:DA_WIDTH]) + _dot(ossd[bb], wo_ref[DA_WIDTH:DA_WIDTH + SSD_INNER])
           + _dot(omla_ref[bb], wo_ref[DA_WIDTH + SSD_INNER:]) for bb in tiles]
    h1 = [h_ref[bb] + mod_at(bb, 2) * mix[bb] for bb in tiles]
    for bb in tiles:
        h1_ref[bb] = h1[bb]
    u = [_ada_norm(h1[bb], ng_ref[...], mod_at(bb, 3), mod_at(bb, 4)) for bb in tiles]
    u16 = [x.astype(BF16) for x in u]
    rw_hi, rw_lo = _split_hi_lo(rw_ref[...])
    u_lo = [(u[bb] - u16[bb].astype(F32)).astype(BF16) for bb in tiles]
    logits = [_dot(u16[bb], rw_hi) + _dot(u_lo[bb], rw_hi) + _dot(u16[bb], rw_lo) for bb in tiles]
    routed = [_route(logits[bb].T[:N_EXPERTS], rb_ref[...]) for bb in tiles]
    cnt = [jnp.where(r[0], 1.0, 0.0) + jnp.where(r[1], 1.0, 0.0) for r in routed]
    rank = [_dot(cnt[bb].astype(BF16), before) for bb in tiles]
    for bb in tiles:
        oh1, oh2, w1, w2 = routed[bb]
        tot = jnp.sum(cnt[bb], axis=1, keepdims=True)
        ptot = jnp.floor((tot + (MOE_BLK - 1)) * (1.0 / MOE_BLK)) * MOE_BLK
        ptot_b = jnp.broadcast_to(ptot, (N_EXPERTS, LANES))
        cnt_ref[bb] = ptot_b
        run = jnp.zeros((1, LANES), F32)
        off = jnp.zeros((N_EXPERTS, LANES), F32)
        for e in range(1, N_EXPERTS):
            run = run + ptot_b[e - 1:e]
            off = jnp.where(row_e == e, run, off)
        slot = off[:, 0:1] + rank[bb]
        dest1 = jnp.sum(jnp.where(oh1, slot, 0.0), axis=0, keepdims=True)
        dest2 = jnp.sum(jnp.where(oh2, slot, 0.0), axis=0, keepdims=True)
        routed[bb] = (r_s == dest1, r_s == dest2, w1, w2)
        meta_ref[bb] = jnp.where(row_m == 0, dest1, jnp.where(row_m == 1, dest2, 0.0)).T
    perm = [jnp.where(routed[bb][0] | routed[bb][1], 1.0, 0.0).astype(BF16) for bb in tiles]
    xs = [_dot(perm[bb], u16[bb]) for bb in tiles]
    for bb in tiles:
        p1, p2, w1, w2 = routed[bb]
        xs_ref[bb, :, 0:d] = xs[bb].astype(BF16)
        wslot = jnp.sum(jnp.where(p1, w1, 0.0) + jnp.where(p2, w2, 0.0), axis=1, keepdims=True)
        w_hi = wslot.astype(BF16).astype(F32)
        xs_ref[bb, :, d:] = jnp.where(lane_e == 0, w_hi, jnp.where(lane_e == 1, wslot - w_hi, 0.0)).astype(BF16)


def _outproj_dispatch(oda, ossd_t, omla, h, mod, ng, wo, rw, rb, *, layer, tm, n_ctx, s_loc, nb):
    bsz, t, d = h.shape
    nt, q = t // tm, SSD_CHUNK
    tok = lambda w: pl.BlockSpec((nb, tm, w), lambda ti, b: (b, ti, 0))
    full = lambda a: pl.BlockSpec(a.shape, lambda ti, b: (0,) * a.ndim)
    tile = lambda r, w: pl.BlockSpec((nb, None, r, w), lambda ti, b: (b, ti, 0, 0))
    lay = lambda a: _layer_spec(a, layer)
    kern = functools.partial(_outproj_kernel, n_ctx_tiles=n_ctx // tm, ctx_row=bsz, d=d, tm=tm, s_loc=s_loc, nb=nb)
    return pl.pallas_call(
        kern,
        grid=(nt, bsz // nb),
        in_specs=[tok(DA_WIDTH), pl.BlockSpec((nb, tm // q, SSD_INNER, q), lambda ti, b: (b, ti, 0, 0)),
                  tok(MLA_WIDTH), tok(d), lay(mod), lay(ng), lay(wo), full(rw), full(rb)],
        out_specs=[tok(d), tile(s_loc, d + MOE_EXTRA), tile(tm, LANES), tile(N_EXPERTS, LANES)],
        out_shape=[jax.ShapeDtypeStruct((bsz, t, d), F32),
                   jax.ShapeDtypeStruct((bsz, nt, s_loc, d + MOE_EXTRA), BF16),
                   jax.ShapeDtypeStruct((bsz, nt, tm, LANES), F32),
                   jax.ShapeDtypeStruct((bsz, nt, N_EXPERTS, LANES), F32)],
        compiler_params=_cparams("arbitrary", "arbitrary"),
        name="outproj_dispatch",
    )(oda, ossd_t, omla, h, mod, ng, wo, rw, rb)


def _expert_kernel(se_ref, bi_ref, sv_ref, sn_ref, *refs, d):
    x_refs = refs[:MOE_STEP_BLKS]
    wg_ref, wu_ref, wd_ref, y_ref, wg16, wu16, wd16 = refs[MOE_STEP_BLKS:]
    s = pl.program_id(0)

    @pl.when(sn_ref[s] > 0)
    def _():
        wg16[...] = wg_ref[...].astype(BF16)
        wu16[...] = wu_ref[...].astype(BF16)
        wd16[...] = wd_ref[...].astype(BF16)

    @pl.when(sv_ref[s] > 0)
    def _():
        x = jnp.concatenate([r[...] for r in x_refs], axis=0)
        xm = x[:, :d]
        wrow = x[:, d:d + 1].astype(F32) + x[:, d + 1:d + 2].astype(F32)
        he = _silu(_dot(xm, wg16[...])) * _dot(xm, wu16[...])
        y_ref[...] = (_dot(he.astype(BF16), wd16[...]) * wrow).astype(BF16)

    @pl.when(sv_ref[s] == 0)
    def _():
        y_ref[...] = jnp.zeros_like(y_ref)


def _expert_ffn(xs2d, wg, wu, wd, step_e, blk_ids, step_valid, step_new, *, layer, n_steps):
    d = wg.shape[2]
    rows = MOE_STEP_BLKS * MOE_BLK
    x_specs = [pl.BlockSpec((MOE_BLK, d + MOE_EXTRA),
                            lambda s, se, bi, sv, sn, j=j: (bi[s * MOE_STEP_BLKS + j], 0))
               for j in range(MOE_STEP_BLKS)]
    wspec = lambda a: pl.BlockSpec((None, None) + a.shape[2:], lambda s, se, bi, sv, sn: (layer, se[s], 0, 0))
    grid_spec = pltpu.PrefetchScalarGridSpec(
        num_scalar_prefetch=4,
        grid=(n_steps,),
        in_specs=x_specs + [wspec(wg), wspec(wu), wspec(wd)],
        out_specs=pl.BlockSpec((rows, d), lambda s, se, bi, sv, sn: (s, 0)),
        scratch_shapes=[pltpu.VMEM(wg.shape[2:], BF16), pltpu.VMEM(wu.shape[2:], BF16),
                        pltpu.VMEM(wd.shape[2:], BF16)],
    )
    return pl.pallas_call(
        functools.partial(_expert_kernel, d=d),
        grid_spec=grid_spec,
        out_shape=jax.ShapeDtypeStruct((n_steps * rows, d), BF16),
        compiler_params=_cparams("arbitrary"),
        name="expert_ffn",
    )(step_e, blk_ids, step_valid, step_new, *([xs2d] * MOE_STEP_BLKS), wg, wu, wd)


def _combine_kernel(inv_ref, *refs, n_lb, n_ctx_tiles, ctx_row, d, tm, s_loc, final, nb):
    y_refs = refs[:nb * n_lb]
    meta_ref, h1_ref, mod_ref, fg_ref, o_ref = refs[nb * n_lb:]
    ti, bp = pl.program_id(0), pl.program_id(1)
    tiles = range(nb)
    lane_s = lax.broadcasted_iota(jnp.int32, (tm, s_loc), 1).astype(F32)
    metas = [meta_ref[bb] for bb in tiles]
    pts = [jnp.where((lane_s == m[:, 0:1]) | (lane_s == m[:, 1:2]), 1.0, 0.0).astype(BF16) for m in metas]
    ys = [jnp.concatenate([r[...] for r in y_refs[bb * n_lb:(bb + 1) * n_lb]], axis=0) for bb in tiles]
    y = [_dot(pts[bb], ys[bb]) for bb in tiles]
    for bb in tiles:
        row = jnp.where(ti < n_ctx_tiles, ctx_row, bp * nb + bb)
        h2 = h1_ref[bb] + mod_ref[pl.ds(row, 1), pl.ds(5 * d, d)] * y[bb]
        o_ref[bb] = _rms(h2, fg_ref[...]) if final else h2


def _combine(ys_em, inv, meta, h1, mod, fg, *, layer, tm, n_ctx, s_loc, final, nb):
    bsz, t, d = h1.shape
    nt = t // tm
    n_lb = s_loc // MOE_BLK
    y_specs = [pl.BlockSpec((MOE_BLK, d), lambda ti, b, inv, bb=bb, j=j: (inv[((b * nb + bb) * nt + ti) * n_lb + j], 0))
               for bb in range(nb) for j in range(n_lb)]
    full = lambda a: pl.BlockSpec(a.shape, lambda ti, b, inv: (0,) * a.ndim)
    grid_spec = pltpu.PrefetchScalarGridSpec(
        num_scalar_prefetch=1,
        grid=(nt, bsz // nb),
        in_specs=y_specs + [pl.BlockSpec((nb, None, tm, LANES), lambda ti, b, inv: (b, ti, 0, 0)),
                            pl.BlockSpec((nb, tm, d), lambda ti, b, inv: (b, ti, 0)),
                            _layer_spec(mod, layer), full(fg)],
        out_specs=pl.BlockSpec((nb, tm, d), lambda ti, b, inv: (b, ti, 0)),
    )
    kern = functools.partial(_combine_kernel, n_lb=n_lb, n_ctx_tiles=n_ctx // tm, ctx_row=bsz, d=d, tm=tm,
                             s_loc=s_loc, final=final, nb=nb)
    return pl.pallas_call(
        kern,
        grid_spec=grid_spec,
        out_shape=jax.ShapeDtypeStruct((bsz, t, d), F32),
        compiler_params=_cparams("arbitrary", "arbitrary"),
        name="moe_combine",
    )(inv, *([ys_em] * (nb * n_lb)), meta, h1, mod, fg)


def _rope_tables(length, dim):
    rows = length // GRID_W
    row = jnp.repeat(jnp.arange(rows), GRID_W).astype(F32)
    col = jnp.tile(jnp.arange(GRID_W), rows).astype(F32)
    n_freq = dim // 4
    inv_freq = ROPE_THETA ** (-jnp.arange(n_freq, dtype=F32) / n_freq)
    ang = jnp.concatenate([row[:, None] * inv_freq, col[:, None] * inv_freq], axis=-1)
    return jnp.cos(ang), jnp.sin(ang)


def _table_set(n_ctx, n_lat):
    def lanes(cos, sin, lane0, width, reps, outside_cos):
        c = jnp.concatenate([cos, cos], axis=-1)
        s = jnp.concatenate([-sin, sin], axis=-1)
        grp_c = jnp.full((n_lat, width), outside_cos, F32).at[:, lane0:lane0 + c.shape[1]].set(c)
        grp_s = jnp.zeros((n_lat, width), F32).at[:, lane0:lane0 + s.shape[1]].set(s)
        ctx_c = jnp.full((n_ctx, width), outside_cos, F32).at[:, lane0:lane0 + c.shape[1]].set(1.0)
        ctx_s = jnp.zeros((n_ctx, width), F32)
        return (jnp.tile(jnp.concatenate([ctx_c, grp_c], axis=0), (1, reps)),
                jnp.tile(jnp.concatenate([ctx_s, grp_s], axis=0), (1, reps)))

    dcos, dsin = _rope_tables(n_lat, DA_QK)
    cda, sda = lanes(dcos, dsin, 0, DA_QK, DA_QW // DA_QK, 1.0)
    mcos, msin = _rope_tables(n_lat, MLA_ROPE)
    cq, sm = lanes(mcos, msin, KR_LANE0, LANES, 1, 1.0)
    ck, _ = lanes(mcos, msin, KR_LANE0, LANES, 1, 0.0)
    return cda, sda, cq, ck, sm


def _pack_w_in(w_in):
    w_in = w_in.astype(BF16)
    depth, d, _ = w_in.shape
    o_ssd = DA_IN
    o_mla = DA_IN + SSD_IN
    z = w_in[..., o_ssd:o_ssd + SSD_INNER]
    xbc = w_in[..., o_ssd + SSD_INNER:o_ssd + SSD_INNER + SSD_CONV_DIM]
    dt = w_in[..., o_ssd + SSD_INNER + SSD_CONV_DIM:o_mla]
    cq = w_in[..., o_mla:o_mla + MLA_Q_RANK]
    ckv = w_in[..., o_mla + MLA_Q_RANK:o_mla + MLA_Q_RANK + MLA_KV_RANK]
    kr = w_in[..., o_mla + MLA_Q_RANK + MLA_KV_RANK:]
    zeros = lambda n: jnp.zeros((depth, d, n), w_in.dtype)
    misc = jnp.concatenate([zeros(KR_LANE0), kr, dt, zeros(LANES - DT_LANE0 - 2 * SSD_HEADS)], axis=-1)
    return jnp.concatenate([w_in[..., :DA_IN], z, xbc, cq, ckv, misc], axis=-1)


def _pack_w_uq(w_uq):
    depth, r, _ = w_uq.shape
    w = w_uq.reshape(depth, r, MLA_HEADS, MLA_NOPE + MLA_ROPE)
    w = jnp.pad(w, ((0, 0), (0, 0), (0, 0), (0, MLA_HEAD_PAD - MLA_NOPE - MLA_ROPE)))
    return w.reshape(depth, r, MLA_QPAD).astype(BF16)


def _pack_w_ukv(w_ukv):
    depth, r, _ = w_ukv.shape
    w = w_ukv.reshape(depth, r, MLA_HEADS, MLA_NOPE + MLA_V)
    kn = jnp.pad(w[..., :MLA_NOPE], ((0, 0), (0, 0), (0, 0), (0, MLA_HEAD_PAD - MLA_NOPE)))
    return jnp.concatenate([kn.reshape(depth, r, MLA_QPAD), w[..., MLA_NOPE:].reshape(depth, r, MLA_WIDTH)],
                           axis=-1).astype(BF16)


def _dispatch_tables(pcnt, n_lb, n_steps):
    ntt = pcnt.shape[0]
    nb = (pcnt / MOE_BLK).astype(jnp.int32)
    lo = jnp.cumsum(nb, axis=1) - nb
    n_e = jnp.sum(nb, axis=0)
    p_e = (n_e + MOE_STEP_BLKS - 1) // MOE_STEP_BLKS * MOE_STEP_BLKS
    ends = jnp.cumsum(p_e)
    base = (ends - p_e)[None, :] + jnp.cumsum(nb, axis=0) - nb
    lb = jnp.arange(n_lb, dtype=jnp.int32)
    owner = (lb[None, :, None] >= lo[:, None, :]) & (lb[None, :, None] < (lo + nb)[:, None, :])
    used = jnp.any(owner, axis=-1)
    pos = jnp.sum(jnp.where(owner, (base - lo)[:, None, :], 0), axis=-1) + lb[None, :]
    inv = jnp.where(used, pos, 0).astype(jnp.int32)
    n_pos = n_steps * MOE_STEP_BLKS
    src = jnp.arange(ntt, dtype=jnp.int32)[:, None] * n_lb + lb[None, :]
    fwd = jnp.zeros((n_pos,), jnp.int32).at[jnp.where(used, pos, n_pos).reshape(-1)].set(src.reshape(-1), mode="drop")
    starts = jnp.arange(n_steps, dtype=jnp.int32) * MOE_STEP_BLKS
    step_e = jnp.minimum(jnp.sum(starts[:, None] >= ends[None, :], axis=1), N_EXPERTS - 1).astype(jnp.int32)
    step_valid = (starts < ends[-1]).astype(jnp.int32)
    step_new = jnp.concatenate([jnp.ones((1,), jnp.int32), (step_e[1:] != step_e[:-1]).astype(jnp.int32)])
    return inv.reshape(-1), fwd, step_e, step_valid, step_new


def kernel(x, c, ctx, c_ctx, norm_mix_g, norm_ffn_g, w_mod, b_mod, w_in, w_out, da_lambda, da_subln_g, ssd_conv_w, ssd_conv_b, ssd_dt_bias, ssd_a_log, ssd_d, ssd_norm_g, mla_q_norm_g, mla_kv_norm_g, mla_w_uq, mla_w_ukv, router_w, router_bias, exp_w_gate, exp_w_up, exp_w_down, final_norm_g):
    bsz, n_lat, d = x.shape
    n_ctx = ctx.shape[1]
    t = n_ctx + n_lat
    tm = min(256, n_ctx)
    assert n_ctx % tm == 0 and n_lat % tm == 0 and n_ctx % SSD_CHUNK == 0 and n_lat % SSD_CHUNK == 0
    nt = t // tm
    s_loc = 2 * tm + 2 * LANES
    n_lb = s_loc // MOE_BLK
    n_steps = -(-(bsz * nt * n_lb + N_EXPERTS * (MOE_STEP_BLKS - 1)) // MOE_STEP_BLKS)
    nc, nc0 = t // SSD_CHUNK, n_ctx // SSD_CHUNK
    nb = 2 if bsz % 2 == 0 else 1
    q = SSD_CHUNK

    r_pad = -(-(bsz + 1) // 8) * 8
    c_rows = jnp.concatenate([c, c_ctx[None, :], jnp.zeros((r_pad - bsz - 1, d), F32)], axis=0)
    mod = _modulation(c_rows, w_mod, b_mod)

    w_in_p = _pack_w_in(w_in)
    w_uq_p = _pack_w_uq(mla_w_uq)
    w_ukv_p = _pack_w_ukv(mla_w_ukv)
    w_out16 = w_out.astype(BF16)
    tabs = _table_set(n_ctx, n_lat)
    rw_pad = jnp.pad(router_w, ((0, 0), (0, LANES - N_EXPERTS)))
    rb = router_bias.reshape(N_EXPERTS, 1)
    lane_b = lambda v: jnp.broadcast_to(v[..., None], v.shape + (q,))
    dsk_rows = jnp.repeat(ssd_d, SSD_P, axis=-1)
    ng_mix, ng_ffn = norm_mix_g[:, None, :], norm_ffn_g[:, None, :]
    qg, kvg = mla_q_norm_g[:, None, :], mla_kv_norm_g[:, None, :]
    sub_g = jnp.tile(da_subln_g, (1, DA_HEADS))[:, None, :]
    ssd_par = (lane_b(ssd_conv_w), lane_b(ssd_conv_b), lane_b(ssd_dt_bias.reshape(DEPTH, -1)),
               lane_b(ssd_a_log.reshape(DEPTH, -1)), lane_b(dsk_rows), lane_b(ssd_norm_g))

    h = jnp.concatenate([ctx, x], axis=1)
    for i in range(DEPTH):
        lam_init = 0.8 - 0.6 * math.exp(-0.3 * i)
        qda, kda_t, vda, z_t, xbc_t, dt_t, q4, kt4, vm = _inproj(
            h, mod, ng_mix, w_in_p, tabs, qg, kvg, w_uq_p, w_ukv_p, layer=i, tm=tm, n_ctx=n_ctx, nb=nb)
        o_da = _da_attention(qda, kda_t, vda, da_lambda, sub_g, layer=i, lam_init=lam_init, tq=tm, n_ctx=n_ctx)
        o_mla = _mla_attention(q4, kt4, vm, tq=tm, n_ctx=n_ctx)
        o_ssd_t = _ssd_mixer(z_t, xbc_t, dt_t, *ssd_par, layer=i, nc0=nc0)
        h1, xs, meta, cnt = _outproj_dispatch(o_da, o_ssd_t, o_mla, h, mod, ng_ffn, w_out16, rw_pad, rb, layer=i, tm=tm,
                                              n_ctx=n_ctx, s_loc=s_loc, nb=nb)
        inv, fwd, step_e, step_valid, step_new = _dispatch_tables(cnt[:, :, :, 0].reshape(bsz * nt, N_EXPERTS), n_lb,
                                                                  n_steps)
        ys_em = _expert_ffn(xs.reshape(bsz * nt * s_loc, d + MOE_EXTRA), exp_w_gate, exp_w_up, exp_w_down, step_e, fwd,
                            step_valid, step_new, layer=i, n_steps=n_steps)
        h = _combine(ys_em, inv, meta, h1, mod, final_norm_g[None, :], layer=i, tm=tm, n_ctx=n_ctx, s_loc=s_loc,
                     final=(i == DEPTH - 1), nb=nb)
    return h[:, n_ctx:]
```

```python
import functools
import math

import jax
import jax.numpy as jnp
from jax import lax
from jax.experimental import pallas as pl
from jax.experimental.pallas import tpu as pltpu

F32 = jnp.float32
BF16 = jnp.bfloat16

DEPTH = 4
GRID_W = 64
EPS = 1e-6
ROPE_THETA = 10000.0
DA_HEADS, DA_QK = 4, 32
DA_V = 2 * DA_QK
DA_WIDTH = DA_HEADS * DA_V
DA_QW = DA_HEADS * 2 * DA_QK
DA_IN = 2 * DA_QW + DA_WIDTH
SSD_HEADS, SSD_P, SSD_GROUPS, SSD_STATE, SSD_CHUNK = 8, 64, 2, 64, 128
SSD_INNER = SSD_HEADS * SSD_P
SSD_GN = SSD_GROUPS * SSD_STATE
SSD_CONV_DIM = SSD_INNER + 2 * SSD_GN
SSD_IN = SSD_INNER + SSD_CONV_DIM + 2 * SSD_HEADS
MLA_HEADS, MLA_Q_RANK, MLA_KV_RANK, MLA_NOPE, MLA_ROPE, MLA_V = 4, 256, 128, 64, 32, 64
MLA_WIDTH = MLA_HEADS * MLA_V
MLA_IN = MLA_Q_RANK + MLA_KV_RANK + MLA_ROPE
MLA_SCALE = (MLA_NOPE + MLA_ROPE) ** -0.5
N_EXPERTS, N_GROUPS, D_EXPERT = 16, 4, 512
PER_GROUP = N_EXPERTS // N_GROUPS

LANES = 128
SUBLANES_BF16 = 16
VMEM_LIMIT_BYTES = 56 * 1024 * 1024

MLA_HEAD_PAD = LANES
MLA_QPAD = MLA_HEADS * MLA_HEAD_PAD
KR_LANE0 = MLA_NOPE
DT_LANE0 = MLA_NOPE + MLA_ROPE
IN_COLS = DA_IN + SSD_INNER + SSD_CONV_DIM + MLA_Q_RANK + MLA_KV_RANK + LANES
C_Z = DA_IN
C_XBC = C_Z + SSD_INNER
C_CQ = C_XBC + SSD_CONV_DIM
C_CKV = C_CQ + MLA_Q_RANK
C_MISC = C_CKV + MLA_KV_RANK
LOG2E = math.log2(math.e)
DA_QSCALE = DA_QK ** -0.5 * LOG2E
MLA_QSCALE = MLA_SCALE * LOG2E

KEY_CHUNK = 256
MOE_BLK = SUBLANES_BF16
MOE_STEP_BLKS = 32
MOE_EXTRA = LANES


def _sigmoid(x):
    return 1.0 / (1.0 + jnp.exp(-x))


def _silu(x):
    return x * _sigmoid(x)


def _rms(x, g, axis=-1):
    return x * lax.rsqrt(jnp.mean(x * x, axis=axis, keepdims=True) + EPS) * g


def _ada_norm(hv, g, shift, scale):
    return _rms(hv, g) * (1.0 + scale) + shift


def _rope(x, cos, sin_signed, half):
    w = x.shape[-1]
    lane = lax.broadcasted_iota(jnp.int32, x.shape, x.ndim - 1)
    first = (lane % (2 * half)) < half
    partner = jnp.where(first, pltpu.roll(x, w - half, x.ndim - 1), pltpu.roll(x, half, x.ndim - 1))
    return x * cos + partner * sin_signed


def _dot(a, b):
    return jnp.dot(a, b, preferred_element_type=F32)


def _dot_hi(a, b):
    return jnp.dot(a, b, preferred_element_type=F32, precision=lax.Precision.HIGHEST)


def _dot_split3(x, m01):
    x1 = x.astype(BF16)
    r1 = x - x1.astype(F32)
    x2 = r1.astype(BF16)
    x3 = (r1 - x2.astype(F32)).astype(BF16)
    return _dot(x1, m01) + _dot(x2, m01) + _dot(x3, m01)


def _layer_spec(a, i):
    return pl.BlockSpec((None,) + a.shape[1:], lambda *_: (i,) + (0,) * (a.ndim - 1))


def _cparams(*sem):
    return pltpu.CompilerParams(dimension_semantics=sem, vmem_limit_bytes=VMEM_LIMIT_BYTES)


def _mod_kernel(c_ref, w_ref, b_ref, o_ref):
    o_ref[...] = _dot_hi(_silu(c_ref[...]), w_ref[...]) + b_ref[...]


def _modulation(c_rows, w_mod, b_mod):
    depth, d, n = w_mod.shape
    r = c_rows.shape[0]
    tn = 1536
    return pl.pallas_call(
        _mod_kernel,
        grid=(depth, n // tn),
        in_specs=[pl.BlockSpec((r, d), lambda l, j: (0, 0)),
                  pl.BlockSpec((None, d, tn), lambda l, j: (l, 0, j)),
                  pl.BlockSpec((None, 1, tn), lambda l, j: (l, 0, j))],
        out_specs=pl.BlockSpec((None, r, tn), lambda l, j: (l, 0, j)),
        out_shape=jax.ShapeDtypeStruct((depth, r, n), F32),
        compiler_params=_cparams("arbitrary", "arbitrary"),
        name="modulation",
    )(c_rows, w_mod, b_mod.reshape(depth, 1, n))


def _inproj_kernel(h_ref, mod_ref, ng_ref, w_ref, cda_ref, sda_ref, cq_ref, ck_ref, sm_ref,
                   qg_ref, kvg_ref, wuq_ref, wukv_ref,
                   qdat_ref, kda_ref, vdat_ref, zt_ref, xbct_ref, dtt_ref, q4t_ref, k4_ref, vmt_ref,
                   *, n_ctx_tiles, ctx_row, d, nb, tm):
    ti, bp = pl.program_id(0), pl.program_id(1)
    cda, sda, sm = cda_ref[...], sda_ref[...], sm_ref[...]
    cos_q = jnp.concatenate([cq_ref[...]] * MLA_HEADS, axis=1)
    sin_q = jnp.concatenate([sm] * MLA_HEADS, axis=1)
    q = SSD_CHUNK
    accs = []
    for bb in range(nb):
        row = jnp.where(ti < n_ctx_tiles, ctx_row, bp * nb + bb)
        shift = mod_ref[pl.ds(row, 1), pl.ds(0, d)]
        scale = mod_ref[pl.ds(row, 1), pl.ds(d, d)]
        u = _ada_norm(h_ref[bb], ng_ref[...], shift, scale).astype(BF16)
        accs.append(_dot(u, w_ref[...]))
    for bb in range(nb):
        acc = accs[bb]
        qdat_ref[bb] = (_rope(acc[:, 0:DA_QW], cda, sda, DA_QK // 2) * DA_QSCALE).T.astype(BF16)
        kda_ref[bb] = _rope(acc[:, DA_QW:2 * DA_QW], cda, sda, DA_QK // 2).astype(BF16)
        vdat_ref[bb] = acc[:, 2 * DA_QW:DA_IN].T.astype(BF16)
        z_t = acc[:, C_Z:C_XBC].T
        xbc_t = acc[:, C_XBC:C_CQ].T
        misc = acc[:, C_MISC:IN_COLS]
        dt_t = misc.T[DT_LANE0:DT_LANE0 + 2 * SSD_HEADS]
        for c in range(tm // q):
            zt_ref[bb, c] = z_t[:, c * q:(c + 1) * q].astype(BF16)
            xbct_ref[bb, c] = xbc_t[:, c * q:(c + 1) * q]
            dtt_ref[bb, c] = dt_t[:, c * q:(c + 1) * q]
        cqn = _rms(acc[:, C_CQ:C_CKV], qg_ref[...]).astype(BF16)
        qm_t = (_rope(_dot(cqn, wuq_ref[...]), cos_q, sin_q, MLA_ROPE // 2) * MLA_QSCALE).T
        ckvn = _rms(acc[:, C_CKV:C_MISC], kvg_ref[...]).astype(BF16)
        kv = _dot(ckvn, wukv_ref[...])
        kr = _rope(misc, ck_ref[...], sm, MLA_ROPE // 2)
        km = kv[:, :MLA_QPAD] + jnp.concatenate([kr] * MLA_HEADS, axis=1)
        for hh in range(MLA_HEADS):
            q4t_ref[bb, hh] = qm_t[hh * MLA_HEAD_PAD:(hh + 1) * MLA_HEAD_PAD].astype(BF16)
            k4_ref[bb, hh] = km[:, hh * MLA_HEAD_PAD:(hh + 1) * MLA_HEAD_PAD].astype(BF16)
        vmt_ref[bb] = kv[:, MLA_QPAD:].T.astype(BF16)


def _inproj(h, mod, ng, w_in_p, tabs, qg, kvg, wuq_p, wukv_p, *, layer, tm, n_ctx, nb):
    bsz, t, d = h.shape
    nt, q = t // tm, SSD_CHUNK
    cpt = tm // q
    tok = lambda w: pl.BlockSpec((nb, tm, w), lambda ti, b: (b, ti, 0))
    tab = lambda w: pl.BlockSpec((tm, w), lambda ti, b: (ti, 0))
    chunked = lambda f: pl.BlockSpec((nb, cpt, f, q), lambda ti, b: (b, ti, 0, 0))
    lay = lambda a: _layer_spec(a, layer)
    cda, sda, cq, ck, sm = tabs
    sds = jax.ShapeDtypeStruct
    tok_t = lambda w: pl.BlockSpec((nb, w, tm), lambda ti, b: (b, 0, ti))
    out_specs = [tok_t(DA_QW), tok(DA_QW), tok_t(DA_WIDTH),
                 chunked(SSD_INNER), chunked(SSD_CONV_DIM), chunked(2 * SSD_HEADS),
                 pl.BlockSpec((nb, MLA_HEADS, MLA_HEAD_PAD, tm), lambda ti, b: (b, 0, 0, ti)),
                 pl.BlockSpec((nb, MLA_HEADS, tm, MLA_HEAD_PAD), lambda ti, b: (b, 0, ti, 0)), tok_t(MLA_WIDTH)]
    out_shape = [sds((bsz, DA_QW, t), BF16), sds((bsz, t, DA_QW), BF16), sds((bsz, DA_WIDTH, t), BF16),
                 sds((bsz, t // q, SSD_INNER, q), BF16), sds((bsz, t // q, SSD_CONV_DIM, q), F32),
                 sds((bsz, t // q, 2 * SSD_HEADS, q), F32),
                 sds((bsz, MLA_HEADS, MLA_HEAD_PAD, t), BF16), sds((bsz, MLA_HEADS, t, MLA_HEAD_PAD), BF16),
                 sds((bsz, MLA_WIDTH, t), BF16)]
    kern = functools.partial(_inproj_kernel, n_ctx_tiles=n_ctx // tm, ctx_row=bsz, d=d, nb=nb, tm=tm)
    return pl.pallas_call(
        kern,
        grid=(nt, bsz // nb),
        in_specs=[tok(d), lay(mod), lay(ng), lay(w_in_p), tab(DA_QW), tab(DA_QW), tab(LANES), tab(LANES),
                  tab(LANES), lay(qg), lay(kvg), lay(wuq_p), lay(wukv_p)],
        out_specs=out_specs,
        out_shape=out_shape,
        compiler_params=_cparams("arbitrary", "arbitrary"),
        name="inproj",
    )(h, mod, ng, w_in_p, cda, sda, cq, ck, sm, qg, kvg, wuq_p, wukv_p)


V_AUG = DA_V + SUBLANES_BF16


def _scores_pass(k_at, qtm, nk, s_scr):
    kc = KEY_CHUNK if nk % KEY_CHUNK == 0 else LANES
    m = None
    for c0 in range(0, nk, kc):
        s_c = _dot(k_at(c0, c0 + kc), qtm)
        s_scr[c0:c0 + kc, :] = s_c
        part = jnp.max(s_c, axis=0, keepdims=True)
        m = part if m is None else jnp.maximum(m, part)
    return m


def _pv_pass(vaug_at, nk, s_scr, m):
    kc = KEY_CHUNK if nk % KEY_CHUNK == 0 else LANES
    acc = None
    for c0 in range(0, nk, kc):
        e = jnp.exp2(s_scr[c0:c0 + kc, :] - m).astype(BF16)
        inc = _dot(vaug_at(c0, c0 + kc), e)
        acc = inc if acc is None else acc + inc
    return acc


def _attend_heads(k_of, qt_of, vaug_of, nk, scr, n_sub):
    m = _scores_pass(k_of(0), qt_of(0), nk, scr[0])
    outs = []
    for j in range(n_sub):
        if j + 1 < n_sub:
            m_next = _scores_pass(k_of(j + 1), qt_of(j + 1), nk, scr[(j + 1) % 2])
        acc = _pv_pass(vaug_of(j), nk, scr[j % 2], m)
        outs.append(acc[:DA_V] * (1.0 / acc[DA_V:DA_V + 1]))
        if j + 1 < n_sub:
            m = m_next
    return outs


def _fill_vaug(vt_ref, vaug_scr, n_heads):
    t = vt_ref.shape[1]
    for hh in range(n_heads):
        vaug_scr[hh, 0:DA_V, :] = vt_ref[hh * DA_V:(hh + 1) * DA_V, :]
        vaug_scr[hh, DA_V:V_AUG, :] = jnp.ones((V_AUG - DA_V, t), BF16)


def _da_attn_kernel(lam_ref, g_ref, qt_ref, k_ref, vt_ref, o_ref, s0_scr, s1_scr, vaug_scr, *, n_ctx, n_ctx_tiles,
                    lam_init, tq):
    qi = pl.program_id(1)

    @pl.when(qi == 0)
    def _():
        _fill_vaug(vt_ref, vaug_scr, DA_HEADS)

    lv = lam_ref[...]
    lam = (jnp.exp(jnp.sum(lv[0:1] * lv[1:2], axis=-1, keepdims=True))
           - jnp.exp(jnp.sum(lv[2:3] * lv[3:4], axis=-1, keepdims=True)) + lam_init)
    row_q = lax.broadcasted_iota(jnp.int32, (DA_QW, 1), 0)

    def attend(nk):
        qt = qt_ref[...]
        qt_of = lambda j: qt * jnp.where(row_q // DA_QK == j, 1.0, 0.0).astype(BF16)
        k_of = lambda j: (lambda c0, c1: k_ref[c0:c1, :])
        vaug_of = lambda j: (lambda c0, c1: vaug_scr[j // 2, :, c0:c1])
        outs = _attend_heads(k_of, qt_of, vaug_of, nk, (s0_scr, s1_scr), 2 * DA_HEADS)
        heads = []
        for hh in range(DA_HEADS):
            o = outs[2 * hh] - lam * outs[2 * hh + 1]
            heads.append(o * lax.rsqrt(jnp.mean(o * o, axis=0, keepdims=True) + EPS))
        o_t = jnp.concatenate(heads, axis=0) * g_ref[...] * (1.0 - lam_init)
        o_ref[...] = o_t.T.astype(BF16)

    @pl.when(qi < n_ctx_tiles)
    def _():
        attend(n_ctx)

    @pl.when(qi >= n_ctx_tiles)
    def _():
        attend(k_ref.shape[0])


def _da_attention(qt, k, vt, lam_vec, g_col, *, layer, lam_init, tq, n_ctx):
    bsz, _, t = qt.shape
    kern = functools.partial(_da_attn_kernel, n_ctx=n_ctx, n_ctx_tiles=n_ctx // tq, lam_init=lam_init, tq=tq)
    return pl.pallas_call(
        kern,
        grid=(bsz, t // tq),
        in_specs=[_layer_spec(lam_vec, layer), _layer_spec(g_col, layer),
                  pl.BlockSpec((None, DA_QW, tq), lambda b, i: (b, 0, i)),
                  pl.BlockSpec((None, t, DA_QW), lambda b, i: (b, 0, 0)),
                  pl.BlockSpec((None, DA_WIDTH, t), lambda b, i: (b, 0, 0))],
        out_specs=pl.BlockSpec((None, tq, DA_WIDTH), lambda b, i: (b, i, 0)),
        out_shape=jax.ShapeDtypeStruct((bsz, t, DA_WIDTH), BF16),
        scratch_shapes=[pltpu.VMEM((t, tq), F32), pltpu.VMEM((t, tq), F32), pltpu.VMEM((DA_HEADS, V_AUG, t), BF16)],
        compiler_params=_cparams("arbitrary", "arbitrary"),
        name="da_attention",
    )(lam_vec, g_col, qt, k, vt)


def _mla_attn_kernel(qt_ref, k_ref, vt_ref, o_ref, s0_scr, s1_scr, vaug_scr, *, n_ctx, n_ctx_tiles, tq):
    qi = pl.program_id(1)

    @pl.when(qi == 0)
    def _():
        _fill_vaug(vt_ref, vaug_scr, MLA_HEADS)

    def attend(nk):
        k_of = lambda hh: (lambda c0, c1: k_ref[hh, c0:c1, :])
        vaug_of = lambda hh: (lambda c0, c1: vaug_scr[hh, :, c0:c1])
        outs = _attend_heads(k_of, lambda hh: qt_ref[hh], vaug_of, nk, (s0_scr, s1_scr), MLA_HEADS)
        o_ref[...] = jnp.concatenate(outs, axis=0).T.astype(BF16)

    @pl.when(qi < n_ctx_tiles)
    def _():
        attend(n_ctx)

    @pl.when(qi >= n_ctx_tiles)
    def _():
        attend(k_ref.shape[1])


def _mla_attention(q4t, k4, vt, *, tq, n_ctx):
    bsz, _, _, t = q4t.shape
    kern = functools.partial(_mla_attn_kernel, n_ctx=n_ctx, n_ctx_tiles=n_ctx // tq, tq=tq)
    return pl.pallas_call(
        kern,
        grid=(bsz, t // tq),
        in_specs=[pl.BlockSpec((None, MLA_HEADS, MLA_HEAD_PAD, tq), lambda b, i: (b, 0, 0, i)),
                  pl.BlockSpec((None, MLA_HEADS, t, MLA_HEAD_PAD), lambda b, i: (b, 0, 0, 0)),
                  pl.BlockSpec((None, MLA_WIDTH, t), lambda b, i: (b, 0, 0))],
        out_specs=pl.BlockSpec((None, tq, MLA_WIDTH), lambda b, i: (b, i, 0)),
        out_shape=jax.ShapeDtypeStruct((bsz, t, MLA_WIDTH), BF16),
        scratch_shapes=[pltpu.VMEM((t, tq), F32), pltpu.VMEM((t, tq), F32), pltpu.VMEM((MLA_HEADS, V_AUG, t), BF16)],
        compiler_params=_cparams("arbitrary", "arbitrary"),
        name="mla_attention",
    )(q4t, k4, vt)


def _ssd_kernel(z_ref, xbc_ref, dt_ref, cw_ref, cb_ref, dtb_ref, alog_ref, dsk_ref, ng_ref,
                o_ref, xc_scr, y_scr, hf_scr, hb_scr, *, nc, nc0):
    q = SSD_CHUNK
    hp = SSD_HEADS * SSD_P
    lane_x = lax.broadcasted_iota(jnp.int32, (SSD_CONV_DIM, q), 1)

    def conv_body(c, carry):
        xc = xbc_ref[c]
        keep_prev = jnp.where((c == 0) | (c == nc0), 0.0, 1.0)
        keep_next = jnp.where((c == nc0 - 1) | (c == nc - 1), 0.0, 1.0)
        xp = xbc_ref[jnp.maximum(c - 1, 0)] * keep_prev
        xn = xbc_ref[jnp.minimum(c + 1, nc - 1)] * keep_next
        prev = pltpu.roll(jnp.where(lane_x == q - 1, xp, xc), 1, 1)
        nxt = pltpu.roll(jnp.where(lane_x == 0, xn, xc), q - 1, 1)
        a = _silu(cw_ref[0] * prev + cw_ref[1] * xc + cw_ref[2] * nxt + cb_ref[...])
        xc_scr[c] = a
        y_scr[c] = dsk_ref[...] * a[:hp]
        return carry

    lax.fori_loop(0, nc, conv_body, 0)
    hf_scr[...] = jnp.zeros_like(hf_scr)
    hb_scr[...] = jnp.zeros_like(hb_scr)

    sub = lax.broadcasted_iota(jnp.int32, (q, q), 0)
    lan = lax.broadcasted_iota(jnp.int32, (q, q), 1)
    lane_n = lax.broadcasted_iota(jnp.int32, (1, SSD_GN), 1)
    pad_rows = jnp.zeros((q - SSD_HEADS, q), F32)

    per_g = SSD_HEADS // SSD_GROUPS
    rows_g = hp // SSD_GROUPS
    gmask = [lane_n // SSD_STATE == g for g in range(SSD_GROUPS)]
    tris = (sub <= lan, sub >= lan)
    tri01 = [jnp.where(tr, 1.0, 0.0).astype(BF16) for tr in tris]
    ends = (q - 1, 0)
    h_scrs = (hf_scr, hb_scr)
    dirs = (0, 1)

    def step(s, carry):
        cs = (s, jnp.where(s < nc0, nc0 - 1 - s, nc - 1 - (s - nc0)))
        xc = [xc_scr[cs[d]] for d in dirs]
        xt = [x[:hp] for x in xc]
        ct16 = [x[hp + SSD_GN:].astype(BF16) for x in xc]
        dts, cum_pad = [], []
        for d in dirs:
            hs = slice(SSD_HEADS * d, SSD_HEADS * (d + 1))
            dtl = dt_ref[cs[d]][hs] + dtb_ref[hs]
            dt = jnp.maximum(dtl, 0.0) + jnp.log(1.0 + jnp.exp(-jnp.abs(dtl)))
            dts.append(dt)
            dta = dt * (-jnp.exp(alog_ref[hs]))
            cum_pad.append(_dot_split3(jnp.concatenate([dta, pad_rows], axis=0), tri01[d]))
        btok = [x[hp:hp + SSD_GN].T for x in xc]
        hm = [h_scrs[d][...] for d in dirs]
        gts = [[_dot(jnp.where(gmask[g], btok[d], 0.0).astype(BF16), ct16[d]) for g in range(SSD_GROUPS)]
               for d in dirs]
        yoffs = [[_dot(hm[d][g * rows_g:(g + 1) * rows_g].astype(BF16), ct16[d]) for g in range(SSD_GROUPS)]
                 for d in dirs]
        cum_row = [cp[:SSD_HEADS] for cp in cum_pad]
        cum_col = [cp.T for cp in cum_pad]
        ys, xws, decs = ([], []), ([], []), ([], [])
        for hh in range(SSD_HEADS):
            g = hh // per_g
            for d in dirs:
                ar = cum_row[d][hh:hh + 1, :]
                ac = cum_col[d][:, hh:hh + 1]
                sct = (gts[d][g] * jnp.where(tris[d], jnp.exp(ar - ac), 0.0)).astype(BF16)
                xdt = xt[d][hh * SSD_P:(hh + 1) * SSD_P] * dts[d][hh:hh + 1, :]
                ydiag = _dot(xdt.astype(BF16), sct)
                yoff = yoffs[d][g][(hh % per_g) * SSD_P:(hh % per_g + 1) * SSD_P] * jnp.exp(ar)
                ys[d].append(ydiag + yoff)
                a_end = ar[:, ends[d]:ends[d] + 1]
                xws[d].append(xdt * jnp.exp(a_end - ar))
                decs[d].append(jnp.exp(a_end))
        for d in dirs:
            y_scr[cs[d]] += jnp.concatenate(ys[d], axis=0)
        incs = [[jnp.where(gmask[g], _dot(jnp.concatenate(xws[d][g * per_g:(g + 1) * per_g], axis=0).astype(BF16),
                                          btok[d].astype(BF16)), 0.0) for g in range(SSD_GROUPS)] for d in dirs]
        for d in dirs:
            new_rows = [hm[d][hh * SSD_P:(hh + 1) * SSD_P] * decs[d][hh]
                        + incs[d][hh // per_g][(hh % per_g) * SSD_P:(hh % per_g + 1) * SSD_P]
                        for hh in range(SSD_HEADS)]
            h_scrs[d][...] = jnp.concatenate(new_rows, axis=0)
        return carry

    lax.fori_loop(0, nc, step, 0)

    def fin_body(c, carry):
        gated = y_scr[c] * _silu(z_ref[c].astype(F32))
        o_ref[c] = _rms(gated, ng_ref[...], axis=0).astype(BF16)
        return carry

    lax.fori_loop(0, nc, fin_body, 0)


def _ssd_mixer(z_t, xbc_t, dt_t, cw, cb, dtb, alog, dsk, ng, *, layer, nc0):
    bsz, nc, _, q = xbc_t.shape
    hp = SSD_HEADS * SSD_P
    per_b = lambda f: pl.BlockSpec((None, nc, f, q), lambda b: (b, 0, 0, 0))
    full = lambda a: _layer_spec(a, layer)
    kern = functools.partial(_ssd_kernel, nc=nc, nc0=nc0)
    return pl.pallas_call(
        kern,
        grid=(bsz,),
        in_specs=[per_b(hp), per_b(SSD_CONV_DIM), per_b(2 * SSD_HEADS),
                  full(cw), full(cb), full(dtb), full(alog), full(dsk), full(ng)],
        out_specs=per_b(hp),
        out_shape=jax.ShapeDtypeStruct((bsz, nc, hp, q), BF16),
        scratch_shapes=[pltpu.VMEM((nc, SSD_CONV_DIM, q), F32), pltpu.VMEM((nc, hp, q), F32),
                        pltpu.VMEM((hp, SSD_GN), F32), pltpu.VMEM((hp, SSD_GN), F32)],
        compiler_params=_cparams("arbitrary"),
        name="ssd_mixer",
    )(z_t, xbc_t, dt_t, cw, cb, dtb, alog, dsk, ng)


def _route(logits_t, bias_col):
    aff = _sigmoid(logits_t)
    sel = aff + bias_col
    rows = [sel[e:e + 1, :] for e in range(N_EXPERTS)]
    gscore = []
    for g in range(N_GROUPS):
        a, b, c, d = rows[PER_GROUP * g:PER_GROUP * (g + 1)]
        hi1, lo1, hi2, lo2 = jnp.maximum(a, b), jnp.minimum(a, b), jnp.maximum(c, d), jnp.minimum(c, d)
        gscore.append(jnp.maximum(hi1, hi2) + jnp.maximum(jnp.minimum(hi1, hi2), jnp.maximum(lo1, lo2)))
    best = jnp.zeros_like(gscore[0], dtype=jnp.int32)
    cur = gscore[0]
    for g in range(1, N_GROUPS):
        better = gscore[g] > cur
        best = jnp.where(better, g, best)
        cur = jnp.where(better, gscore[g], cur)
    eidx = lax.broadcasted_iota(jnp.int32, sel.shape, 0)
    masked = jnp.where(eidx // PER_GROUP == best, sel, -jnp.inf)
    m1 = jnp.max(masked, axis=0, keepdims=True)
    idx1 = jnp.min(jnp.where(masked == m1, eidx, N_EXPERTS), axis=0, keepdims=True)
    masked2 = jnp.where(eidx == idx1, -jnp.inf, masked)
    m2 = jnp.max(masked2, axis=0, keepdims=True)
    idx2 = jnp.min(jnp.where(masked2 == m2, eidx, N_EXPERTS), axis=0, keepdims=True)
    oh1, oh2 = eidx == idx1, eidx == idx2
    w1 = jnp.sum(jnp.where(oh1, aff, 0.0), axis=0, keepdims=True)
    w2 = jnp.sum(jnp.where(oh2, aff, 0.0), axis=0, keepdims=True)
    den = w1 + w2
    return oh1, oh2, w1 / den, w2 / den


def _split_hi_lo(x):
    hi = x.astype(BF16)
    return hi, (x - hi.astype(F32)).astype(BF16)


def _outproj_kernel(oda_ref, ossdt_ref, omla_ref, h_ref, mod_ref, ng_ref, wo_ref, rw_ref, rb_ref,
                    h1_ref, xs_ref, meta_ref, cnt_ref, *, n_ctx_tiles, ctx_row, d, tm, s_loc, nb):
    ti, bp = pl.program_id(0), pl.program_id(1)
    tiles = range(nb)
    r_i = lax.broadcasted_iota(jnp.int32, (tm, tm), 0)
    c_i = lax.broadcasted_iota(jnp.int32, (tm, tm), 1)
    before = jnp.where(r_i < c_i, 1.0, 0.0).astype(BF16)
    row_e = lax.broadcasted_iota(jnp.int32, (N_EXPERTS, LANES), 0)
    r_s = lax.broadcasted_iota(jnp.int32, (s_loc, tm), 0).astype(F32)
    lane_e = lax.broadcasted_iota(jnp.int32, (s_loc, MOE_EXTRA), 1)
    row_m = lax.broadcasted_iota(jnp.int32, (LANES, tm), 0)
    rows = [jnp.where(ti < n_ctx_tiles, ctx_row, bp * nb + bb) for bb in tiles]
    mod_at = lambda bb, k: mod_ref[pl.ds(rows[bb], 1), pl.ds(k * d, d)]
    ossd = [jnp.concatenate([ossdt_ref[bb, c].astype(F32).T for c in range(tm // SSD_CHUNK)], axis=0).astype(BF16)
            for bb in tiles]
    mix = [_dot(oda_ref[bb], wo_ref[0:DA_WIDTH]) + _dot(ossd[bb], wo_ref[DA_WIDTH:DA_WIDTH + SSD_INNER])
           + _dot(omla_ref[bb], wo_ref[DA_WIDTH + SSD_INNER:]) for bb in tiles]
    h1 = [h_ref[bb] + mod_at(bb, 2) * mix[bb] for bb in tiles]
    for bb in tiles:
        h1_ref[bb] = h1[bb]
    u = [_ada_norm(h1[bb], ng_ref[...], mod_at(bb, 3), mod_at(bb, 4)) for bb in tiles]
    u16 = [x.astype(BF16) for x in u]
    rw_hi, rw_lo = _split_hi_lo(rw_ref[...])
    u_lo = [(u[bb] - u16[bb].astype(F32)).astype(BF16) for bb in tiles]
    logits = [_dot(u16[bb], rw_hi) + _dot(u_lo[bb], rw_hi) + _dot(u16[bb], rw_lo) for bb in tiles]
    routed = [_route(logits[bb].T[:N_EXPERTS], rb_ref[...]) for bb in tiles]
    cnt = [jnp.where(r[0], 1.0, 0.0) + jnp.where(r[1], 1.0, 0.0) for r in routed]
    rank = [_dot(cnt[bb].astype(BF16), before) for bb in tiles]
    for bb in tiles:
        oh1, oh2, w1, w2 = routed[bb]
        tot = jnp.sum(cnt[bb], axis=1, keepdims=True)
        ptot = jnp.floor((tot + (MOE_BLK - 1)) * (1.0 / MOE_BLK)) * MOE_BLK
        ptot_b = jnp.broadcast_to(ptot, (N_EXPERTS, LANES))
        cnt_ref[bb] = ptot_b
        run = jnp.zeros((1, LANES), F32)
        off = jnp.zeros((N_EXPERTS, LANES), F32)
        for e in range(1, N_EXPERTS):
            run = run + ptot_b[e - 1:e]
            off = jnp.where(row_e == e, run, off)
        slot = off[:, 0:1] + rank[bb]
        dest1 = jnp.sum(jnp.where(oh1, slot, 0.0), axis=0, keepdims=True)
        dest2 = jnp.sum(jnp.where(oh2, slot, 0.0), axis=0, keepdims=True)
        routed[bb] = (r_s == dest1, r_s == dest2, w1, w2)
        meta_ref[bb] = jnp.where(row_m == 0, dest1, jnp.where(row_m == 1, dest2, 0.0)).T
    perm = [jnp.where(routed[bb][0] | routed[bb][1], 1.0, 0.0).astype(BF16) for bb in tiles]
    xs = [_dot(perm[bb], u16[bb]) for bb in tiles]
    for bb in tiles:
        p1, p2, w1, w2 = routed[bb]
        xs_ref[bb, :, 0:d] = xs[bb].astype(BF16)
        wslot = jnp.sum(jnp.where(p1, w1, 0.0) + jnp.where(p2, w2, 0.0), axis=1, keepdims=True)
        w_hi = wslot.astype(BF16).astype(F32)
        xs_ref[bb, :, d:] = jnp.where(lane_e == 0, w_hi, jnp.where(lane_e == 1, wslot - w_hi, 0.0)).astype(BF16)


def _outproj_dispatch(oda, ossd_t, omla, h, mod, ng, wo, rw, rb, *, layer, tm, n_ctx, s_loc, nb):
    bsz, t, d = h.shape
    nt, q = t // tm, SSD_CHUNK
    tok = lambda w: pl.BlockSpec((nb, tm, w), lambda ti, b: (b, ti, 0))
    full = lambda a: pl.BlockSpec(a.shape, lambda ti, b: (0,) * a.ndim)
    tile = lambda r, w: pl.BlockSpec((nb, None, r, w), lambda ti, b: (b, ti, 0, 0))
    lay = lambda a: _layer_spec(a, layer)
    kern = functools.partial(_outproj_kernel, n_ctx_tiles=n_ctx // tm, ctx_row=bsz, d=d, tm=tm, s_loc=s_loc, nb=nb)
    return pl.pallas_call(
        kern,
        grid=(nt, bsz // nb),
        in_specs=[tok(DA_WIDTH), pl.BlockSpec((nb, tm // q, SSD_INNER, q), lambda ti, b: (b, ti, 0, 0)),
                  tok(MLA_WIDTH), tok(d), lay(mod), lay(ng), lay(wo), full(rw), full(rb)],
        out_specs=[tok(d), tile(s_loc, d + MOE_EXTRA), tile(tm, LANES), tile(N_EXPERTS, LANES)],
        out_shape=[jax.ShapeDtypeStruct((bsz, t, d), F32),
                   jax.ShapeDtypeStruct((bsz, nt, s_loc, d + MOE_EXTRA), BF16),
                   jax.ShapeDtypeStruct((bsz, nt, tm, LANES), F32),
                   jax.ShapeDtypeStruct((bsz, nt, N_EXPERTS, LANES), F32)],
        compiler_params=_cparams("arbitrary", "arbitrary"),
        name="outproj_dispatch",
    )(oda, ossd_t, omla, h, mod, ng, wo, rw, rb)


def _expert_kernel(se_ref, bi_ref, sv_ref, sn_ref, *refs, d):
    x_refs = refs[:MOE_STEP_BLKS]
    wg_ref, wu_ref, wd_ref, y_ref, wg16, wu16, wd16 = refs[MOE_STEP_BLKS:]
    s = pl.program_id(0)

    @pl.when(sn_ref[s] > 0)
    def _():
        wg16[...] = wg_ref[...].astype(BF16)
        wu16[...] = wu_ref[...].astype(BF16)
        wd16[...] = wd_ref[...].astype(BF16)

    @pl.when(sv_ref[s] > 0)
    def _():
        x = jnp.concatenate([r[...] for r in x_refs], axis=0)
        xm = x[:, :d]
        wrow = x[:, d:d + 1].astype(F32) + x[:, d + 1:d + 2].astype(F32)
        he = _silu(_dot(xm, wg16[...])) * _dot(xm, wu16[...])
        y_ref[...] = (_dot(he.astype(BF16), wd16[...]) * wrow).astype(BF16)

    @pl.when(sv_ref[s] == 0)
    def _():
        y_ref[...] = jnp.zeros_like(y_ref)


def _expert_ffn(xs2d, wg, wu, wd, step_e, blk_ids, step_valid, step_new, *, layer, n_steps):
    d = wg.shape[2]
    rows = MOE_STEP_BLKS * MOE_BLK
    x_specs = [pl.BlockSpec((MOE_BLK, d + MOE_EXTRA),
                            lambda s, se, bi, sv, sn, j=j: (bi[s * MOE_STEP_BLKS + j], 0))
               for j in range(MOE_STEP_BLKS)]
    wspec = lambda a: pl.BlockSpec((None, None) + a.shape[2:], lambda s, se, bi, sv, sn: (layer, se[s], 0, 0))
    grid_spec = pltpu.PrefetchScalarGridSpec(
        num_scalar_prefetch=4,
        grid=(n_steps,),
        in_specs=x_specs + [wspec(wg), wspec(wu), wspec(wd)],
        out_specs=pl.BlockSpec((rows, d), lambda s, se, bi, sv, sn: (s, 0)),
        scratch_shapes=[pltpu.VMEM(wg.shape[2:], BF16), pltpu.VMEM(wu.shape[2:], BF16),
                        pltpu.VMEM(wd.shape[2:], BF16)],
    )
    return pl.pallas_call(
        functools.partial(_expert_kernel, d=d),
        grid_spec=grid_spec,
        out_shape=jax.ShapeDtypeStruct((n_steps * rows, d), BF16),
        compiler_params=_cparams("arbitrary"),
        name="expert_ffn",
    )(step_e, blk_ids, step_valid, step_new, *([xs2d] * MOE_STEP_BLKS), wg, wu, wd)


def _combine_kernel(inv_ref, *refs, n_lb, n_ctx_tiles, ctx_row, d, tm, s_loc, final, nb):
    y_refs = refs[:nb * n_lb]
    meta_ref, h1_ref, mod_ref, fg_ref, o_ref = refs[nb * n_lb:]
    ti, bp = pl.program_id(0), pl.program_id(1)
    tiles = range(nb)
    lane_s = lax.broadcasted_iota(jnp.int32, (tm, s_loc), 1).astype(F32)
    metas = [meta_ref[bb] for bb in tiles]
    pts = [jnp.where((lane_s == m[:, 0:1]) | (lane_s == m[:, 1:2]), 1.0, 0.0).astype(BF16) for m in metas]
    ys = [jnp.concatenate([r[...] for r in y_refs[bb * n_lb:(bb + 1) * n_lb]], axis=0) for bb in tiles]
    y = [_dot(pts[bb], ys[bb]) for bb in tiles]
    for bb in tiles:
        row = jnp.where(ti < n_ctx_tiles, ctx_row, bp * nb + bb)
        h2 = h1_ref[bb] + mod_ref[pl.ds(row, 1), pl.ds(5 * d, d)] * y[bb]
        o_ref[bb] = _rms(h2, fg_ref[...]) if final else h2


def _combine(ys_em, inv, meta, h1, mod, fg, *, layer, tm, n_ctx, s_loc, final, nb):
    bsz, t, d = h1.shape
    nt = t // tm
    n_lb = s_loc // MOE_BLK
    y_specs = [pl.BlockSpec((MOE_BLK, d), lambda ti, b, inv, bb=bb, j=j: (inv[((b * nb + bb) * nt + ti) * n_lb + j], 0))
               for bb in range(nb) for j in range(n_lb)]
    full = lambda a: pl.BlockSpec(a.shape, lambda ti, b, inv: (0,) * a.ndim)
    grid_spec = pltpu.PrefetchScalarGridSpec(
        num_scalar_prefetch=1,
        grid=(nt, bsz // nb),
        in_specs=y_specs + [pl.BlockSpec((nb, None, tm, LANES), lambda ti, b, inv: (b, ti, 0, 0)),
                            pl.BlockSpec((nb, tm, d), lambda ti, b, inv: (b, ti, 0)),
                            _layer_spec(mod, layer), full(fg)],
        out_specs=pl.BlockSpec((nb, tm, d), lambda ti, b, inv: (b, ti, 0)),
    )
    kern = functools.partial(_combine_kernel, n_lb=n_lb, n_ctx_tiles=n_ctx // tm, ctx_row=bsz, d=d, tm=tm,
                             s_loc=s_loc, final=final, nb=nb)
    return pl.pallas_call(
        kern,
        grid_spec=grid_spec,
        out_shape=jax.ShapeDtypeStruct((bsz, t, d), F32),
        compiler_params=_cparams("arbitrary", "arbitrary"),
        name="moe_combine",
    )(inv, *([ys_em] * (nb * n_lb)), meta, h1, mod, fg)


def _rope_tables(length, dim):
    rows = length // GRID_W
    row = jnp.repeat(jnp.arange(rows), GRID_W).astype(F32)
    col = jnp.tile(jnp.arange(GRID_W), rows).astype(F32)
    n_freq = dim // 4
    inv_freq = ROPE_THETA ** (-jnp.arange(n_freq, dtype=F32) / n_freq)
    ang = jnp.concatenate([row[:, None] * inv_freq, col[:, None] * inv_freq], axis=-1)
    return jnp.cos(ang), jnp.sin(ang)


def _table_set(n_ctx, n_lat):
    def lanes(cos, sin, lane0, width, reps, outside_cos):
        c = jnp.concatenate([cos, cos], axis=-1)
        s = jnp.concatenate([-sin, sin], axis=-1)
        grp_c = jnp.full((n_lat, width), outside_cos, F32).at[:, lane0:lane0 + c.shape[1]].set(c)
        grp_s = jnp.zeros((n_lat, width), F32).at[:, lane0:lane0 + s.shape[1]].set(s)
        ctx_c = jnp.full((n_ctx, width), outside_cos, F32).at[:, lane0:lane0 + c.shape[1]].set(1.0)
        ctx_s = jnp.zeros((n_ctx, width), F32)
        return (jnp.tile(jnp.concatenate([ctx_c, grp_c], axis=0), (1, reps)),
                jnp.tile(jnp.concatenate([ctx_s, grp_s], axis=0), (1, reps)))

    dcos, dsin = _rope_tables(n_lat, DA_QK)
    cda, sda = lanes(dcos, dsin, 0, DA_QK, DA_QW // DA_QK, 1.0)
    mcos, msin = _rope_tables(n_lat, MLA_ROPE)
    cq, sm = lanes(mcos, msin, KR_LANE0, LANES, 1, 1.0)
    ck, _ = lanes(mcos, msin, KR_LANE0, LANES, 1, 0.0)
    return cda, sda, cq, ck, sm


def _pack_w_in(w_in):
    w_in = w_in.astype(BF16)
    depth, d, _ = w_in.shape
    o_ssd = DA_IN
    o_mla = DA_IN + SSD_IN
    z = w_in[..., o_ssd:o_ssd + SSD_INNER]
    xbc = w_in[..., o_ssd + SSD_INNER:o_ssd + SSD_INNER + SSD_CONV_DIM]
    dt = w_in[..., o_ssd + SSD_INNER + SSD_CONV_DIM:o_mla]
    cq = w_in[..., o_mla:o_mla + MLA_Q_RANK]
    ckv = w_in[..., o_mla + MLA_Q_RANK:o_mla + MLA_Q_RANK + MLA_KV_RANK]
    kr = w_in[..., o_mla + MLA_Q_RANK + MLA_KV_RANK:]
    zeros = lambda n: jnp.zeros((depth, d, n), w_in.dtype)
    misc = jnp.concatenate([zeros(KR_LANE0), kr, dt, zeros(LANES - DT_LANE0 - 2 * SSD_HEADS)], axis=-1)
    return jnp.concatenate([w_in[..., :DA_IN], z, xbc, cq, ckv, misc], axis=-1)


def _pack_w_uq(w_uq):
    depth, r, _ = w_uq.shape
    w = w_uq.reshape(depth, r, MLA_HEADS, MLA_NOPE + MLA_ROPE)
    w = jnp.pad(w, ((0, 0), (0, 0), (0, 0), (0, MLA_HEAD_PAD - MLA_NOPE - MLA_ROPE)))
    return w.reshape(depth, r, MLA_QPAD).astype(BF16)


def _pack_w_ukv(w_ukv):
    depth, r, _ = w_ukv.shape
    w = w_ukv.reshape(depth, r, MLA_HEADS, MLA_NOPE + MLA_V)
    kn = jnp.pad(w[..., :MLA_NOPE], ((0, 0), (0, 0), (0, 0), (0, MLA_HEAD_PAD - MLA_NOPE)))
    return jnp.concatenate([kn.reshape(depth, r, MLA_QPAD), w[..., MLA_NOPE:].reshape(depth, r, MLA_WIDTH)],
                           axis=-1).astype(BF16)


def _dispatch_tables(pcnt, n_lb, n_steps):
    ntt = pcnt.shape[0]
    nb = (pcnt / MOE_BLK).astype(jnp.int32)
    lo = jnp.cumsum(nb, axis=1) - nb
    n_e = jnp.sum(nb, axis=0)
    p_e = (n_e + MOE_STEP_BLKS - 1) // MOE_STEP_BLKS * MOE_STEP_BLKS
    ends = jnp.cumsum(p_e)
    base = (ends - p_e)[None, :] + jnp.cumsum(nb, axis=0) - nb
    lb = jnp.arange(n_lb, dtype=jnp.int32)
    owner = (lb[None, :, None] >= lo[:, None, :]) & (lb[None, :, None] < (lo + nb)[:, None, :])
    used = jnp.any(owner, axis=-1)
    pos = jnp.sum(jnp.where(owner, (base - lo)[:, None, :], 0), axis=-1) + lb[None, :]
    inv = jnp.where(used, pos, 0).astype(jnp.int32)
    n_pos = n_steps * MOE_STEP_BLKS
    src = jnp.arange(ntt, dtype=jnp.int32)[:, None] * n_lb + lb[None, :]
    fwd = jnp.zeros((n_pos,), jnp.int32).at[jnp.where(used, pos, n_pos).reshape(-1)].set(src.reshape(-1), mode="drop")
    starts = jnp.arange(n_steps, dtype=jnp.int32) * MOE_STEP_BLKS
    step_e = jnp.minimum(jnp.sum(starts[:, None] >= ends[None, :], axis=1), N_EXPERTS - 1).astype(jnp.int32)
    step_valid = (starts < ends[-1]).astype(jnp.int32)
    step_new = jnp.concatenate([jnp.ones((1,), jnp.int32), (step_e[1:] != step_e[:-1]).astype(jnp.int32)])
    fwd2 = fwd.reshape(n_steps, MOE_STEP_BLKS)
    last = lax.dynamic_slice_in_dim(fwd2, jnp.maximum(ends[-1] // MOE_STEP_BLKS - 1, 0), 1, axis=0)
    fwd = jnp.where(step_valid[:, None] > 0, fwd2, last).reshape(-1)
    return inv.reshape(-1), fwd, step_e, step_valid, step_new


def kernel(x, c, ctx, c_ctx, norm_mix_g, norm_ffn_g, w_mod, b_mod, w_in, w_out, da_lambda, da_subln_g, ssd_conv_w, ssd_conv_b, ssd_dt_bias, ssd_a_log, ssd_d, ssd_norm_g, mla_q_norm_g, mla_kv_norm_g, mla_w_uq, mla_w_ukv, router_w, router_bias, exp_w_gate, exp_w_up, exp_w_down, final_norm_g):
    bsz, n_lat, d = x.shape
    n_ctx = ctx.shape[1]
    t = n_ctx + n_lat
    tm = min(256, n_ctx)
    assert n_ctx % tm == 0 and n_lat % tm == 0 and n_ctx % SSD_CHUNK == 0 and n_lat % SSD_CHUNK == 0
    nt = t // tm
    s_loc = 2 * tm + 2 * LANES
    n_lb = s_loc // MOE_BLK
    n_steps = -(-(bsz * nt * n_lb + N_EXPERTS * (MOE_STEP_BLKS - 1)) // MOE_STEP_BLKS)
    nc, nc0 = t // SSD_CHUNK, n_ctx // SSD_CHUNK
    nb = 2 if bsz % 2 == 0 else 1
    q = SSD_CHUNK

    r_pad = -(-(bsz + 1) // 8) * 8
    c_rows = jnp.concatenate([c, c_ctx[None, :], jnp.zeros((r_pad - bsz - 1, d), F32)], axis=0)
    mod = _modulation(c_rows, w_mod, b_mod)

    w_in_p = _pack_w_in(w_in)
    w_uq_p = _pack_w_uq(mla_w_uq)
    w_ukv_p = _pack_w_ukv(mla_w_ukv)
    w_out16 = w_out.astype(BF16)
    tabs = _table_set(n_ctx, n_lat)
    rw_pad = jnp.pad(router_w, ((0, 0), (0, LANES - N_EXPERTS)))
    rb = router_bias.reshape(N_EXPERTS, 1)
    lane_b = lambda v: jnp.broadcast_to(v[..., None], v.shape + (q,))
    dsk_rows = jnp.repeat(ssd_d, SSD_P, axis=-1)
    ng_mix, ng_ffn = norm_mix_g[:, None, :], norm_ffn_g[:, None, :]
    qg, kvg = mla_q_norm_g[:, None, :], mla_kv_norm_g[:, None, :]
    sub_g = jnp.tile(da_subln_g, (1, DA_HEADS))[:, :, None]
    ssd_par = (lane_b(ssd_conv_w), lane_b(ssd_conv_b), lane_b(ssd_dt_bias.reshape(DEPTH, -1)),
               lane_b(ssd_a_log.reshape(DEPTH, -1)), lane_b(dsk_rows), lane_b(ssd_norm_g))

    h = jnp.concatenate([ctx, x], axis=1)
    for i in range(DEPTH):
        lam_init = 0.8 - 0.6 * math.exp(-0.3 * i)
        qda_t, kda, vda_t, z_t, xbc_t, dt_t, q4t, k4, vm_t = _inproj(
            h, mod, ng_mix, w_in_p, tabs, qg, kvg, w_uq_p, w_ukv_p, layer=i, tm=tm, n_ctx=n_ctx, nb=nb)
        o_da = _da_attention(qda_t, kda, vda_t, da_lambda, sub_g, layer=i, lam_init=lam_init, tq=tm, n_ctx=n_ctx)
        o_mla = _mla_attention(q4t, k4, vm_t, tq=tm, n_ctx=n_ctx)
        o_ssd_t = _ssd_mixer(z_t, xbc_t, dt_t, *ssd_par, layer=i, nc0=nc0)
        h1, xs, meta, cnt = _outproj_dispatch(o_da, o_ssd_t, o_mla, h, mod, ng_ffn, w_out16, rw_pad, rb, layer=i, tm=tm,
                                              n_ctx=n_ctx, s_loc=s_loc, nb=nb)
        inv, fwd, step_e, step_valid, step_new = _dispatch_tables(cnt[:, :, :, 0].reshape(bsz * nt, N_EXPERTS), n_lb,
                                                                  n_steps)
        ys_em = _expert_ffn(xs.reshape(bsz * nt * s_loc, d + MOE_EXTRA), exp_w_gate, exp_w_up, exp_w_down, step_e, fwd,
                            step_valid, step_new, layer=i, n_steps=n_steps)
        h = _combine(ys_em, inv, meta, h1, mod, final_norm_g[None, :], layer=i, tm=tm, n_ctx=n_ctx, s_loc=s_loc,
                     final=(i == DEPTH - 1), nb=nb)
    return h[:, n_ctx:]
```

```python
import functools
import math

import jax
import jax.numpy as jnp
from jax import lax
from jax.experimental import pallas as pl
from jax.experimental.pallas import tpu as pltpu

F32 = jnp.float32
BF16 = jnp.bfloat16

DEPTH = 4
GRID_W = 64
EPS = 1e-6
ROPE_THETA = 10000.0
DA_HEADS, DA_QK = 4, 32
DA_V = 2 * DA_QK
DA_WIDTH = DA_HEADS * DA_V
DA_QW = DA_HEADS * 2 * DA_QK
DA_IN = 2 * DA_QW + DA_WIDTH
SSD_HEADS, SSD_P, SSD_GROUPS, SSD_STATE, SSD_CHUNK = 8, 64, 2, 64, 128
SSD_INNER = SSD_HEADS * SSD_P
SSD_GN = SSD_GROUPS * SSD_STATE
SSD_CONV_DIM = SSD_INNER + 2 * SSD_GN
SSD_IN = SSD_INNER + SSD_CONV_DIM + 2 * SSD_HEADS
MLA_HEADS, MLA_Q_RANK, MLA_KV_RANK, MLA_NOPE, MLA_ROPE, MLA_V = 4, 256, 128, 64, 32, 64
MLA_WIDTH = MLA_HEADS * MLA_V
MLA_IN = MLA_Q_RANK + MLA_KV_RANK + MLA_ROPE
MLA_SCALE = (MLA_NOPE + MLA_ROPE) ** -0.5
N_EXPERTS, N_GROUPS, D_EXPERT = 16, 4, 512
PER_GROUP = N_EXPERTS // N_GROUPS

LANES = 128
SUBLANES_BF16 = 16
VMEM_LIMIT_BYTES = 56 * 1024 * 1024

MLA_HEAD_PAD = LANES
MLA_QPAD = MLA_HEADS * MLA_HEAD_PAD
KR_LANE0 = MLA_NOPE
DT_LANE0 = MLA_NOPE + MLA_ROPE
IN_COLS = DA_IN + SSD_INNER + SSD_CONV_DIM + MLA_Q_RANK + MLA_KV_RANK + LANES
C_Z = DA_IN
C_XBC = C_Z + SSD_INNER
C_CQ = C_XBC + SSD_CONV_DIM
C_CKV = C_CQ + MLA_Q_RANK
C_MISC = C_CKV + MLA_KV_RANK
LOG2E = math.log2(math.e)
DA_QSCALE = DA_QK ** -0.5 * LOG2E
MLA_QSCALE = MLA_SCALE * LOG2E

KEY_CHUNK = 256
MOE_BLK = SUBLANES_BF16
MOE_STEP_BLKS = 32
MOE_EXTRA = LANES


def _sigmoid(x):
    return 1.0 / (1.0 + jnp.exp(-x))


def _silu(x):
    return x * _sigmoid(x)


def _rms(x, g, axis=-1):
    return x * lax.rsqrt(jnp.mean(x * x, axis=axis, keepdims=True) + EPS) * g


def _ada_norm(hv, g, shift, scale):
    return _rms(hv, g) * (1.0 + scale) + shift


def _rope(x, cos, sin_signed, half):
    w = x.shape[-1]
    lane = lax.broadcasted_iota(jnp.int32, x.shape, x.ndim - 1)
    first = (lane % (2 * half)) < half
    partner = jnp.where(first, pltpu.roll(x, w - half, x.ndim - 1), pltpu.roll(x, half, x.ndim - 1))
    return x * cos + partner * sin_signed


def _dot(a, b):
    return jnp.dot(a, b, preferred_element_type=F32)


def _dot_hi(a, b):
    return jnp.dot(a, b, preferred_element_type=F32, precision=lax.Precision.HIGHEST)


def _dot_split3(x, m01):
    x1 = x.astype(BF16)
    r1 = x - x1.astype(F32)
    x2 = r1.astype(BF16)
    x3 = (r1 - x2.astype(F32)).astype(BF16)
    return _dot(x1, m01) + _dot(x2, m01) + _dot(x3, m01)


def _layer_spec(a, i):
    return pl.BlockSpec((None,) + a.shape[1:], lambda *_: (i,) + (0,) * (a.ndim - 1))


def _cparams(*sem):
    return pltpu.CompilerParams(dimension_semantics=sem, vmem_limit_bytes=VMEM_LIMIT_BYTES)


def _mod_kernel(c_ref, w_ref, b_ref, o_ref):
    o_ref[...] = _dot_hi(_silu(c_ref[...]), w_ref[...]) + b_ref[...]


def _modulation(c_rows, w_mod, b_mod):
    depth, d, n = w_mod.shape
    r = c_rows.shape[0]
    tn = 1536
    return pl.pallas_call(
        _mod_kernel,
        grid=(depth, n // tn),
        in_specs=[pl.BlockSpec((r, d), lambda l, j: (0, 0)),
                  pl.BlockSpec((None, d, tn), lambda l, j: (l, 0, j)),
                  pl.BlockSpec((None, 1, tn), lambda l, j: (l, 0, j))],
        out_specs=pl.BlockSpec((None, r, tn), lambda l, j: (l, 0, j)),
        out_shape=jax.ShapeDtypeStruct((depth, r, n), F32),
        compiler_params=_cparams("arbitrary", "arbitrary"),
        name="modulation",
    )(c_rows, w_mod, b_mod.reshape(depth, 1, n))


def _inproj_kernel(h_ref, mod_ref, ng_ref, w_ref, cda_ref, sda_ref, cq_ref, ck_ref, sm_ref,
                   qg_ref, kvg_ref, wuq_ref, wukv_ref,
                   qdat_ref, kda_ref, vdat_ref, zt_ref, xbct_ref, dtt_ref, q4t_ref, k4_ref, vmt_ref,
                   *, n_ctx_tiles, ctx_row, d, nb, tm):
    ti, bp = pl.program_id(0), pl.program_id(1)
    cda, sda, sm = cda_ref[...], sda_ref[...], sm_ref[...]
    cos_q = jnp.concatenate([cq_ref[...]] * MLA_HEADS, axis=1)
    sin_q = jnp.concatenate([sm] * MLA_HEADS, axis=1)
    q = SSD_CHUNK
    accs = []
    for bb in range(nb):
        row = jnp.where(ti < n_ctx_tiles, ctx_row, bp * nb + bb)
        shift = mod_ref[pl.ds(row, 1), pl.ds(0, d)]
        scale = mod_ref[pl.ds(row, 1), pl.ds(d, d)]
        u = _ada_norm(h_ref[bb], ng_ref[...], shift, scale).astype(BF16)
        accs.append(_dot(u, w_ref[...]))
    for bb in range(nb):
        acc = accs[bb]
        qdat_ref[bb] = (_rope(acc[:, 0:DA_QW], cda, sda, DA_QK // 2) * DA_QSCALE).T.astype(BF16)
        kda_ref[bb] = _rope(acc[:, DA_QW:2 * DA_QW], cda, sda, DA_QK // 2).astype(BF16)
        vdat_ref[bb] = acc[:, 2 * DA_QW:DA_IN].T.astype(BF16)
        z_t = acc[:, C_Z:C_XBC].T
        xbc_t = acc[:, C_XBC:C_CQ].T
        misc = acc[:, C_MISC:IN_COLS]
        dt_t = misc.T[DT_LANE0:DT_LANE0 + 2 * SSD_HEADS]
        for c in range(tm // q):
            zt_ref[bb, c] = z_t[:, c * q:(c + 1) * q].astype(BF16)
            xbct_ref[bb, c] = xbc_t[:, c * q:(c + 1) * q]
            dtt_ref[bb, c] = dt_t[:, c * q:(c + 1) * q]
        cqn = _rms(acc[:, C_CQ:C_CKV], qg_ref[...]).astype(BF16)
        qm_t = (_rope(_dot(cqn, wuq_ref[...]), cos_q, sin_q, MLA_ROPE // 2) * MLA_QSCALE).T
        ckvn = _rms(acc[:, C_CKV:C_MISC], kvg_ref[...]).astype(BF16)
        kv = _dot(ckvn, wukv_ref[...])
        kr = _rope(misc, ck_ref[...], sm, MLA_ROPE // 2)
        km = kv[:, :MLA_QPAD] + jnp.concatenate([kr] * MLA_HEADS, axis=1)
        for hh in range(MLA_HEADS):
            q4t_ref[bb, hh] = qm_t[hh * MLA_HEAD_PAD:(hh + 1) * MLA_HEAD_PAD].astype(BF16)
            k4_ref[bb, hh] = km[:, hh * MLA_HEAD_PAD:(hh + 1) * MLA_HEAD_PAD].astype(BF16)
        vmt_ref[bb] = kv[:, MLA_QPAD:].T.astype(BF16)


def _inproj(h, mod, ng, w_in_p, tabs, qg, kvg, wuq_p, wukv_p, *, layer, tm, n_ctx, nb):
    bsz, t, d = h.shape
    nt, q = t // tm, SSD_CHUNK
    cpt = tm // q
    tok = lambda w: pl.BlockSpec((nb, tm, w), lambda ti, b: (b, ti, 0))
    tab = lambda w: pl.BlockSpec((tm, w), lambda ti, b: (ti, 0))
    chunked = lambda f: pl.BlockSpec((nb, cpt, f, q), lambda ti, b: (b, ti, 0, 0))
    lay = lambda a: _layer_spec(a, layer)
    cda, sda, cq, ck, sm = tabs
    sds = jax.ShapeDtypeStruct
    tok_t = lambda w: pl.BlockSpec((nb, w, tm), lambda ti, b: (b, 0, ti))
    out_specs = [tok_t(DA_QW), tok(DA_QW), tok_t(DA_WIDTH),
                 chunked(SSD_INNER), chunked(SSD_CONV_DIM), chunked(2 * SSD_HEADS),
                 pl.BlockSpec((nb, MLA_HEADS, MLA_HEAD_PAD, tm), lambda ti, b: (b, 0, 0, ti)),
                 pl.BlockSpec((nb, MLA_HEADS, tm, MLA_HEAD_PAD), lambda ti, b: (b, 0, ti, 0)), tok_t(MLA_WIDTH)]
    out_shape = [sds((bsz, DA_QW, t), BF16), sds((bsz, t, DA_QW), BF16), sds((bsz, DA_WIDTH, t), BF16),
                 sds((bsz, t // q, SSD_INNER, q), BF16), sds((bsz, t // q, SSD_CONV_DIM, q), F32),
                 sds((bsz, t // q, 2 * SSD_HEADS, q), F32),
                 sds((bsz, MLA_HEADS, MLA_HEAD_PAD, t), BF16), sds((bsz, MLA_HEADS, t, MLA_HEAD_PAD), BF16),
                 sds((bsz, MLA_WIDTH, t), BF16)]
    kern = functools.partial(_inproj_kernel, n_ctx_tiles=n_ctx // tm, ctx_row=bsz, d=d, nb=nb, tm=tm)
    return pl.pallas_call(
        kern,
        grid=(nt, bsz // nb),
        in_specs=[tok(d), lay(mod), lay(ng), lay(w_in_p), tab(DA_QW), tab(DA_QW), tab(LANES), tab(LANES),
                  tab(LANES), lay(qg), lay(kvg), lay(wuq_p), lay(wukv_p)],
        out_specs=out_specs,
        out_shape=out_shape,
        compiler_params=_cparams("arbitrary", "arbitrary"),
        name="inproj",
    )(h, mod, ng, w_in_p, cda, sda, cq, ck, sm, qg, kvg, wuq_p, wukv_p)


V_AUG = DA_V + SUBLANES_BF16


def _scores_pass(k_at, qtm, nk, s_scr):
    kc = KEY_CHUNK if nk % KEY_CHUNK == 0 else LANES
    m = None
    for c0 in range(0, nk, kc):
        s_c = _dot(k_at(c0, c0 + kc), qtm)
        s_scr[c0:c0 + kc, :] = s_c
        part = jnp.max(s_c, axis=0, keepdims=True)
        m = part if m is None else jnp.maximum(m, part)
    return m


def _pv_pass(vaug_at, nk, s_scr, m):
    kc = KEY_CHUNK if nk % KEY_CHUNK == 0 else LANES
    acc = None
    for c0 in range(0, nk, kc):
        e = jnp.exp2(s_scr[c0:c0 + kc, :] - m).astype(BF16)
        inc = _dot(vaug_at(c0, c0 + kc), e)
        acc = inc if acc is None else acc + inc
    return acc


def _attend_heads(k_of, qt_of, vaug_of, nk, scr, n_sub):
    m = _scores_pass(k_of(0), qt_of(0), nk, scr[0])
    outs = []
    for j in range(n_sub):
        if j + 1 < n_sub:
            m_next = _scores_pass(k_of(j + 1), qt_of(j + 1), nk, scr[(j + 1) % 2])
        acc = _pv_pass(vaug_of(j), nk, scr[j % 2], m)
        outs.append(acc[:DA_V] * (1.0 / acc[DA_V:DA_V + 1]))
        if j + 1 < n_sub:
            m = m_next
    return outs


def _fill_vaug(vt_ref, vaug_scr, n_heads):
    t = vt_ref.shape[1]
    for hh in range(n_heads):
        vaug_scr[hh, 0:DA_V, :] = vt_ref[hh * DA_V:(hh + 1) * DA_V, :]
        vaug_scr[hh, DA_V:V_AUG, :] = jnp.ones((V_AUG - DA_V, t), BF16)


def _da_attn_kernel(lam_ref, g_ref, qt_ref, k_ref, vt_ref, o_ref, s0_scr, s1_scr, vaug_scr, *, n_ctx, n_ctx_tiles,
                    lam_init, tq):
    qi = pl.program_id(1)

    @pl.when(qi == 0)
    def _():
        _fill_vaug(vt_ref, vaug_scr, DA_HEADS)

    lv = lam_ref[...]
    lam = (jnp.exp(jnp.sum(lv[0:1] * lv[1:2], axis=-1, keepdims=True))
           - jnp.exp(jnp.sum(lv[2:3] * lv[3:4], axis=-1, keepdims=True)) + lam_init)
    row_q = lax.broadcasted_iota(jnp.int32, (DA_QW, 1), 0)

    def attend(nk):
        qt = qt_ref[...]
        qt_of = lambda j: qt * jnp.where(row_q // DA_QK == j, 1.0, 0.0).astype(BF16)
        k_of = lambda j: (lambda c0, c1: k_ref[c0:c1, :])
        vaug_of = lambda j: (lambda c0, c1: vaug_scr[j // 2, :, c0:c1])
        outs = _attend_heads(k_of, qt_of, vaug_of, nk, (s0_scr, s1_scr), 2 * DA_HEADS)
        heads = []
        for hh in range(DA_HEADS):
            o = outs[2 * hh] - lam * outs[2 * hh + 1]
            heads.append(o * lax.rsqrt(jnp.mean(o * o, axis=0, keepdims=True) + EPS))
        o_t = jnp.concatenate(heads, axis=0) * g_ref[...] * (1.0 - lam_init)
        o_ref[...] = o_t.T.astype(BF16)

    @pl.when(qi < n_ctx_tiles)
    def _():
        attend(n_ctx)

    @pl.when(qi >= n_ctx_tiles)
    def _():
        attend(k_ref.shape[0])


def _da_attention(qt, k, vt, lam_vec, g_col, *, layer, lam_init, tq, n_ctx):
    bsz, _, t = qt.shape
    kern = functools.partial(_da_attn_kernel, n_ctx=n_ctx, n_ctx_tiles=n_ctx // tq, lam_init=lam_init, tq=tq)
    return pl.pallas_call(
        kern,
        grid=(bsz, t // tq),
        in_specs=[_layer_spec(lam_vec, layer), _layer_spec(g_col, layer),
                  pl.BlockSpec((None, DA_QW, tq), lambda b, i: (b, 0, i)),
                  pl.BlockSpec((None, t, DA_QW), lambda b, i: (b, 0, 0)),
                  pl.BlockSpec((None, DA_WIDTH, t), lambda b, i: (b, 0, 0))],
        out_specs=pl.BlockSpec((None, tq, DA_WIDTH), lambda b, i: (b, i, 0)),
        out_shape=jax.ShapeDtypeStruct((bsz, t, DA_WIDTH), BF16),
        scratch_shapes=[pltpu.VMEM((t, tq), F32), pltpu.VMEM((t, tq), F32), pltpu.VMEM((DA_HEADS, V_AUG, t), BF16)],
        compiler_params=_cparams("arbitrary", "arbitrary"),
        name="da_attention",
    )(lam_vec, g_col, qt, k, vt)


def _mla_attn_kernel(qt_ref, k_ref, vt_ref, o_ref, s0_scr, s1_scr, vaug_scr, *, n_ctx, n_ctx_tiles, tq):
    qi = pl.program_id(1)

    @pl.when(qi == 0)
    def _():
        _fill_vaug(vt_ref, vaug_scr, MLA_HEADS)

    def attend(nk):
        k_of = lambda hh: (lambda c0, c1: k_ref[hh, c0:c1, :])
        vaug_of = lambda hh: (lambda c0, c1: vaug_scr[hh, :, c0:c1])
        outs = _attend_heads(k_of, lambda hh: qt_ref[hh], vaug_of, nk, (s0_scr, s1_scr), MLA_HEADS)
        o_ref[...] = jnp.concatenate(outs, axis=0).T.astype(BF16)

    @pl.when(qi < n_ctx_tiles)
    def _():
        attend(n_ctx)

    @pl.when(qi >= n_ctx_tiles)
    def _():
        attend(k_ref.shape[1])


def _mla_attention(q4t, k4, vt, *, tq, n_ctx):
    bsz, _, _, t = q4t.shape
    kern = functools.partial(_mla_attn_kernel, n_ctx=n_ctx, n_ctx_tiles=n_ctx // tq, tq=tq)
    return pl.pallas_call(
        kern,
        grid=(bsz, t // tq),
        in_specs=[pl.BlockSpec((None, MLA_HEADS, MLA_HEAD_PAD, tq), lambda b, i: (b, 0, 0, i)),
                  pl.BlockSpec((None, MLA_HEADS, t, MLA_HEAD_PAD), lambda b, i: (b, 0, 0, 0)),
                  pl.BlockSpec((None, MLA_WIDTH, t), lambda b, i: (b, 0, 0))],
        out_specs=pl.BlockSpec((None, tq, MLA_WIDTH), lambda b, i: (b, i, 0)),
        out_shape=jax.ShapeDtypeStruct((bsz, t, MLA_WIDTH), BF16),
        scratch_shapes=[pltpu.VMEM((t, tq), F32), pltpu.VMEM((t, tq), F32), pltpu.VMEM((MLA_HEADS, V_AUG, t), BF16)],
        compiler_params=_cparams("arbitrary", "arbitrary"),
        name="mla_attention",
    )(q4t, k4, vt)


def _ssd_kernel(z_ref, xbc_ref, dt_ref, cw_ref, cb_ref, dtb_ref, alog_ref, dsk_ref, ng_ref,
                o_ref, xc_scr, y_scr, hf_scr, hb_scr, *, nc, nc0):
    q = SSD_CHUNK
    hp = SSD_HEADS * SSD_P
    lane_x = lax.broadcasted_iota(jnp.int32, (SSD_CONV_DIM, q), 1)

    def conv_body(c, carry):
        xc = xbc_ref[c]
        keep_prev = jnp.where((c == 0) | (c == nc0), 0.0, 1.0)
        keep_next = jnp.where((c == nc0 - 1) | (c == nc - 1), 0.0, 1.0)
        xp = xbc_ref[jnp.maximum(c - 1, 0)] * keep_prev
        xn = xbc_ref[jnp.minimum(c + 1, nc - 1)] * keep_next
        prev = pltpu.roll(jnp.where(lane_x == q - 1, xp, xc), 1, 1)
        nxt = pltpu.roll(jnp.where(lane_x == 0, xn, xc), q - 1, 1)
        a = _silu(cw_ref[0] * prev + cw_ref[1] * xc + cw_ref[2] * nxt + cb_ref[...])
        xc_scr[c] = a
        y_scr[c] = dsk_ref[...] * a[:hp]
        return carry

    lax.fori_loop(0, nc, conv_body, 0)
    hf_scr[...] = jnp.zeros_like(hf_scr)
    hb_scr[...] = jnp.zeros_like(hb_scr)

    sub = lax.broadcasted_iota(jnp.int32, (q, q), 0)
    lan = lax.broadcasted_iota(jnp.int32, (q, q), 1)
    lane_n = lax.broadcasted_iota(jnp.int32, (1, SSD_GN), 1)
    pad_rows = jnp.zeros((q - SSD_HEADS, q), F32)

    per_g = SSD_HEADS // SSD_GROUPS
    rows_g = hp // SSD_GROUPS
    gmask = [lane_n // SSD_STATE == g for g in range(SSD_GROUPS)]
    tris = (sub <= lan, sub >= lan)
    tri01 = [jnp.where(tr, 1.0, 0.0).astype(BF16) for tr in tris]
    ends = (q - 1, 0)
    h_scrs = (hf_scr, hb_scr)
    dirs = (0, 1)

    lanes = 2 if nc % 2 == 0 else 1

    def prep(c, d):
        xc = xc_scr[c]
        hs = slice(SSD_HEADS * d, SSD_HEADS * (d + 1))
        dtl = dt_ref[c][hs] + dtb_ref[hs]
        dt = jnp.maximum(dtl, 0.0) + jnp.log(1.0 + jnp.exp(-jnp.abs(dtl)))
        dta = dt * (-jnp.exp(alog_ref[hs]))
        cum_pad = _dot_split3(jnp.concatenate([dta, pad_rows], axis=0), tri01[d])
        btok = xc[hp:hp + SSD_GN].T
        ct16 = xc[hp + SSD_GN:].astype(BF16)
        gts = [_dot(jnp.where(gmask[g], btok, 0.0).astype(BF16), ct16) for g in range(SSD_GROUPS)]
        return dict(c=c, xt=xc[:hp], dt=dt, cum_pad=cum_pad, btok16=btok.astype(BF16), ct16=ct16, gts=gts)

    def head_work(p, d):
        cum_row = p["cum_pad"][:SSD_HEADS]
        cum_col = p["cum_pad"].T
        ydiag, xws, decs, ear = [], [], [], []
        for hh in range(SSD_HEADS):
            ar = cum_row[hh:hh + 1, :]
            ac = cum_col[:, hh:hh + 1]
            sct = (p["gts"][hh // per_g] * jnp.where(tris[d], jnp.exp(ar - ac), 0.0)).astype(BF16)
            xdt = p["xt"][hh * SSD_P:(hh + 1) * SSD_P] * p["dt"][hh:hh + 1, :]
            ydiag.append(_dot(xdt.astype(BF16), sct))
            a_end = ar[:, ends[d]:ends[d] + 1]
            xws.append(xdt * jnp.exp(a_end - ar))
            decs.append(jnp.exp(a_end))
            ear.append(jnp.exp(ar))
        return ydiag, xws, decs, ear

    def state_incs(p, xws):
        return [jnp.where(gmask[g], _dot(jnp.concatenate(xws[g * per_g:(g + 1) * per_g], axis=0).astype(BF16),
                                         p["btok16"]), 0.0) for g in range(SSD_GROUPS)]

    def carried_out(p, hm):
        return [_dot(hm[g * rows_g:(g + 1) * rows_g].astype(BF16), p["ct16"]) for g in range(SSD_GROUPS)]

    def step(s, carry):
        pos = [s * lanes + u for u in range(lanes)]
        cs = [(p_, jnp.where(p_ < nc0, nc0 - 1 - p_, nc - 1 - (p_ - nc0))) for p_ in pos]
        pre = [[prep(cs[u][d], d) for d in dirs] for u in range(lanes)]
        hm = [h_scrs[d][...] for d in dirs]
        yo = [carried_out(pre[0][d], hm[d]) for d in dirs]
        hw = [head_work(pre[0][d], d) for d in dirs]
        for u in range(lanes):
            incs = [state_incs(pre[u][d], hw[d][1]) for d in dirs]
            hw_next = [head_work(pre[u + 1][d], d) for d in dirs] if u + 1 < lanes else None
            for d in dirs:
                ydiag, _, decs, ear = hw[d]
                y = [ydiag[hh] + yo[d][hh // per_g][(hh % per_g) * SSD_P:(hh % per_g + 1) * SSD_P] * ear[hh]
                     for hh in range(SSD_HEADS)]
                y_scr[pre[u][d]["c"]] += jnp.concatenate(y, axis=0)
                hm[d] = jnp.concatenate(
                    [hm[d][hh * SSD_P:(hh + 1) * SSD_P] * decs[hh]
                     + incs[d][hh // per_g][(hh % per_g) * SSD_P:(hh % per_g + 1) * SSD_P]
                     for hh in range(SSD_HEADS)], axis=0)
            if u + 1 < lanes:
                yo = [carried_out(pre[u + 1][d], hm[d]) for d in dirs]
                hw = hw_next
        for d in dirs:
            h_scrs[d][...] = hm[d]
        return carry

    lax.fori_loop(0, nc // lanes, step, 0)

    def fin_body(c, carry):
        gated = y_scr[c] * _silu(z_ref[c].astype(F32))
        o_ref[c] = _rms(gated, ng_ref[...], axis=0).astype(BF16)
        return carry

    lax.fori_loop(0, nc, fin_body, 0)


def _ssd_mixer(z_t, xbc_t, dt_t, cw, cb, dtb, alog, dsk, ng, *, layer, nc0):
    bsz, nc, _, q = xbc_t.shape
    hp = SSD_HEADS * SSD_P
    per_b = lambda f: pl.BlockSpec((None, nc, f, q), lambda b: (b, 0, 0, 0))
    full = lambda a: _layer_spec(a, layer)
    kern = functools.partial(_ssd_kernel, nc=nc, nc0=nc0)
    return pl.pallas_call(
        kern,
        grid=(bsz,),
        in_specs=[per_b(hp), per_b(SSD_CONV_DIM), per_b(2 * SSD_HEADS),
                  full(cw), full(cb), full(dtb), full(alog), full(dsk), full(ng)],
        out_specs=per_b(hp),
        out_shape=jax.ShapeDtypeStruct((bsz, nc, hp, q), BF16),
        scratch_shapes=[pltpu.VMEM((nc, SSD_CONV_DIM, q), F32), pltpu.VMEM((nc, hp, q), F32),
                        pltpu.VMEM((hp, SSD_GN), F32), pltpu.VMEM((hp, SSD_GN), F32)],
        compiler_params=_cparams("arbitrary"),
        name="ssd_mixer",
    )(z_t, xbc_t, dt_t, cw, cb, dtb, alog, dsk, ng)


def _route(logits_t, bias_col):
    aff = _sigmoid(logits_t)
    sel = aff + bias_col
    rows = [sel[e:e + 1, :] for e in range(N_EXPERTS)]
    gscore = []
    for g in range(N_GROUPS):
        a, b, c, d = rows[PER_GROUP * g:PER_GROUP * (g + 1)]
        hi1, lo1, hi2, lo2 = jnp.maximum(a, b), jnp.minimum(a, b), jnp.maximum(c, d), jnp.minimum(c, d)
        gscore.append(jnp.maximum(hi1, hi2) + jnp.maximum(jnp.minimum(hi1, hi2), jnp.maximum(lo1, lo2)))
    best = jnp.zeros_like(gscore[0], dtype=jnp.int32)
    cur = gscore[0]
    for g in range(1, N_GROUPS):
        better = gscore[g] > cur
        best = jnp.where(better, g, best)
        cur = jnp.where(better, gscore[g], cur)
    eidx = lax.broadcasted_iota(jnp.int32, sel.shape, 0)
    masked = jnp.where(eidx // PER_GROUP == best, sel, -jnp.inf)
    m1 = jnp.max(masked, axis=0, keepdims=True)
    idx1 = jnp.min(jnp.where(masked == m1, eidx, N_EXPERTS), axis=0, keepdims=True)
    masked2 = jnp.where(eidx == idx1, -jnp.inf, masked)
    m2 = jnp.max(masked2, axis=0, keepdims=True)
    idx2 = jnp.min(jnp.where(masked2 == m2, eidx, N_EXPERTS), axis=0, keepdims=True)
    oh1, oh2 = eidx == idx1, eidx == idx2
    w1 = jnp.sum(jnp.where(oh1, aff, 0.0), axis=0, keepdims=True)
    w2 = jnp.sum(jnp.where(oh2, aff, 0.0), axis=0, keepdims=True)
    den = w1 + w2
    return oh1, oh2, w1 / den, w2 / den


def _split_hi_lo(x):
    hi = x.astype(BF16)
    return hi, (x - hi.astype(F32)).astype(BF16)


def _outproj_kernel(oda_ref, ossdt_ref, omla_ref, h_ref, mod_ref, ng_ref, wo_ref, rw_ref, rb_ref,
                    h1_ref, xs_ref, meta_ref, cnt_ref, *, n_ctx_tiles, ctx_row, d, tm, s_loc, nb):
    ti, bp = pl.program_id(0), pl.program_id(1)
    tiles = range(nb)
    r_i = lax.broadcasted_iota(jnp.int32, (tm, tm), 0)
    c_i = lax.broadcasted_iota(jnp.int32, (tm, tm), 1)
    before = jnp.where(r_i < c_i, 1.0, 0.0).astype(BF16)
    row_e = lax.broadcasted_iota(jnp.int32, (N_EXPERTS, LANES), 0)
    r_s = lax.broadcasted_iota(jnp.int32, (s_loc, tm), 0).astype(F32)
    lane_e = lax.broadcasted_iota(jnp.int32, (s_loc, MOE_EXTRA), 1)
    row_m = lax.broadcasted_iota(jnp.int32, (LANES, tm), 0)
    rows = [jnp.where(ti < n_ctx_tiles, ctx_row, bp * nb + bb) for bb in tiles]
    mod_at = lambda bb, k: mod_ref[pl.ds(rows[bb], 1), pl.ds(k * d, d)]
    ossd = [jnp.concatenate([ossdt_ref[bb, c].astype(F32).T for c in range(tm // SSD_CHUNK)], axis=0).astype(BF16)
            for bb in tiles]
    mix = [_dot(oda_ref[bb], wo_ref[0:DA_WIDTH]) + _dot(ossd[bb], wo_ref[DA_WIDTH:DA_WIDTH + SSD_INNER])
           + _dot(omla_ref[bb], wo_ref[DA_WIDTH + SSD_INNER:]) for bb in tiles]
    h1 = [h_ref[bb] + mod_at(bb, 2) * mix[bb] for bb in tiles]
    for bb in tiles:
        h1_ref[bb] = h1[bb]
    u = [_ada_norm(h1[bb], ng_ref[...], mod_at(bb, 3), mod_at(bb, 4)) for bb in tiles]
    u16 = [x.astype(BF16) for x in u]
    rw_hi, rw_lo = _split_hi_lo(rw_ref[...])
    u_lo = [(u[bb] - u16[bb].astype(F32)).astype(BF16) for bb in tiles]
    logits = [_dot(u16[bb], rw_hi) + _dot(u_lo[bb], rw_hi) + _dot(u16[bb], rw_lo) for bb in tiles]
    routed = [_route(logits[bb].T[:N_EXPERTS], rb_ref[...]) for bb in tiles]
    cnt = [jnp.where(r[0], 1.0, 0.0) + jnp.where(r[1], 1.0, 0.0) for r in routed]
    rank = [_dot(cnt[bb].astype(BF16), before) for bb in tiles]
    for bb in tiles:
        oh1, oh2, w1, w2 = routed[bb]
        tot = jnp.sum(cnt[bb], axis=1, keepdims=True)
        ptot = jnp.floor((tot + (MOE_BLK - 1)) * (1.0 / MOE_BLK)) * MOE_BLK
        ptot_b = jnp.broadcast_to(ptot, (N_EXPERTS, LANES))
        cnt_ref[bb] = ptot_b
        run = jnp.zeros((1, LANES), F32)
        off = jnp.zeros((N_EXPERTS, LANES), F32)
        for e in range(1, N_EXPERTS):
            run = run + ptot_b[e - 1:e]
            off = jnp.where(row_e == e, run, off)
        slot = off[:, 0:1] + rank[bb]
        dest1 = jnp.sum(jnp.where(oh1, slot, 0.0), axis=0, keepdims=True)
        dest2 = jnp.sum(jnp.where(oh2, slot, 0.0), axis=0, keepdims=True)
        routed[bb] = (r_s == dest1, r_s == dest2, w1, w2)
        meta_ref[bb] = jnp.where(row_m == 0, dest1, jnp.where(row_m == 1, dest2, 0.0)).T
    perm = [jnp.where(routed[bb][0] | routed[bb][1], 1.0, 0.0).astype(BF16) for bb in tiles]
    xs = [_dot(perm[bb], u16[bb]) for bb in tiles]
    for bb in tiles:
        p1, p2, w1, w2 = routed[bb]
        xs_ref[bb, :, 0:d] = xs[bb].astype(BF16)
        wslot = jnp.sum(jnp.where(p1, w1, 0.0) + jnp.where(p2, w2, 0.0), axis=1, keepdims=True)
        w_hi = wslot.astype(BF16).astype(F32)
        xs_ref[bb, :, d:] = jnp.where(lane_e == 0, w_hi, jnp.where(lane_e == 1, wslot - w_hi, 0.0)).astype(BF16)


def _outproj_dispatch(oda, ossd_t, omla, h, mod, ng, wo, rw, rb, *, layer, tm, n_ctx, s_loc, nb):
    bsz, t, d = h.shape
    nt, q = t // tm, SSD_CHUNK
    tok = lambda w: pl.BlockSpec((nb, tm, w), lambda ti, b: (b, ti, 0))
    full = lambda a: pl.BlockSpec(a.shape, lambda ti, b: (0,) * a.ndim)
    tile = lambda r, w: pl.BlockSpec((nb, None, r, w), lambda ti, b: (b, ti, 0, 0))
    lay = lambda a: _layer_spec(a, layer)
    kern = functools.partial(_outproj_kernel, n_ctx_tiles=n_ctx // tm, ctx_row=bsz, d=d, tm=tm, s_loc=s_loc, nb=nb)
    return pl.pallas_call(
        kern,
        grid=(nt, bsz // nb),
        in_specs=[tok(DA_WIDTH), pl.BlockSpec((nb, tm // q, SSD_INNER, q), lambda ti, b: (b, ti, 0, 0)),
                  tok(MLA_WIDTH), tok(d), lay(mod), lay(ng), lay(wo), full(rw), full(rb)],
        out_specs=[tok(d), tile(s_loc, d + MOE_EXTRA), tile(tm, LANES), tile(N_EXPERTS, LANES)],
        out_shape=[jax.ShapeDtypeStruct((bsz, t, d), F32),
                   jax.ShapeDtypeStruct((bsz, nt, s_loc, d + MOE_EXTRA), BF16),
                   jax.ShapeDtypeStruct((bsz, nt, tm, LANES), F32),
                   jax.ShapeDtypeStruct((bsz, nt, N_EXPERTS, LANES), F32)],
        compiler_params=_cparams("arbitrary", "arbitrary"),
        name="outproj_dispatch",
    )(oda, ossd_t, omla, h, mod, ng, wo, rw, rb)


def _expert_kernel(se_ref, bi_ref, sv_ref, sn_ref, *refs, d):
    x_refs = refs[:MOE_STEP_BLKS]
    wg_ref, wu_ref, wd_ref, y_ref, wg16, wu16, wd16 = refs[MOE_STEP_BLKS:]
    s = pl.program_id(0)

    @pl.when(sn_ref[s] > 0)
    def _():
        wg16[...] = wg_ref[...].astype(BF16)
        wu16[...] = wu_ref[...].astype(BF16)
        wd16[...] = wd_ref[...].astype(BF16)

    @pl.when(sv_ref[s] > 0)
    def _():
        x = jnp.concatenate([r[...] for r in x_refs], axis=0)
        xm = x[:, :d]
        wrow = x[:, d:d + 1].astype(F32) + x[:, d + 1:d + 2].astype(F32)
        he = _silu(_dot(xm, wg16[...])) * _dot(xm, wu16[...])
        y_ref[...] = (_dot(he.astype(BF16), wd16[...]) * wrow).astype(BF16)

    @pl.when(sv_ref[s] == 0)
    def _():
        y_ref[...] = jnp.zeros_like(y_ref)


def _expert_ffn(xs2d, wg, wu, wd, step_e, blk_ids, step_valid, step_new, *, layer, n_steps):
    d = wg.shape[2]
    rows = MOE_STEP_BLKS * MOE_BLK
    x_specs = [pl.BlockSpec((MOE_BLK, d + MOE_EXTRA),
                            lambda s, se, bi, sv, sn, j=j: (bi[s * MOE_STEP_BLKS + j], 0))
               for j in range(MOE_STEP_BLKS)]
    wspec = lambda a: pl.BlockSpec((None, None) + a.shape[2:], lambda s, se, bi, sv, sn: (layer, se[s], 0, 0))
    grid_spec = pltpu.PrefetchScalarGridSpec(
        num_scalar_prefetch=4,
        grid=(n_steps,),
        in_specs=x_specs + [wspec(wg), wspec(wu), wspec(wd)],
        out_specs=pl.BlockSpec((rows, d), lambda s, se, bi, sv, sn: (s, 0)),
        scratch_shapes=[pltpu.VMEM(wg.shape[2:], BF16), pltpu.VMEM(wu.shape[2:], BF16),
                        pltpu.VMEM(wd.shape[2:], BF16)],
    )
    return pl.pallas_call(
        functools.partial(_expert_kernel, d=d),
        grid_spec=grid_spec,
        out_shape=jax.ShapeDtypeStruct((n_steps * rows, d), BF16),
        compiler_params=_cparams("arbitrary"),
        name="expert_ffn",
    )(step_e, blk_ids, step_valid, step_new, *([xs2d] * MOE_STEP_BLKS), wg, wu, wd)


def _combine_kernel(inv_ref, *refs, n_lb, n_ctx_tiles, ctx_row, d, tm, s_loc, final, nb):
    y_refs = refs[:nb * n_lb]
    meta_ref, h1_ref, mod_ref, fg_ref, o_ref = refs[nb * n_lb:]
    ti, bp = pl.program_id(0), pl.program_id(1)
    tiles = range(nb)
    lane_s = lax.broadcasted_iota(jnp.int32, (tm, s_loc), 1).astype(F32)
    metas = [meta_ref[bb] for bb in tiles]
    pts = [jnp.where((lane_s == m[:, 0:1]) | (lane_s == m[:, 1:2]), 1.0, 0.0).astype(BF16) for m in metas]
    ys = [jnp.concatenate([r[...] for r in y_refs[bb * n_lb:(bb + 1) * n_lb]], axis=0) for bb in tiles]
    y = [_dot(pts[bb], ys[bb]) for bb in tiles]
    for bb in tiles:
        row = jnp.where(ti < n_ctx_tiles, ctx_row, bp * nb + bb)
        h2 = h1_ref[bb] + mod_ref[pl.ds(row, 1), pl.ds(5 * d, d)] * y[bb]
        o_ref[bb] = _rms(h2, fg_ref[...]) if final else h2


def _combine(ys_em, inv, meta, h1, mod, fg, *, layer, tm, n_ctx, s_loc, final, nb):
    bsz, t, d = h1.shape
    nt = t // tm
    n_lb = s_loc // MOE_BLK
    y_specs = [pl.BlockSpec((MOE_BLK, d), lambda ti, b, inv, bb=bb, j=j: (inv[((b * nb + bb) * nt + ti) * n_lb + j], 0))
               for bb in range(nb) for j in range(n_lb)]
    full = lambda a: pl.BlockSpec(a.shape, lambda ti, b, inv: (0,) * a.ndim)
    grid_spec = pltpu.PrefetchScalarGridSpec(
        num_scalar_prefetch=1,
        grid=(nt, bsz // nb),
        in_specs=y_specs + [pl.BlockSpec((nb, None, tm, LANES), lambda ti, b, inv: (b, ti, 0, 0)),
                            pl.BlockSpec((nb, tm, d), lambda ti, b, inv: (b, ti, 0)),
                            _layer_spec(mod, layer), full(fg)],
        out_specs=pl.BlockSpec((nb, tm, d), lambda ti, b, inv: (b, ti, 0)),
    )
    kern = functools.partial(_combine_kernel, n_lb=n_lb, n_ctx_tiles=n_ctx // tm, ctx_row=bsz, d=d, tm=tm,
                             s_loc=s_loc, final=final, nb=nb)
    return pl.pallas_call(
        kern,
        grid_spec=grid_spec,
        out_shape=jax.ShapeDtypeStruct((bsz, t, d), F32),
        compiler_params=_cparams("arbitrary", "arbitrary"),
        name="moe_combine",
    )(inv, *([ys_em] * (nb * n_lb)), meta, h1, mod, fg)


def _rope_tables(length, dim):
    rows = length // GRID_W
    row = jnp.repeat(jnp.arange(rows), GRID_W).astype(F32)
    col = jnp.tile(jnp.arange(GRID_W), rows).astype(F32)
    n_freq = dim // 4
    inv_freq = ROPE_THETA ** (-jnp.arange(n_freq, dtype=F32) / n_freq)
    ang = jnp.concatenate([row[:, None] * inv_freq, col[:, None] * inv_freq], axis=-1)
    return jnp.cos(ang), jnp.sin(ang)


def _table_set(n_ctx, n_lat):
    def lanes(cos, sin, lane0, width, reps, outside_cos):
        c = jnp.concatenate([cos, cos], axis=-1)
        s = jnp.concatenate([-sin, sin], axis=-1)
        grp_c = jnp.full((n_lat, width), outside_cos, F32).at[:, lane0:lane0 + c.shape[1]].set(c)
        grp_s = jnp.zeros((n_lat, width), F32).at[:, lane0:lane0 + s.shape[1]].set(s)
        ctx_c = jnp.full((n_ctx, width), outside_cos, F32).at[:, lane0:lane0 + c.shape[1]].set(1.0)
        ctx_s = jnp.zeros((n_ctx, width), F32)
        return (jnp.tile(jnp.concatenate([ctx_c, grp_c], axis=0), (1, reps)),
                jnp.tile(jnp.concatenate([ctx_s, grp_s], axis=0), (1, reps)))

    dcos, dsin = _rope_tables(n_lat, DA_QK)
    cda, sda = lanes(dcos, dsin, 0, DA_QK, DA_QW // DA_QK, 1.0)
    mcos, msin = _rope_tables(n_lat, MLA_ROPE)
    cq, sm = lanes(mcos, msin, KR_LANE0, LANES, 1, 1.0)
    ck, _ = lanes(mcos, msin, KR_LANE0, LANES, 1, 0.0)
    return cda, sda, cq, ck, sm


def _pack_w_in(w_in):
    w_in = w_in.astype(BF16)
    depth, d, _ = w_in.shape
    o_ssd = DA_IN
    o_mla = DA_IN + SSD_IN
    z = w_in[..., o_ssd:o_ssd + SSD_INNER]
    xbc = w_in[..., o_ssd + SSD_INNER:o_ssd + SSD_INNER + SSD_CONV_DIM]
    dt = w_in[..., o_ssd + SSD_INNER + SSD_CONV_DIM:o_mla]
    cq = w_in[..., o_mla:o_mla + MLA_Q_RANK]
    ckv = w_in[..., o_mla + MLA_Q_RANK:o_mla + MLA_Q_RANK + MLA_KV_RANK]
    kr = w_in[..., o_mla + MLA_Q_RANK + MLA_KV_RANK:]
    zeros = lambda n: jnp.zeros((depth, d, n), w_in.dtype)
    misc = jnp.concatenate([zeros(KR_LANE0), kr, dt, zeros(LANES - DT_LANE0 - 2 * SSD_HEADS)], axis=-1)
    return jnp.concatenate([w_in[..., :DA_IN], z, xbc, cq, ckv, misc], axis=-1)


def _pack_w_uq(w_uq):
    depth, r, _ = w_uq.shape
    w = w_uq.reshape(depth, r, MLA_HEADS, MLA_NOPE + MLA_ROPE)
    w = jnp.pad(w, ((0, 0), (0, 0), (0, 0), (0, MLA_HEAD_PAD - MLA_NOPE - MLA_ROPE)))
    return w.reshape(depth, r, MLA_QPAD).astype(BF16)


def _pack_w_ukv(w_ukv):
    depth, r, _ = w_ukv.shape
    w = w_ukv.reshape(depth, r, MLA_HEADS, MLA_NOPE + MLA_V)
    kn = jnp.pad(w[..., :MLA_NOPE], ((0, 0), (0, 0), (0, 0), (0, MLA_HEAD_PAD - MLA_NOPE)))
    return jnp.concatenate([kn.reshape(depth, r, MLA_QPAD), w[..., MLA_NOPE:].reshape(depth, r, MLA_WIDTH)],
                           axis=-1).astype(BF16)


def _dispatch_tables(pcnt, n_lb, n_steps):
    ntt = pcnt.shape[0]
    nb = (pcnt / MOE_BLK).astype(jnp.int32)
    lo = jnp.cumsum(nb, axis=1) - nb
    n_e = jnp.sum(nb, axis=0)
    p_e = (n_e + MOE_STEP_BLKS - 1) // MOE_STEP_BLKS * MOE_STEP_BLKS
    ends = jnp.cumsum(p_e)
    base = (ends - p_e)[None, :] + jnp.cumsum(nb, axis=0) - nb
    lb = jnp.arange(n_lb, dtype=jnp.int32)
    owner = (lb[None, :, None] >= lo[:, None, :]) & (lb[None, :, None] < (lo + nb)[:, None, :])
    used = jnp.any(owner, axis=-1)
    pos = jnp.sum(jnp.where(owner, (base - lo)[:, None, :], 0), axis=-1) + lb[None, :]
    inv = jnp.where(used, pos, 0).astype(jnp.int32)
    n_pos = n_steps * MOE_STEP_BLKS
    src = jnp.arange(ntt, dtype=jnp.int32)[:, None] * n_lb + lb[None, :]
    fwd = jnp.zeros((n_pos,), jnp.int32).at[jnp.where(used, pos, n_pos).reshape(-1)].set(src.reshape(-1), mode="drop")
    starts = jnp.arange(n_steps, dtype=jnp.int32) * MOE_STEP_BLKS
    step_e = jnp.minimum(jnp.sum(starts[:, None] >= ends[None, :], axis=1), N_EXPERTS - 1).astype(jnp.int32)
    step_valid = (starts < ends[-1]).astype(jnp.int32)
    step_new = jnp.concatenate([jnp.ones((1,), jnp.int32), (step_e[1:] != step_e[:-1]).astype(jnp.int32)])
    fwd2 = fwd.reshape(n_steps, MOE_STEP_BLKS)
    last = lax.dynamic_slice_in_dim(fwd2, jnp.maximum(ends[-1] // MOE_STEP_BLKS - 1, 0), 1, axis=0)
    fwd = jnp.where(step_valid[:, None] > 0, fwd2, last).reshape(-1)
    return inv.reshape(-1), fwd, step_e, step_valid, step_new


def kernel(x, c, ctx, c_ctx, norm_mix_g, norm_ffn_g, w_mod, b_mod, w_in, w_out, da_lambda, da_subln_g, ssd_conv_w, ssd_conv_b, ssd_dt_bias, ssd_a_log, ssd_d, ssd_norm_g, mla_q_norm_g, mla_kv_norm_g, mla_w_uq, mla_w_ukv, router_w, router_bias, exp_w_gate, exp_w_up, exp_w_down, final_norm_g):
    bsz, n_lat, d = x.shape
    n_ctx = ctx.shape[1]
    t = n_ctx + n_lat
    tm = min(256, n_ctx)
    assert n_ctx % tm == 0 and n_lat % tm == 0 and n_ctx % SSD_CHUNK == 0 and n_lat % SSD_CHUNK == 0
    nt = t // tm
    s_loc = 2 * tm + 2 * LANES
    n_lb = s_loc // MOE_BLK
    n_steps = -(-(bsz * nt * n_lb + N_EXPERTS * (MOE_STEP_BLKS - 1)) // MOE_STEP_BLKS)
    nc, nc0 = t // SSD_CHUNK, n_ctx // SSD_CHUNK
    nb = 2 if bsz % 2 == 0 else 1
    nb_wide = 4 if bsz % 4 == 0 else nb
    q = SSD_CHUNK

    r_pad = -(-(bsz + 1) // 8) * 8
    c_rows = jnp.concatenate([c, c_ctx[None, :], jnp.zeros((r_pad - bsz - 1, d), F32)], axis=0)
    mod = _modulation(c_rows, w_mod, b_mod)

    w_in_p = _pack_w_in(w_in)
    w_uq_p = _pack_w_uq(mla_w_uq)
    w_ukv_p = _pack_w_ukv(mla_w_ukv)
    w_out16 = w_out.astype(BF16)
    tabs = _table_set(n_ctx, n_lat)
    rw_pad = jnp.pad(router_w, ((0, 0), (0, LANES - N_EXPERTS)))
    rb = router_bias.reshape(N_EXPERTS, 1)
    lane_b = lambda v: jnp.broadcast_to(v[..., None], v.shape + (q,))
    dsk_rows = jnp.repeat(ssd_d, SSD_P, axis=-1)
    ng_mix, ng_ffn = norm_mix_g[:, None, :], norm_ffn_g[:, None, :]
    qg, kvg = mla_q_norm_g[:, None, :], mla_kv_norm_g[:, None, :]
    sub_g = jnp.tile(da_subln_g, (1, DA_HEADS))[:, :, None]
    ssd_par = (lane_b(ssd_conv_w), lane_b(ssd_conv_b), lane_b(ssd_dt_bias.reshape(DEPTH, -1)),
               lane_b(ssd_a_log.reshape(DEPTH, -1)), lane_b(dsk_rows), lane_b(ssd_norm_g))

    h = jnp.concatenate([ctx, x], axis=1)
    for i in range(DEPTH):
        lam_init = 0.8 - 0.6 * math.exp(-0.3 * i)
        qda_t, kda, vda_t, z_t, xbc_t, dt_t, q4t, k4, vm_t = _inproj(
            h, mod, ng_mix, w_in_p, tabs, qg, kvg, w_uq_p, w_ukv_p, layer=i, tm=tm, n_ctx=n_ctx, nb=nb_wide)
        o_da = _da_attention(qda_t, kda, vda_t, da_lambda, sub_g, layer=i, lam_init=lam_init, tq=tm, n_ctx=n_ctx)
        o_mla = _mla_attention(q4t, k4, vm_t, tq=tm, n_ctx=n_ctx)
        o_ssd_t = _ssd_mixer(z_t, xbc_t, dt_t, *ssd_par, layer=i, nc0=nc0)
        h1, xs, meta, cnt = _outproj_dispatch(o_da, o_ssd_t, o_mla, h, mod, ng_ffn, w_out16, rw_pad, rb, layer=i, tm=tm,
                                              n_ctx=n_ctx, s_loc=s_loc, nb=nb_wide)
        inv, fwd, step_e, step_valid, step_new = _dispatch_tables(cnt[:, :, :, 0].reshape(bsz * nt, N_EXPERTS), n_lb,
                                                                  n_steps)
        ys_em = _expert_ffn(xs.reshape(bsz * nt * s_loc, d + MOE_EXTRA), exp_w_gate, exp_w_up, exp_w_down, step_e, fwd,
                            step_valid, step_new, layer=i, n_steps=n_steps)
        h = _combine(ys_em, inv, meta, h1, mod, final_norm_g[None, :], layer=i, tm=tm, n_ctx=n_ctx, s_loc=s_loc,
                     final=(i == DEPTH - 1), nb=nb)
    return h[:, n_ctx:]
```

```python
import functools
import math

import jax
import jax.numpy as jnp
from jax import lax
from jax.experimental import pallas as pl
from jax.experimental.pallas import tpu as pltpu

F32 = jnp.float32
BF16 = jnp.bfloat16

DEPTH = 4
GRID_W = 64
EPS = 1e-6
ROPE_THETA = 10000.0
DA_HEADS, DA_QK = 4, 32
DA_V = 2 * DA_QK
DA_WIDTH = DA_HEADS * DA_V
DA_QW = DA_HEADS * 2 * DA_QK
DA_IN = 2 * DA_QW + DA_WIDTH
SSD_HEADS, SSD_P, SSD_GROUPS, SSD_STATE, SSD_CHUNK = 8, 64, 2, 64, 128
SSD_INNER = SSD_HEADS * SSD_P
SSD_GN = SSD_GROUPS * SSD_STATE
SSD_CONV_DIM = SSD_INNER + 2 * SSD_GN
SSD_IN = SSD_INNER + SSD_CONV_DIM + 2 * SSD_HEADS
MLA_HEADS, MLA_Q_RANK, MLA_KV_RANK, MLA_NOPE, MLA_ROPE, MLA_V = 4, 256, 128, 64, 32, 64
MLA_WIDTH = MLA_HEADS * MLA_V
MLA_IN = MLA_Q_RANK + MLA_KV_RANK + MLA_ROPE
MLA_SCALE = (MLA_NOPE + MLA_ROPE) ** -0.5
N_EXPERTS, N_GROUPS, D_EXPERT = 16, 4, 512
PER_GROUP = N_EXPERTS // N_GROUPS

LANES = 128
SUBLANES_BF16 = 16
VMEM_LIMIT_BYTES = 56 * 1024 * 1024

MLA_HEAD_PAD = LANES
MLA_QPAD = MLA_HEADS * MLA_HEAD_PAD
KR_LANE0 = MLA_NOPE
DT_LANE0 = MLA_NOPE + MLA_ROPE
IN_COLS = DA_IN + SSD_INNER + SSD_CONV_DIM + MLA_Q_RANK + MLA_KV_RANK + LANES
C_Z = DA_IN
C_XBC = C_Z + SSD_INNER
C_CQ = C_XBC + SSD_CONV_DIM
C_CKV = C_CQ + MLA_Q_RANK
C_MISC = C_CKV + MLA_KV_RANK
LOG2E = math.log2(math.e)
DA_QSCALE = DA_QK ** -0.5 * LOG2E
MLA_QSCALE = MLA_SCALE * LOG2E

KEY_CHUNK = 256
MOE_BLK = SUBLANES_BF16
MOE_STEP_BLKS = 32
MOE_EXTRA = LANES


def _sigmoid(x):
    return 1.0 / (1.0 + jnp.exp(-x))


def _silu(x):
    return x * _sigmoid(x)


def _rms(x, g, axis=-1):
    return x * lax.rsqrt(jnp.mean(x * x, axis=axis, keepdims=True) + EPS) * g


def _ada_norm(hv, g, shift, scale):
    return _rms(hv, g) * (1.0 + scale) + shift


def _rope(x, cos, sin_signed, half):
    w = x.shape[-1]
    lane = lax.broadcasted_iota(jnp.int32, x.shape, x.ndim - 1)
    first = (lane % (2 * half)) < half
    partner = jnp.where(first, pltpu.roll(x, w - half, x.ndim - 1), pltpu.roll(x, half, x.ndim - 1))
    return x * cos + partner * sin_signed


def _dot(a, b):
    return jnp.dot(a, b, preferred_element_type=F32)


def _dot_hi(a, b):
    return jnp.dot(a, b, preferred_element_type=F32, precision=lax.Precision.HIGHEST)


def _dot_split3(x, m01):
    x1 = x.astype(BF16)
    r1 = x - x1.astype(F32)
    x2 = r1.astype(BF16)
    x3 = (r1 - x2.astype(F32)).astype(BF16)
    return _dot(x1, m01) + _dot(x2, m01) + _dot(x3, m01)


def _layer_spec(a, i):
    return pl.BlockSpec((None,) + a.shape[1:], lambda *_: (i,) + (0,) * (a.ndim - 1))


def _cparams(*sem):
    return pltpu.CompilerParams(dimension_semantics=sem, vmem_limit_bytes=VMEM_LIMIT_BYTES)


def _mod_kernel(c_ref, w_ref, b_ref, o_ref):
    o_ref[...] = _dot_hi(_silu(c_ref[...]), w_ref[...]) + b_ref[...]


def _modulation(c_rows, w_mod, b_mod):
    depth, d, n = w_mod.shape
    r = c_rows.shape[0]
    tn = 1536
    return pl.pallas_call(
        _mod_kernel,
        grid=(depth, n // tn),
        in_specs=[pl.BlockSpec((r, d), lambda l, j: (0, 0)),
                  pl.BlockSpec((None, d, tn), lambda l, j: (l, 0, j)),
                  pl.BlockSpec((None, 1, tn), lambda l, j: (l, 0, j))],
        out_specs=pl.BlockSpec((None, r, tn), lambda l, j: (l, 0, j)),
        out_shape=jax.ShapeDtypeStruct((depth, r, n), F32),
        compiler_params=_cparams("arbitrary", "arbitrary"),
        name="modulation",
    )(c_rows, w_mod, b_mod.reshape(depth, 1, n))


def _inproj_kernel(h_ref, mod_ref, ng_ref, w_ref, cda_ref, sda_ref, cq_ref, ck_ref, sm_ref,
                   qg_ref, kvg_ref, wuq_ref, wukv_ref,
                   qdat_ref, kda_ref, vdat_ref, zt_ref, xbct_ref, dtt_ref, q4t_ref, k4_ref, vmt_ref,
                   *, n_ctx_tiles, ctx_row, d, nb, tm):
    ti, bp = pl.program_id(0), pl.program_id(1)
    cda, sda, sm = cda_ref[...], sda_ref[...], sm_ref[...]
    cos_q = jnp.concatenate([cq_ref[...]] * MLA_HEADS, axis=1)
    sin_q = jnp.concatenate([sm] * MLA_HEADS, axis=1)
    q = SSD_CHUNK
    accs = []
    for bb in range(nb):
        row = jnp.where(ti < n_ctx_tiles, ctx_row, bp * nb + bb)
        shift = mod_ref[pl.ds(row, 1), pl.ds(0, d)]
        scale = mod_ref[pl.ds(row, 1), pl.ds(d, d)]
        u = _ada_norm(h_ref[bb], ng_ref[...], shift, scale).astype(BF16)
        accs.append(_dot(u, w_ref[...]))
    for bb in range(nb):
        acc = accs[bb]
        qdat_ref[bb] = (_rope(acc[:, 0:DA_QW], cda, sda, DA_QK // 2) * DA_QSCALE).T.astype(BF16)
        kda_ref[bb] = _rope(acc[:, DA_QW:2 * DA_QW], cda, sda, DA_QK // 2).astype(BF16)
        vdat_ref[bb] = acc[:, 2 * DA_QW:DA_IN].T.astype(BF16)
        z_t = acc[:, C_Z:C_XBC].T
        xbc_t = acc[:, C_XBC:C_CQ].T
        misc = acc[:, C_MISC:IN_COLS]
        dt_t = misc.T[DT_LANE0:DT_LANE0 + 2 * SSD_HEADS]
        for c in range(tm // q):
            zt_ref[bb, c] = z_t[:, c * q:(c + 1) * q].astype(BF16)
            xbct_ref[bb, c] = xbc_t[:, c * q:(c + 1) * q]
            dtt_ref[bb, c] = dt_t[:, c * q:(c + 1) * q]
        cqn = _rms(acc[:, C_CQ:C_CKV], qg_ref[...]).astype(BF16)
        qm_t = (_rope(_dot(cqn, wuq_ref[...]), cos_q, sin_q, MLA_ROPE // 2) * MLA_QSCALE).T
        ckvn = _rms(acc[:, C_CKV:C_MISC], kvg_ref[...]).astype(BF16)
        kv = _dot(ckvn, wukv_ref[...])
        kr = _rope(misc, ck_ref[...], sm, MLA_ROPE // 2)
        km = kv[:, :MLA_QPAD] + jnp.concatenate([kr] * MLA_HEADS, axis=1)
        for hh in range(MLA_HEADS):
            q4t_ref[bb, hh] = qm_t[hh * MLA_HEAD_PAD:(hh + 1) * MLA_HEAD_PAD].astype(BF16)
            k4_ref[bb, hh] = km[:, hh * MLA_HEAD_PAD:(hh + 1) * MLA_HEAD_PAD].astype(BF16)
        vmt_ref[bb] = kv[:, MLA_QPAD:].T.astype(BF16)


def _inproj(h, mod, ng, w_in_p, tabs, qg, kvg, wuq_p, wukv_p, *, layer, tm, n_ctx, nb):
    bsz, t, d = h.shape
    nt, q = t // tm, SSD_CHUNK
    cpt = tm // q
    tok = lambda w: pl.BlockSpec((nb, tm, w), lambda ti, b: (b, ti, 0))
    tab = lambda w: pl.BlockSpec((tm, w), lambda ti, b: (ti, 0))
    chunked = lambda f: pl.BlockSpec((nb, cpt, f, q), lambda ti, b: (b, ti, 0, 0))
    lay = lambda a: _layer_spec(a, layer)
    cda, sda, cq, ck, sm = tabs
    sds = jax.ShapeDtypeStruct
    tok_t = lambda w: pl.BlockSpec((nb, w, tm), lambda ti, b: (b, 0, ti))
    out_specs = [tok_t(DA_QW), tok(DA_QW), tok_t(DA_WIDTH),
                 chunked(SSD_INNER), chunked(SSD_CONV_DIM), chunked(2 * SSD_HEADS),
                 pl.BlockSpec((nb, MLA_HEADS, MLA_HEAD_PAD, tm), lambda ti, b: (b, 0, 0, ti)),
                 pl.BlockSpec((nb, MLA_HEADS, tm, MLA_HEAD_PAD), lambda ti, b: (b, 0, ti, 0)), tok_t(MLA_WIDTH)]
    out_shape = [sds((bsz, DA_QW, t), BF16), sds((bsz, t, DA_QW), BF16), sds((bsz, DA_WIDTH, t), BF16),
                 sds((bsz, t // q, SSD_INNER, q), BF16), sds((bsz, t // q, SSD_CONV_DIM, q), F32),
                 sds((bsz, t // q, 2 * SSD_HEADS, q), F32),
                 sds((bsz, MLA_HEADS, MLA_HEAD_PAD, t), BF16), sds((bsz, MLA_HEADS, t, MLA_HEAD_PAD), BF16),
                 sds((bsz, MLA_WIDTH, t), BF16)]
    kern = functools.partial(_inproj_kernel, n_ctx_tiles=n_ctx // tm, ctx_row=bsz, d=d, nb=nb, tm=tm)
    return pl.pallas_call(
        kern,
        grid=(nt, bsz // nb),
        in_specs=[tok(d), lay(mod), lay(ng), lay(w_in_p), tab(DA_QW), tab(DA_QW), tab(LANES), tab(LANES),
                  tab(LANES), lay(qg), lay(kvg), lay(wuq_p), lay(wukv_p)],
        out_specs=out_specs,
        out_shape=out_shape,
        compiler_params=_cparams("arbitrary", "arbitrary"),
        name="inproj",
    )(h, mod, ng, w_in_p, cda, sda, cq, ck, sm, qg, kvg, wuq_p, wukv_p)


V_AUG = DA_V + SUBLANES_BF16


def _scores_pass(k_at, qtm, nk, s_scr):
    kc = KEY_CHUNK if nk % KEY_CHUNK == 0 else LANES
    m = None
    for c0 in range(0, nk, kc):
        s_c = _dot(k_at(c0, c0 + kc), qtm)
        s_scr[c0:c0 + kc, :] = s_c
        part = jnp.max(s_c, axis=0, keepdims=True)
        m = part if m is None else jnp.maximum(m, part)
    return m


def _pv_pass(vaug_at, nk, s_scr, m):
    kc = KEY_CHUNK if nk % KEY_CHUNK == 0 else LANES
    acc = None
    for c0 in range(0, nk, kc):
        e = jnp.exp2(s_scr[c0:c0 + kc, :] - m).astype(BF16)
        inc = _dot(vaug_at(c0, c0 + kc), e)
        acc = inc if acc is None else acc + inc
    return acc


def _attend_heads(k_of, qt_of, vaug_of, nk, scr, n_sub):
    m = _scores_pass(k_of(0), qt_of(0), nk, scr[0])
    outs = []
    for j in range(n_sub):
        if j + 1 < n_sub:
            m_next = _scores_pass(k_of(j + 1), qt_of(j + 1), nk, scr[(j + 1) % 2])
        acc = _pv_pass(vaug_of(j), nk, scr[j % 2], m)
        outs.append(acc[:DA_V] * (1.0 / acc[DA_V:DA_V + 1]))
        if j + 1 < n_sub:
            m = m_next
    return outs


def _fill_vaug(vt_ref, vaug_scr, n_heads):
    t = vt_ref.shape[1]
    for hh in range(n_heads):
        vaug_scr[hh, 0:DA_V, :] = vt_ref[hh * DA_V:(hh + 1) * DA_V, :]
        vaug_scr[hh, DA_V:V_AUG, :] = jnp.ones((V_AUG - DA_V, t), BF16)


def _da_attn_kernel(lam_ref, g_ref, qt_ref, k_ref, vt_ref, o_ref, s0_scr, s1_scr, vaug_scr, *, n_ctx, n_ctx_tiles,
                    lam_init, tq):
    qi = pl.program_id(1)

    @pl.when(qi == 0)
    def _():
        _fill_vaug(vt_ref, vaug_scr, DA_HEADS)

    lv = lam_ref[...]
    lam = (jnp.exp(jnp.sum(lv[0:1] * lv[1:2], axis=-1, keepdims=True))
           - jnp.exp(jnp.sum(lv[2:3] * lv[3:4], axis=-1, keepdims=True)) + lam_init)
    row_q = lax.broadcasted_iota(jnp.int32, (DA_QW, 1), 0)

    def attend(nk):
        qt = qt_ref[...]
        qt_of = lambda j: qt * jnp.where(row_q // DA_QK == j, 1.0, 0.0).astype(BF16)
        k_of = lambda j: (lambda c0, c1: k_ref[c0:c1, :])
        vaug_of = lambda j: (lambda c0, c1: vaug_scr[j // 2, :, c0:c1])
        outs = _attend_heads(k_of, qt_of, vaug_of, nk, (s0_scr, s1_scr), 2 * DA_HEADS)
        heads = []
        for hh in range(DA_HEADS):
            o = outs[2 * hh] - lam * outs[2 * hh + 1]
            heads.append(o * lax.rsqrt(jnp.mean(o * o, axis=0, keepdims=True) + EPS))
        o_t = jnp.concatenate(heads, axis=0) * g_ref[...] * (1.0 - lam_init)
        o_ref[...] = o_t.T.astype(BF16)

    @pl.when(qi < n_ctx_tiles)
    def _():
        attend(n_ctx)

    @pl.when(qi >= n_ctx_tiles)
    def _():
        attend(k_ref.shape[0])


def _da_attention(qt, k, vt, lam_vec, g_col, *, layer, lam_init, tq, n_ctx):
    bsz, _, t = qt.shape
    kern = functools.partial(_da_attn_kernel, n_ctx=n_ctx, n_ctx_tiles=n_ctx // tq, lam_init=lam_init, tq=tq)
    return pl.pallas_call(
        kern,
        grid=(bsz, t // tq),
        in_specs=[_layer_spec(lam_vec, layer), _layer_spec(g_col, layer),
                  pl.BlockSpec((None, DA_QW, tq), lambda b, i: (b, 0, i)),
                  pl.BlockSpec((None, t, DA_QW), lambda b, i: (b, 0, 0)),
                  pl.BlockSpec((None, DA_WIDTH, t), lambda b, i: (b, 0, 0))],
        out_specs=pl.BlockSpec((None, tq, DA_WIDTH), lambda b, i: (b, i, 0)),
        out_shape=jax.ShapeDtypeStruct((bsz, t, DA_WIDTH), BF16),
        scratch_shapes=[pltpu.VMEM((t, tq), F32), pltpu.VMEM((t, tq), F32), pltpu.VMEM((DA_HEADS, V_AUG, t), BF16)],
        compiler_params=_cparams("arbitrary", "arbitrary"),
        name="da_attention",
    )(lam_vec, g_col, qt, k, vt)


def _mla_attn_kernel(qt_ref, k_ref, vt_ref, o_ref, s0_scr, s1_scr, vaug_scr, *, n_ctx, n_ctx_tiles, tq):
    qi = pl.program_id(1)

    @pl.when(qi == 0)
    def _():
        _fill_vaug(vt_ref, vaug_scr, MLA_HEADS)

    def attend(nk):
        k_of = lambda hh: (lambda c0, c1: k_ref[hh, c0:c1, :])
        vaug_of = lambda hh: (lambda c0, c1: vaug_scr[hh, :, c0:c1])
        outs = _attend_heads(k_of, lambda hh: qt_ref[hh], vaug_of, nk, (s0_scr, s1_scr), MLA_HEADS)
        o_ref[...] = jnp.concatenate(outs, axis=0).T.astype(BF16)

    @pl.when(qi < n_ctx_tiles)
    def _():
        attend(n_ctx)

    @pl.when(qi >= n_ctx_tiles)
    def _():
        attend(k_ref.shape[1])


def _mla_attention(q4t, k4, vt, *, tq, n_ctx):
    bsz, _, _, t = q4t.shape
    kern = functools.partial(_mla_attn_kernel, n_ctx=n_ctx, n_ctx_tiles=n_ctx // tq, tq=tq)
    return pl.pallas_call(
        kern,
        grid=(bsz, t // tq),
        in_specs=[pl.BlockSpec((None, MLA_HEADS, MLA_HEAD_PAD, tq), lambda b, i: (b, 0, 0, i)),
                  pl.BlockSpec((None, MLA_HEADS, t, MLA_HEAD_PAD), lambda b, i: (b, 0, 0, 0)),
                  pl.BlockSpec((None, MLA_WIDTH, t), lambda b, i: (b, 0, 0))],
        out_specs=pl.BlockSpec((None, tq, MLA_WIDTH), lambda b, i: (b, i, 0)),
        out_shape=jax.ShapeDtypeStruct((bsz, t, MLA_WIDTH), BF16),
        scratch_shapes=[pltpu.VMEM((t, tq), F32), pltpu.VMEM((t, tq), F32), pltpu.VMEM((MLA_HEADS, V_AUG, t), BF16)],
        compiler_params=_cparams("arbitrary", "arbitrary"),
        name="mla_attention",
    )(q4t, k4, vt)


def _ssd_kernel(z_ref, xbc_ref, dt_ref, cw_ref, cb_ref, dtb_ref, alog_ref, dsk_ref, ng_ref,
                o_ref, xc_scr, y_scr, hf_scr, hb_scr, *, nc, nc0):
    q = SSD_CHUNK
    hp = SSD_HEADS * SSD_P
    lane_x = lax.broadcasted_iota(jnp.int32, (SSD_CONV_DIM, q), 1)

    def conv_body(c, carry):
        xc = xbc_ref[c]
        keep_prev = jnp.where((c == 0) | (c == nc0), 0.0, 1.0)
        keep_next = jnp.where((c == nc0 - 1) | (c == nc - 1), 0.0, 1.0)
        xp = xbc_ref[jnp.maximum(c - 1, 0)] * keep_prev
        xn = xbc_ref[jnp.minimum(c + 1, nc - 1)] * keep_next
        prev = pltpu.roll(jnp.where(lane_x == q - 1, xp, xc), 1, 1)
        nxt = pltpu.roll(jnp.where(lane_x == 0, xn, xc), q - 1, 1)
        a = _silu(cw_ref[0] * prev + cw_ref[1] * xc + cw_ref[2] * nxt + cb_ref[...])
        xc_scr[c] = a
        y_scr[c] = dsk_ref[...] * a[:hp]
        return carry

    lax.fori_loop(0, nc, conv_body, 0)
    hf_scr[...] = jnp.zeros_like(hf_scr)
    hb_scr[...] = jnp.zeros_like(hb_scr)

    sub = lax.broadcasted_iota(jnp.int32, (q, q), 0)
    lan = lax.broadcasted_iota(jnp.int32, (q, q), 1)
    lane_n = lax.broadcasted_iota(jnp.int32, (1, SSD_GN), 1)
    pad_rows = jnp.zeros((q - SSD_HEADS, q), F32)

    per_g = SSD_HEADS // SSD_GROUPS
    rows_g = hp // SSD_GROUPS
    gmask = [lane_n // SSD_STATE == g for g in range(SSD_GROUPS)]
    tris = (sub <= lan, sub >= lan)
    tri01 = [jnp.where(tr, 1.0, 0.0).astype(BF16) for tr in tris]
    ends = (q - 1, 0)
    h_scrs = (hf_scr, hb_scr)
    dirs = (0, 1)

    lanes = 2 if nc % 2 == 0 else 1

    def prep(c, d):
        xc = xc_scr[c]
        hs = slice(SSD_HEADS * d, SSD_HEADS * (d + 1))
        dtl = dt_ref[c][hs] + dtb_ref[hs]
        dt = jnp.maximum(dtl, 0.0) + jnp.log(1.0 + jnp.exp(-jnp.abs(dtl)))
        dta = dt * (-jnp.exp(alog_ref[hs]))
        cum_pad = _dot_split3(jnp.concatenate([dta, pad_rows], axis=0), tri01[d])
        btok = xc[hp:hp + SSD_GN].T
        ct16 = xc[hp + SSD_GN:].astype(BF16)
        gts = [_dot(jnp.where(gmask[g], btok, 0.0).astype(BF16), ct16) for g in range(SSD_GROUPS)]
        return dict(c=c, xt=xc[:hp], dt=dt, cum_pad=cum_pad, btok16=btok.astype(BF16), ct16=ct16, gts=gts)

    def head_work(p, d):
        cum_row = p["cum_pad"][:SSD_HEADS]
        cum_col = p["cum_pad"].T
        ydiag, xws, decs, ear = [], [], [], []
        for hh in range(SSD_HEADS):
            ar = cum_row[hh:hh + 1, :]
            ac = cum_col[:, hh:hh + 1]
            sct = (p["gts"][hh // per_g] * jnp.where(tris[d], jnp.exp(ar - ac), 0.0)).astype(BF16)
            xdt = p["xt"][hh * SSD_P:(hh + 1) * SSD_P] * p["dt"][hh:hh + 1, :]
            ydiag.append(_dot(xdt.astype(BF16), sct))
            a_end = ar[:, ends[d]:ends[d] + 1]
            xws.append(xdt * jnp.exp(a_end - ar))
            decs.append(jnp.exp(a_end))
            ear.append(jnp.exp(ar))
        return ydiag, xws, decs, ear

    def state_incs(p, xws):
        return [jnp.where(gmask[g], _dot(jnp.concatenate(xws[g * per_g:(g + 1) * per_g], axis=0).astype(BF16),
                                         p["btok16"]), 0.0) for g in range(SSD_GROUPS)]

    def carried_out(p, hm):
        return [_dot(hm[g * rows_g:(g + 1) * rows_g].astype(BF16), p["ct16"]) for g in range(SSD_GROUPS)]

    def step(s, carry):
        pos = [s * lanes + u for u in range(lanes)]
        cs = [(p_, jnp.where(p_ < nc0, nc0 - 1 - p_, nc - 1 - (p_ - nc0))) for p_ in pos]
        pre = [[prep(cs[u][d], d) for d in dirs] for u in range(lanes)]
        hm = [h_scrs[d][...] for d in dirs]
        yo = [carried_out(pre[0][d], hm[d]) for d in dirs]
        hw = [head_work(pre[0][d], d) for d in dirs]
        for u in range(lanes):
            incs = [state_incs(pre[u][d], hw[d][1]) for d in dirs]
            hw_next = [head_work(pre[u + 1][d], d) for d in dirs] if u + 1 < lanes else None
            for d in dirs:
                ydiag, _, decs, ear = hw[d]
                y = [ydiag[hh] + yo[d][hh // per_g][(hh % per_g) * SSD_P:(hh % per_g + 1) * SSD_P] * ear[hh]
                     for hh in range(SSD_HEADS)]
                y_scr[pre[u][d]["c"]] += jnp.concatenate(y, axis=0)
                hm[d] = jnp.concatenate(
                    [hm[d][hh * SSD_P:(hh + 1) * SSD_P] * decs[hh]
                     + incs[d][hh // per_g][(hh % per_g) * SSD_P:(hh % per_g + 1) * SSD_P]
                     for hh in range(SSD_HEADS)], axis=0)
            if u + 1 < lanes:
                yo = [carried_out(pre[u + 1][d], hm[d]) for d in dirs]
                hw = hw_next
        for d in dirs:
            h_scrs[d][...] = hm[d]
        return carry

    lax.fori_loop(0, nc // lanes, step, 0)

    def fin_body(c, carry):
        gated = y_scr[c] * _silu(z_ref[c].astype(F32))
        o_ref[c] = _rms(gated, ng_ref[...], axis=0).astype(BF16)
        return carry

    lax.fori_loop(0, nc, fin_body, 0)


def _ssd_mixer(z_t, xbc_t, dt_t, cw, cb, dtb, alog, dsk, ng, *, layer, nc0):
    bsz, nc, _, q = xbc_t.shape
    hp = SSD_HEADS * SSD_P
    per_b = lambda f: pl.BlockSpec((None, nc, f, q), lambda b: (b, 0, 0, 0))
    full = lambda a: _layer_spec(a, layer)
    kern = functools.partial(_ssd_kernel, nc=nc, nc0=nc0)
    return pl.pallas_call(
        kern,
        grid=(bsz,),
        in_specs=[per_b(hp), per_b(SSD_CONV_DIM), per_b(2 * SSD_HEADS),
                  full(cw), full(cb), full(dtb), full(alog), full(dsk), full(ng)],
        out_specs=per_b(hp),
        out_shape=jax.ShapeDtypeStruct((bsz, nc, hp, q), BF16),
        scratch_shapes=[pltpu.VMEM((nc, SSD_CONV_DIM, q), F32), pltpu.VMEM((nc, hp, q), F32),
                        pltpu.VMEM((hp, SSD_GN), F32), pltpu.VMEM((hp, SSD_GN), F32)],
        compiler_params=_cparams("arbitrary"),
        name="ssd_mixer",
    )(z_t, xbc_t, dt_t, cw, cb, dtb, alog, dsk, ng)


def _route(logits_t, bias_col):
    aff = _sigmoid(logits_t)
    sel = aff + bias_col
    rows = [sel[e:e + 1, :] for e in range(N_EXPERTS)]
    gscore = []
    for g in range(N_GROUPS):
        a, b, c, d = rows[PER_GROUP * g:PER_GROUP * (g + 1)]
        hi1, lo1, hi2, lo2 = jnp.maximum(a, b), jnp.minimum(a, b), jnp.maximum(c, d), jnp.minimum(c, d)
        gscore.append(jnp.maximum(hi1, hi2) + jnp.maximum(jnp.minimum(hi1, hi2), jnp.maximum(lo1, lo2)))
    best = jnp.zeros_like(gscore[0], dtype=jnp.int32)
    cur = gscore[0]
    for g in range(1, N_GROUPS):
        better = gscore[g] > cur
        best = jnp.where(better, g, best)
        cur = jnp.where(better, gscore[g], cur)
    eidx = lax.broadcasted_iota(jnp.int32, sel.shape, 0)
    masked = jnp.where(eidx // PER_GROUP == best, sel, -jnp.inf)
    m1 = jnp.max(masked, axis=0, keepdims=True)
    idx1 = jnp.min(jnp.where(masked == m1, eidx, N_EXPERTS), axis=0, keepdims=True)
    masked2 = jnp.where(eidx == idx1, -jnp.inf, masked)
    m2 = jnp.max(masked2, axis=0, keepdims=True)
    idx2 = jnp.min(jnp.where(masked2 == m2, eidx, N_EXPERTS), axis=0, keepdims=True)
    oh1, oh2 = eidx == idx1, eidx == idx2
    w1 = jnp.sum(jnp.where(oh1, aff, 0.0), axis=0, keepdims=True)
    w2 = jnp.sum(jnp.where(oh2, aff, 0.0), axis=0, keepdims=True)
    den = w1 + w2
    return oh1, oh2, w1 / den, w2 / den


def _split_hi_lo(x):
    hi = x.astype(BF16)
    return hi, (x - hi.astype(F32)).astype(BF16)


def _outproj_kernel(oda_ref, ossdt_ref, omla_ref, h_ref, mod_ref, ng_ref, wo_ref, rw_ref, rb_ref,
                    h1_ref, xs_ref, meta_ref, cnt_ref, *, n_ctx_tiles, ctx_row, d, tm, s_loc, nb):
    ti, bp = pl.program_id(0), pl.program_id(1)
    tiles = range(nb)
    r_i = lax.broadcasted_iota(jnp.int32, (tm, tm), 0)
    c_i = lax.broadcasted_iota(jnp.int32, (tm, tm), 1)
    before = jnp.where(r_i < c_i, 1.0, 0.0).astype(BF16)
    row_e = lax.broadcasted_iota(jnp.int32, (N_EXPERTS, LANES), 0)
    r_s = lax.broadcasted_iota(jnp.int32, (s_loc, tm), 0).astype(F32)
    lane_e = lax.broadcasted_iota(jnp.int32, (s_loc, MOE_EXTRA), 1)
    row_m = lax.broadcasted_iota(jnp.int32, (LANES, tm), 0)
    rows = [jnp.where(ti < n_ctx_tiles, ctx_row, bp * nb + bb) for bb in tiles]
    mod_at = lambda bb, k: mod_ref[pl.ds(rows[bb], 1), pl.ds(k * d, d)]
    ossd = [jnp.concatenate([ossdt_ref[bb, c].astype(F32).T for c in range(tm // SSD_CHUNK)], axis=0).astype(BF16)
            for bb in tiles]
    mix = [_dot(oda_ref[bb], wo_ref[0:DA_WIDTH]) + _dot(ossd[bb], wo_ref[DA_WIDTH:DA_WIDTH + SSD_INNER])
           + _dot(omla_ref[bb], wo_ref[DA_WIDTH + SSD_INNER:]) for bb in tiles]
    h1 = [h_ref[bb] + mod_at(bb, 2) * mix[bb] for bb in tiles]
    for bb in tiles:
        h1_ref[bb] = h1[bb]
    u = [_ada_norm(h1[bb], ng_ref[...], mod_at(bb, 3), mod_at(bb, 4)) for bb in tiles]
    u16 = [x.astype(BF16) for x in u]
    rw_hi, rw_lo = _split_hi_lo(rw_ref[...])
    u_lo = [(u[bb] - u16[bb].astype(F32)).astype(BF16) for bb in tiles]
    logits = [_dot(u16[bb], rw_hi) + _dot(u_lo[bb], rw_hi) + _dot(u16[bb], rw_lo) for bb in tiles]
    routed = [_route(logits[bb].T[:N_EXPERTS], rb_ref[...]) for bb in tiles]
    cnt = [jnp.where(r[0], 1.0, 0.0) + jnp.where(r[1], 1.0, 0.0) for r in routed]
    rank = [_dot(cnt[bb].astype(BF16), before) for bb in tiles]
    for bb in tiles:
        oh1, oh2, w1, w2 = routed[bb]
        tot = jnp.sum(cnt[bb], axis=1, keepdims=True)
        ptot = jnp.floor((tot + (MOE_BLK - 1)) * (1.0 / MOE_BLK)) * MOE_BLK
        ptot_b = jnp.broadcast_to(ptot, (N_EXPERTS, LANES))
        cnt_ref[bb] = ptot_b
        run = jnp.zeros((1, LANES), F32)
        off = jnp.zeros((N_EXPERTS, LANES), F32)
        for e in range(1, N_EXPERTS):
            run = run + ptot_b[e - 1:e]
            off = jnp.where(row_e == e, run, off)
        slot = off[:, 0:1] + rank[bb]
        dest1 = jnp.sum(jnp.where(oh1, slot, 0.0), axis=0, keepdims=True)
        dest2 = jnp.sum(jnp.where(oh2, slot, 0.0), axis=0, keepdims=True)
        routed[bb] = (r_s == dest1, r_s == dest2, w1, w2)
        meta_ref[bb] = jnp.where(row_m == 0, dest1, jnp.where(row_m == 1, dest2, 0.0)).T
    perm = [jnp.where(routed[bb][0] | routed[bb][1], 1.0, 0.0).astype(BF16) for bb in tiles]
    xs = [_dot(perm[bb], u16[bb]) for bb in tiles]
    for bb in tiles:
        p1, p2, w1, w2 = routed[bb]
        xs_ref[bb, :, 0:d] = xs[bb].astype(BF16)
        wslot = jnp.sum(jnp.where(p1, w1, 0.0) + jnp.where(p2, w2, 0.0), axis=1, keepdims=True)
        w_hi = wslot.astype(BF16).astype(F32)
        xs_ref[bb, :, d:] = jnp.where(lane_e == 0, w_hi, jnp.where(lane_e == 1, wslot - w_hi, 0.0)).astype(BF16)


def _outproj_dispatch(oda, ossd_t, omla, h, mod, ng, wo, rw, rb, *, layer, tm, n_ctx, s_loc, nb):
    bsz, t, d = h.shape
    nt, q = t // tm, SSD_CHUNK
    tok = lambda w: pl.BlockSpec((nb, tm, w), lambda ti, b: (b, ti, 0))
    full = lambda a: pl.BlockSpec(a.shape, lambda ti, b: (0,) * a.ndim)
    tile = lambda r, w: pl.BlockSpec((nb, None, r, w), lambda ti, b: (b, ti, 0, 0))
    lay = lambda a: _layer_spec(a, layer)
    kern = functools.partial(_outproj_kernel, n_ctx_tiles=n_ctx // tm, ctx_row=bsz, d=d, tm=tm, s_loc=s_loc, nb=nb)
    return pl.pallas_call(
        kern,
        grid=(nt, bsz // nb),
        in_specs=[tok(DA_WIDTH), pl.BlockSpec((nb, tm // q, SSD_INNER, q), lambda ti, b: (b, ti, 0, 0)),
                  tok(MLA_WIDTH), tok(d), lay(mod), lay(ng), lay(wo), full(rw), full(rb)],
        out_specs=[tok(d), tile(s_loc, d + MOE_EXTRA), tile(tm, LANES), tile(N_EXPERTS, LANES)],
        out_shape=[jax.ShapeDtypeStruct((bsz, t, d), F32),
                   jax.ShapeDtypeStruct((bsz, nt, s_loc, d + MOE_EXTRA), BF16),
                   jax.ShapeDtypeStruct((bsz, nt, tm, LANES), F32),
                   jax.ShapeDtypeStruct((bsz, nt, N_EXPERTS, LANES), F32)],
        compiler_params=_cparams("arbitrary", "arbitrary"),
        name="outproj_dispatch",
    )(oda, ossd_t, omla, h, mod, ng, wo, rw, rb)


def _expert_kernel(se_ref, bi_ref, sv_ref, sn_ref, xs_hbm, wg_ref, wu_ref, wd_ref, y_ref, xbuf, sem, wg16, wu16, wd16,
                   *, d):
    s = pl.program_id(0)
    last = pl.num_programs(0) - 1
    slot = s % 2

    def start_all(step_, slot_):
        for j in range(MOE_STEP_BLKS):
            blk = bi_ref[step_ * MOE_STEP_BLKS + j]
            pltpu.make_async_copy(xs_hbm.at[pl.ds(pl.multiple_of(blk * MOE_BLK, MOE_BLK), MOE_BLK), :],
                                  xbuf.at[slot_, pl.ds(j * MOE_BLK, MOE_BLK), :], sem.at[slot_]).start()

    def wait_all(slot_):
        for j in range(MOE_STEP_BLKS):
            pltpu.make_async_copy(xs_hbm.at[pl.ds(0, MOE_BLK), :], xbuf.at[slot_, pl.ds(j * MOE_BLK, MOE_BLK), :],
                                  sem.at[slot_]).wait()

    @pl.when(s == 0)
    def _():
        start_all(s, slot)

    start_all(jnp.minimum(s + 1, last), 1 - slot)
    wait_all(slot)

    @pl.when(sn_ref[s] > 0)
    def _():
        wg16[...] = wg_ref[...].astype(BF16)
        wu16[...] = wu_ref[...].astype(BF16)
        wd16[...] = wd_ref[...].astype(BF16)

    @pl.when(sv_ref[s] > 0)
    def _():
        xm = xbuf[slot, :, 0:d]
        wx = xbuf[slot, :, d:]
        wrow = wx[:, 0:1].astype(F32) + wx[:, 1:2].astype(F32)
        he = _silu(_dot(xm, wg16[...])) * _dot(xm, wu16[...])
        y_ref[...] = (_dot(he.astype(BF16), wd16[...]) * wrow).astype(BF16)

    @pl.when(sv_ref[s] == 0)
    def _():
        y_ref[...] = jnp.zeros_like(y_ref)

    @pl.when(s == last)
    def _():
        wait_all(1 - slot)


def _expert_ffn(xs2d, wg, wu, wd, step_e, blk_ids, step_valid, step_new, *, layer, n_steps):
    d = wg.shape[2]
    rows = MOE_STEP_BLKS * MOE_BLK
    wspec = lambda a: pl.BlockSpec((None, None) + a.shape[2:], lambda s, se, bi, sv, sn: (layer, se[s], 0, 0))
    grid_spec = pltpu.PrefetchScalarGridSpec(
        num_scalar_prefetch=4,
        grid=(n_steps,),
        in_specs=[pl.BlockSpec(memory_space=pl.ANY), wspec(wg), wspec(wu), wspec(wd)],
        out_specs=pl.BlockSpec((rows, d), lambda s, se, bi, sv, sn: (s, 0)),
        scratch_shapes=[pltpu.VMEM((2, rows, d + MOE_EXTRA), BF16), pltpu.SemaphoreType.DMA((2,)),
                        pltpu.VMEM(wg.shape[2:], BF16), pltpu.VMEM(wu.shape[2:], BF16),
                        pltpu.VMEM(wd.shape[2:], BF16)],
    )
    return pl.pallas_call(
        functools.partial(_expert_kernel, d=d),
        grid_spec=grid_spec,
        out_shape=jax.ShapeDtypeStruct((n_steps * rows, d), BF16),
        compiler_params=_cparams("arbitrary"),
        name="expert_ffn",
    )(step_e, blk_ids, step_valid, step_new, xs2d, wg, wu, wd)


def _combine_kernel(inv_ref, ys_hbm, meta_ref, h1_ref, mod_ref, fg_ref, o_ref, ybuf, sem, *, n_lb, nt, n_ctx_tiles,
                    ctx_row, d, tm, s_loc, final, nb):
    ti, bp = pl.program_id(0), pl.program_id(1)
    nbp = pl.num_programs(1)
    step = ti * nbp + bp
    last = pl.num_programs(0) * nbp - 1
    slot = step % 2

    def block_copy(ti_, bp_, k, slot_):
        bb, j = divmod(k, n_lb)
        blk = inv_ref[((bp_ * nb + bb) * nt + ti_) * n_lb + j]
        return pltpu.make_async_copy(ys_hbm.at[pl.ds(pl.multiple_of(blk * MOE_BLK, MOE_BLK), MOE_BLK), :],
                                     ybuf.at[slot_, pl.ds(k * MOE_BLK, MOE_BLK), :], sem.at[slot_])

    def start_all(ti_, bp_, slot_):
        for k in range(nb * n_lb):
            block_copy(ti_, bp_, k, slot_).start()

    def wait_all(slot_):
        for k in range(nb * n_lb):
            pltpu.make_async_copy(ys_hbm.at[pl.ds(0, MOE_BLK), :], ybuf.at[slot_, pl.ds(k * MOE_BLK, MOE_BLK), :],
                                  sem.at[slot_]).wait()

    @pl.when(step == 0)
    def _():
        start_all(ti, bp, slot)

    wrap = bp + 1 == nbp
    ti_n = jnp.where(step == last, ti, jnp.where(wrap, ti + 1, ti))
    bp_n = jnp.where(step == last, bp, jnp.where(wrap, 0, bp + 1))
    start_all(ti_n, bp_n, 1 - slot)
    wait_all(slot)

    tiles = range(nb)
    lane_s = lax.broadcasted_iota(jnp.int32, (tm, s_loc), 1).astype(F32)
    metas = [meta_ref[bb] for bb in tiles]
    pts = [jnp.where((lane_s == m[:, 0:1]) | (lane_s == m[:, 1:2]), 1.0, 0.0).astype(BF16) for m in metas]
    y = [_dot(pts[bb], ybuf[slot, bb * s_loc:(bb + 1) * s_loc, :]) for bb in tiles]
    for bb in tiles:
        row = jnp.where(ti < n_ctx_tiles, ctx_row, bp * nb + bb)
        h2 = h1_ref[bb] + mod_ref[pl.ds(row, 1), pl.ds(5 * d, d)] * y[bb]
        o_ref[bb] = _rms(h2, fg_ref[...]) if final else h2

    @pl.when(step == last)
    def _():
        wait_all(1 - slot)


def _combine(ys_em, inv, meta, h1, mod, fg, *, layer, tm, n_ctx, s_loc, final, nb):
    bsz, t, d = h1.shape
    nt = t // tm
    n_lb = s_loc // MOE_BLK
    full = lambda a: pl.BlockSpec(a.shape, lambda ti, b, inv: (0,) * a.ndim)
    grid_spec = pltpu.PrefetchScalarGridSpec(
        num_scalar_prefetch=1,
        grid=(nt, bsz // nb),
        in_specs=[pl.BlockSpec(memory_space=pl.ANY),
                  pl.BlockSpec((nb, None, tm, LANES), lambda ti, b, inv: (b, ti, 0, 0)),
                  pl.BlockSpec((nb, tm, d), lambda ti, b, inv: (b, ti, 0)),
                  _layer_spec(mod, layer), full(fg)],
        out_specs=pl.BlockSpec((nb, tm, d), lambda ti, b, inv: (b, ti, 0)),
        scratch_shapes=[pltpu.VMEM((2, nb * s_loc, d), BF16), pltpu.SemaphoreType.DMA((2,))],
    )
    kern = functools.partial(_combine_kernel, n_lb=n_lb, nt=nt, n_ctx_tiles=n_ctx // tm, ctx_row=bsz, d=d, tm=tm,
                             s_loc=s_loc, final=final, nb=nb)
    return pl.pallas_call(
        kern,
        grid_spec=grid_spec,
        out_shape=jax.ShapeDtypeStruct((bsz, t, d), F32),
        compiler_params=_cparams("arbitrary", "arbitrary"),
        name="moe_combine",
    )(inv, ys_em, meta, h1, mod, fg)


def _rope_tables(length, dim):
    rows = length // GRID_W
    row = jnp.repeat(jnp.arange(rows), GRID_W).astype(F32)
    col = jnp.tile(jnp.arange(GRID_W), rows).astype(F32)
    n_freq = dim // 4
    inv_freq = ROPE_THETA ** (-jnp.arange(n_freq, dtype=F32) / n_freq)
    ang = jnp.concatenate([row[:, None] * inv_freq, col[:, None] * inv_freq], axis=-1)
    return jnp.cos(ang), jnp.sin(ang)


def _table_set(n_ctx, n_lat):
    def lanes(cos, sin, lane0, width, reps, outside_cos):
        c = jnp.concatenate([cos, cos], axis=-1)
        s = jnp.concatenate([-sin, sin], axis=-1)
        grp_c = jnp.full((n_lat, width), outside_cos, F32).at[:, lane0:lane0 + c.shape[1]].set(c)
        grp_s = jnp.zeros((n_lat, width), F32).at[:, lane0:lane0 + s.shape[1]].set(s)
        ctx_c = jnp.full((n_ctx, width), outside_cos, F32).at[:, lane0:lane0 + c.shape[1]].set(1.0)
        ctx_s = jnp.zeros((n_ctx, width), F32)
        return (jnp.tile(jnp.concatenate([ctx_c, grp_c], axis=0), (1, reps)),
                jnp.tile(jnp.concatenate([ctx_s, grp_s], axis=0), (1, reps)))

    dcos, dsin = _rope_tables(n_lat, DA_QK)
    cda, sda = lanes(dcos, dsin, 0, DA_QK, DA_QW // DA_QK, 1.0)
    mcos, msin = _rope_tables(n_lat, MLA_ROPE)
    cq, sm = lanes(mcos, msin, KR_LANE0, LANES, 1, 1.0)
    ck, _ = lanes(mcos, msin, KR_LANE0, LANES, 1, 0.0)
    return cda, sda, cq, ck, sm


def _pack_w_in(w_in):
    w_in = w_in.astype(BF16)
    depth, d, _ = w_in.shape
    o_ssd = DA_IN
    o_mla = DA_IN + SSD_IN
    z = w_in[..., o_ssd:o_ssd + SSD_INNER]
    xbc = w_in[..., o_ssd + SSD_INNER:o_ssd + SSD_INNER + SSD_CONV_DIM]
    dt = w_in[..., o_ssd + SSD_INNER + SSD_CONV_DIM:o_mla]
    cq = w_in[..., o_mla:o_mla + MLA_Q_RANK]
    ckv = w_in[..., o_mla + MLA_Q_RANK:o_mla + MLA_Q_RANK + MLA_KV_RANK]
    kr = w_in[..., o_mla + MLA_Q_RANK + MLA_KV_RANK:]
    zeros = lambda n: jnp.zeros((depth, d, n), w_in.dtype)
    misc = jnp.concatenate([zeros(KR_LANE0), kr, dt, zeros(LANES - DT_LANE0 - 2 * SSD_HEADS)], axis=-1)
    return jnp.concatenate([w_in[..., :DA_IN], z, xbc, cq, ckv, misc], axis=-1)


def _pack_w_uq(w_uq):
    depth, r, _ = w_uq.shape
    w = w_uq.reshape(depth, r, MLA_HEADS, MLA_NOPE + MLA_ROPE)
    w = jnp.pad(w, ((0, 0), (0, 0), (0, 0), (0, MLA_HEAD_PAD - MLA_NOPE - MLA_ROPE)))
    return w.reshape(depth, r, MLA_QPAD).astype(BF16)


def _pack_w_ukv(w_ukv):
    depth, r, _ = w_ukv.shape
    w = w_ukv.reshape(depth, r, MLA_HEADS, MLA_NOPE + MLA_V)
    kn = jnp.pad(w[..., :MLA_NOPE], ((0, 0), (0, 0), (0, 0), (0, MLA_HEAD_PAD - MLA_NOPE)))
    return jnp.concatenate([kn.reshape(depth, r, MLA_QPAD), w[..., MLA_NOPE:].reshape(depth, r, MLA_WIDTH)],
                           axis=-1).astype(BF16)


def _dispatch_tables(pcnt, n_lb, n_steps):
    ntt = pcnt.shape[0]
    nb = (pcnt / MOE_BLK).astype(jnp.int32)
    lo = jnp.cumsum(nb, axis=1) - nb
    n_e = jnp.sum(nb, axis=0)
    p_e = (n_e + MOE_STEP_BLKS - 1) // MOE_STEP_BLKS * MOE_STEP_BLKS
    ends = jnp.cumsum(p_e)
    base = (ends - p_e)[None, :] + jnp.cumsum(nb, axis=0) - nb
    lb = jnp.arange(n_lb, dtype=jnp.int32)
    owner = (lb[None, :, None] >= lo[:, None, :]) & (lb[None, :, None] < (lo + nb)[:, None, :])
    used = jnp.any(owner, axis=-1)
    pos = jnp.sum(jnp.where(owner, (base - lo)[:, None, :], 0), axis=-1) + lb[None, :]
    inv = jnp.where(used, pos, 0).astype(jnp.int32)
    n_pos = n_steps * MOE_STEP_BLKS
    src = jnp.arange(ntt, dtype=jnp.int32)[:, None] * n_lb + lb[None, :]
    fwd = jnp.zeros((n_pos,), jnp.int32).at[jnp.where(used, pos, n_pos).reshape(-1)].set(src.reshape(-1), mode="drop")
    starts = jnp.arange(n_steps, dtype=jnp.int32) * MOE_STEP_BLKS
    step_e = jnp.minimum(jnp.sum(starts[:, None] >= ends[None, :], axis=1), N_EXPERTS - 1).astype(jnp.int32)
    step_valid = (starts < ends[-1]).astype(jnp.int32)
    step_new = jnp.concatenate([jnp.ones((1,), jnp.int32), (step_e[1:] != step_e[:-1]).astype(jnp.int32)])
    fwd2 = fwd.reshape(n_steps, MOE_STEP_BLKS)
    last = lax.dynamic_slice_in_dim(fwd2, jnp.maximum(ends[-1] // MOE_STEP_BLKS - 1, 0), 1, axis=0)
    fwd = jnp.where(step_valid[:, None] > 0, fwd2, last).reshape(-1)
    return inv.reshape(-1), fwd, step_e, step_valid, step_new


def kernel(x, c, ctx, c_ctx, norm_mix_g, norm_ffn_g, w_mod, b_mod, w_in, w_out, da_lambda, da_subln_g, ssd_conv_w, ssd_conv_b, ssd_dt_bias, ssd_a_log, ssd_d, ssd_norm_g, mla_q_norm_g, mla_kv_norm_g, mla_w_uq, mla_w_ukv, router_w, router_bias, exp_w_gate, exp_w_up, exp_w_down, final_norm_g):
    bsz, n_lat, d = x.shape
    n_ctx = ctx.shape[1]
    t = n_ctx + n_lat
    tm = min(256, n_ctx)
    assert n_ctx % tm == 0 and n_lat % tm == 0 and n_ctx % SSD_CHUNK == 0 and n_lat % SSD_CHUNK == 0
    nt = t // tm
    s_loc = 2 * tm + 2 * LANES
    n_lb = s_loc // MOE_BLK
    n_steps = -(-(bsz * nt * n_lb + N_EXPERTS * (MOE_STEP_BLKS - 1)) // MOE_STEP_BLKS)
    nc, nc0 = t // SSD_CHUNK, n_ctx // SSD_CHUNK
    nb = 2 if bsz % 2 == 0 else 1
    nb_wide = 4 if bsz % 4 == 0 else nb
    q = SSD_CHUNK

    r_pad = -(-(bsz + 1) // 8) * 8
    c_rows = jnp.concatenate([c, c_ctx[None, :], jnp.zeros((r_pad - bsz - 1, d), F32)], axis=0)
    mod = _modulation(c_rows, w_mod, b_mod)

    w_in_p = _pack_w_in(w_in)
    w_uq_p = _pack_w_uq(mla_w_uq)
    w_ukv_p = _pack_w_ukv(mla_w_ukv)
    w_out16 = w_out.astype(BF16)
    tabs = _table_set(n_ctx, n_lat)
    rw_pad = jnp.pad(router_w, ((0, 0), (0, LANES - N_EXPERTS)))
    rb = router_bias.reshape(N_EXPERTS, 1)
    lane_b = lambda v: jnp.broadcast_to(v[..., None], v.shape + (q,))
    dsk_rows = jnp.repeat(ssd_d, SSD_P, axis=-1)
    ng_mix, ng_ffn = norm_mix_g[:, None, :], norm_ffn_g[:, None, :]
    qg, kvg = mla_q_norm_g[:, None, :], mla_kv_norm_g[:, None, :]
    sub_g = jnp.tile(da_subln_g, (1, DA_HEADS))[:, :, None]
    ssd_par = (lane_b(ssd_conv_w), lane_b(ssd_conv_b), lane_b(ssd_dt_bias.reshape(DEPTH, -1)),
               lane_b(ssd_a_log.reshape(DEPTH, -1)), lane_b(dsk_rows), lane_b(ssd_norm_g))

    h = jnp.concatenate([ctx, x], axis=1)
    for i in range(DEPTH):
        lam_init = 0.8 - 0.6 * math.exp(-0.3 * i)
        qda_t, kda, vda_t, z_t, xbc_t, dt_t, q4t, k4, vm_t = _inproj(
            h, mod, ng_mix, w_in_p, tabs, qg, kvg, w_uq_p, w_ukv_p, layer=i, tm=tm, n_ctx=n_ctx, nb=nb_wide)
        o_da = _da_attention(qda_t, kda, vda_t, da_lambda, sub_g, layer=i, lam_init=lam_init, tq=tm, n_ctx=n_ctx)
        o_mla = _mla_attention(q4t, k4, vm_t, tq=tm, n_ctx=n_ctx)
        o_ssd_t = _ssd_mixer(z_t, xbc_t, dt_t, *ssd_par, layer=i, nc0=nc0)
        h1, xs, meta, cnt = _outproj_dispatch(o_da, o_ssd_t, o_mla, h, mod, ng_ffn, w_out16, rw_pad, rb, layer=i, tm=tm,
                                              n_ctx=n_ctx, s_loc=s_loc, nb=nb_wide)
        inv, fwd, step_e, step_valid, step_new = _dispatch_tables(cnt[:, :, :, 0].reshape(bsz * nt, N_EXPERTS), n_lb,
                                                                  n_steps)
        ys_em = _expert_ffn(xs.reshape(bsz * nt * s_loc, d + MOE_EXTRA), exp_w_gate, exp_w_up, exp_w_down, step_e, fwd,
                            step_valid, step_new, layer=i, n_steps=n_steps)
        h = _combine(ys_em, inv, meta, h1, mod, final_norm_g[None, :], layer=i, tm=tm, n_ctx=n_ctx, s_loc=s_loc,
                     final=(i == DEPTH - 1), nb=nb)
    return h[:, n_ctx:]
```

```python
import functools
import math

import jax
import jax.numpy as jnp
from jax import lax
from jax.experimental import pallas as pl
from jax.experimental.pallas import tpu as pltpu

F32 = jnp.float32
BF16 = jnp.bfloat16

DEPTH = 4
GRID_W = 64
EPS = 1e-6
ROPE_THETA = 10000.0
DA_HEADS, DA_QK = 4, 32
DA_V = 2 * DA_QK
DA_WIDTH = DA_HEADS * DA_V
DA_QW = DA_HEADS * 2 * DA_QK
DA_IN = 2 * DA_QW + DA_WIDTH
SSD_HEADS, SSD_P, SSD_GROUPS, SSD_STATE, SSD_CHUNK = 8, 64, 2, 64, 128
SSD_INNER = SSD_HEADS * SSD_P
SSD_GN = SSD_GROUPS * SSD_STATE
SSD_CONV_DIM = SSD_INNER + 2 * SSD_GN
SSD_IN = SSD_INNER + SSD_CONV_DIM + 2 * SSD_HEADS
MLA_HEADS, MLA_Q_RANK, MLA_KV_RANK, MLA_NOPE, MLA_ROPE, MLA_V = 4, 256, 128, 64, 32, 64
MLA_WIDTH = MLA_HEADS * MLA_V
MLA_IN = MLA_Q_RANK + MLA_KV_RANK + MLA_ROPE
MLA_SCALE = (MLA_NOPE + MLA_ROPE) ** -0.5
N_EXPERTS, N_GROUPS, D_EXPERT = 16, 4, 512
PER_GROUP = N_EXPERTS // N_GROUPS

LANES = 128
SUBLANES_BF16 = 16
VMEM_LIMIT_BYTES = 56 * 1024 * 1024

MLA_HEAD_PAD = LANES
MLA_QPAD = MLA_HEADS * MLA_HEAD_PAD
KR_LANE0 = MLA_NOPE
DT_LANE0 = MLA_NOPE + MLA_ROPE
IN_COLS = DA_IN + SSD_INNER + SSD_CONV_DIM + MLA_Q_RANK + MLA_KV_RANK + LANES
C_Z = DA_IN
C_XBC = C_Z + SSD_INNER
C_CQ = C_XBC + SSD_CONV_DIM
C_CKV = C_CQ + MLA_Q_RANK
C_MISC = C_CKV + MLA_KV_RANK
LOG2E = math.log2(math.e)
DA_QSCALE = DA_QK ** -0.5 * LOG2E
MLA_QSCALE = MLA_SCALE * LOG2E

KEY_CHUNK = 256
MOE_BLK = SUBLANES_BF16
MOE_STEP_BLKS = 32
MOE_EXTRA = LANES


def _sigmoid(x):
    return 1.0 / (1.0 + jnp.exp(-x))


def _silu(x):
    return x * _sigmoid(x)


def _rms(x, g, axis=-1):
    return x * lax.rsqrt(jnp.mean(x * x, axis=axis, keepdims=True) + EPS) * g


def _ada_norm(hv, g, shift, scale):
    return _rms(hv, g) * (1.0 + scale) + shift


def _rope(x, cos, sin_signed, half):
    w = x.shape[-1]
    lane = lax.broadcasted_iota(jnp.int32, x.shape, x.ndim - 1)
    first = (lane % (2 * half)) < half
    partner = jnp.where(first, pltpu.roll(x, w - half, x.ndim - 1), pltpu.roll(x, half, x.ndim - 1))
    return x * cos + partner * sin_signed


def _dot(a, b):
    return jnp.dot(a, b, preferred_element_type=F32)


def _dot_hi(a, b):
    return jnp.dot(a, b, preferred_element_type=F32, precision=lax.Precision.HIGHEST)


def _dot_split3(x, m01):
    x1 = x.astype(BF16)
    r1 = x - x1.astype(F32)
    x2 = r1.astype(BF16)
    x3 = (r1 - x2.astype(F32)).astype(BF16)
    return _dot(x1, m01) + _dot(x2, m01) + _dot(x3, m01)


def _layer_spec(a, i):
    return pl.BlockSpec((None,) + a.shape[1:], lambda *_: (i,) + (0,) * (a.ndim - 1))


def _cparams(*sem):
    return pltpu.CompilerParams(dimension_semantics=sem, vmem_limit_bytes=VMEM_LIMIT_BYTES)


def _mod_kernel(c_ref, w_ref, b_ref, o_ref):
    o_ref[...] = _dot_hi(_silu(c_ref[...]), w_ref[...]) + b_ref[...]


def _modulation(c_rows, w_mod, b_mod):
    depth, d, n = w_mod.shape
    r = c_rows.shape[0]
    tn = 1536
    return pl.pallas_call(
        _mod_kernel,
        grid=(depth, n // tn),
        in_specs=[pl.BlockSpec((r, d), lambda l, j: (0, 0)),
                  pl.BlockSpec((None, d, tn), lambda l, j: (l, 0, j)),
                  pl.BlockSpec((None, 1, tn), lambda l, j: (l, 0, j))],
        out_specs=pl.BlockSpec((None, r, tn), lambda l, j: (l, 0, j)),
        out_shape=jax.ShapeDtypeStruct((depth, r, n), F32),
        compiler_params=_cparams("arbitrary", "arbitrary"),
        name="modulation",
    )(c_rows, w_mod, b_mod.reshape(depth, 1, n))


def _inproj_kernel(*refs, n_ctx_tiles, ctx_row, d, nb, tm, moe_geom):
    if moe_geom is None:
        h_ref, refs = refs[0], refs[1:]
    else:
        (inv_ref, ys_hbm, meta_ref, h1_ref, modp_ref), refs = refs[:5], refs[5:]
        h2_ref, ybuf, sem = refs[-3:]
        refs = refs[:-3]
    (mod_ref, ng_ref, w_ref, cda_ref, sda_ref, cq_ref, ck_ref, sm_ref, qg_ref, kvg_ref, wuq_ref, wukv_ref,
     qdat_ref, kda_ref, vdat_ref, zt_ref, xbct_ref, dtt_ref, q4t_ref, k4_ref, vmt_ref) = refs
    ti, bp = pl.program_id(0), pl.program_id(1)
    if moe_geom is None:
        hs = [h_ref[bb] for bb in range(nb)]
    else:
        hs, drain = _moe_combine_rows(inv_ref, ys_hbm, meta_ref, h1_ref, modp_ref, ybuf, sem, nb=nb, tm=tm, d=d,
                                      n_ctx_tiles=n_ctx_tiles, ctx_row=ctx_row, **moe_geom)
        for bb in range(nb):
            h2_ref[bb] = hs[bb]
    cda, sda, sm = cda_ref[...], sda_ref[...], sm_ref[...]
    cos_q = jnp.concatenate([cq_ref[...]] * MLA_HEADS, axis=1)
    sin_q = jnp.concatenate([sm] * MLA_HEADS, axis=1)
    q = SSD_CHUNK
    accs = []
    for bb in range(nb):
        row = jnp.where(ti < n_ctx_tiles, ctx_row, bp * nb + bb)
        shift = mod_ref[pl.ds(row, 1), pl.ds(0, d)]
        scale = mod_ref[pl.ds(row, 1), pl.ds(d, d)]
        u = _ada_norm(hs[bb], ng_ref[...], shift, scale).astype(BF16)
        accs.append(_dot(u, w_ref[...]))
    for bb in range(nb):
        acc = accs[bb]
        qdat_ref[bb] = (_rope(acc[:, 0:DA_QW], cda, sda, DA_QK // 2) * DA_QSCALE).T.astype(BF16)
        kda_ref[bb] = _rope(acc[:, DA_QW:2 * DA_QW], cda, sda, DA_QK // 2).astype(BF16)
        vdat_ref[bb] = acc[:, 2 * DA_QW:DA_IN].T.astype(BF16)
        z_t = acc[:, C_Z:C_XBC].T
        xbc_t = acc[:, C_XBC:C_CQ].T
        misc = acc[:, C_MISC:IN_COLS]
        dt_t = misc.T[DT_LANE0:DT_LANE0 + 2 * SSD_HEADS]
        for c in range(tm // q):
            zt_ref[bb, c] = z_t[:, c * q:(c + 1) * q].astype(BF16)
            xbct_ref[bb, c] = xbc_t[:, c * q:(c + 1) * q]
            dtt_ref[bb, c] = dt_t[:, c * q:(c + 1) * q]
        cqn = _rms(acc[:, C_CQ:C_CKV], qg_ref[...]).astype(BF16)
        qm_t = (_rope(_dot(cqn, wuq_ref[...]), cos_q, sin_q, MLA_ROPE // 2) * MLA_QSCALE).T
        ckvn = _rms(acc[:, C_CKV:C_MISC], kvg_ref[...]).astype(BF16)
        kv = _dot(ckvn, wukv_ref[...])
        kr = _rope(misc, ck_ref[...], sm, MLA_ROPE // 2)
        km = kv[:, :MLA_QPAD] + jnp.concatenate([kr] * MLA_HEADS, axis=1)
        for hh in range(MLA_HEADS):
            q4t_ref[bb, hh] = qm_t[hh * MLA_HEAD_PAD:(hh + 1) * MLA_HEAD_PAD].astype(BF16)
            k4_ref[bb, hh] = km[:, hh * MLA_HEAD_PAD:(hh + 1) * MLA_HEAD_PAD].astype(BF16)
        vmt_ref[bb] = kv[:, MLA_QPAD:].T.astype(BF16)
    if moe_geom is not None:
        drain()


def _inproj(h, mod, ng, w_in_p, tabs, qg, kvg, wuq_p, wukv_p, *, layer, tm, n_ctx, nb, moe=None):
    bsz, t, d = (h if moe is None else moe[3]).shape
    nt, q = t // tm, SSD_CHUNK
    cpt = tm // q
    tok = lambda w: pl.BlockSpec((nb, tm, w), lambda ti, b, *_: (b, ti, 0))
    tab = lambda w: pl.BlockSpec((tm, w), lambda ti, b, *_: (ti, 0))
    chunked = lambda f: pl.BlockSpec((nb, cpt, f, q), lambda ti, b, *_: (b, ti, 0, 0))
    lay = lambda a: _layer_spec(a, layer)
    cda, sda, cq, ck, sm = tabs
    sds = jax.ShapeDtypeStruct
    tok_t = lambda w: pl.BlockSpec((nb, w, tm), lambda ti, b, *_: (b, 0, ti))
    out_specs = [tok_t(DA_QW), tok(DA_QW), tok_t(DA_WIDTH),
                 chunked(SSD_INNER), chunked(SSD_CONV_DIM), chunked(2 * SSD_HEADS),
                 pl.BlockSpec((nb, MLA_HEADS, MLA_HEAD_PAD, tm), lambda ti, b, *_: (b, 0, 0, ti)),
                 pl.BlockSpec((nb, MLA_HEADS, tm, MLA_HEAD_PAD), lambda ti, b, *_: (b, 0, ti, 0)), tok_t(MLA_WIDTH)]
    out_shape = [sds((bsz, DA_QW, t), BF16), sds((bsz, t, DA_QW), BF16), sds((bsz, DA_WIDTH, t), BF16),
                 sds((bsz, t // q, SSD_INNER, q), BF16), sds((bsz, t // q, SSD_CONV_DIM, q), F32),
                 sds((bsz, t // q, 2 * SSD_HEADS, q), F32),
                 sds((bsz, MLA_HEADS, MLA_HEAD_PAD, t), BF16), sds((bsz, MLA_HEADS, t, MLA_HEAD_PAD), BF16),
                 sds((bsz, MLA_WIDTH, t), BF16)]
    common_specs = [lay(mod), lay(ng), lay(w_in_p), tab(DA_QW), tab(DA_QW), tab(LANES), tab(LANES), tab(LANES),
                    lay(qg), lay(kvg), lay(wuq_p), lay(wukv_p)]
    common_args = (mod, ng, w_in_p, cda, sda, cq, ck, sm, qg, kvg, wuq_p, wukv_p)
    geom = dict(n_ctx_tiles=n_ctx // tm, ctx_row=bsz, d=d, nb=nb, tm=tm)
    if moe is None:
        return pl.pallas_call(
            functools.partial(_inproj_kernel, moe_geom=None, **geom),
            grid=(nt, bsz // nb),
            in_specs=[tok(d)] + common_specs,
            out_specs=out_specs,
            out_shape=out_shape,
            compiler_params=_cparams("arbitrary", "arbitrary"),
            name="inproj",
        )(h, *common_args)
    ys_em, inv, meta, h1, s_loc = moe
    grid_spec = pltpu.PrefetchScalarGridSpec(
        num_scalar_prefetch=1,
        grid=(nt, bsz // nb),
        in_specs=[pl.BlockSpec(memory_space=pl.ANY),
                  pl.BlockSpec((nb, None, tm, LANES), lambda ti, b, inv: (b, ti, 0, 0)), tok(d),
                  _layer_spec(mod, layer - 1)] + common_specs,
        out_specs=out_specs + [tok(d)],
        scratch_shapes=[pltpu.VMEM((2, nb * s_loc, d), BF16), pltpu.SemaphoreType.DMA((2,))],
    )
    moe_geom = dict(n_lb=s_loc // MOE_BLK, nt=nt, s_loc=s_loc)
    return pl.pallas_call(
        functools.partial(_inproj_kernel, moe_geom=moe_geom, **geom),
        grid_spec=grid_spec,
        out_shape=out_shape + [sds((bsz, t, d), F32)],
        compiler_params=_cparams("arbitrary", "arbitrary"),
        name="combine_inproj",
    )(inv, ys_em, meta, h1, mod, *common_args)


V_AUG = DA_V + SUBLANES_BF16


def _scores_pass(k_at, qtm, nk, s_scr):
    kc = KEY_CHUNK if nk % KEY_CHUNK == 0 else LANES
    m = None
    for c0 in range(0, nk, kc):
        s_c = _dot(k_at(c0, c0 + kc), qtm)
        s_scr[c0:c0 + kc, :] = s_c
        part = jnp.max(s_c, axis=0, keepdims=True)
        m = part if m is None else jnp.maximum(m, part)
    return m


def _pv_pass(vaug_at, nk, s_scr, m):
    kc = KEY_CHUNK if nk % KEY_CHUNK == 0 else LANES
    acc = None
    for c0 in range(0, nk, kc):
        e = jnp.exp2(s_scr[c0:c0 + kc, :] - m).astype(BF16)
        inc = _dot(vaug_at(c0, c0 + kc), e)
        acc = inc if acc is None else acc + inc
    return acc


def _attend_heads(k_of, qt_of, vaug_of, nk, scr, n_sub):
    m = _scores_pass(k_of(0), qt_of(0), nk, scr[0])
    outs = []
    for j in range(n_sub):
        if j + 1 < n_sub:
            m_next = _scores_pass(k_of(j + 1), qt_of(j + 1), nk, scr[(j + 1) % 2])
        acc = _pv_pass(vaug_of(j), nk, scr[j % 2], m)
        outs.append(acc[:DA_V] * (1.0 / acc[DA_V:DA_V + 1]))
        if j + 1 < n_sub:
            m = m_next
    return outs


def _fill_vaug(vt_ref, vaug_scr, n_heads):
    t = vt_ref.shape[1]
    for hh in range(n_heads):
        vaug_scr[hh, 0:DA_V, :] = vt_ref[hh * DA_V:(hh + 1) * DA_V, :]
        vaug_scr[hh, DA_V:V_AUG, :] = jnp.ones((V_AUG - DA_V, t), BF16)


def _da_attn_kernel(lam_ref, g_ref, qt_ref, k_ref, vt_ref, o_ref, s0_scr, s1_scr, vaug_scr, *, n_ctx, n_ctx_tiles,
                    lam_init, tq):
    qi = pl.program_id(1)

    @pl.when(qi == 0)
    def _():
        _fill_vaug(vt_ref, vaug_scr, DA_HEADS)

    lv = lam_ref[...]
    lam = (jnp.exp(jnp.sum(lv[0:1] * lv[1:2], axis=-1, keepdims=True))
           - jnp.exp(jnp.sum(lv[2:3] * lv[3:4], axis=-1, keepdims=True)) + lam_init)
    row_q = lax.broadcasted_iota(jnp.int32, (DA_QW, 1), 0)

    def attend(nk):
        qt = qt_ref[...]
        qt_of = lambda j: qt * jnp.where(row_q // DA_QK == j, 1.0, 0.0).astype(BF16)
        k_of = lambda j: (lambda c0, c1: k_ref[c0:c1, :])
        vaug_of = lambda j: (lambda c0, c1: vaug_scr[j // 2, :, c0:c1])
        outs = _attend_heads(k_of, qt_of, vaug_of, nk, (s0_scr, s1_scr), 2 * DA_HEADS)
        heads = []
        for hh in range(DA_HEADS):
            o = outs[2 * hh] - lam * outs[2 * hh + 1]
            heads.append(o * lax.rsqrt(jnp.mean(o * o, axis=0, keepdims=True) + EPS))
        o_t = jnp.concatenate(heads, axis=0) * g_ref[...] * (1.0 - lam_init)
        o_ref[...] = o_t.T.astype(BF16)

    @pl.when(qi < n_ctx_tiles)
    def _():
        attend(n_ctx)

    @pl.when(qi >= n_ctx_tiles)
    def _():
        attend(k_ref.shape[0])


def _da_attention(qt, k, vt, lam_vec, g_col, *, layer, lam_init, tq, n_ctx):
    bsz, _, t = qt.shape
    kern = functools.partial(_da_attn_kernel, n_ctx=n_ctx, n_ctx_tiles=n_ctx // tq, lam_init=lam_init, tq=tq)
    return pl.pallas_call(
        kern,
        grid=(bsz, t // tq),
        in_specs=[_layer_spec(lam_vec, layer), _layer_spec(g_col, layer),
                  pl.BlockSpec((None, DA_QW, tq), lambda b, i: (b, 0, i)),
                  pl.BlockSpec((None, t, DA_QW), lambda b, i: (b, 0, 0)),
                  pl.BlockSpec((None, DA_WIDTH, t), lambda b, i: (b, 0, 0))],
        out_specs=pl.BlockSpec((None, tq, DA_WIDTH), lambda b, i: (b, i, 0)),
        out_shape=jax.ShapeDtypeStruct((bsz, t, DA_WIDTH), BF16),
        scratch_shapes=[pltpu.VMEM((t, tq), F32), pltpu.VMEM((t, tq), F32), pltpu.VMEM((DA_HEADS, V_AUG, t), BF16)],
        compiler_params=_cparams("arbitrary", "arbitrary"),
        name="da_attention",
    )(lam_vec, g_col, qt, k, vt)


def _mla_attn_kernel(qt_ref, k_ref, vt_ref, o_ref, s0_scr, s1_scr, vaug_scr, *, n_ctx, n_ctx_tiles, tq):
    qi = pl.program_id(1)

    @pl.when(qi == 0)
    def _():
        _fill_vaug(vt_ref, vaug_scr, MLA_HEADS)

    def attend(nk):
        k_of = lambda hh: (lambda c0, c1: k_ref[hh, c0:c1, :])
        vaug_of = lambda hh: (lambda c0, c1: vaug_scr[hh, :, c0:c1])
        outs = _attend_heads(k_of, lambda hh: qt_ref[hh], vaug_of, nk, (s0_scr, s1_scr), MLA_HEADS)
        o_ref[...] = jnp.concatenate(outs, axis=0).T.astype(BF16)

    @pl.when(qi < n_ctx_tiles)
    def _():
        attend(n_ctx)

    @pl.when(qi >= n_ctx_tiles)
    def _():
        attend(k_ref.shape[1])


def _mla_attention(q4t, k4, vt, *, tq, n_ctx):
    bsz, _, _, t = q4t.shape
    kern = functools.partial(_mla_attn_kernel, n_ctx=n_ctx, n_ctx_tiles=n_ctx // tq, tq=tq)
    return pl.pallas_call(
        kern,
        grid=(bsz, t // tq),
        in_specs=[pl.BlockSpec((None, MLA_HEADS, MLA_HEAD_PAD, tq), lambda b, i: (b, 0, 0, i)),
                  pl.BlockSpec((None, MLA_HEADS, t, MLA_HEAD_PAD), lambda b, i: (b, 0, 0, 0)),
                  pl.BlockSpec((None, MLA_WIDTH, t), lambda b, i: (b, 0, 0))],
        out_specs=pl.BlockSpec((None, tq, MLA_WIDTH), lambda b, i: (b, i, 0)),
        out_shape=jax.ShapeDtypeStruct((bsz, t, MLA_WIDTH), BF16),
        scratch_shapes=[pltpu.VMEM((t, tq), F32), pltpu.VMEM((t, tq), F32), pltpu.VMEM((MLA_HEADS, V_AUG, t), BF16)],
        compiler_params=_cparams("arbitrary", "arbitrary"),
        name="mla_attention",
    )(q4t, k4, vt)


def _ssd_kernel(z_ref, xbc_ref, dt_ref, cw_ref, cb_ref, dtb_ref, alog_ref, dsk_ref, ng_ref,
                o_ref, xc_scr, y_scr, hf_scr, hb_scr, *, nc, nc0):
    q = SSD_CHUNK
    hp = SSD_HEADS * SSD_P
    lane_x = lax.broadcasted_iota(jnp.int32, (SSD_CONV_DIM, q), 1)

    def conv_body(c, carry):
        xc = xbc_ref[c]
        keep_prev = jnp.where((c == 0) | (c == nc0), 0.0, 1.0)
        keep_next = jnp.where((c == nc0 - 1) | (c == nc - 1), 0.0, 1.0)
        xp = xbc_ref[jnp.maximum(c - 1, 0)] * keep_prev
        xn = xbc_ref[jnp.minimum(c + 1, nc - 1)] * keep_next
        prev = pltpu.roll(jnp.where(lane_x == q - 1, xp, xc), 1, 1)
        nxt = pltpu.roll(jnp.where(lane_x == 0, xn, xc), q - 1, 1)
        a = _silu(cw_ref[0] * prev + cw_ref[1] * xc + cw_ref[2] * nxt + cb_ref[...])
        xc_scr[c] = a
        y_scr[c] = dsk_ref[...] * a[:hp]
        return carry

    lax.fori_loop(0, nc, conv_body, 0)
    hf_scr[...] = jnp.zeros_like(hf_scr)
    hb_scr[...] = jnp.zeros_like(hb_scr)

    sub = lax.broadcasted_iota(jnp.int32, (q, q), 0)
    lan = lax.broadcasted_iota(jnp.int32, (q, q), 1)
    lane_n = lax.broadcasted_iota(jnp.int32, (1, SSD_GN), 1)
    pad_rows = jnp.zeros((q - SSD_HEADS, q), F32)

    per_g = SSD_HEADS // SSD_GROUPS
    rows_g = hp // SSD_GROUPS
    gmask = [lane_n // SSD_STATE == g for g in range(SSD_GROUPS)]
    tris = (sub <= lan, sub >= lan)
    tri01 = [jnp.where(tr, 1.0, 0.0).astype(BF16) for tr in tris]
    ends = (q - 1, 0)
    h_scrs = (hf_scr, hb_scr)
    dirs = (0, 1)

    lanes = 2 if nc % 2 == 0 else 1

    def prep(c, d):
        xc = xc_scr[c]
        hs = slice(SSD_HEADS * d, SSD_HEADS * (d + 1))
        dtl = dt_ref[c][hs] + dtb_ref[hs]
        dt = jnp.maximum(dtl, 0.0) + jnp.log(1.0 + jnp.exp(-jnp.abs(dtl)))
        dta = dt * (-jnp.exp(alog_ref[hs]))
        cum_pad = _dot_split3(jnp.concatenate([dta, pad_rows], axis=0), tri01[d])
        btok = xc[hp:hp + SSD_GN].T
        ct16 = xc[hp + SSD_GN:].astype(BF16)
        gts = [_dot(jnp.where(gmask[g], btok, 0.0).astype(BF16), ct16) for g in range(SSD_GROUPS)]
        return dict(c=c, xt=xc[:hp], dt=dt, cum_pad=cum_pad, btok16=btok.astype(BF16), ct16=ct16, gts=gts)

    def head_work(p, d):
        cum_row = p["cum_pad"][:SSD_HEADS]
        cum_col = p["cum_pad"].T
        ydiag, xws, decs, ear = [], [], [], []
        for hh in range(SSD_HEADS):
            ar = cum_row[hh:hh + 1, :]
            ac = cum_col[:, hh:hh + 1]
            sct = (p["gts"][hh // per_g] * jnp.where(tris[d], jnp.exp(ar - ac), 0.0)).astype(BF16)
            xdt = p["xt"][hh * SSD_P:(hh + 1) * SSD_P] * p["dt"][hh:hh + 1, :]
            ydiag.append(_dot(xdt.astype(BF16), sct))
            a_end = ar[:, ends[d]:ends[d] + 1]
            xws.append(xdt * jnp.exp(a_end - ar))
            decs.append(jnp.exp(a_end))
            ear.append(jnp.exp(ar))
        return ydiag, xws, decs, ear

    def state_incs(p, xws):
        return [jnp.where(gmask[g], _dot(jnp.concatenate(xws[g * per_g:(g + 1) * per_g], axis=0).astype(BF16),
                                         p["btok16"]), 0.0) for g in range(SSD_GROUPS)]

    def carried_out(p, hm):
        return [_dot(hm[g * rows_g:(g + 1) * rows_g].astype(BF16), p["ct16"]) for g in range(SSD_GROUPS)]

    def step(s, carry):
        pos = [s * lanes + u for u in range(lanes)]
        cs = [(p_, jnp.where(p_ < nc0, nc0 - 1 - p_, nc - 1 - (p_ - nc0))) for p_ in pos]
        pre = [[prep(cs[u][d], d) for d in dirs] for u in range(lanes)]
        hm = [h_scrs[d][...] for d in dirs]
        yo = [carried_out(pre[0][d], hm[d]) for d in dirs]
        hw = [head_work(pre[0][d], d) for d in dirs]
        for u in range(lanes):
            incs = [state_incs(pre[u][d], hw[d][1]) for d in dirs]
            hw_next = [head_work(pre[u + 1][d], d) for d in dirs] if u + 1 < lanes else None
            for d in dirs:
                ydiag, _, decs, ear = hw[d]
                y = [ydiag[hh] + yo[d][hh // per_g][(hh % per_g) * SSD_P:(hh % per_g + 1) * SSD_P] * ear[hh]
                     for hh in range(SSD_HEADS)]
                y_scr[pre[u][d]["c"]] += jnp.concatenate(y, axis=0)
                hm[d] = jnp.concatenate(
                    [hm[d][hh * SSD_P:(hh + 1) * SSD_P] * decs[hh]
                     + incs[d][hh // per_g][(hh % per_g) * SSD_P:(hh % per_g + 1) * SSD_P]
                     for hh in range(SSD_HEADS)], axis=0)
            if u + 1 < lanes:
                yo = [carried_out(pre[u + 1][d], hm[d]) for d in dirs]
                hw = hw_next
        for d in dirs:
            h_scrs[d][...] = hm[d]
        return carry

    lax.fori_loop(0, nc // lanes, step, 0)

    def fin_body(c, carry):
        gated = y_scr[c] * _silu(z_ref[c].astype(F32))
        o_ref[c] = _rms(gated, ng_ref[...], axis=0).astype(BF16)
        return carry

    lax.fori_loop(0, nc, fin_body, 0)


def _ssd_mixer(z_t, xbc_t, dt_t, cw, cb, dtb, alog, dsk, ng, *, layer, nc0):
    bsz, nc, _, q = xbc_t.shape
    hp = SSD_HEADS * SSD_P
    per_b = lambda f: pl.BlockSpec((None, nc, f, q), lambda b: (b, 0, 0, 0))
    full = lambda a: _layer_spec(a, layer)
    kern = functools.partial(_ssd_kernel, nc=nc, nc0=nc0)
    return pl.pallas_call(
        kern,
        grid=(bsz,),
        in_specs=[per_b(hp), per_b(SSD_CONV_DIM), per_b(2 * SSD_HEADS),
                  full(cw), full(cb), full(dtb), full(alog), full(dsk), full(ng)],
        out_specs=per_b(hp),
        out_shape=jax.ShapeDtypeStruct((bsz, nc, hp, q), BF16),
        scratch_shapes=[pltpu.VMEM((nc, SSD_CONV_DIM, q), F32), pltpu.VMEM((nc, hp, q), F32),
                        pltpu.VMEM((hp, SSD_GN), F32), pltpu.VMEM((hp, SSD_GN), F32)],
        compiler_params=_cparams("arbitrary"),
        name="ssd_mixer",
    )(z_t, xbc_t, dt_t, cw, cb, dtb, alog, dsk, ng)


def _route(logits_t, bias_col):
    aff = _sigmoid(logits_t)
    sel = aff + bias_col
    rows = [sel[e:e + 1, :] for e in range(N_EXPERTS)]
    gscore = []
    for g in range(N_GROUPS):
        a, b, c, d = rows[PER_GROUP * g:PER_GROUP * (g + 1)]
        hi1, lo1, hi2, lo2 = jnp.maximum(a, b), jnp.minimum(a, b), jnp.maximum(c, d), jnp.minimum(c, d)
        gscore.append(jnp.maximum(hi1, hi2) + jnp.maximum(jnp.minimum(hi1, hi2), jnp.maximum(lo1, lo2)))
    best = jnp.zeros_like(gscore[0], dtype=jnp.int32)
    cur = gscore[0]
    for g in range(1, N_GROUPS):
        better = gscore[g] > cur
        best = jnp.where(better, g, best)
        cur = jnp.where(better, gscore[g], cur)
    eidx = lax.broadcasted_iota(jnp.int32, sel.shape, 0)
    masked = jnp.where(eidx // PER_GROUP == best, sel, -jnp.inf)
    m1 = jnp.max(masked, axis=0, keepdims=True)
    idx1 = jnp.min(jnp.where(masked == m1, eidx, N_EXPERTS), axis=0, keepdims=True)
    masked2 = jnp.where(eidx == idx1, -jnp.inf, masked)
    m2 = jnp.max(masked2, axis=0, keepdims=True)
    idx2 = jnp.min(jnp.where(masked2 == m2, eidx, N_EXPERTS), axis=0, keepdims=True)
    oh1, oh2 = eidx == idx1, eidx == idx2
    w1 = jnp.sum(jnp.where(oh1, aff, 0.0), axis=0, keepdims=True)
    w2 = jnp.sum(jnp.where(oh2, aff, 0.0), axis=0, keepdims=True)
    den = w1 + w2
    return oh1, oh2, w1 / den, w2 / den


def _split_hi_lo(x):
    hi = x.astype(BF16)
    return hi, (x - hi.astype(F32)).astype(BF16)


def _outproj_kernel(oda_ref, ossdt_ref, omla_ref, h_ref, mod_ref, ng_ref, wo_ref, rw_ref, rb_ref,
                    h1_ref, xs_ref, meta_ref, cnt_ref, *, n_ctx_tiles, ctx_row, d, tm, s_loc, nb):
    ti, bp = pl.program_id(0), pl.program_id(1)
    tiles = range(nb)
    r_i = lax.broadcasted_iota(jnp.int32, (tm, tm), 0)
    c_i = lax.broadcasted_iota(jnp.int32, (tm, tm), 1)
    before = jnp.where(r_i < c_i, 1.0, 0.0).astype(BF16)
    row_e = lax.broadcasted_iota(jnp.int32, (N_EXPERTS, LANES), 0)
    r_s = lax.broadcasted_iota(jnp.int32, (s_loc, tm), 0).astype(F32)
    lane_e = lax.broadcasted_iota(jnp.int32, (s_loc, MOE_EXTRA), 1)
    row_m = lax.broadcasted_iota(jnp.int32, (LANES, tm), 0)
    rows = [jnp.where(ti < n_ctx_tiles, ctx_row, bp * nb + bb) for bb in tiles]
    mod_at = lambda bb, k: mod_ref[pl.ds(rows[bb], 1), pl.ds(k * d, d)]
    ossd = [jnp.concatenate([ossdt_ref[bb, c].astype(F32).T for c in range(tm // SSD_CHUNK)], axis=0).astype(BF16)
            for bb in tiles]
    mix = [_dot(oda_ref[bb], wo_ref[0:DA_WIDTH]) + _dot(ossd[bb], wo_ref[DA_WIDTH:DA_WIDTH + SSD_INNER])
           + _dot(omla_ref[bb], wo_ref[DA_WIDTH + SSD_INNER:]) for bb in tiles]
    h1 = [h_ref[bb] + mod_at(bb, 2) * mix[bb] for bb in tiles]
    for bb in tiles:
        h1_ref[bb] = h1[bb]
    u = [_ada_norm(h1[bb], ng_ref[...], mod_at(bb, 3), mod_at(bb, 4)) for bb in tiles]
    u16 = [x.astype(BF16) for x in u]
    rw_hi, rw_lo = _split_hi_lo(rw_ref[...])
    u_lo = [(u[bb] - u16[bb].astype(F32)).astype(BF16) for bb in tiles]
    logits = [_dot(u16[bb], rw_hi) + _dot(u_lo[bb], rw_hi) + _dot(u16[bb], rw_lo) for bb in tiles]
    routed = [_route(logits[bb].T[:N_EXPERTS], rb_ref[...]) for bb in tiles]
    cnt = [jnp.where(r[0], 1.0, 0.0) + jnp.where(r[1], 1.0, 0.0) for r in routed]
    rank = [_dot(cnt[bb].astype(BF16), before) for bb in tiles]
    for bb in tiles:
        oh1, oh2, w1, w2 = routed[bb]
        tot = jnp.sum(cnt[bb], axis=1, keepdims=True)
        ptot = jnp.floor((tot + (MOE_BLK - 1)) * (1.0 / MOE_BLK)) * MOE_BLK
        ptot_b = jnp.broadcast_to(ptot, (N_EXPERTS, LANES))
        cnt_ref[bb] = ptot_b
        run = jnp.zeros((1, LANES), F32)
        off = jnp.zeros((N_EXPERTS, LANES), F32)
        for e in range(1, N_EXPERTS):
            run = run + ptot_b[e - 1:e]
            off = jnp.where(row_e == e, run, off)
        slot = off[:, 0:1] + rank[bb]
        dest1 = jnp.sum(jnp.where(oh1, slot, 0.0), axis=0, keepdims=True)
        dest2 = jnp.sum(jnp.where(oh2, slot, 0.0), axis=0, keepdims=True)
        routed[bb] = (r_s == dest1, r_s == dest2, w1, w2)
        meta_ref[bb] = jnp.where(row_m == 0, dest1, jnp.where(row_m == 1, dest2, 0.0)).T
    perm = [jnp.where(routed[bb][0] | routed[bb][1], 1.0, 0.0).astype(BF16) for bb in tiles]
    xs = [_dot(perm[bb], u16[bb]) for bb in tiles]
    for bb in tiles:
        p1, p2, w1, w2 = routed[bb]
        xs_ref[bb, :, 0:d] = xs[bb].astype(BF16)
        wslot = jnp.sum(jnp.where(p1, w1, 0.0) + jnp.where(p2, w2, 0.0), axis=1, keepdims=True)
        w_hi = wslot.astype(BF16).astype(F32)
        xs_ref[bb, :, d:] = jnp.where(lane_e == 0, w_hi, jnp.where(lane_e == 1, wslot - w_hi, 0.0)).astype(BF16)


def _outproj_dispatch(oda, ossd_t, omla, h, mod, ng, wo, rw, rb, *, layer, tm, n_ctx, s_loc, nb):
    bsz, t, d = h.shape
    nt, q = t // tm, SSD_CHUNK
    tok = lambda w: pl.BlockSpec((nb, tm, w), lambda ti, b: (b, ti, 0))
    full = lambda a: pl.BlockSpec(a.shape, lambda ti, b: (0,) * a.ndim)
    tile = lambda r, w: pl.BlockSpec((nb, None, r, w), lambda ti, b: (b, ti, 0, 0))
    lay = lambda a: _layer_spec(a, layer)
    kern = functools.partial(_outproj_kernel, n_ctx_tiles=n_ctx // tm, ctx_row=bsz, d=d, tm=tm, s_loc=s_loc, nb=nb)
    return pl.pallas_call(
        kern,
        grid=(nt, bsz // nb),
        in_specs=[tok(DA_WIDTH), pl.BlockSpec((nb, tm // q, SSD_INNER, q), lambda ti, b: (b, ti, 0, 0)),
                  tok(MLA_WIDTH), tok(d), lay(mod), lay(ng), lay(wo), full(rw), full(rb)],
        out_specs=[tok(d), tile(s_loc, d + MOE_EXTRA), tile(tm, LANES), tile(N_EXPERTS, LANES)],
        out_shape=[jax.ShapeDtypeStruct((bsz, t, d), F32),
                   jax.ShapeDtypeStruct((bsz, nt, s_loc, d + MOE_EXTRA), BF16),
                   jax.ShapeDtypeStruct((bsz, nt, tm, LANES), F32),
                   jax.ShapeDtypeStruct((bsz, nt, N_EXPERTS, LANES), F32)],
        compiler_params=_cparams("arbitrary", "arbitrary"),
        name="outproj_dispatch",
    )(oda, ossd_t, omla, h, mod, ng, wo, rw, rb)


def _expert_kernel(se_ref, bi_ref, sv_ref, sn_ref, xs_hbm, wg_ref, wu_ref, wd_ref, y_ref, xbuf, sem, wg16, wu16, wd16,
                   *, d):
    s = pl.program_id(0)
    last = pl.num_programs(0) - 1
    slot = s % 2

    def start_all(step_, slot_):
        for j in range(MOE_STEP_BLKS):
            blk = bi_ref[step_ * MOE_STEP_BLKS + j]
            pltpu.make_async_copy(xs_hbm.at[pl.ds(pl.multiple_of(blk * MOE_BLK, MOE_BLK), MOE_BLK), :],
                                  xbuf.at[slot_, pl.ds(j * MOE_BLK, MOE_BLK), :], sem.at[slot_]).start(priority=j % 2)

    def wait_all(slot_):
        for j in range(MOE_STEP_BLKS):
            pltpu.make_async_copy(xs_hbm.at[pl.ds(0, MOE_BLK), :], xbuf.at[slot_, pl.ds(j * MOE_BLK, MOE_BLK), :],
                                  sem.at[slot_]).wait()

    @pl.when(s == 0)
    def _():
        start_all(s, slot)

    start_all(jnp.minimum(s + 1, last), 1 - slot)
    wait_all(slot)

    @pl.when(sn_ref[s] > 0)
    def _():
        wg16[...] = wg_ref[...].astype(BF16)
        wu16[...] = wu_ref[...].astype(BF16)
        wd16[...] = wd_ref[...].astype(BF16)

    @pl.when(sv_ref[s] > 0)
    def _():
        xm = xbuf[slot, :, 0:d]
        wx = xbuf[slot, :, d:]
        wrow = wx[:, 0:1].astype(F32) + wx[:, 1:2].astype(F32)
        he = _silu(_dot(xm, wg16[...])) * _dot(xm, wu16[...])
        y_ref[...] = (_dot(he.astype(BF16), wd16[...]) * wrow).astype(BF16)

    @pl.when(sv_ref[s] == 0)
    def _():
        y_ref[...] = jnp.zeros_like(y_ref)

    @pl.when(s == last)
    def _():
        wait_all(1 - slot)


def _expert_ffn(xs2d, wg, wu, wd, step_e, blk_ids, step_valid, step_new, *, layer, n_steps):
    d = wg.shape[2]
    rows = MOE_STEP_BLKS * MOE_BLK
    wspec = lambda a: pl.BlockSpec((None, None) + a.shape[2:], lambda s, se, bi, sv, sn: (layer, se[s], 0, 0))
    grid_spec = pltpu.PrefetchScalarGridSpec(
        num_scalar_prefetch=4,
        grid=(n_steps,),
        in_specs=[pl.BlockSpec(memory_space=pl.ANY), wspec(wg), wspec(wu), wspec(wd)],
        out_specs=pl.BlockSpec((rows, d), lambda s, se, bi, sv, sn: (s, 0)),
        scratch_shapes=[pltpu.VMEM((2, rows, d + MOE_EXTRA), BF16), pltpu.SemaphoreType.DMA((2,)),
                        pltpu.VMEM(wg.shape[2:], BF16), pltpu.VMEM(wu.shape[2:], BF16),
                        pltpu.VMEM(wd.shape[2:], BF16)],
    )
    return pl.pallas_call(
        functools.partial(_expert_kernel, d=d),
        grid_spec=grid_spec,
        out_shape=jax.ShapeDtypeStruct((n_steps * rows, d), BF16),
        compiler_params=_cparams("arbitrary"),
        name="expert_ffn",
    )(step_e, blk_ids, step_valid, step_new, xs2d, wg, wu, wd)


def _moe_combine_rows(inv_ref, ys_hbm, meta_ref, h1_ref, mod_ref, ybuf, sem, *, n_lb, nt, n_ctx_tiles, ctx_row, d, tm,
                      s_loc, nb):
    ti, bp = pl.program_id(0), pl.program_id(1)
    nbp = pl.num_programs(1)
    step = ti * nbp + bp
    last = pl.num_programs(0) * nbp - 1
    slot = step % 2

    def block_copy(ti_, bp_, k, slot_):
        bb, j = divmod(k, n_lb)
        blk = inv_ref[((bp_ * nb + bb) * nt + ti_) * n_lb + j]
        return pltpu.make_async_copy(ys_hbm.at[pl.ds(pl.multiple_of(blk * MOE_BLK, MOE_BLK), MOE_BLK), :],
                                     ybuf.at[slot_, pl.ds(k * MOE_BLK, MOE_BLK), :], sem.at[slot_])

    def start_all(ti_, bp_, slot_):
        for k in range(nb * n_lb):
            block_copy(ti_, bp_, k, slot_).start(priority=k % 2)

    def wait_all(slot_):
        for k in range(nb * n_lb):
            pltpu.make_async_copy(ys_hbm.at[pl.ds(0, MOE_BLK), :], ybuf.at[slot_, pl.ds(k * MOE_BLK, MOE_BLK), :],
                                  sem.at[slot_]).wait()

    @pl.when(step == 0)
    def _():
        start_all(ti, bp, slot)

    wrap = bp + 1 == nbp
    ti_n = jnp.where(step == last, ti, jnp.where(wrap, ti + 1, ti))
    bp_n = jnp.where(step == last, bp, jnp.where(wrap, 0, bp + 1))
    start_all(ti_n, bp_n, 1 - slot)
    wait_all(slot)

    tiles = range(nb)
    lane_s = lax.broadcasted_iota(jnp.int32, (tm, s_loc), 1).astype(F32)
    metas = [meta_ref[bb] for bb in tiles]
    pts = [jnp.where((lane_s == m[:, 0:1]) | (lane_s == m[:, 1:2]), 1.0, 0.0).astype(BF16) for m in metas]
    y = [_dot(pts[bb], ybuf[slot, bb * s_loc:(bb + 1) * s_loc, :]) for bb in tiles]
    h2 = []
    for bb in tiles:
        row = jnp.where(ti < n_ctx_tiles, ctx_row, bp * nb + bb)
        h2.append(h1_ref[bb] + mod_ref[pl.ds(row, 1), pl.ds(5 * d, d)] * y[bb])

    def drain():
        @pl.when(step == last)
        def _():
            wait_all(1 - slot)

    return h2, drain


def _combine_kernel(inv_ref, ys_hbm, meta_ref, h1_ref, mod_ref, fg_ref, o_ref, ybuf, sem, *, final, nb, **geom):
    h2, drain = _moe_combine_rows(inv_ref, ys_hbm, meta_ref, h1_ref, mod_ref, ybuf, sem, nb=nb, **geom)
    for bb in range(nb):
        o_ref[bb] = _rms(h2[bb], fg_ref[...]) if final else h2[bb]
    drain()


def _combine(ys_em, inv, meta, h1, mod, fg, *, layer, tm, n_ctx, s_loc, final, nb):
    bsz, t, d = h1.shape
    nt = t // tm
    n_lb = s_loc // MOE_BLK
    full = lambda a: pl.BlockSpec(a.shape, lambda ti, b, inv: (0,) * a.ndim)
    grid_spec = pltpu.PrefetchScalarGridSpec(
        num_scalar_prefetch=1,
        grid=(nt, bsz // nb),
        in_specs=[pl.BlockSpec(memory_space=pl.ANY),
                  pl.BlockSpec((nb, None, tm, LANES), lambda ti, b, inv: (b, ti, 0, 0)),
                  pl.BlockSpec((nb, tm, d), lambda ti, b, inv: (b, ti, 0)),
                  _layer_spec(mod, layer), full(fg)],
        out_specs=pl.BlockSpec((nb, tm, d), lambda ti, b, inv: (b, ti, 0)),
        scratch_shapes=[pltpu.VMEM((2, nb * s_loc, d), BF16), pltpu.SemaphoreType.DMA((2,))],
    )
    kern = functools.partial(_combine_kernel, n_lb=n_lb, nt=nt, n_ctx_tiles=n_ctx // tm, ctx_row=bsz, d=d, tm=tm,
                             s_loc=s_loc, final=final, nb=nb)
    return pl.pallas_call(
        kern,
        grid_spec=grid_spec,
        out_shape=jax.ShapeDtypeStruct((bsz, t, d), F32),
        compiler_params=_cparams("arbitrary", "arbitrary"),
        name="moe_combine",
    )(inv, ys_em, meta, h1, mod, fg)


def _rope_tables(length, dim):
    rows = length // GRID_W
    row = jnp.repeat(jnp.arange(rows), GRID_W).astype(F32)
    col = jnp.tile(jnp.arange(GRID_W), rows).astype(F32)
    n_freq = dim // 4
    inv_freq = ROPE_THETA ** (-jnp.arange(n_freq, dtype=F32) / n_freq)
    ang = jnp.concatenate([row[:, None] * inv_freq, col[:, None] * inv_freq], axis=-1)
    return jnp.cos(ang), jnp.sin(ang)


def _table_set(n_ctx, n_lat):
    def lanes(cos, sin, lane0, width, reps, outside_cos):
        c = jnp.concatenate([cos, cos], axis=-1)
        s = jnp.concatenate([-sin, sin], axis=-1)
        grp_c = jnp.full((n_lat, width), outside_cos, F32).at[:, lane0:lane0 + c.shape[1]].set(c)
        grp_s = jnp.zeros((n_lat, width), F32).at[:, lane0:lane0 + s.shape[1]].set(s)
        ctx_c = jnp.full((n_ctx, width), outside_cos, F32).at[:, lane0:lane0 + c.shape[1]].set(1.0)
        ctx_s = jnp.zeros((n_ctx, width), F32)
        return (jnp.tile(jnp.concatenate([ctx_c, grp_c], axis=0), (1, reps)),
                jnp.tile(jnp.concatenate([ctx_s, grp_s], axis=0), (1, reps)))

    dcos, dsin = _rope_tables(n_lat, DA_QK)
    cda, sda = lanes(dcos, dsin, 0, DA_QK, DA_QW // DA_QK, 1.0)
    mcos, msin = _rope_tables(n_lat, MLA_ROPE)
    cq, sm = lanes(mcos, msin, KR_LANE0, LANES, 1, 1.0)
    ck, _ = lanes(mcos, msin, KR_LANE0, LANES, 1, 0.0)
    return cda, sda, cq, ck, sm


def _pack_w_in(w_in):
    w_in = w_in.astype(BF16)
    depth, d, _ = w_in.shape
    o_ssd = DA_IN
    o_mla = DA_IN + SSD_IN
    z = w_in[..., o_ssd:o_ssd + SSD_INNER]
    xbc = w_in[..., o_ssd + SSD_INNER:o_ssd + SSD_INNER + SSD_CONV_DIM]
    dt = w_in[..., o_ssd + SSD_INNER + SSD_CONV_DIM:o_mla]
    cq = w_in[..., o_mla:o_mla + MLA_Q_RANK]
    ckv = w_in[..., o_mla + MLA_Q_RANK:o_mla + MLA_Q_RANK + MLA_KV_RANK]
    kr = w_in[..., o_mla + MLA_Q_RANK + MLA_KV_RANK:]
    zeros = lambda n: jnp.zeros((depth, d, n), w_in.dtype)
    misc = jnp.concatenate([zeros(KR_LANE0), kr, dt, zeros(LANES - DT_LANE0 - 2 * SSD_HEADS)], axis=-1)
    return jnp.concatenate([w_in[..., :DA_IN], z, xbc, cq, ckv, misc], axis=-1)


def _pack_w_uq(w_uq):
    depth, r, _ = w_uq.shape
    w = w_uq.reshape(depth, r, MLA_HEADS, MLA_NOPE + MLA_ROPE)
    w = jnp.pad(w, ((0, 0), (0, 0), (0, 0), (0, MLA_HEAD_PAD - MLA_NOPE - MLA_ROPE)))
    return w.reshape(depth, r, MLA_QPAD).astype(BF16)


def _pack_w_ukv(w_ukv):
    depth, r, _ = w_ukv.shape
    w = w_ukv.reshape(depth, r, MLA_HEADS, MLA_NOPE + MLA_V)
    kn = jnp.pad(w[..., :MLA_NOPE], ((0, 0), (0, 0), (0, 0), (0, MLA_HEAD_PAD - MLA_NOPE)))
    return jnp.concatenate([kn.reshape(depth, r, MLA_QPAD), w[..., MLA_NOPE:].reshape(depth, r, MLA_WIDTH)],
                           axis=-1).astype(BF16)


def _dispatch_tables(pcnt, n_lb, n_steps):
    ntt = pcnt.shape[0]
    nb = (pcnt / MOE_BLK).astype(jnp.int32)
    lo = jnp.cumsum(nb, axis=1) - nb
    n_e = jnp.sum(nb, axis=0)
    p_e = (n_e + MOE_STEP_BLKS - 1) // MOE_STEP_BLKS * MOE_STEP_BLKS
    ends = jnp.cumsum(p_e)
    base = (ends - p_e)[None, :] + jnp.cumsum(nb, axis=0) - nb
    lb = jnp.arange(n_lb, dtype=jnp.int32)
    owner = (lb[None, :, None] >= lo[:, None, :]) & (lb[None, :, None] < (lo + nb)[:, None, :])
    used = jnp.any(owner, axis=-1)
    pos = jnp.sum(jnp.where(owner, (base - lo)[:, None, :], 0), axis=-1) + lb[None, :]
    inv = jnp.where(used, pos, 0).astype(jnp.int32)
    n_pos = n_steps * MOE_STEP_BLKS
    src = jnp.arange(ntt, dtype=jnp.int32)[:, None] * n_lb + lb[None, :]
    fwd = jnp.zeros((n_pos,), jnp.int32).at[jnp.where(used, pos, n_pos).reshape(-1)].set(src.reshape(-1), mode="drop")
    starts = jnp.arange(n_steps, dtype=jnp.int32) * MOE_STEP_BLKS
    step_e = jnp.minimum(jnp.sum(starts[:, None] >= ends[None, :], axis=1), N_EXPERTS - 1).astype(jnp.int32)
    step_valid = (starts < ends[-1]).astype(jnp.int32)
    step_new = jnp.concatenate([jnp.ones((1,), jnp.int32), (step_e[1:] != step_e[:-1]).astype(jnp.int32)])
    fwd2 = fwd.reshape(n_steps, MOE_STEP_BLKS)
    last = lax.dynamic_slice_in_dim(fwd2, jnp.maximum(ends[-1] // MOE_STEP_BLKS - 1, 0), 1, axis=0)
    fwd = jnp.where(step_valid[:, None] > 0, fwd2, last).reshape(-1)
    return inv.reshape(-1), fwd, step_e, step_valid, step_new


def kernel(x, c, ctx, c_ctx, norm_mix_g, norm_ffn_g, w_mod, b_mod, w_in, w_out, da_lambda, da_subln_g, ssd_conv_w, ssd_conv_b, ssd_dt_bias, ssd_a_log, ssd_d, ssd_norm_g, mla_q_norm_g, mla_kv_norm_g, mla_w_uq, mla_w_ukv, router_w, router_bias, exp_w_gate, exp_w_up, exp_w_down, final_norm_g):
    bsz, n_lat, d = x.shape
    n_ctx = ctx.shape[1]
    t = n_ctx + n_lat
    tm = min(256, n_ctx)
    assert n_ctx % tm == 0 and n_lat % tm == 0 and n_ctx % SSD_CHUNK == 0 and n_lat % SSD_CHUNK == 0
    nt = t // tm
    s_loc = 2 * tm + 2 * LANES
    n_lb = s_loc // MOE_BLK
    n_steps = -(-(bsz * nt * n_lb + N_EXPERTS * (MOE_STEP_BLKS - 1)) // MOE_STEP_BLKS)
    nc, nc0 = t // SSD_CHUNK, n_ctx // SSD_CHUNK
    nb = 2 if bsz % 2 == 0 else 1
    nb_wide = 4 if bsz % 4 == 0 else nb
    q = SSD_CHUNK

    r_pad = -(-(bsz + 1) // 8) * 8
    c_rows = jnp.concatenate([c, c_ctx[None, :], jnp.zeros((r_pad - bsz - 1, d), F32)], axis=0)
    mod = _modulation(c_rows, w_mod, b_mod)

    w_in_p = _pack_w_in(w_in)
    w_uq_p = _pack_w_uq(mla_w_uq)
    w_ukv_p = _pack_w_ukv(mla_w_ukv)
    w_out16 = w_out.astype(BF16)
    tabs = _table_set(n_ctx, n_lat)
    rw_pad = jnp.pad(router_w, ((0, 0), (0, LANES - N_EXPERTS)))
    rb = router_bias.reshape(N_EXPERTS, 1)
    lane_b = lambda v: jnp.broadcast_to(v[..., None], v.shape + (q,))
    dsk_rows = jnp.repeat(ssd_d, SSD_P, axis=-1)
    ng_mix, ng_ffn = norm_mix_g[:, None, :], norm_ffn_g[:, None, :]
    qg, kvg = mla_q_norm_g[:, None, :], mla_kv_norm_g[:, None, :]
    sub_g = jnp.tile(da_subln_g, (1, DA_HEADS))[:, :, None]
    ssd_par = (lane_b(ssd_conv_w), lane_b(ssd_conv_b), lane_b(ssd_dt_bias.reshape(DEPTH, -1)),
               lane_b(ssd_a_log.reshape(DEPTH, -1)), lane_b(dsk_rows), lane_b(ssd_norm_g))

    h = jnp.concatenate([ctx, x], axis=1)
    moe = None
    for i in range(DEPTH):
        lam_init = 0.8 - 0.6 * math.exp(-0.3 * i)
        if moe is None:
            qda_t, kda, vda_t, z_t, xbc_t, dt_t, q4t, k4, vm_t = _inproj(
                h, mod, ng_mix, w_in_p, tabs, qg, kvg, w_uq_p, w_ukv_p, layer=i, tm=tm, n_ctx=n_ctx, nb=nb_wide)
        else:
            qda_t, kda, vda_t, z_t, xbc_t, dt_t, q4t, k4, vm_t, h = _inproj(
                None, mod, ng_mix, w_in_p, tabs, qg, kvg, w_uq_p, w_ukv_p, layer=i, tm=tm, n_ctx=n_ctx, nb=nb, moe=moe)
        o_da = _da_attention(qda_t, kda, vda_t, da_lambda, sub_g, layer=i, lam_init=lam_init, tq=tm, n_ctx=n_ctx)
        o_mla = _mla_attention(q4t, k4, vm_t, tq=tm, n_ctx=n_ctx)
        o_ssd_t = _ssd_mixer(z_t, xbc_t, dt_t, *ssd_par, layer=i, nc0=nc0)
        h1, xs, meta, cnt = _outproj_dispatch(o_da, o_ssd_t, o_mla, h, mod, ng_ffn, w_out16, rw_pad, rb, layer=i, tm=tm,
                                              n_ctx=n_ctx, s_loc=s_loc, nb=nb_wide)
        inv, fwd, step_e, step_valid, step_new = _dispatch_tables(cnt[:, :, :, 0].reshape(bsz * nt, N_EXPERTS), n_lb,
                                                                  n_steps)
        ys_em = _expert_ffn(xs.reshape(bsz * nt * s_loc, d + MOE_EXTRA), exp_w_gate, exp_w_up, exp_w_down, step_e, fwd,
                            step_valid, step_new, layer=i, n_steps=n_steps)
        moe = (ys_em, inv, meta, h1, s_loc)
    h = _combine(ys_em, inv, meta, h1, mod, final_norm_g[None, :], layer=DEPTH - 1, tm=tm, n_ctx=n_ctx, s_loc=s_loc,
                 final=True, nb=nb)
    return h[:, n_ctx:]
```

```python
import functools
import math

import jax
import jax.numpy as jnp
from jax import lax
from jax.experimental import pallas as pl
from jax.experimental.pallas import tpu as pltpu

F32 = jnp.float32
BF16 = jnp.bfloat16

DEPTH = 4
GRID_W = 64
EPS = 1e-6
ROPE_THETA = 10000.0
DA_HEADS, DA_QK = 4, 32
DA_V = 2 * DA_QK
DA_WIDTH = DA_HEADS * DA_V
DA_QW = DA_HEADS * 2 * DA_QK
DA_IN = 2 * DA_QW + DA_WIDTH
SSD_HEADS, SSD_P, SSD_GROUPS, SSD_STATE, SSD_CHUNK = 8, 64, 2, 64, 128
SSD_INNER = SSD_HEADS * SSD_P
SSD_GN = SSD_GROUPS * SSD_STATE
SSD_CONV_DIM = SSD_INNER + 2 * SSD_GN
SSD_IN = SSD_INNER + SSD_CONV_DIM + 2 * SSD_HEADS
MLA_HEADS, MLA_Q_RANK, MLA_KV_RANK, MLA_NOPE, MLA_ROPE, MLA_V = 4, 256, 128, 64, 32, 64
MLA_WIDTH = MLA_HEADS * MLA_V
MLA_IN = MLA_Q_RANK + MLA_KV_RANK + MLA_ROPE
MLA_SCALE = (MLA_NOPE + MLA_ROPE) ** -0.5
N_EXPERTS, N_GROUPS, D_EXPERT = 16, 4, 512
PER_GROUP = N_EXPERTS // N_GROUPS

LANES = 128
SUBLANES_BF16 = 16
VMEM_LIMIT_BYTES = 56 * 1024 * 1024

MLA_HEAD_PAD = LANES
MLA_QPAD = MLA_HEADS * MLA_HEAD_PAD
KR_LANE0 = MLA_NOPE
DT_LANE0 = MLA_NOPE + MLA_ROPE
MAIN_COLS = DA_IN + SSD_INNER + SSD_CONV_DIM
C_Z = DA_IN
C_XBC = C_Z + SSD_INNER
TAIL_COLS = MLA_Q_RANK + MLA_KV_RANK + LANES
T_CKV = MLA_Q_RANK
T_MISC = T_CKV + MLA_KV_RANK
LOG2E = math.log2(math.e)
DA_QSCALE = DA_QK ** -0.5 * LOG2E
MLA_QSCALE = MLA_SCALE * LOG2E

KEY_CHUNK = 256
MOE_BLK = SUBLANES_BF16
MOE_STEP_BLKS = 32
MOE_EXTRA = LANES


def _sigmoid(x):
    return 1.0 / (1.0 + jnp.exp(-x))


def _silu(x):
    return x * _sigmoid(x)


def _rms(x, g, axis=-1):
    return x * lax.rsqrt(jnp.mean(x * x, axis=axis, keepdims=True) + EPS) * g


def _ada_norm(hv, g, shift, scale):
    return _rms(hv, g) * (1.0 + scale) + shift


def _rope(x, cos, sin_signed, half):
    w = x.shape[-1]
    lane = lax.broadcasted_iota(jnp.int32, x.shape, x.ndim - 1)
    first = (lane % (2 * half)) < half
    partner = jnp.where(first, pltpu.roll(x, w - half, x.ndim - 1), pltpu.roll(x, half, x.ndim - 1))
    return x * cos + partner * sin_signed


def _dot(a, b):
    return jnp.dot(a, b, preferred_element_type=F32)


def _dot_hi(a, b):
    return jnp.dot(a, b, preferred_element_type=F32, precision=lax.Precision.HIGHEST)


def _dot_split3(x, m01):
    x1 = x.astype(BF16)
    r1 = x - x1.astype(F32)
    x2 = r1.astype(BF16)
    x3 = (r1 - x2.astype(F32)).astype(BF16)
    return _dot(x1, m01) + _dot(x2, m01) + _dot(x3, m01)


def _layer_spec(a, i):
    return pl.BlockSpec((None,) + a.shape[1:], lambda *_: (i,) + (0,) * (a.ndim - 1))


def _cparams(*sem):
    return pltpu.CompilerParams(dimension_semantics=sem, vmem_limit_bytes=VMEM_LIMIT_BYTES)


def _mod_kernel(c_ref, w_ref, b_ref, o_ref):
    o_ref[...] = _dot_hi(_silu(c_ref[...]), w_ref[...]) + b_ref[...]


def _modulation(c_rows, w_mod, b_mod):
    depth, d, n = w_mod.shape
    r = c_rows.shape[0]
    tn = 1536
    return pl.pallas_call(
        _mod_kernel,
        grid=(depth, n // tn),
        in_specs=[pl.BlockSpec((r, d), lambda l, j: (0, 0)),
                  pl.BlockSpec((None, d, tn), lambda l, j: (l, 0, j)),
                  pl.BlockSpec((None, 1, tn), lambda l, j: (l, 0, j))],
        out_specs=pl.BlockSpec((None, r, tn), lambda l, j: (l, 0, j)),
        out_shape=jax.ShapeDtypeStruct((depth, r, n), F32),
        compiler_params=_cparams("arbitrary", "arbitrary"),
        name="modulation",
    )(c_rows, w_mod, b_mod.reshape(depth, 1, n))


def _inproj_kernel(*refs, n_ctx_tiles, ctx_row, d, nb, tm, moe_geom):
    if moe_geom is None:
        h_ref, refs = refs[0], refs[1:]
    else:
        (inv_ref, ys_hbm, meta_ref, h1_ref, modp_ref), refs = refs[:5], refs[5:]
        h2_ref, ybuf, sem = refs[-3:]
        refs = refs[:-3]
    (mod_ref, ng_ref, wa_ref, wb_ref, cda_ref, sda_ref, cq_ref, ck_ref, sm_ref, qg_ref, kvg_ref, wuq_ref, wukv_ref,
     qdat_ref, kda_ref, vdat_ref, zt_ref, xbct_ref, dtt_ref, q4t_ref, k4_ref, vmt_ref) = refs
    ti, bp = pl.program_id(0), pl.program_id(1)
    if moe_geom is None:
        hs = [h_ref[bb] for bb in range(nb)]
    else:
        hs, drain = _moe_combine_rows(inv_ref, ys_hbm, meta_ref, h1_ref, modp_ref, ybuf, sem, nb=nb, tm=tm, d=d,
                                      n_ctx_tiles=n_ctx_tiles, ctx_row=ctx_row, **moe_geom)
        for bb in range(nb):
            h2_ref[bb] = hs[bb]
    cda, sda, sm = cda_ref[...], sda_ref[...], sm_ref[...]
    cos_q = jnp.concatenate([cq_ref[...]] * MLA_HEADS, axis=1)
    sin_q = jnp.concatenate([sm] * MLA_HEADS, axis=1)
    q = SSD_CHUNK
    accs = []
    for bb in range(nb):
        row = jnp.where(ti < n_ctx_tiles, ctx_row, bp * nb + bb)
        shift = mod_ref[pl.ds(row, 1), pl.ds(0, d)]
        scale = mod_ref[pl.ds(row, 1), pl.ds(d, d)]
        u = _ada_norm(hs[bb], ng_ref[...], shift, scale).astype(BF16)
        accs.append((_dot(u, wa_ref[...]), _dot(u, wb_ref[...])))
    for bb in range(nb):
        acc, tail = accs[bb]
        qdat_ref[bb] = (_rope(acc[:, 0:DA_QW], cda, sda, DA_QK // 2) * DA_QSCALE).T.astype(BF16)
        kda_ref[bb] = _rope(acc[:, DA_QW:2 * DA_QW], cda, sda, DA_QK // 2).astype(BF16)
        vdat_ref[bb] = acc[:, 2 * DA_QW:DA_IN].T.astype(BF16)
        z_t = acc[:, C_Z:C_XBC].T
        xbc_t = acc[:, C_XBC:MAIN_COLS].T
        misc = tail[:, T_MISC:TAIL_COLS]
        dt_t = misc.T[DT_LANE0:DT_LANE0 + 2 * SSD_HEADS]
        for c in range(tm // q):
            zt_ref[bb, c] = z_t[:, c * q:(c + 1) * q].astype(BF16)
            xbct_ref[bb, c] = xbc_t[:, c * q:(c + 1) * q]
            dtt_ref[bb, c] = dt_t[:, c * q:(c + 1) * q]
        cqn = _rms(tail[:, 0:T_CKV], qg_ref[...]).astype(BF16)
        qm_t = (_rope(_dot(cqn, wuq_ref[...]), cos_q, sin_q, MLA_ROPE // 2) * MLA_QSCALE).T
        ckvn = _rms(tail[:, T_CKV:T_MISC], kvg_ref[...]).astype(BF16)
        kv = _dot(ckvn, wukv_ref[...])
        kr = _rope(misc, ck_ref[...], sm, MLA_ROPE // 2)
        km = kv[:, :MLA_QPAD] + jnp.concatenate([kr] * MLA_HEADS, axis=1)
        for hh in range(MLA_HEADS):
            q4t_ref[bb, hh] = qm_t[hh * MLA_HEAD_PAD:(hh + 1) * MLA_HEAD_PAD].astype(BF16)
            k4_ref[bb, hh] = km[:, hh * MLA_HEAD_PAD:(hh + 1) * MLA_HEAD_PAD].astype(BF16)
        vmt_ref[bb] = kv[:, MLA_QPAD:].T.astype(BF16)
    if moe_geom is not None:
        drain()


def _inproj(h, mod, ng, w_in_p, tabs, qg, kvg, wuq_p, wukv_p, *, layer, tm, n_ctx, nb, moe=None):
    bsz, t, d = (h if moe is None else moe[3]).shape
    nt, q = t // tm, SSD_CHUNK
    cpt = tm // q
    tok = lambda w: pl.BlockSpec((nb, tm, w), lambda ti, b, *_: (b, ti, 0))
    tab = lambda w: pl.BlockSpec((tm, w), lambda ti, b, *_: (ti, 0))
    chunked = lambda f: pl.BlockSpec((nb, cpt, f, q), lambda ti, b, *_: (b, ti, 0, 0))
    lay = lambda a: _layer_spec(a, layer)
    cda, sda, cq, ck, sm = tabs
    sds = jax.ShapeDtypeStruct
    tok_t = lambda w: pl.BlockSpec((nb, w, tm), lambda ti, b, *_: (b, 0, ti))
    out_specs = [tok_t(DA_QW), tok(DA_QW), tok_t(DA_WIDTH),
                 chunked(SSD_INNER), chunked(SSD_CONV_DIM), chunked(2 * SSD_HEADS),
                 pl.BlockSpec((nb, MLA_HEADS, MLA_HEAD_PAD, tm), lambda ti, b, *_: (b, 0, 0, ti)),
                 pl.BlockSpec((nb, MLA_HEADS, tm, MLA_HEAD_PAD), lambda ti, b, *_: (b, 0, ti, 0)), tok_t(MLA_WIDTH)]
    out_shape = [sds((bsz, DA_QW, t), BF16), sds((bsz, t, DA_QW), BF16), sds((bsz, DA_WIDTH, t), BF16),
                 sds((bsz, t // q, SSD_INNER, q), BF16), sds((bsz, t // q, SSD_CONV_DIM, q), F32),
                 sds((bsz, t // q, 2 * SSD_HEADS, q), F32),
                 sds((bsz, MLA_HEADS, MLA_HEAD_PAD, t), BF16), sds((bsz, MLA_HEADS, t, MLA_HEAD_PAD), BF16),
                 sds((bsz, MLA_WIDTH, t), BF16)]
    w_main, w_tail = w_in_p
    common_specs = [lay(mod), lay(ng), lay(w_main), lay(w_tail), tab(DA_QW), tab(DA_QW), tab(LANES), tab(LANES),
                    tab(LANES), lay(qg), lay(kvg), lay(wuq_p), lay(wukv_p)]
    common_args = (mod, ng, w_main, w_tail, cda, sda, cq, ck, sm, qg, kvg, wuq_p, wukv_p)
    geom = dict(n_ctx_tiles=n_ctx // tm, ctx_row=bsz, d=d, nb=nb, tm=tm)
    if moe is None:
        return pl.pallas_call(
            functools.partial(_inproj_kernel, moe_geom=None, **geom),
            grid=(nt, bsz // nb),
            in_specs=[tok(d)] + common_specs,
            out_specs=out_specs,
            out_shape=out_shape,
            compiler_params=_cparams("arbitrary", "arbitrary"),
            name="inproj",
        )(h, *common_args)
    ys_em, inv, meta, h1, s_loc = moe
    grid_spec = pltpu.PrefetchScalarGridSpec(
        num_scalar_prefetch=1,
        grid=(nt, bsz // nb),
        in_specs=[pl.BlockSpec(memory_space=pl.ANY),
                  pl.BlockSpec((nb, None, tm, LANES), lambda ti, b, inv: (b, ti, 0, 0)), tok(d),
                  _layer_spec(mod, layer - 1)] + common_specs,
        out_specs=out_specs + [tok(d)],
        scratch_shapes=[pltpu.VMEM((2, nb * s_loc, d), BF16), pltpu.SemaphoreType.DMA((2,))],
    )
    moe_geom = dict(n_lb=s_loc // MOE_BLK, nt=nt, s_loc=s_loc)
    return pl.pallas_call(
        functools.partial(_inproj_kernel, moe_geom=moe_geom, **geom),
        grid_spec=grid_spec,
        out_shape=out_shape + [sds((bsz, t, d), F32)],
        compiler_params=_cparams("arbitrary", "arbitrary"),
        name="combine_inproj",
    )(inv, ys_em, meta, h1, mod, *common_args)


V_AUG = DA_V + SUBLANES_BF16


def _scores_pass(k_at, qtm, nk, s_scr):
    kc = KEY_CHUNK if nk % KEY_CHUNK == 0 else LANES
    m = None
    for c0 in range(0, nk, kc):
        s_c = _dot(k_at(c0, c0 + kc), qtm)
        s_scr[c0:c0 + kc, :] = s_c
        part = jnp.max(s_c, axis=0, keepdims=True)
        m = part if m is None else jnp.maximum(m, part)
    return m


def _pv_pass(vaug_at, nk, s_scr, m):
    kc = KEY_CHUNK if nk % KEY_CHUNK == 0 else LANES
    acc = None
    for c0 in range(0, nk, kc):
        e = jnp.exp2(s_scr[c0:c0 + kc, :] - m).astype(BF16)
        inc = _dot(vaug_at(c0, c0 + kc), e)
        acc = inc if acc is None else acc + inc
    return acc


def _attend_heads(k_of, qt_of, vaug_of, nk, scr, n_sub):
    m = _scores_pass(k_of(0), qt_of(0), nk, scr[0])
    outs = []
    for j in range(n_sub):
        if j + 1 < n_sub:
            m_next = _scores_pass(k_of(j + 1), qt_of(j + 1), nk, scr[(j + 1) % 2])
        acc = _pv_pass(vaug_of(j), nk, scr[j % 2], m)
        outs.append(acc[:DA_V] * (1.0 / acc[DA_V:DA_V + 1]))
        if j + 1 < n_sub:
            m = m_next
    return outs


def _fill_vaug(vt_ref, vaug_scr, n_heads):
    t = vt_ref.shape[1]
    for hh in range(n_heads):
        vaug_scr[hh, 0:DA_V, :] = vt_ref[hh * DA_V:(hh + 1) * DA_V, :]
        vaug_scr[hh, DA_V:V_AUG, :] = jnp.ones((V_AUG - DA_V, t), BF16)


def _da_attn_kernel(lam_ref, g_ref, qt_ref, k_ref, vt_ref, o_ref, s0_scr, s1_scr, vaug_scr, *, n_ctx, n_ctx_tiles,
                    lam_init, tq):
    qi = pl.program_id(1)

    @pl.when(qi == 0)
    def _():
        _fill_vaug(vt_ref, vaug_scr, DA_HEADS)

    lv = lam_ref[...]
    lam = (jnp.exp(jnp.sum(lv[0:1] * lv[1:2], axis=-1, keepdims=True))
           - jnp.exp(jnp.sum(lv[2:3] * lv[3:4], axis=-1, keepdims=True)) + lam_init)
    row_q = lax.broadcasted_iota(jnp.int32, (DA_QW, 1), 0)

    def attend(nk):
        qt = qt_ref[...]
        qt_of = lambda j: qt * jnp.where(row_q // DA_QK == j, 1.0, 0.0).astype(BF16)
        k_of = lambda j: (lambda c0, c1: k_ref[c0:c1, :])
        vaug_of = lambda j: (lambda c0, c1: vaug_scr[j // 2, :, c0:c1])
        outs = _attend_heads(k_of, qt_of, vaug_of, nk, (s0_scr, s1_scr), 2 * DA_HEADS)
        heads = []
        for hh in range(DA_HEADS):
            o = outs[2 * hh] - lam * outs[2 * hh + 1]
            heads.append(o * lax.rsqrt(jnp.mean(o * o, axis=0, keepdims=True) + EPS))
        o_t = jnp.concatenate(heads, axis=0) * g_ref[...] * (1.0 - lam_init)
        o_ref[...] = o_t.T.astype(BF16)

    @pl.when(qi < n_ctx_tiles)
    def _():
        attend(n_ctx)

    @pl.when(qi >= n_ctx_tiles)
    def _():
        attend(k_ref.shape[0])


def _da_attention(qt, k, vt, lam_vec, g_col, *, layer, lam_init, tq, n_ctx):
    bsz, _, t = qt.shape
    kern = functools.partial(_da_attn_kernel, n_ctx=n_ctx, n_ctx_tiles=n_ctx // tq, lam_init=lam_init, tq=tq)
    return pl.pallas_call(
        kern,
        grid=(bsz, t // tq),
        in_specs=[_layer_spec(lam_vec, layer), _layer_spec(g_col, layer),
                  pl.BlockSpec((None, DA_QW, tq), lambda b, i: (b, 0, i)),
                  pl.BlockSpec((None, t, DA_QW), lambda b, i: (b, 0, 0)),
                  pl.BlockSpec((None, DA_WIDTH, t), lambda b, i: (b, 0, 0))],
        out_specs=pl.BlockSpec((None, tq, DA_WIDTH), lambda b, i: (b, i, 0)),
        out_shape=jax.ShapeDtypeStruct((bsz, t, DA_WIDTH), BF16),
        scratch_shapes=[pltpu.VMEM((t, tq), F32), pltpu.VMEM((t, tq), F32), pltpu.VMEM((DA_HEADS, V_AUG, t), BF16)],
        compiler_params=_cparams("arbitrary", "arbitrary"),
        name="da_attention",
    )(lam_vec, g_col, qt, k, vt)


def _mla_attn_kernel(qt_ref, k_ref, vt_ref, o_ref, s0_scr, s1_scr, vaug_scr, *, n_ctx, n_ctx_tiles, tq):
    qi = pl.program_id(1)

    @pl.when(qi == 0)
    def _():
        _fill_vaug(vt_ref, vaug_scr, MLA_HEADS)

    def attend(nk):
        k_of = lambda hh: (lambda c0, c1: k_ref[hh, c0:c1, :])
        vaug_of = lambda hh: (lambda c0, c1: vaug_scr[hh, :, c0:c1])
        outs = _attend_heads(k_of, lambda hh: qt_ref[hh], vaug_of, nk, (s0_scr, s1_scr), MLA_HEADS)
        o_ref[...] = jnp.concatenate(outs, axis=0).T.astype(BF16)

    @pl.when(qi < n_ctx_tiles)
    def _():
        attend(n_ctx)

    @pl.when(qi >= n_ctx_tiles)
    def _():
        attend(k_ref.shape[1])


def _mla_attention(q4t, k4, vt, *, tq, n_ctx):
    bsz, _, _, t = q4t.shape
    kern = functools.partial(_mla_attn_kernel, n_ctx=n_ctx, n_ctx_tiles=n_ctx // tq, tq=tq)
    return pl.pallas_call(
        kern,
        grid=(bsz, t // tq),
        in_specs=[pl.BlockSpec((None, MLA_HEADS, MLA_HEAD_PAD, tq), lambda b, i: (b, 0, 0, i)),
                  pl.BlockSpec((None, MLA_HEADS, t, MLA_HEAD_PAD), lambda b, i: (b, 0, 0, 0)),
                  pl.BlockSpec((None, MLA_WIDTH, t), lambda b, i: (b, 0, 0))],
        out_specs=pl.BlockSpec((None, tq, MLA_WIDTH), lambda b, i: (b, i, 0)),
        out_shape=jax.ShapeDtypeStruct((bsz, t, MLA_WIDTH), BF16),
        scratch_shapes=[pltpu.VMEM((t, tq), F32), pltpu.VMEM((t, tq), F32), pltpu.VMEM((MLA_HEADS, V_AUG, t), BF16)],
        compiler_params=_cparams("arbitrary", "arbitrary"),
        name="mla_attention",
    )(q4t, k4, vt)


def _ssd_kernel(z_ref, xbc_ref, dt_ref, cw_ref, cb_ref, dtb_ref, alog_ref, dsk_ref, ng_ref,
                o_ref, xc_scr, y_scr, hf_scr, hb_scr, *, nc, nc0):
    q = SSD_CHUNK
    hp = SSD_HEADS * SSD_P
    lane_x = lax.broadcasted_iota(jnp.int32, (SSD_CONV_DIM, q), 1)

    def conv_body(c, carry):
        xc = xbc_ref[c]
        keep_prev = jnp.where((c == 0) | (c == nc0), 0.0, 1.0)
        keep_next = jnp.where((c == nc0 - 1) | (c == nc - 1), 0.0, 1.0)
        xp = xbc_ref[jnp.maximum(c - 1, 0)] * keep_prev
        xn = xbc_ref[jnp.minimum(c + 1, nc - 1)] * keep_next
        prev = pltpu.roll(jnp.where(lane_x == q - 1, xp, xc), 1, 1)
        nxt = pltpu.roll(jnp.where(lane_x == 0, xn, xc), q - 1, 1)
        a = _silu(cw_ref[0] * prev + cw_ref[1] * xc + cw_ref[2] * nxt + cb_ref[...])
        xc_scr[c] = a
        y_scr[c] = dsk_ref[...] * a[:hp]
        return carry

    lax.fori_loop(0, nc, conv_body, 0)
    hf_scr[...] = jnp.zeros_like(hf_scr)
    hb_scr[...] = jnp.zeros_like(hb_scr)

    sub = lax.broadcasted_iota(jnp.int32, (q, q), 0)
    lan = lax.broadcasted_iota(jnp.int32, (q, q), 1)
    lane_n = lax.broadcasted_iota(jnp.int32, (1, SSD_GN), 1)
    pad_rows = jnp.zeros((q - SSD_HEADS, q), F32)

    per_g = SSD_HEADS // SSD_GROUPS
    rows_g = hp // SSD_GROUPS
    gmask = [lane_n // SSD_STATE == g for g in range(SSD_GROUPS)]
    tris = (sub <= lan, sub >= lan)
    tri01 = [jnp.where(tr, 1.0, 0.0).astype(BF16) for tr in tris]
    ends = (q - 1, 0)
    h_scrs = (hf_scr, hb_scr)
    dirs = (0, 1)

    lanes = 2 if nc % 2 == 0 else 1

    def prep(c, d):
        xc = xc_scr[c]
        hs = slice(SSD_HEADS * d, SSD_HEADS * (d + 1))
        dtl = dt_ref[c][hs] + dtb_ref[hs]
        dt = jnp.maximum(dtl, 0.0) + jnp.log(1.0 + jnp.exp(-jnp.abs(dtl)))
        dta = dt * (-jnp.exp(alog_ref[hs]))
        cum_pad = _dot_split3(jnp.concatenate([dta, pad_rows], axis=0), tri01[d])
        btok = xc[hp:hp + SSD_GN].T
        ct16 = xc[hp + SSD_GN:].astype(BF16)
        gts = [_dot(jnp.where(gmask[g], btok, 0.0).astype(BF16), ct16) for g in range(SSD_GROUPS)]
        return dict(c=c, xt=xc[:hp], dt=dt, cum_pad=cum_pad, btok16=btok.astype(BF16), ct16=ct16, gts=gts)

    def head_work(p, d):
        cum_row = p["cum_pad"][:SSD_HEADS]
        cum_col = p["cum_pad"].T
        ydiag, xws, decs, ear = [], [], [], []
        for hh in range(SSD_HEADS):
            ar = cum_row[hh:hh + 1, :]
            ac = cum_col[:, hh:hh + 1]
            sct = (p["gts"][hh // per_g] * jnp.where(tris[d], jnp.exp(ar - ac), 0.0)).astype(BF16)
            xdt = p["xt"][hh * SSD_P:(hh + 1) * SSD_P] * p["dt"][hh:hh + 1, :]
            ydiag.append(_dot(xdt.astype(BF16), sct))
            a_end = ar[:, ends[d]:ends[d] + 1]
            xws.append(xdt * jnp.exp(a_end - ar))
            decs.append(jnp.exp(a_end))
            ear.append(jnp.exp(ar))
        return ydiag, xws, decs, ear

    def state_incs(p, xws):
        return [jnp.where(gmask[g], _dot(jnp.concatenate(xws[g * per_g:(g + 1) * per_g], axis=0).astype(BF16),
                                         p["btok16"]), 0.0) for g in range(SSD_GROUPS)]

    def carried_out(p, hm):
        return [_dot(hm[g * rows_g:(g + 1) * rows_g].astype(BF16), p["ct16"]) for g in range(SSD_GROUPS)]

    def step(s, carry):
        pos = [s * lanes + u for u in range(lanes)]
        cs = [(p_, jnp.where(p_ < nc0, nc0 - 1 - p_, nc - 1 - (p_ - nc0))) for p_ in pos]
        pre = [[prep(cs[u][d], d) for d in dirs] for u in range(lanes)]
        hm = [h_scrs[d][...] for d in dirs]
        yo = [carried_out(pre[0][d], hm[d]) for d in dirs]
        hw = [head_work(pre[0][d], d) for d in dirs]
        for u in range(lanes):
            incs = [state_incs(pre[u][d], hw[d][1]) for d in dirs]
            hw_next = [head_work(pre[u + 1][d], d) for d in dirs] if u + 1 < lanes else None
            for d in dirs:
                ydiag, _, decs, ear = hw[d]
                y = [ydiag[hh] + yo[d][hh // per_g][(hh % per_g) * SSD_P:(hh % per_g + 1) * SSD_P] * ear[hh]
                     for hh in range(SSD_HEADS)]
                y_scr[pre[u][d]["c"]] += jnp.concatenate(y, axis=0)
                hm[d] = jnp.concatenate(
                    [hm[d][hh * SSD_P:(hh + 1) * SSD_P] * decs[hh]
                     + incs[d][hh // per_g][(hh % per_g) * SSD_P:(hh % per_g + 1) * SSD_P]
                     for hh in range(SSD_HEADS)], axis=0)
            if u + 1 < lanes:
                yo = [carried_out(pre[u + 1][d], hm[d]) for d in dirs]
                hw = hw_next
        for d in dirs:
            h_scrs[d][...] = hm[d]
        return carry

    lax.fori_loop(0, nc // lanes, step, 0)

    def fin_body(c, carry):
        gated = y_scr[c] * _silu(z_ref[c].astype(F32))
        o_ref[c] = _rms(gated, ng_ref[...], axis=0).astype(BF16)
        return carry

    lax.fori_loop(0, nc, fin_body, 0)


def _ssd_mixer(z_t, xbc_t, dt_t, cw, cb, dtb, alog, dsk, ng, *, layer, nc0):
    bsz, nc, _, q = xbc_t.shape
    hp = SSD_HEADS * SSD_P
    per_b = lambda f: pl.BlockSpec((None, nc, f, q), lambda b: (b, 0, 0, 0))
    full = lambda a: _layer_spec(a, layer)
    kern = functools.partial(_ssd_kernel, nc=nc, nc0=nc0)
    return pl.pallas_call(
        kern,
        grid=(bsz,),
        in_specs=[per_b(hp), per_b(SSD_CONV_DIM), per_b(2 * SSD_HEADS),
                  full(cw), full(cb), full(dtb), full(alog), full(dsk), full(ng)],
        out_specs=per_b(hp),
        out_shape=jax.ShapeDtypeStruct((bsz, nc, hp, q), BF16),
        scratch_shapes=[pltpu.VMEM((nc, SSD_CONV_DIM, q), F32), pltpu.VMEM((nc, hp, q), F32),
                        pltpu.VMEM((hp, SSD_GN), F32), pltpu.VMEM((hp, SSD_GN), F32)],
        compiler_params=_cparams("arbitrary"),
        name="ssd_mixer",
    )(z_t, xbc_t, dt_t, cw, cb, dtb, alog, dsk, ng)


def _route(logits_t, bias_col):
    aff = _sigmoid(logits_t)
    sel = aff + bias_col
    rows = [sel[e:e + 1, :] for e in range(N_EXPERTS)]
    gscore = []
    for g in range(N_GROUPS):
        a, b, c, d = rows[PER_GROUP * g:PER_GROUP * (g + 1)]
        hi1, lo1, hi2, lo2 = jnp.maximum(a, b), jnp.minimum(a, b), jnp.maximum(c, d), jnp.minimum(c, d)
        gscore.append(jnp.maximum(hi1, hi2) + jnp.maximum(jnp.minimum(hi1, hi2), jnp.maximum(lo1, lo2)))
    best = jnp.zeros_like(gscore[0], dtype=jnp.int32)
    cur = gscore[0]
    for g in range(1, N_GROUPS):
        better = gscore[g] > cur
        best = jnp.where(better, g, best)
        cur = jnp.where(better, gscore[g], cur)
    eidx = lax.broadcasted_iota(jnp.int32, sel.shape, 0)
    masked = jnp.where(eidx // PER_GROUP == best, sel, -jnp.inf)
    m1 = jnp.max(masked, axis=0, keepdims=True)
    idx1 = jnp.min(jnp.where(masked == m1, eidx, N_EXPERTS), axis=0, keepdims=True)
    masked2 = jnp.where(eidx == idx1, -jnp.inf, masked)
    m2 = jnp.max(masked2, axis=0, keepdims=True)
    idx2 = jnp.min(jnp.where(masked2 == m2, eidx, N_EXPERTS), axis=0, keepdims=True)
    oh1, oh2 = eidx == idx1, eidx == idx2
    w1 = jnp.sum(jnp.where(oh1, aff, 0.0), axis=0, keepdims=True)
    w2 = jnp.sum(jnp.where(oh2, aff, 0.0), axis=0, keepdims=True)
    den = w1 + w2
    return oh1, oh2, w1 / den, w2 / den


def _split_hi_lo(x):
    hi = x.astype(BF16)
    return hi, (x - hi.astype(F32)).astype(BF16)


def _outproj_kernel(oda_ref, ossdt_ref, omla_ref, h_ref, mod_ref, ng_ref, wo_ref, rw_ref, rb_ref,
                    h1_ref, xs_ref, meta_ref, cnt_ref, *, n_ctx_tiles, ctx_row, d, tm, s_loc, nb):
    ti, bp = pl.program_id(0), pl.program_id(1)
    tiles = range(nb)
    r_i = lax.broadcasted_iota(jnp.int32, (tm, tm), 0)
    c_i = lax.broadcasted_iota(jnp.int32, (tm, tm), 1)
    before = jnp.where(r_i < c_i, 1.0, 0.0).astype(BF16)
    row_e = lax.broadcasted_iota(jnp.int32, (N_EXPERTS, LANES), 0)
    r_s = lax.broadcasted_iota(jnp.int32, (s_loc, tm), 0).astype(F32)
    lane_e = lax.broadcasted_iota(jnp.int32, (s_loc, MOE_EXTRA), 1)
    row_m = lax.broadcasted_iota(jnp.int32, (LANES, tm), 0)
    rows = [jnp.where(ti < n_ctx_tiles, ctx_row, bp * nb + bb) for bb in tiles]
    mod_at = lambda bb, k: mod_ref[pl.ds(rows[bb], 1), pl.ds(k * d, d)]
    ossd = [jnp.concatenate([ossdt_ref[bb, c].astype(F32).T for c in range(tm // SSD_CHUNK)], axis=0).astype(BF16)
            for bb in tiles]
    mix = [_dot(oda_ref[bb], wo_ref[0:DA_WIDTH]) + _dot(ossd[bb], wo_ref[DA_WIDTH:DA_WIDTH + SSD_INNER])
           + _dot(omla_ref[bb], wo_ref[DA_WIDTH + SSD_INNER:]) for bb in tiles]
    h1 = [h_ref[bb] + mod_at(bb, 2) * mix[bb] for bb in tiles]
    for bb in tiles:
        h1_ref[bb] = h1[bb]
    u = [_ada_norm(h1[bb], ng_ref[...], mod_at(bb, 3), mod_at(bb, 4)) for bb in tiles]
    u16 = [x.astype(BF16) for x in u]
    rw_hi, rw_lo = _split_hi_lo(rw_ref[...])
    u_lo = [(u[bb] - u16[bb].astype(F32)).astype(BF16) for bb in tiles]
    logits = [_dot(u16[bb], rw_hi) + _dot(u_lo[bb], rw_hi) + _dot(u16[bb], rw_lo) for bb in tiles]
    routed = [_route(logits[bb].T[:N_EXPERTS], rb_ref[...]) for bb in tiles]
    cnt = [jnp.where(r[0], 1.0, 0.0) + jnp.where(r[1], 1.0, 0.0) for r in routed]
    rank = [_dot(cnt[bb].astype(BF16), before) for bb in tiles]
    for bb in tiles:
        oh1, oh2, w1, w2 = routed[bb]
        tot = jnp.sum(cnt[bb], axis=1, keepdims=True)
        ptot = jnp.floor((tot + (MOE_BLK - 1)) * (1.0 / MOE_BLK)) * MOE_BLK
        ptot_b = jnp.broadcast_to(ptot, (N_EXPERTS, LANES))
        cnt_ref[bb] = ptot_b
        run = jnp.zeros((1, LANES), F32)
        off = jnp.zeros((N_EXPERTS, LANES), F32)
        for e in range(1, N_EXPERTS):
            run = run + ptot_b[e - 1:e]
            off = jnp.where(row_e == e, run, off)
        slot = off[:, 0:1] + rank[bb]
        dest1 = jnp.sum(jnp.where(oh1, slot, 0.0), axis=0, keepdims=True)
        dest2 = jnp.sum(jnp.where(oh2, slot, 0.0), axis=0, keepdims=True)
        routed[bb] = (r_s == dest1, r_s == dest2, w1, w2)
        meta_ref[bb] = jnp.where(row_m == 0, dest1, jnp.where(row_m == 1, dest2, 0.0)).T
    perm = [jnp.where(routed[bb][0] | routed[bb][1], 1.0, 0.0).astype(BF16) for bb in tiles]
    xs = [_dot(perm[bb], u16[bb]) for bb in tiles]
    for bb in tiles:
        p1, p2, w1, w2 = routed[bb]
        xs_ref[bb, :, 0:d] = xs[bb].astype(BF16)
        wslot = jnp.sum(jnp.where(p1, w1, 0.0) + jnp.where(p2, w2, 0.0), axis=1, keepdims=True)
        w_hi = wslot.astype(BF16).astype(F32)
        xs_ref[bb, :, d:] = jnp.where(lane_e == 0, w_hi, jnp.where(lane_e == 1, wslot - w_hi, 0.0)).astype(BF16)


def _outproj_dispatch(oda, ossd_t, omla, h, mod, ng, wo, rw, rb, *, layer, tm, n_ctx, s_loc, nb):
    bsz, t, d = h.shape
    nt, q = t // tm, SSD_CHUNK
    tok = lambda w: pl.BlockSpec((nb, tm, w), lambda ti, b: (b, ti, 0))
    full = lambda a: pl.BlockSpec(a.shape, lambda ti, b: (0,) * a.ndim)
    tile = lambda r, w: pl.BlockSpec((nb, None, r, w), lambda ti, b: (b, ti, 0, 0))
    lay = lambda a: _layer_spec(a, layer)
    kern = functools.partial(_outproj_kernel, n_ctx_tiles=n_ctx // tm, ctx_row=bsz, d=d, tm=tm, s_loc=s_loc, nb=nb)
    return pl.pallas_call(
        kern,
        grid=(nt, bsz // nb),
        in_specs=[tok(DA_WIDTH), pl.BlockSpec((nb, tm // q, SSD_INNER, q), lambda ti, b: (b, ti, 0, 0)),
                  tok(MLA_WIDTH), tok(d), lay(mod), lay(ng), lay(wo), full(rw), full(rb)],
        out_specs=[tok(d), tile(s_loc, d + MOE_EXTRA), tile(tm, LANES), tile(N_EXPERTS, LANES)],
        out_shape=[jax.ShapeDtypeStruct((bsz, t, d), F32),
                   jax.ShapeDtypeStruct((bsz, nt, s_loc, d + MOE_EXTRA), BF16),
                   jax.ShapeDtypeStruct((bsz, nt, tm, LANES), F32),
                   jax.ShapeDtypeStruct((bsz, nt, N_EXPERTS, LANES), F32)],
        compiler_params=_cparams("arbitrary", "arbitrary"),
        name="outproj_dispatch",
    )(oda, ossd_t, omla, h, mod, ng, wo, rw, rb)


def _expert_kernel(se_ref, bi_ref, sv_ref, sn_ref, so_ref, xs_hbm, wg_ref, wu_ref, wd_ref, y_ref, xbuf, sem, wg16,
                   wu16, wd16, *, d):
    s = pl.program_id(0)
    last = pl.num_programs(0) - 1
    slot = s % 2
    real = sv_ref[s] > 0

    def start_all(step_, slot_):
        for j in range(MOE_STEP_BLKS):
            blk = bi_ref[step_ * MOE_STEP_BLKS + j]
            pltpu.make_async_copy(xs_hbm.at[pl.ds(pl.multiple_of(blk * MOE_BLK, MOE_BLK), MOE_BLK), :],
                                  xbuf.at[slot_, pl.ds(j * MOE_BLK, MOE_BLK), :], sem.at[slot_]).start(priority=j % 2)

    def wait_all(slot_):
        for j in range(MOE_STEP_BLKS):
            pltpu.make_async_copy(xs_hbm.at[pl.ds(0, MOE_BLK), :], xbuf.at[slot_, pl.ds(j * MOE_BLK, MOE_BLK), :],
                                  sem.at[slot_]).wait()

    @pl.when(jnp.logical_and(s == 0, real))
    def _():
        start_all(s, slot)

    nxt = jnp.minimum(s + 1, last)

    @pl.when(jnp.logical_and(s < last, sv_ref[nxt] > 0))
    def _():
        start_all(nxt, 1 - slot)

    @pl.when(jnp.logical_and(real, sn_ref[s] > 0))
    def _():
        wg16[...] = wg_ref[...].astype(BF16)
        wu16[...] = wu_ref[...].astype(BF16)
        wd16[...] = wd_ref[...].astype(BF16)

    @pl.when(real)
    def _():
        wait_all(slot)
        xm = xbuf[slot, :, 0:d]
        wx = xbuf[slot, :, d:]
        wrow = wx[:, 0:1].astype(F32) + wx[:, 1:2].astype(F32)
        he = _silu(_dot(xm, wg16[...])) * _dot(xm, wu16[...])
        y_ref[...] = (_dot(he.astype(BF16), wd16[...]) * wrow).astype(BF16)


def _expert_ffn(xs2d, wg, wu, wd, step_e, blk_ids, step_valid, step_new, step_out, *, layer, n_steps):
    d = wg.shape[2]
    rows = MOE_STEP_BLKS * MOE_BLK
    wspec = lambda a: pl.BlockSpec((None, None) + a.shape[2:], lambda s, se, bi, sv, sn, so: (layer, se[s], 0, 0))
    grid_spec = pltpu.PrefetchScalarGridSpec(
        num_scalar_prefetch=5,
        grid=(n_steps,),
        in_specs=[pl.BlockSpec(memory_space=pl.ANY), wspec(wg), wspec(wu), wspec(wd)],
        out_specs=pl.BlockSpec((rows, d), lambda s, se, bi, sv, sn, so: (so[s], 0)),
        scratch_shapes=[pltpu.VMEM((2, rows, d + MOE_EXTRA), BF16), pltpu.SemaphoreType.DMA((2,)),
                        pltpu.VMEM(wg.shape[2:], BF16), pltpu.VMEM(wu.shape[2:], BF16),
                        pltpu.VMEM(wd.shape[2:], BF16)],
    )
    return pl.pallas_call(
        functools.partial(_expert_kernel, d=d),
        grid_spec=grid_spec,
        out_shape=jax.ShapeDtypeStruct((n_steps * rows, d), BF16),
        compiler_params=_cparams("arbitrary"),
        name="expert_ffn",
    )(step_e, blk_ids, step_valid, step_new, step_out, xs2d, wg, wu, wd)


def _moe_combine_rows(inv_ref, ys_hbm, meta_ref, h1_ref, mod_ref, ybuf, sem, *, n_lb, nt, n_ctx_tiles, ctx_row, d, tm,
                      s_loc, nb):
    ti, bp = pl.program_id(0), pl.program_id(1)
    nbp = pl.num_programs(1)
    step = ti * nbp + bp
    last = pl.num_programs(0) * nbp - 1
    slot = step % 2

    def block_copy(ti_, bp_, k, slot_):
        bb, j = divmod(k, n_lb)
        blk = inv_ref[((bp_ * nb + bb) * nt + ti_) * n_lb + j]
        return pltpu.make_async_copy(ys_hbm.at[pl.ds(pl.multiple_of(blk * MOE_BLK, MOE_BLK), MOE_BLK), :],
                                     ybuf.at[slot_, pl.ds(k * MOE_BLK, MOE_BLK), :], sem.at[slot_])

    def start_all(ti_, bp_, slot_):
        for k in range(nb * n_lb):
            block_copy(ti_, bp_, k, slot_).start(priority=k % 2)

    def wait_all(slot_):
        for k in range(nb * n_lb):
            pltpu.make_async_copy(ys_hbm.at[pl.ds(0, MOE_BLK), :], ybuf.at[slot_, pl.ds(k * MOE_BLK, MOE_BLK), :],
                                  sem.at[slot_]).wait()

    @pl.when(step == 0)
    def _():
        start_all(ti, bp, slot)

    wrap = bp + 1 == nbp
    ti_n = jnp.where(step == last, ti, jnp.where(wrap, ti + 1, ti))
    bp_n = jnp.where(step == last, bp, jnp.where(wrap, 0, bp + 1))
    start_all(ti_n, bp_n, 1 - slot)
    wait_all(slot)

    tiles = range(nb)
    lane_s = lax.broadcasted_iota(jnp.int32, (tm, s_loc), 1).astype(F32)
    metas = [meta_ref[bb] for bb in tiles]
    pts = [jnp.where((lane_s == m[:, 0:1]) | (lane_s == m[:, 1:2]), 1.0, 0.0).astype(BF16) for m in metas]
    y = [_dot(pts[bb], ybuf[slot, bb * s_loc:(bb + 1) * s_loc, :]) for bb in tiles]
    h2 = []
    for bb in tiles:
        row = jnp.where(ti < n_ctx_tiles, ctx_row, bp * nb + bb)
        h2.append(h1_ref[bb] + mod_ref[pl.ds(row, 1), pl.ds(5 * d, d)] * y[bb])

    def drain():
        @pl.when(step == last)
        def _():
            wait_all(1 - slot)

    return h2, drain


def _combine_kernel(inv_ref, ys_hbm, meta_ref, h1_ref, mod_ref, fg_ref, o_ref, ybuf, sem, *, final, nb, **geom):
    h2, drain = _moe_combine_rows(inv_ref, ys_hbm, meta_ref, h1_ref, mod_ref, ybuf, sem, nb=nb, **geom)
    for bb in range(nb):
        o_ref[bb] = _rms(h2[bb], fg_ref[...]) if final else h2[bb]
    drain()


def _combine(ys_em, inv, meta, h1, mod, fg, *, layer, tm, n_ctx, s_loc, final, nb):
    bsz, t, d = h1.shape
    nt = t // tm
    n_lb = s_loc // MOE_BLK
    full = lambda a: pl.BlockSpec(a.shape, lambda ti, b, inv: (0,) * a.ndim)
    grid_spec = pltpu.PrefetchScalarGridSpec(
        num_scalar_prefetch=1,
        grid=(nt, bsz // nb),
        in_specs=[pl.BlockSpec(memory_space=pl.ANY),
                  pl.BlockSpec((nb, None, tm, LANES), lambda ti, b, inv: (b, ti, 0, 0)),
                  pl.BlockSpec((nb, tm, d), lambda ti, b, inv: (b, ti, 0)),
                  _layer_spec(mod, layer), full(fg)],
        out_specs=pl.BlockSpec((nb, tm, d), lambda ti, b, inv: (b, ti, 0)),
        scratch_shapes=[pltpu.VMEM((2, nb * s_loc, d), BF16), pltpu.SemaphoreType.DMA((2,))],
    )
    kern = functools.partial(_combine_kernel, n_lb=n_lb, nt=nt, n_ctx_tiles=n_ctx // tm, ctx_row=bsz, d=d, tm=tm,
                             s_loc=s_loc, final=final, nb=nb)
    return pl.pallas_call(
        kern,
        grid_spec=grid_spec,
        out_shape=jax.ShapeDtypeStruct((bsz, t, d), F32),
        compiler_params=_cparams("arbitrary", "arbitrary"),
        name="moe_combine",
    )(inv, ys_em, meta, h1, mod, fg)


def _rope_tables(length, dim):
    rows = length // GRID_W
    row = jnp.repeat(jnp.arange(rows), GRID_W).astype(F32)
    col = jnp.tile(jnp.arange(GRID_W), rows).astype(F32)
    n_freq = dim // 4
    inv_freq = ROPE_THETA ** (-jnp.arange(n_freq, dtype=F32) / n_freq)
    ang = jnp.concatenate([row[:, None] * inv_freq, col[:, None] * inv_freq], axis=-1)
    return jnp.cos(ang), jnp.sin(ang)


def _table_set(n_ctx, n_lat):
    def lanes(cos, sin, lane0, width, reps, outside_cos):
        c = jnp.concatenate([cos, cos], axis=-1)
        s = jnp.concatenate([-sin, sin], axis=-1)
        grp_c = jnp.full((n_lat, width), outside_cos, F32).at[:, lane0:lane0 + c.shape[1]].set(c)
        grp_s = jnp.zeros((n_lat, width), F32).at[:, lane0:lane0 + s.shape[1]].set(s)
        ctx_c = jnp.full((n_ctx, width), outside_cos, F32).at[:, lane0:lane0 + c.shape[1]].set(1.0)
        ctx_s = jnp.zeros((n_ctx, width), F32)
        return (jnp.tile(jnp.concatenate([ctx_c, grp_c], axis=0), (1, reps)),
                jnp.tile(jnp.concatenate([ctx_s, grp_s], axis=0), (1, reps)))

    dcos, dsin = _rope_tables(n_lat, DA_QK)
    cda, sda = lanes(dcos, dsin, 0, DA_QK, DA_QW // DA_QK, 1.0)
    mcos, msin = _rope_tables(n_lat, MLA_ROPE)
    cq, sm = lanes(mcos, msin, KR_LANE0, LANES, 1, 1.0)
    ck, _ = lanes(mcos, msin, KR_LANE0, LANES, 1, 0.0)
    return cda, sda, cq, ck, sm


def _pack_w_in(w_in):
    depth, d, _ = w_in.shape
    o_mla = DA_IN + SSD_IN
    dt = w_in[..., MAIN_COLS:o_mla]
    cq_ckv = w_in[..., o_mla:o_mla + MLA_Q_RANK + MLA_KV_RANK]
    kr = w_in[..., o_mla + MLA_Q_RANK + MLA_KV_RANK:]
    zeros = lambda n: jnp.zeros((depth, d, n), w_in.dtype)
    tail = jnp.concatenate([cq_ckv, zeros(KR_LANE0), kr, dt, zeros(LANES - DT_LANE0 - 2 * SSD_HEADS)], axis=-1)
    return w_in[..., :MAIN_COLS].astype(BF16), tail.astype(BF16)


def _pack_w_uq(w_uq):
    depth, r, _ = w_uq.shape
    w = w_uq.reshape(depth, r, MLA_HEADS, MLA_NOPE + MLA_ROPE)
    w = jnp.pad(w, ((0, 0), (0, 0), (0, 0), (0, MLA_HEAD_PAD - MLA_NOPE - MLA_ROPE)))
    return w.reshape(depth, r, MLA_QPAD).astype(BF16)


def _pack_w_ukv(w_ukv):
    depth, r, _ = w_ukv.shape
    w = w_ukv.reshape(depth, r, MLA_HEADS, MLA_NOPE + MLA_V)
    kn = jnp.pad(w[..., :MLA_NOPE], ((0, 0), (0, 0), (0, 0), (0, MLA_HEAD_PAD - MLA_NOPE)))
    return jnp.concatenate([kn.reshape(depth, r, MLA_QPAD), w[..., MLA_NOPE:].reshape(depth, r, MLA_WIDTH)],
                           axis=-1).astype(BF16)


def _dispatch_tables(pcnt, n_lb, n_steps):
    ntt = pcnt.shape[0]
    nb = (pcnt / MOE_BLK).astype(jnp.int32)
    lo = jnp.cumsum(nb, axis=1) - nb
    n_e = jnp.sum(nb, axis=0)
    p_e = (n_e + MOE_STEP_BLKS - 1) // MOE_STEP_BLKS * MOE_STEP_BLKS
    ends = jnp.cumsum(p_e)
    base = (ends - p_e)[None, :] + jnp.cumsum(nb, axis=0) - nb
    lb = jnp.arange(n_lb, dtype=jnp.int32)
    owner = (lb[None, :, None] >= lo[:, None, :]) & (lb[None, :, None] < (lo + nb)[:, None, :])
    used = jnp.any(owner, axis=-1)
    pos = jnp.sum(jnp.where(owner, (base - lo)[:, None, :], 0), axis=-1) + lb[None, :]
    inv = jnp.where(used, pos, 0).astype(jnp.int32)
    n_pos = n_steps * MOE_STEP_BLKS
    src = jnp.arange(ntt, dtype=jnp.int32)[:, None] * n_lb + lb[None, :]
    fwd = jnp.zeros((n_pos,), jnp.int32).at[jnp.where(used, pos, n_pos).reshape(-1)].set(src.reshape(-1), mode="drop")
    starts = jnp.arange(n_steps, dtype=jnp.int32) * MOE_STEP_BLKS
    step_e = jnp.minimum(jnp.sum(starts[:, None] >= ends[None, :], axis=1), N_EXPERTS - 1).astype(jnp.int32)
    step_valid = (starts < ends[-1]).astype(jnp.int32)
    step_new = jnp.concatenate([jnp.ones((1,), jnp.int32), (step_e[1:] != step_e[:-1]).astype(jnp.int32)])
    step_out = jnp.minimum(jnp.arange(n_steps, dtype=jnp.int32), jnp.maximum(ends[-1] // MOE_STEP_BLKS - 1, 0))
    return inv.reshape(-1), fwd, step_e, step_valid, step_new, step_out


def kernel(x, c, ctx, c_ctx, norm_mix_g, norm_ffn_g, w_mod, b_mod, w_in, w_out, da_lambda, da_subln_g, ssd_conv_w, ssd_conv_b, ssd_dt_bias, ssd_a_log, ssd_d, ssd_norm_g, mla_q_norm_g, mla_kv_norm_g, mla_w_uq, mla_w_ukv, router_w, router_bias, exp_w_gate, exp_w_up, exp_w_down, final_norm_g):
    bsz, n_lat, d = x.shape
    n_ctx = ctx.shape[1]
    t = n_ctx + n_lat
    tm = min(256, n_ctx)
    assert n_ctx % tm == 0 and n_lat % tm == 0 and n_ctx % SSD_CHUNK == 0 and n_lat % SSD_CHUNK == 0
    nt = t // tm
    s_loc = 2 * tm + 2 * LANES
    n_lb = s_loc // MOE_BLK
    n_steps = -(-(bsz * nt * n_lb + N_EXPERTS * (MOE_STEP_BLKS - 1)) // MOE_STEP_BLKS)
    nc, nc0 = t // SSD_CHUNK, n_ctx // SSD_CHUNK
    nb = 2 if bsz % 2 == 0 else 1
    nb_wide = 4 if bsz % 4 == 0 else nb
    q = SSD_CHUNK

    r_pad = -(-(bsz + 1) // 8) * 8
    c_rows = jnp.concatenate([c, c_ctx[None, :], jnp.zeros((r_pad - bsz - 1, d), F32)], axis=0)
    mod = _modulation(c_rows, w_mod, b_mod)

    w_in_p = _pack_w_in(w_in)
    w_uq_p = _pack_w_uq(mla_w_uq)
    w_ukv_p = _pack_w_ukv(mla_w_ukv)
    w_out16 = w_out.astype(BF16)
    tabs = _table_set(n_ctx, n_lat)
    rw_pad = jnp.pad(router_w, ((0, 0), (0, LANES - N_EXPERTS)))
    rb = router_bias.reshape(N_EXPERTS, 1)
    lane_b = lambda v: jnp.broadcast_to(v[..., None], v.shape + (q,))
    dsk_rows = jnp.repeat(ssd_d, SSD_P, axis=-1)
    ng_mix, ng_ffn = norm_mix_g[:, None, :], norm_ffn_g[:, None, :]
    qg, kvg = mla_q_norm_g[:, None, :], mla_kv_norm_g[:, None, :]
    sub_g = jnp.tile(da_subln_g, (1, DA_HEADS))[:, :, None]
    ssd_par = (lane_b(ssd_conv_w), lane_b(ssd_conv_b), lane_b(ssd_dt_bias.reshape(DEPTH, -1)),
               lane_b(ssd_a_log.reshape(DEPTH, -1)), lane_b(dsk_rows), lane_b(ssd_norm_g))

    h = jnp.concatenate([ctx, x], axis=1)
    moe = None
    for i in range(DEPTH):
        lam_init = 0.8 - 0.6 * math.exp(-0.3 * i)
        if moe is None:
            qda_t, kda, vda_t, z_t, xbc_t, dt_t, q4t, k4, vm_t = _inproj(
                h, mod, ng_mix, w_in_p, tabs, qg, kvg, w_uq_p, w_ukv_p, layer=i, tm=tm, n_ctx=n_ctx, nb=nb_wide)
        else:
            qda_t, kda, vda_t, z_t, xbc_t, dt_t, q4t, k4, vm_t, h = _inproj(
                None, mod, ng_mix, w_in_p, tabs, qg, kvg, w_uq_p, w_ukv_p, layer=i, tm=tm, n_ctx=n_ctx, nb=nb, moe=moe)
        o_da = _da_attention(qda_t, kda, vda_t, da_lambda, sub_g, layer=i, lam_init=lam_init, tq=tm, n_ctx=n_ctx)
        o_mla = _mla_attention(q4t, k4, vm_t, tq=tm, n_ctx=n_ctx)
        o_ssd_t = _ssd_mixer(z_t, xbc_t, dt_t, *ssd_par, layer=i, nc0=nc0)
        h1, xs, meta, cnt = _outproj_dispatch(o_da, o_ssd_t, o_mla, h, mod, ng_ffn, w_out16, rw_pad, rb, layer=i, tm=tm,
                                              n_ctx=n_ctx, s_loc=s_loc, nb=nb_wide)
        inv, fwd, *steps = _dispatch_tables(cnt[:, :, :, 0].reshape(bsz * nt, N_EXPERTS), n_lb, n_steps)
        step_e, step_valid, step_new, step_out = steps
        ys_em = _expert_ffn(xs.reshape(bsz * nt * s_loc, d + MOE_EXTRA), exp_w_gate, exp_w_up, exp_w_down, step_e, fwd,
                            step_valid, step_new, step_out, layer=i, n_steps=n_steps)
        moe = (ys_em, inv, meta, h1, s_loc)
    h = _combine(ys_em, inv, meta, h1, mod, final_norm_g[None, :], layer=DEPTH - 1, tm=tm, n_ctx=n_ctx, s_loc=s_loc,
                 final=True, nb=nb)
    return h[:, n_ctx:]
```

```python
import functools
import math

import jax
import jax.numpy as jnp
from jax import lax
from jax.experimental import pallas as pl
from jax.experimental.pallas import tpu as pltpu

F32 = jnp.float32
BF16 = jnp.bfloat16

DEPTH = 4
GRID_W = 64
EPS = 1e-6
ROPE_THETA = 10000.0
DA_HEADS, DA_QK = 4, 32
DA_V = 2 * DA_QK
DA_WIDTH = DA_HEADS * DA_V
DA_QW = DA_HEADS * 2 * DA_QK
DA_IN = 2 * DA_QW + DA_WIDTH
SSD_HEADS, SSD_P, SSD_GROUPS, SSD_STATE, SSD_CHUNK = 8, 64, 2, 64, 128
SSD_INNER = SSD_HEADS * SSD_P
SSD_GN = SSD_GROUPS * SSD_STATE
SSD_CONV_DIM = SSD_INNER + 2 * SSD_GN
SSD_IN = SSD_INNER + SSD_CONV_DIM + 2 * SSD_HEADS
MLA_HEADS, MLA_Q_RANK, MLA_KV_RANK, MLA_NOPE, MLA_ROPE, MLA_V = 4, 256, 128, 64, 32, 64
MLA_WIDTH = MLA_HEADS * MLA_V
MLA_IN = MLA_Q_RANK + MLA_KV_RANK + MLA_ROPE
MLA_SCALE = (MLA_NOPE + MLA_ROPE) ** -0.5
N_EXPERTS, N_GROUPS, D_EXPERT = 16, 4, 512
PER_GROUP = N_EXPERTS // N_GROUPS

LANES = 128
SUBLANES_BF16 = 16
VMEM_LIMIT_BYTES = 56 * 1024 * 1024

MLA_HEAD_PAD = LANES
MLA_QPAD = MLA_HEADS * MLA_HEAD_PAD
KR_LANE0 = MLA_NOPE
DT_LANE0 = MLA_NOPE + MLA_ROPE
MAIN_COLS = DA_IN + SSD_INNER + SSD_CONV_DIM
C_Z = DA_IN
C_XBC = C_Z + SSD_INNER
TAIL_COLS = MLA_Q_RANK + MLA_KV_RANK + LANES
T_CKV = MLA_Q_RANK
T_MISC = T_CKV + MLA_KV_RANK
LOG2E = math.log2(math.e)
DA_QSCALE = DA_QK ** -0.5 * LOG2E
MLA_QSCALE = MLA_SCALE * LOG2E

KEY_CHUNK = 256
MOE_BLK = SUBLANES_BF16
MOE_STEP_BLKS = 32
MOE_EXTRA = LANES


def _sigmoid(x):
    return 1.0 / (1.0 + jnp.exp(-x))


def _silu(x):
    return x * _sigmoid(x)


def _rms(x, g, axis=-1):
    return x * lax.rsqrt(jnp.mean(x * x, axis=axis, keepdims=True) + EPS) * g


def _ada_norm(hv, g, shift, scale):
    return _rms(hv, g) * (1.0 + scale) + shift


def _rope(x, cos, sin_signed, half):
    w = x.shape[-1]
    lane = lax.broadcasted_iota(jnp.int32, x.shape, x.ndim - 1)
    first = (lane % (2 * half)) < half
    partner = jnp.where(first, pltpu.roll(x, w - half, x.ndim - 1), pltpu.roll(x, half, x.ndim - 1))
    return x * cos + partner * sin_signed


def _dot(a, b):
    return jnp.dot(a, b, preferred_element_type=F32)


def _dot_hi(a, b):
    return jnp.dot(a, b, preferred_element_type=F32, precision=lax.Precision.HIGHEST)


def _dot_split3(x, m01):
    x1 = x.astype(BF16)
    r1 = x - x1.astype(F32)
    x2 = r1.astype(BF16)
    x3 = (r1 - x2.astype(F32)).astype(BF16)
    return _dot(x1, m01) + _dot(x2, m01) + _dot(x3, m01)


def _layer_spec(a, i):
    return pl.BlockSpec((None,) + a.shape[1:], lambda *_: (i,) + (0,) * (a.ndim - 1))


def _cparams(*sem):
    return pltpu.CompilerParams(dimension_semantics=sem, vmem_limit_bytes=VMEM_LIMIT_BYTES)


def _mod_kernel(c_ref, w_ref, b_ref, o_ref):
    o_ref[...] = _dot_hi(_silu(c_ref[...]), w_ref[...]) + b_ref[...]


def _modulation(c_rows, w_mod, b_mod):
    depth, d, n = w_mod.shape
    r = c_rows.shape[0]
    tn = 1536
    return pl.pallas_call(
        _mod_kernel,
        grid=(depth, n // tn),
        in_specs=[pl.BlockSpec((r, d), lambda l, j: (0, 0)),
                  pl.BlockSpec((None, d, tn), lambda l, j: (l, 0, j)),
                  pl.BlockSpec((None, 1, tn), lambda l, j: (l, 0, j))],
        out_specs=pl.BlockSpec((None, r, tn), lambda l, j: (l, 0, j)),
        out_shape=jax.ShapeDtypeStruct((depth, r, n), F32),
        compiler_params=_cparams("arbitrary", "arbitrary"),
        name="modulation",
    )(c_rows, w_mod, b_mod.reshape(depth, 1, n))


def _inproj_kernel(*refs, n_ctx_tiles, ctx_row, d, nb, tm, moe_geom):
    if moe_geom is None:
        h_ref, refs = refs[0], refs[1:]
    else:
        (inv_ref, ys_hbm, meta_ref, h1_ref, modp_ref), refs = refs[:5], refs[5:]
        h2_ref, ybuf, sem = refs[-3:]
        refs = refs[:-3]
    (mod_ref, ng_ref, wa_ref, wb_ref, cda_ref, sda_ref, cq_ref, ck_ref, sm_ref, qg_ref, kvg_ref, wuq_ref, wukv_ref,
     qdat_ref, kda_ref, vdat_ref, zt_ref, xbct_ref, dtt_ref, q4t_ref, k4_ref, vmt_ref) = refs
    ti, bp = pl.program_id(0), pl.program_id(1)
    if moe_geom is None:
        hs = [h_ref[bb] for bb in range(nb)]
    else:
        hs, drain = _moe_combine_rows(inv_ref, ys_hbm, meta_ref, h1_ref, modp_ref, ybuf, sem, nb=nb, tm=tm, d=d,
                                      n_ctx_tiles=n_ctx_tiles, ctx_row=ctx_row, **moe_geom)
        for bb in range(nb):
            h2_ref[bb] = hs[bb]
    cda, sda, sm = cda_ref[...], sda_ref[...], sm_ref[...]
    cos_q = jnp.concatenate([cq_ref[...]] * MLA_HEADS, axis=1)
    sin_q = jnp.concatenate([sm] * MLA_HEADS, axis=1)
    q = SSD_CHUNK
    accs = []
    for bb in range(nb):
        row = jnp.where(ti < n_ctx_tiles, ctx_row, bp * nb + bb)
        shift = mod_ref[pl.ds(row, 1), pl.ds(0, d)]
        scale = mod_ref[pl.ds(row, 1), pl.ds(d, d)]
        u = _ada_norm(hs[bb], ng_ref[...], shift, scale).astype(BF16)
        accs.append((_dot(u, wa_ref[...]), _dot(u, wb_ref[...])))
    for bb in range(nb):
        acc, tail = accs[bb]
        qdat_ref[bb] = (_rope(acc[:, 0:DA_QW], cda, sda, DA_QK // 2) * DA_QSCALE).T.astype(BF16)
        kda_ref[bb] = _rope(acc[:, DA_QW:2 * DA_QW], cda, sda, DA_QK // 2).astype(BF16)
        vdat_ref[bb] = acc[:, 2 * DA_QW:DA_IN].T.astype(BF16)
        z_t = acc[:, C_Z:C_XBC].T
        xbc_t = acc[:, C_XBC:MAIN_COLS].T
        misc = tail[:, T_MISC:TAIL_COLS]
        dt_t = misc.T[DT_LANE0:DT_LANE0 + 2 * SSD_HEADS]
        for c in range(tm // q):
            zt_ref[bb, c] = z_t[:, c * q:(c + 1) * q].astype(BF16)
            xbct_ref[bb, c] = xbc_t[:, c * q:(c + 1) * q]
            dtt_ref[bb, c] = dt_t[:, c * q:(c + 1) * q]
        cqn = _rms(tail[:, 0:T_CKV], qg_ref[...]).astype(BF16)
        qm_t = (_rope(_dot(cqn, wuq_ref[...]), cos_q, sin_q, MLA_ROPE // 2) * MLA_QSCALE).T
        ckvn = _rms(tail[:, T_CKV:T_MISC], kvg_ref[...]).astype(BF16)
        kv = _dot(ckvn, wukv_ref[...])
        kr = _rope(misc, ck_ref[...], sm, MLA_ROPE // 2)
        km = kv[:, :MLA_QPAD] + jnp.concatenate([kr] * MLA_HEADS, axis=1)
        for hh in range(MLA_HEADS):
            q4t_ref[bb, hh] = qm_t[hh * MLA_HEAD_PAD:(hh + 1) * MLA_HEAD_PAD].astype(BF16)
            k4_ref[bb, hh] = km[:, hh * MLA_HEAD_PAD:(hh + 1) * MLA_HEAD_PAD].astype(BF16)
        vmt_ref[bb] = kv[:, MLA_QPAD:].T.astype(BF16)
    if moe_geom is not None:
        drain()


def _inproj(h, mod, ng, w_in_p, tabs, qg, kvg, wuq_p, wukv_p, *, layer, tm, n_ctx, nb, moe=None):
    bsz, t, d = (h if moe is None else moe[3]).shape
    nt, q = t // tm, SSD_CHUNK
    cpt = tm // q
    tok = lambda w: pl.BlockSpec((nb, tm, w), lambda ti, b, *_: (b, ti, 0))
    tab = lambda w: pl.BlockSpec((tm, w), lambda ti, b, *_: (ti, 0))
    chunked = lambda f: pl.BlockSpec((nb, cpt, f, q), lambda ti, b, *_: (b, ti, 0, 0))
    lay = lambda a: _layer_spec(a, layer)
    cda, sda, cq, ck, sm = tabs
    sds = jax.ShapeDtypeStruct
    tok_t = lambda w: pl.BlockSpec((nb, w, tm), lambda ti, b, *_: (b, 0, ti))
    out_specs = [tok_t(DA_QW), tok(DA_QW), tok_t(DA_WIDTH),
                 chunked(SSD_INNER), chunked(SSD_CONV_DIM), chunked(2 * SSD_HEADS),
                 pl.BlockSpec((nb, MLA_HEADS, MLA_HEAD_PAD, tm), lambda ti, b, *_: (b, 0, 0, ti)),
                 pl.BlockSpec((nb, MLA_HEADS, tm, MLA_HEAD_PAD), lambda ti, b, *_: (b, 0, ti, 0)), tok_t(MLA_WIDTH)]
    out_shape = [sds((bsz, DA_QW, t), BF16), sds((bsz, t, DA_QW), BF16), sds((bsz, DA_WIDTH, t), BF16),
                 sds((bsz, t // q, SSD_INNER, q), BF16), sds((bsz, t // q, SSD_CONV_DIM, q), F32),
                 sds((bsz, t // q, 2 * SSD_HEADS, q), F32),
                 sds((bsz, MLA_HEADS, MLA_HEAD_PAD, t), BF16), sds((bsz, MLA_HEADS, t, MLA_HEAD_PAD), BF16),
                 sds((bsz, MLA_WIDTH, t), BF16)]
    w_main, w_tail = w_in_p
    common_specs = [lay(mod), lay(ng), lay(w_main), lay(w_tail), tab(DA_QW), tab(DA_QW), tab(LANES), tab(LANES),
                    tab(LANES), lay(qg), lay(kvg), lay(wuq_p), lay(wukv_p)]
    common_args = (mod, ng, w_main, w_tail, cda, sda, cq, ck, sm, qg, kvg, wuq_p, wukv_p)
    geom = dict(n_ctx_tiles=n_ctx // tm, ctx_row=bsz, d=d, nb=nb, tm=tm)
    if moe is None:
        return pl.pallas_call(
            functools.partial(_inproj_kernel, moe_geom=None, **geom),
            grid=(nt, bsz // nb),
            in_specs=[tok(d)] + common_specs,
            out_specs=out_specs,
            out_shape=out_shape,
            compiler_params=_cparams("arbitrary", "arbitrary"),
            name="inproj",
        )(h, *common_args)
    ys_em, inv, meta, h1, s_loc = moe
    grid_spec = pltpu.PrefetchScalarGridSpec(
        num_scalar_prefetch=1,
        grid=(nt, bsz // nb),
        in_specs=[pl.BlockSpec(memory_space=pl.ANY),
                  pl.BlockSpec((nb, None, tm, LANES), lambda ti, b, inv: (b, ti, 0, 0)), tok(d),
                  _layer_spec(mod, layer - 1)] + common_specs,
        out_specs=out_specs + [tok(d)],
        scratch_shapes=[pltpu.VMEM((2, nb * s_loc, d), BF16), pltpu.SemaphoreType.DMA((2,))],
    )
    moe_geom = dict(n_lb=s_loc // MOE_BLK, nt=nt, s_loc=s_loc)
    return pl.pallas_call(
        functools.partial(_inproj_kernel, moe_geom=moe_geom, **geom),
        grid_spec=grid_spec,
        out_shape=out_shape + [sds((bsz, t, d), F32)],
        compiler_params=_cparams("arbitrary", "arbitrary"),
        name="combine_inproj",
    )(inv, ys_em, meta, h1, mod, *common_args)


V_AUG = DA_V + SUBLANES_BF16


def _scores_pass(k_at, qtm, nk, s_scr):
    kc = KEY_CHUNK if nk % KEY_CHUNK == 0 else LANES
    m = None
    for c0 in range(0, nk, kc):
        s_c = _dot(k_at(c0, c0 + kc), qtm)
        s_scr[c0:c0 + kc, :] = s_c
        part = jnp.max(s_c, axis=0, keepdims=True)
        m = part if m is None else jnp.maximum(m, part)
    return m


def _pv_pass(vaug_at, nk, s_scr, m):
    kc = KEY_CHUNK if nk % KEY_CHUNK == 0 else LANES
    acc = None
    for c0 in range(0, nk, kc):
        e = jnp.exp2(s_scr[c0:c0 + kc, :] - m).astype(BF16)
        inc = _dot(vaug_at(c0, c0 + kc), e)
        acc = inc if acc is None else acc + inc
    return acc


def _attend_heads(k_of, qt_of, vaug_of, nk, scr, n_sub):
    m = _scores_pass(k_of(0), qt_of(0), nk, scr[0])
    outs = []
    for j in range(n_sub):
        if j + 1 < n_sub:
            m_next = _scores_pass(k_of(j + 1), qt_of(j + 1), nk, scr[(j + 1) % 2])
        acc = _pv_pass(vaug_of(j), nk, scr[j % 2], m)
        outs.append(acc[:DA_V] * (1.0 / acc[DA_V:DA_V + 1]))
        if j + 1 < n_sub:
            m = m_next
    return outs


def _fill_vaug(vt_ref, vaug_scr, n_heads):
    t = vt_ref.shape[1]
    for hh in range(n_heads):
        vaug_scr[hh, 0:DA_V, :] = vt_ref[hh * DA_V:(hh + 1) * DA_V, :]
        vaug_scr[hh, DA_V:V_AUG, :] = jnp.ones((V_AUG - DA_V, t), BF16)


def _da_attn_kernel(lam_ref, g_ref, qt_ref, k_ref, vt_ref, o_ref, s0_scr, s1_scr, vaug_scr, *, n_ctx, n_ctx_tiles,
                    lam_init, tq):
    qi = pl.program_id(1)

    @pl.when(qi == 0)
    def _():
        _fill_vaug(vt_ref, vaug_scr, DA_HEADS)

    lv = lam_ref[...]
    lam = (jnp.exp(jnp.sum(lv[0:1] * lv[1:2], axis=-1, keepdims=True))
           - jnp.exp(jnp.sum(lv[2:3] * lv[3:4], axis=-1, keepdims=True)) + lam_init)
    row_q = lax.broadcasted_iota(jnp.int32, (DA_QW, 1), 0)

    def attend(nk):
        qt = qt_ref[...]
        qt_of = lambda j: qt * jnp.where(row_q // DA_QK == j, 1.0, 0.0).astype(BF16)
        k_of = lambda j: (lambda c0, c1: k_ref[c0:c1, :])
        vaug_of = lambda j: (lambda c0, c1: vaug_scr[j // 2, :, c0:c1])
        outs = _attend_heads(k_of, qt_of, vaug_of, nk, (s0_scr, s1_scr), 2 * DA_HEADS)
        heads = []
        for hh in range(DA_HEADS):
            o = outs[2 * hh] - lam * outs[2 * hh + 1]
            heads.append(o * lax.rsqrt(jnp.mean(o * o, axis=0, keepdims=True) + EPS))
        o_t = jnp.concatenate(heads, axis=0) * g_ref[...] * (1.0 - lam_init)
        o_ref[...] = o_t.T.astype(BF16)

    @pl.when(qi < n_ctx_tiles)
    def _():
        attend(n_ctx)

    @pl.when(qi >= n_ctx_tiles)
    def _():
        attend(k_ref.shape[0])


def _da_attention(qt, k, vt, lam_vec, g_col, *, layer, lam_init, tq, n_ctx):
    bsz, _, t = qt.shape
    kern = functools.partial(_da_attn_kernel, n_ctx=n_ctx, n_ctx_tiles=n_ctx // tq, lam_init=lam_init, tq=tq)
    return pl.pallas_call(
        kern,
        grid=(bsz, t // tq),
        in_specs=[_layer_spec(lam_vec, layer), _layer_spec(g_col, layer),
                  pl.BlockSpec((None, DA_QW, tq), lambda b, i: (b, 0, i)),
                  pl.BlockSpec((None, t, DA_QW), lambda b, i: (b, 0, 0)),
                  pl.BlockSpec((None, DA_WIDTH, t), lambda b, i: (b, 0, 0))],
        out_specs=pl.BlockSpec((None, tq, DA_WIDTH), lambda b, i: (b, i, 0)),
        out_shape=jax.ShapeDtypeStruct((bsz, t, DA_WIDTH), BF16),
        scratch_shapes=[pltpu.VMEM((t, tq), F32), pltpu.VMEM((t, tq), F32), pltpu.VMEM((DA_HEADS, V_AUG, t), BF16)],
        compiler_params=_cparams("arbitrary", "arbitrary"),
        name="da_attention",
    )(lam_vec, g_col, qt, k, vt)


def _mla_attn_kernel(qt_ref, k_ref, vt_ref, o_ref, s0_scr, s1_scr, vaug_scr, *, n_ctx, n_ctx_tiles, tq):
    qi = pl.program_id(1)

    @pl.when(qi == 0)
    def _():
        _fill_vaug(vt_ref, vaug_scr, MLA_HEADS)

    def attend(nk):
        k_of = lambda hh: (lambda c0, c1: k_ref[hh, c0:c1, :])
        vaug_of = lambda hh: (lambda c0, c1: vaug_scr[hh, :, c0:c1])
        outs = _attend_heads(k_of, lambda hh: qt_ref[hh], vaug_of, nk, (s0_scr, s1_scr), MLA_HEADS)
        o_ref[...] = jnp.concatenate(outs, axis=0).T.astype(BF16)

    @pl.when(qi < n_ctx_tiles)
    def _():
        attend(n_ctx)

    @pl.when(qi >= n_ctx_tiles)
    def _():
        attend(k_ref.shape[1])


def _mla_attention(q4t, k4, vt, *, tq, n_ctx):
    bsz, _, _, t = q4t.shape
    kern = functools.partial(_mla_attn_kernel, n_ctx=n_ctx, n_ctx_tiles=n_ctx // tq, tq=tq)
    return pl.pallas_call(
        kern,
        grid=(bsz, t // tq),
        in_specs=[pl.BlockSpec((None, MLA_HEADS, MLA_HEAD_PAD, tq), lambda b, i: (b, 0, 0, i)),
                  pl.BlockSpec((None, MLA_HEADS, t, MLA_HEAD_PAD), lambda b, i: (b, 0, 0, 0)),
                  pl.BlockSpec((None, MLA_WIDTH, t), lambda b, i: (b, 0, 0))],
        out_specs=pl.BlockSpec((None, tq, MLA_WIDTH), lambda b, i: (b, i, 0)),
        out_shape=jax.ShapeDtypeStruct((bsz, t, MLA_WIDTH), BF16),
        scratch_shapes=[pltpu.VMEM((t, tq), F32), pltpu.VMEM((t, tq), F32), pltpu.VMEM((MLA_HEADS, V_AUG, t), BF16)],
        compiler_params=_cparams("arbitrary", "arbitrary"),
        name="mla_attention",
    )(q4t, k4, vt)


def _ssd_kernel(z_ref, xbc_ref, dt_ref, cw_ref, cb_ref, dtb_ref, alog_ref, dsk_ref, ng_ref,
                o_ref, xc_scr, y_scr, hf_scr, hb_scr, *, nc, nc0):
    q = SSD_CHUNK
    hp = SSD_HEADS * SSD_P
    lane_x = lax.broadcasted_iota(jnp.int32, (SSD_CONV_DIM, q), 1)

    def conv_body(c, carry):
        xc = xbc_ref[c]
        keep_prev = jnp.where((c == 0) | (c == nc0), 0.0, 1.0)
        keep_next = jnp.where((c == nc0 - 1) | (c == nc - 1), 0.0, 1.0)
        xp = xbc_ref[jnp.maximum(c - 1, 0)] * keep_prev
        xn = xbc_ref[jnp.minimum(c + 1, nc - 1)] * keep_next
        prev = pltpu.roll(jnp.where(lane_x == q - 1, xp, xc), 1, 1)
        nxt = pltpu.roll(jnp.where(lane_x == 0, xn, xc), q - 1, 1)
        a = _silu(cw_ref[0] * prev + cw_ref[1] * xc + cw_ref[2] * nxt + cb_ref[...])
        xc_scr[c] = a
        y_scr[c] = dsk_ref[...] * a[:hp]
        return carry

    lax.fori_loop(0, nc, conv_body, 0)
    hf_scr[...] = jnp.zeros_like(hf_scr)
    hb_scr[...] = jnp.zeros_like(hb_scr)

    sub = lax.broadcasted_iota(jnp.int32, (q, q), 0)
    lan = lax.broadcasted_iota(jnp.int32, (q, q), 1)
    lane_n = lax.broadcasted_iota(jnp.int32, (1, SSD_GN), 1)
    pad_rows = jnp.zeros((q - SSD_HEADS, q), F32)

    per_g = SSD_HEADS // SSD_GROUPS
    rows_g = hp // SSD_GROUPS
    gmask = [lane_n // SSD_STATE == g for g in range(SSD_GROUPS)]
    tris = (sub <= lan, sub >= lan)
    tri01 = [jnp.where(tr, 1.0, 0.0).astype(BF16) for tr in tris]
    ends = (q - 1, 0)
    h_scrs = (hf_scr, hb_scr)
    dirs = (0, 1)

    lanes = 2 if nc % 2 == 0 else 1

    def prep(c, d):
        xc = xc_scr[c]
        hs = slice(SSD_HEADS * d, SSD_HEADS * (d + 1))
        dtl = dt_ref[c][hs] + dtb_ref[hs]
        dt = jnp.maximum(dtl, 0.0) + jnp.log(1.0 + jnp.exp(-jnp.abs(dtl)))
        dta = dt * (-jnp.exp(alog_ref[hs]))
        cum_pad = _dot_split3(jnp.concatenate([dta, pad_rows], axis=0), tri01[d])
        btok = xc[hp:hp + SSD_GN].T
        ct16 = xc[hp + SSD_GN:].astype(BF16)
        gts = [_dot(jnp.where(gmask[g], btok, 0.0).astype(BF16), ct16) for g in range(SSD_GROUPS)]
        return dict(c=c, xt=xc[:hp], dt=dt, cum_pad=cum_pad, btok16=btok.astype(BF16), ct16=ct16, gts=gts)

    def head_work(p, d):
        cum_row = p["cum_pad"][:SSD_HEADS]
        cum_col = p["cum_pad"].T
        ydiag, xws, decs, ear = [], [], [], []
        for hh in range(SSD_HEADS):
            ar = cum_row[hh:hh + 1, :]
            ac = cum_col[:, hh:hh + 1]
            sct = (p["gts"][hh // per_g] * jnp.where(tris[d], jnp.exp(ar - ac), 0.0)).astype(BF16)
            xdt = p["xt"][hh * SSD_P:(hh + 1) * SSD_P] * p["dt"][hh:hh + 1, :]
            ydiag.append(_dot(xdt.astype(BF16), sct))
            a_end = ar[:, ends[d]:ends[d] + 1]
            xws.append(xdt * jnp.exp(a_end - ar))
            decs.append(jnp.exp(a_end))
            ear.append(jnp.exp(ar))
        return ydiag, xws, decs, ear

    def state_incs(p, xws):
        return [jnp.where(gmask[g], _dot(jnp.concatenate(xws[g * per_g:(g + 1) * per_g], axis=0).astype(BF16),
                                         p["btok16"]), 0.0) for g in range(SSD_GROUPS)]

    def carried_out(p, hm):
        return [_dot(hm[g * rows_g:(g + 1) * rows_g].astype(BF16), p["ct16"]) for g in range(SSD_GROUPS)]

    def step(s, carry):
        pos = [s * lanes + u for u in range(lanes)]
        cs = [(p_, jnp.where(p_ < nc0, nc0 - 1 - p_, nc - 1 - (p_ - nc0))) for p_ in pos]
        pre = [[prep(cs[u][d], d) for d in dirs] for u in range(lanes)]
        hm = [h_scrs[d][...] for d in dirs]
        yo = [carried_out(pre[0][d], hm[d]) for d in dirs]
        hw = [head_work(pre[0][d], d) for d in dirs]
        for u in range(lanes):
            incs = [state_incs(pre[u][d], hw[d][1]) for d in dirs]
            hw_next = [head_work(pre[u + 1][d], d) for d in dirs] if u + 1 < lanes else None
            for d in dirs:
                ydiag, _, decs, ear = hw[d]
                y = [ydiag[hh] + yo[d][hh // per_g][(hh % per_g) * SSD_P:(hh % per_g + 1) * SSD_P] * ear[hh]
                     for hh in range(SSD_HEADS)]
                y_scr[pre[u][d]["c"]] += jnp.concatenate(y, axis=0)
                hm[d] = jnp.concatenate(
                    [hm[d][hh * SSD_P:(hh + 1) * SSD_P] * decs[hh]
                     + incs[d][hh // per_g][(hh % per_g) * SSD_P:(hh % per_g + 1) * SSD_P]
                     for hh in range(SSD_HEADS)], axis=0)
            if u + 1 < lanes:
                yo = [carried_out(pre[u + 1][d], hm[d]) for d in dirs]
                hw = hw_next
        for d in dirs:
            h_scrs[d][...] = hm[d]
        return carry

    lax.fori_loop(0, nc // lanes, step, 0)

    def fin_body(c, carry):
        gated = y_scr[c] * _silu(z_ref[c].astype(F32))
        o_ref[c] = _rms(gated, ng_ref[...], axis=0).astype(BF16)
        return carry

    lax.fori_loop(0, nc, fin_body, 0)


def _ssd_mixer(z_t, xbc_t, dt_t, cw, cb, dtb, alog, dsk, ng, *, layer, nc0):
    bsz, nc, _, q = xbc_t.shape
    hp = SSD_HEADS * SSD_P
    per_b = lambda f: pl.BlockSpec((None, nc, f, q), lambda b: (b, 0, 0, 0))
    full = lambda a: _layer_spec(a, layer)
    kern = functools.partial(_ssd_kernel, nc=nc, nc0=nc0)
    return pl.pallas_call(
        kern,
        grid=(bsz,),
        in_specs=[per_b(hp), per_b(SSD_CONV_DIM), per_b(2 * SSD_HEADS),
                  full(cw), full(cb), full(dtb), full(alog), full(dsk), full(ng)],
        out_specs=per_b(hp),
        out_shape=jax.ShapeDtypeStruct((bsz, nc, hp, q), BF16),
        scratch_shapes=[pltpu.VMEM((nc, SSD_CONV_DIM, q), F32), pltpu.VMEM((nc, hp, q), F32),
                        pltpu.VMEM((hp, SSD_GN), F32), pltpu.VMEM((hp, SSD_GN), F32)],
        compiler_params=_cparams("arbitrary"),
        name="ssd_mixer",
    )(z_t, xbc_t, dt_t, cw, cb, dtb, alog, dsk, ng)


def _route(logits_t, bias_col):
    aff = _sigmoid(logits_t)
    sel = aff + bias_col
    rows = [sel[e:e + 1, :] for e in range(N_EXPERTS)]
    gscore = []
    for g in range(N_GROUPS):
        a, b, c, d = rows[PER_GROUP * g:PER_GROUP * (g + 1)]
        hi1, lo1, hi2, lo2 = jnp.maximum(a, b), jnp.minimum(a, b), jnp.maximum(c, d), jnp.minimum(c, d)
        gscore.append(jnp.maximum(hi1, hi2) + jnp.maximum(jnp.minimum(hi1, hi2), jnp.maximum(lo1, lo2)))
    best = jnp.zeros_like(gscore[0], dtype=jnp.int32)
    cur = gscore[0]
    for g in range(1, N_GROUPS):
        better = gscore[g] > cur
        best = jnp.where(better, g, best)
        cur = jnp.where(better, gscore[g], cur)
    eidx = lax.broadcasted_iota(jnp.int32, sel.shape, 0)
    masked = jnp.where(eidx // PER_GROUP == best, sel, -jnp.inf)
    m1 = jnp.max(masked, axis=0, keepdims=True)
    idx1 = jnp.min(jnp.where(masked == m1, eidx, N_EXPERTS), axis=0, keepdims=True)
    masked2 = jnp.where(eidx == idx1, -jnp.inf, masked)
    m2 = jnp.max(masked2, axis=0, keepdims=True)
    idx2 = jnp.min(jnp.where(masked2 == m2, eidx, N_EXPERTS), axis=0, keepdims=True)
    oh1, oh2 = eidx == idx1, eidx == idx2
    w1 = jnp.sum(jnp.where(oh1, aff, 0.0), axis=0, keepdims=True)
    w2 = jnp.sum(jnp.where(oh2, aff, 0.0), axis=0, keepdims=True)
    den = w1 + w2
    return oh1, oh2, w1 / den, w2 / den


def _split_hi_lo(x):
    hi = x.astype(BF16)
    return hi, (x - hi.astype(F32)).astype(BF16)


def _outproj_kernel(oda_ref, ossdt_ref, omla_ref, h_ref, mod_ref, ng_ref, wo_ref, rw_ref, rb_ref,
                    h1_ref, xs_ref, meta_ref, cnt_ref, *, n_ctx_tiles, ctx_row, d, tm, s_loc, nb):
    ti, bp = pl.program_id(0), pl.program_id(1)
    tiles = range(nb)
    r_i = lax.broadcasted_iota(jnp.int32, (tm, tm), 0)
    c_i = lax.broadcasted_iota(jnp.int32, (tm, tm), 1)
    before = jnp.where(r_i < c_i, 1.0, 0.0).astype(BF16)
    row_e = lax.broadcasted_iota(jnp.int32, (N_EXPERTS, LANES), 0)
    r_s = lax.broadcasted_iota(jnp.int32, (s_loc, tm), 0).astype(F32)
    lane_e = lax.broadcasted_iota(jnp.int32, (s_loc, MOE_EXTRA), 1)
    row_m = lax.broadcasted_iota(jnp.int32, (LANES, tm), 0)
    rows = [jnp.where(ti < n_ctx_tiles, ctx_row, bp * nb + bb) for bb in tiles]
    mod_at = lambda bb, k: mod_ref[pl.ds(rows[bb], 1), pl.ds(k * d, d)]
    ossd = [jnp.concatenate([ossdt_ref[bb, c].astype(F32).T for c in range(tm // SSD_CHUNK)], axis=0).astype(BF16)
            for bb in tiles]
    mix = [_dot(oda_ref[bb], wo_ref[0:DA_WIDTH]) + _dot(ossd[bb], wo_ref[DA_WIDTH:DA_WIDTH + SSD_INNER])
           + _dot(omla_ref[bb], wo_ref[DA_WIDTH + SSD_INNER:]) for bb in tiles]
    h1 = [h_ref[bb] + mod_at(bb, 2) * mix[bb] for bb in tiles]
    for bb in tiles:
        h1_ref[bb] = h1[bb]
    u = [_ada_norm(h1[bb], ng_ref[...], mod_at(bb, 3), mod_at(bb, 4)) for bb in tiles]
    u16 = [x.astype(BF16) for x in u]
    rw_hi, rw_lo = _split_hi_lo(rw_ref[...])
    u_lo = [(u[bb] - u16[bb].astype(F32)).astype(BF16) for bb in tiles]
    logits = [_dot(u16[bb], rw_hi) + _dot(u_lo[bb], rw_hi) + _dot(u16[bb], rw_lo) for bb in tiles]
    routed = [_route(logits[bb].T[:N_EXPERTS], rb_ref[...]) for bb in tiles]
    cnt = [jnp.where(r[0], 1.0, 0.0) + jnp.where(r[1], 1.0, 0.0) for r in routed]
    rank = [_dot(cnt[bb].astype(BF16), before) for bb in tiles]
    for bb in tiles:
        oh1, oh2, w1, w2 = routed[bb]
        tot = jnp.sum(cnt[bb], axis=1, keepdims=True)
        ptot = jnp.floor((tot + (MOE_BLK - 1)) * (1.0 / MOE_BLK)) * MOE_BLK
        ptot_b = jnp.broadcast_to(ptot, (N_EXPERTS, LANES))
        cnt_ref[bb] = ptot_b
        run = jnp.zeros((1, LANES), F32)
        off = jnp.zeros((N_EXPERTS, LANES), F32)
        for e in range(1, N_EXPERTS):
            run = run + ptot_b[e - 1:e]
            off = jnp.where(row_e == e, run, off)
        slot = off[:, 0:1] + rank[bb]
        dest1 = jnp.sum(jnp.where(oh1, slot, 0.0), axis=0, keepdims=True)
        dest2 = jnp.sum(jnp.where(oh2, slot, 0.0), axis=0, keepdims=True)
        routed[bb] = (r_s == dest1, r_s == dest2, w1, w2)
        meta_ref[bb] = jnp.where(row_m == 0, dest1, jnp.where(row_m == 1, dest2, 0.0)).T
    perm = [jnp.where(routed[bb][0] | routed[bb][1], 1.0, 0.0).astype(BF16) for bb in tiles]
    xs = [_dot(perm[bb], u16[bb]) for bb in tiles]
    for bb in tiles:
        p1, p2, w1, w2 = routed[bb]
        xs_ref[bb, :, 0:d] = xs[bb].astype(BF16)
        wslot = jnp.sum(jnp.where(p1, w1, 0.0) + jnp.where(p2, w2, 0.0), axis=1, keepdims=True)
        w_hi = wslot.astype(BF16).astype(F32)
        xs_ref[bb, :, d:] = jnp.where(lane_e == 0, w_hi, jnp.where(lane_e == 1, wslot - w_hi, 0.0)).astype(BF16)


def _outproj_dispatch(oda, ossd_t, omla, h, mod, ng, wo, rw, rb, *, layer, tm, n_ctx, s_loc, nb):
    bsz, t, d = h.shape
    nt, q = t // tm, SSD_CHUNK
    tok = lambda w: pl.BlockSpec((nb, tm, w), lambda ti, b: (b, ti, 0))
    full = lambda a: pl.BlockSpec(a.shape, lambda ti, b: (0,) * a.ndim)
    tile = lambda r, w: pl.BlockSpec((nb, None, r, w), lambda ti, b: (b, ti, 0, 0))
    lay = lambda a: _layer_spec(a, layer)
    kern = functools.partial(_outproj_kernel, n_ctx_tiles=n_ctx // tm, ctx_row=bsz, d=d, tm=tm, s_loc=s_loc, nb=nb)
    return pl.pallas_call(
        kern,
        grid=(nt, bsz // nb),
        in_specs=[tok(DA_WIDTH), pl.BlockSpec((nb, tm // q, SSD_INNER, q), lambda ti, b: (b, ti, 0, 0)),
                  tok(MLA_WIDTH), tok(d), lay(mod), lay(ng), lay(wo), full(rw), full(rb)],
        out_specs=[tok(d), tile(s_loc, d + MOE_EXTRA), tile(tm, LANES), tile(N_EXPERTS, LANES)],
        out_shape=[jax.ShapeDtypeStruct((bsz, t, d), F32),
                   jax.ShapeDtypeStruct((bsz, nt, s_loc, d + MOE_EXTRA), BF16),
                   jax.ShapeDtypeStruct((bsz, nt, tm, LANES), F32),
                   jax.ShapeDtypeStruct((bsz, nt, N_EXPERTS, LANES), F32)],
        compiler_params=_cparams("arbitrary", "arbitrary"),
        name="outproj_dispatch",
    )(oda, ossd_t, omla, h, mod, ng, wo, rw, rb)


def _expert_kernel(se_ref, bi_ref, sv_ref, sn_ref, xs_hbm, wg_ref, wu_ref, wd_ref, y_ref, xbuf, sem, wg16, wu16, wd16,
                   *, d):
    s = pl.program_id(0)
    last = pl.num_programs(0) - 1
    slot = s % 2
    real = sv_ref[s] > 0

    def start_all(step_, slot_):
        for j in range(MOE_STEP_BLKS):
            blk = bi_ref[step_ * MOE_STEP_BLKS + j]
            pltpu.make_async_copy(xs_hbm.at[pl.ds(pl.multiple_of(blk * MOE_BLK, MOE_BLK), MOE_BLK), :],
                                  xbuf.at[slot_, pl.ds(j * MOE_BLK, MOE_BLK), :], sem.at[slot_]).start(priority=j % 2)

    def wait_all(slot_):
        for j in range(MOE_STEP_BLKS):
            pltpu.make_async_copy(xs_hbm.at[pl.ds(0, MOE_BLK), :], xbuf.at[slot_, pl.ds(j * MOE_BLK, MOE_BLK), :],
                                  sem.at[slot_]).wait()

    @pl.when(jnp.logical_and(s == 0, real))
    def _():
        start_all(s, slot)

    nxt = jnp.minimum(s + 1, last)

    @pl.when(jnp.logical_and(s < last, sv_ref[nxt] > 0))
    def _():
        start_all(nxt, 1 - slot)

    @pl.when(jnp.logical_and(real, sn_ref[s] > 0))
    def _():
        wg16[...] = wg_ref[...].astype(BF16)
        wu16[...] = wu_ref[...].astype(BF16)
        wd16[...] = wd_ref[...].astype(BF16)

    @pl.when(real)
    def _():
        wait_all(slot)
        xm = xbuf[slot, :, 0:d]
        wx = xbuf[slot, :, d:]
        wrow = wx[:, 0:1].astype(F32) + wx[:, 1:2].astype(F32)
        he = _silu(_dot(xm, wg16[...])) * _dot(xm, wu16[...])
        y_ref[...] = (_dot(he.astype(BF16), wd16[...]) * wrow).astype(BF16)

    @pl.when(jnp.logical_not(real))
    def _():
        y_ref[...] = jnp.zeros_like(y_ref)


def _expert_ffn(xs2d, wg, wu, wd, step_e, blk_ids, step_valid, step_new, *, layer, n_steps):
    d = wg.shape[2]
    rows = MOE_STEP_BLKS * MOE_BLK
    wspec = lambda a: pl.BlockSpec((None, None) + a.shape[2:], lambda s, se, bi, sv, sn: (layer, se[s], 0, 0))
    grid_spec = pltpu.PrefetchScalarGridSpec(
        num_scalar_prefetch=4,
        grid=(n_steps,),
        in_specs=[pl.BlockSpec(memory_space=pl.ANY), wspec(wg), wspec(wu), wspec(wd)],
        out_specs=pl.BlockSpec((rows, d), lambda s, se, bi, sv, sn: (s, 0)),
        scratch_shapes=[pltpu.VMEM((2, rows, d + MOE_EXTRA), BF16), pltpu.SemaphoreType.DMA((2,)),
                        pltpu.VMEM(wg.shape[2:], BF16), pltpu.VMEM(wu.shape[2:], BF16),
                        pltpu.VMEM(wd.shape[2:], BF16)],
    )
    return pl.pallas_call(
        functools.partial(_expert_kernel, d=d),
        grid_spec=grid_spec,
        out_shape=jax.ShapeDtypeStruct((n_steps * rows, d), BF16),
        compiler_params=_cparams("arbitrary"),
        name="expert_ffn",
    )(step_e, blk_ids, step_valid, step_new, xs2d, wg, wu, wd)


def _moe_combine_rows(inv_ref, ys_hbm, meta_ref, h1_ref, mod_ref, ybuf, sem, *, n_lb, nt, n_ctx_tiles, ctx_row, d, tm,
                      s_loc, nb, tile0=0):
    ti, bp = pl.program_id(0) + tile0, pl.program_id(1)
    nbp = pl.num_programs(1)
    step = pl.program_id(0) * nbp + bp
    last = pl.num_programs(0) * nbp - 1
    slot = step % 2

    def block_copy(ti_, bp_, k, slot_):
        bb, j = divmod(k, n_lb)
        blk = inv_ref[((bp_ * nb + bb) * nt + ti_) * n_lb + j]
        return pltpu.make_async_copy(ys_hbm.at[pl.ds(pl.multiple_of(blk * MOE_BLK, MOE_BLK), MOE_BLK), :],
                                     ybuf.at[slot_, pl.ds(k * MOE_BLK, MOE_BLK), :], sem.at[slot_])

    def start_all(ti_, bp_, slot_):
        for k in range(nb * n_lb):
            block_copy(ti_, bp_, k, slot_).start(priority=k % 2)

    def wait_all(slot_):
        for k in range(nb * n_lb):
            pltpu.make_async_copy(ys_hbm.at[pl.ds(0, MOE_BLK), :], ybuf.at[slot_, pl.ds(k * MOE_BLK, MOE_BLK), :],
                                  sem.at[slot_]).wait()

    @pl.when(step == 0)
    def _():
        start_all(ti, bp, slot)

    wrap = bp + 1 == nbp
    ti_n = jnp.where(step == last, ti, jnp.where(wrap, ti + 1, ti))
    bp_n = jnp.where(step == last, bp, jnp.where(wrap, 0, bp + 1))
    start_all(ti_n, bp_n, 1 - slot)
    wait_all(slot)

    tiles = range(nb)
    lane_s = lax.broadcasted_iota(jnp.int32, (tm, s_loc), 1).astype(F32)
    metas = [meta_ref[bb] for bb in tiles]
    pts = [jnp.where((lane_s == m[:, 0:1]) | (lane_s == m[:, 1:2]), 1.0, 0.0).astype(BF16) for m in metas]
    y = [_dot(pts[bb], ybuf[slot, bb * s_loc:(bb + 1) * s_loc, :]) for bb in tiles]
    h2 = []
    for bb in tiles:
        row = jnp.where(ti < n_ctx_tiles, ctx_row, bp * nb + bb)
        h2.append(h1_ref[bb] + mod_ref[pl.ds(row, 1), pl.ds(5 * d, d)] * y[bb])

    def drain():
        @pl.when(step == last)
        def _():
            wait_all(1 - slot)

    return h2, drain


def _final_combine_kernel(inv_ref, ys_hbm, meta_ref, h1_ref, mod_ref, fg_ref, o_ref, ybuf, sem, *, nb, **geom):
    h2, drain = _moe_combine_rows(inv_ref, ys_hbm, meta_ref, h1_ref, mod_ref, ybuf, sem, nb=nb, **geom)
    for bb in range(nb):
        o_ref[bb] = _rms(h2[bb], fg_ref[...])
    drain()


def _final_combine(ys_em, inv, meta, h1, mod, fg, *, layer, tm, n_ctx, s_loc, nb):
    bsz, t, d = h1.shape
    nt, nct = t // tm, n_ctx // tm
    n_lb = s_loc // MOE_BLK
    full = lambda a: pl.BlockSpec(a.shape, lambda g, b, inv: (0,) * a.ndim)
    grid_spec = pltpu.PrefetchScalarGridSpec(
        num_scalar_prefetch=1,
        grid=(nt - nct, bsz // nb),
        in_specs=[pl.BlockSpec(memory_space=pl.ANY),
                  pl.BlockSpec((nb, None, tm, LANES), lambda g, b, inv: (b, g + nct, 0, 0)),
                  pl.BlockSpec((nb, tm, d), lambda g, b, inv: (b, g + nct, 0)),
                  _layer_spec(mod, layer), full(fg)],
        out_specs=pl.BlockSpec((nb, tm, d), lambda g, b, inv: (b, g, 0)),
        scratch_shapes=[pltpu.VMEM((2, nb * s_loc, d), BF16), pltpu.SemaphoreType.DMA((2,))],
    )
    kern = functools.partial(_final_combine_kernel, n_lb=n_lb, nt=nt, n_ctx_tiles=nct, ctx_row=bsz, d=d, tm=tm,
                             s_loc=s_loc, nb=nb, tile0=nct)
    return pl.pallas_call(
        kern,
        grid_spec=grid_spec,
        out_shape=jax.ShapeDtypeStruct((bsz, t - n_ctx, d), F32),
        compiler_params=_cparams("arbitrary", "arbitrary"),
        name="moe_combine",
    )(inv, ys_em, meta, h1, mod, fg)


def _rope_tables(length, dim):
    rows = length // GRID_W
    row = jnp.repeat(jnp.arange(rows), GRID_W).astype(F32)
    col = jnp.tile(jnp.arange(GRID_W), rows).astype(F32)
    n_freq = dim // 4
    inv_freq = ROPE_THETA ** (-jnp.arange(n_freq, dtype=F32) / n_freq)
    ang = jnp.concatenate([row[:, None] * inv_freq, col[:, None] * inv_freq], axis=-1)
    return jnp.cos(ang), jnp.sin(ang)


def _table_set(n_ctx, n_lat):
    def lanes(cos, sin, lane0, width, reps, outside_cos):
        c = jnp.concatenate([cos, cos], axis=-1)
        s = jnp.concatenate([-sin, sin], axis=-1)
        grp_c = jnp.full((n_lat, width), outside_cos, F32).at[:, lane0:lane0 + c.shape[1]].set(c)
        grp_s = jnp.zeros((n_lat, width), F32).at[:, lane0:lane0 + s.shape[1]].set(s)
        ctx_c = jnp.full((n_ctx, width), outside_cos, F32).at[:, lane0:lane0 + c.shape[1]].set(1.0)
        ctx_s = jnp.zeros((n_ctx, width), F32)
        return (jnp.tile(jnp.concatenate([ctx_c, grp_c], axis=0), (1, reps)),
                jnp.tile(jnp.concatenate([ctx_s, grp_s], axis=0), (1, reps)))

    dcos, dsin = _rope_tables(n_lat, DA_QK)
    cda, sda = lanes(dcos, dsin, 0, DA_QK, DA_QW // DA_QK, 1.0)
    mcos, msin = _rope_tables(n_lat, MLA_ROPE)
    cq, sm = lanes(mcos, msin, KR_LANE0, LANES, 1, 1.0)
    ck, _ = lanes(mcos, msin, KR_LANE0, LANES, 1, 0.0)
    return cda, sda, cq, ck, sm


def _pack_w_in(w_in):
    depth, d, _ = w_in.shape
    o_mla = DA_IN + SSD_IN
    dt = w_in[..., MAIN_COLS:o_mla]
    cq_ckv = w_in[..., o_mla:o_mla + MLA_Q_RANK + MLA_KV_RANK]
    kr = w_in[..., o_mla + MLA_Q_RANK + MLA_KV_RANK:]
    zeros = lambda n: jnp.zeros((depth, d, n), w_in.dtype)
    tail = jnp.concatenate([cq_ckv, zeros(KR_LANE0), kr, dt, zeros(LANES - DT_LANE0 - 2 * SSD_HEADS)], axis=-1)
    return w_in[..., :MAIN_COLS].astype(BF16), tail.astype(BF16)


def _pack_w_uq(w_uq):
    depth, r, _ = w_uq.shape
    w = w_uq.reshape(depth, r, MLA_HEADS, MLA_NOPE + MLA_ROPE)
    w = jnp.pad(w, ((0, 0), (0, 0), (0, 0), (0, MLA_HEAD_PAD - MLA_NOPE - MLA_ROPE)))
    return w.reshape(depth, r, MLA_QPAD).astype(BF16)


def _pack_w_ukv(w_ukv):
    depth, r, _ = w_ukv.shape
    w = w_ukv.reshape(depth, r, MLA_HEADS, MLA_NOPE + MLA_V)
    kn = jnp.pad(w[..., :MLA_NOPE], ((0, 0), (0, 0), (0, 0), (0, MLA_HEAD_PAD - MLA_NOPE)))
    return jnp.concatenate([kn.reshape(depth, r, MLA_QPAD), w[..., MLA_NOPE:].reshape(depth, r, MLA_WIDTH)],
                           axis=-1).astype(BF16)


def _dispatch_tables(pcnt, n_lb, n_steps):
    ntt = pcnt.shape[0]
    nb = (pcnt / MOE_BLK).astype(jnp.int32)
    lo = jnp.cumsum(nb, axis=1) - nb
    n_e = jnp.sum(nb, axis=0)
    p_e = (n_e + MOE_STEP_BLKS - 1) // MOE_STEP_BLKS * MOE_STEP_BLKS
    ends = jnp.cumsum(p_e)
    base = (ends - p_e)[None, :] + jnp.cumsum(nb, axis=0) - nb
    lb = jnp.arange(n_lb, dtype=jnp.int32)
    owner = (lb[None, :, None] >= lo[:, None, :]) & (lb[None, :, None] < (lo + nb)[:, None, :])
    used = jnp.any(owner, axis=-1)
    pos = jnp.sum(jnp.where(owner, (base - lo)[:, None, :], 0), axis=-1) + lb[None, :]
    inv = jnp.where(used, pos, 0).astype(jnp.int32)
    n_pos = n_steps * MOE_STEP_BLKS
    src = jnp.arange(ntt, dtype=jnp.int32)[:, None] * n_lb + lb[None, :]
    fwd = jnp.zeros((n_pos,), jnp.int32).at[jnp.where(used, pos, n_pos).reshape(-1)].set(src.reshape(-1), mode="drop")
    starts = jnp.arange(n_steps, dtype=jnp.int32) * MOE_STEP_BLKS
    step_e = jnp.minimum(jnp.sum(starts[:, None] >= ends[None, :], axis=1), N_EXPERTS - 1).astype(jnp.int32)
    step_valid = (starts < ends[-1]).astype(jnp.int32)
    step_new = jnp.concatenate([jnp.ones((1,), jnp.int32), (step_e[1:] != step_e[:-1]).astype(jnp.int32)])
    return inv.reshape(-1), fwd, step_e, step_valid, step_new


def kernel(x, c, ctx, c_ctx, norm_mix_g, norm_ffn_g, w_mod, b_mod, w_in, w_out, da_lambda, da_subln_g, ssd_conv_w, ssd_conv_b, ssd_dt_bias, ssd_a_log, ssd_d, ssd_norm_g, mla_q_norm_g, mla_kv_norm_g, mla_w_uq, mla_w_ukv, router_w, router_bias, exp_w_gate, exp_w_up, exp_w_down, final_norm_g):
    bsz, n_lat, d = x.shape
    n_ctx = ctx.shape[1]
    t = n_ctx + n_lat
    tm = min(256, n_ctx)
    assert n_ctx % tm == 0 and n_lat % tm == 0 and n_ctx % SSD_CHUNK == 0 and n_lat % SSD_CHUNK == 0
    nt = t // tm
    s_loc = 2 * tm + 2 * LANES
    n_lb = s_loc // MOE_BLK
    n_steps = -(-(bsz * nt * n_lb + N_EXPERTS * (MOE_STEP_BLKS - 1)) // MOE_STEP_BLKS)
    nc, nc0 = t // SSD_CHUNK, n_ctx // SSD_CHUNK
    nb = 2 if bsz % 2 == 0 else 1
    nb_wide = 4 if bsz % 4 == 0 else nb
    q = SSD_CHUNK

    r_pad = -(-(bsz + 1) // 8) * 8
    c_rows = jnp.concatenate([c, c_ctx[None, :], jnp.zeros((r_pad - bsz - 1, d), F32)], axis=0)
    mod = _modulation(c_rows, w_mod, b_mod)

    w_in_p = _pack_w_in(w_in)
    w_uq_p = _pack_w_uq(mla_w_uq)
    w_ukv_p = _pack_w_ukv(mla_w_ukv)
    w_out16 = w_out.astype(BF16)
    tabs = _table_set(n_ctx, n_lat)
    rw_pad = jnp.pad(router_w, ((0, 0), (0, LANES - N_EXPERTS)))
    rb = router_bias.reshape(N_EXPERTS, 1)
    lane_b = lambda v: jnp.broadcast_to(v[..., None], v.shape + (q,))
    dsk_rows = jnp.repeat(ssd_d, SSD_P, axis=-1)
    ng_mix, ng_ffn = norm_mix_g[:, None, :], norm_ffn_g[:, None, :]
    qg, kvg = mla_q_norm_g[:, None, :], mla_kv_norm_g[:, None, :]
    sub_g = jnp.tile(da_subln_g, (1, DA_HEADS))[:, :, None]
    ssd_par = (lane_b(ssd_conv_w), lane_b(ssd_conv_b), lane_b(ssd_dt_bias.reshape(DEPTH, -1)),
               lane_b(ssd_a_log.reshape(DEPTH, -1)), lane_b(dsk_rows), lane_b(ssd_norm_g))

    h = jnp.concatenate([ctx, x], axis=1)
    moe = None
    for i in range(DEPTH):
        lam_init = 0.8 - 0.6 * math.exp(-0.3 * i)
        if moe is None:
            qda_t, kda, vda_t, z_t, xbc_t, dt_t, q4t, k4, vm_t = _inproj(
                h, mod, ng_mix, w_in_p, tabs, qg, kvg, w_uq_p, w_ukv_p, layer=i, tm=tm, n_ctx=n_ctx, nb=nb_wide)
        else:
            qda_t, kda, vda_t, z_t, xbc_t, dt_t, q4t, k4, vm_t, h = _inproj(
                None, mod, ng_mix, w_in_p, tabs, qg, kvg, w_uq_p, w_ukv_p, layer=i, tm=tm, n_ctx=n_ctx, nb=nb, moe=moe)
        o_da = _da_attention(qda_t, kda, vda_t, da_lambda, sub_g, layer=i, lam_init=lam_init, tq=tm, n_ctx=n_ctx)
        o_mla = _mla_attention(q4t, k4, vm_t, tq=tm, n_ctx=n_ctx)
        o_ssd_t = _ssd_mixer(z_t, xbc_t, dt_t, *ssd_par, layer=i, nc0=nc0)
        h1, xs, meta, cnt = _outproj_dispatch(o_da, o_ssd_t, o_mla, h, mod, ng_ffn, w_out16, rw_pad, rb, layer=i, tm=tm,
                                              n_ctx=n_ctx, s_loc=s_loc, nb=nb_wide)
        inv, fwd, step_e, step_valid, step_new = _dispatch_tables(cnt[:, :, :, 0].reshape(bsz * nt, N_EXPERTS), n_lb,
                                                                  n_steps)
        ys_em = _expert_ffn(xs.reshape(bsz * nt * s_loc, d + MOE_EXTRA), exp_w_gate, exp_w_up, exp_w_down, step_e, fwd,
                            step_valid, step_new, layer=i, n_steps=n_steps)
        moe = (ys_em, inv, meta, h1, s_loc)
    return _final_combine(ys_em, inv, meta, h1, mod, final_norm_g[None, :], layer=DEPTH - 1, tm=tm, n_ctx=n_ctx,
                          s_loc=s_loc, nb=nb)
```

```python
import functools
import math

import jax
import jax.numpy as jnp
from jax import lax
from jax.experimental import pallas as pl
from jax.experimental.pallas import tpu as pltpu

F32 = jnp.float32
BF16 = jnp.bfloat16

DEPTH = 4
GRID_W = 64
EPS = 1e-6
ROPE_THETA = 10000.0
DA_HEADS, DA_QK = 4, 32
DA_V = 2 * DA_QK
DA_WIDTH = DA_HEADS * DA_V
DA_QW = DA_HEADS * 2 * DA_QK
DA_IN = 2 * DA_QW + DA_WIDTH
SSD_HEADS, SSD_P, SSD_GROUPS, SSD_STATE, SSD_CHUNK = 8, 64, 2, 64, 128
SSD_INNER = SSD_HEADS * SSD_P
SSD_GN = SSD_GROUPS * SSD_STATE
SSD_CONV_DIM = SSD_INNER + 2 * SSD_GN
SSD_IN = SSD_INNER + SSD_CONV_DIM + 2 * SSD_HEADS
MLA_HEADS, MLA_Q_RANK, MLA_KV_RANK, MLA_NOPE, MLA_ROPE, MLA_V = 4, 256, 128, 64, 32, 64
MLA_WIDTH = MLA_HEADS * MLA_V
MLA_IN = MLA_Q_RANK + MLA_KV_RANK + MLA_ROPE
MLA_SCALE = (MLA_NOPE + MLA_ROPE) ** -0.5
N_EXPERTS, N_GROUPS, D_EXPERT = 16, 4, 512
PER_GROUP = N_EXPERTS // N_GROUPS

LANES = 128
SUBLANES_BF16 = 16
VMEM_LIMIT_BYTES = 56 * 1024 * 1024

MLA_HEAD_PAD = LANES
MLA_QPAD = MLA_HEADS * MLA_HEAD_PAD
KR_LANE0 = MLA_NOPE
DT_LANE0 = MLA_NOPE + MLA_ROPE
MAIN_COLS = DA_IN + SSD_INNER + SSD_CONV_DIM
C_Z = DA_IN
C_XBC = C_Z + SSD_INNER
TAIL_COLS = MLA_Q_RANK + MLA_KV_RANK + LANES
T_CKV = MLA_Q_RANK
T_MISC = T_CKV + MLA_KV_RANK
LOG2E = math.log2(math.e)
DA_QSCALE = DA_QK ** -0.5 * LOG2E
MLA_QSCALE = MLA_SCALE * LOG2E

KEY_CHUNK = 256
MOE_BLK = SUBLANES_BF16
MOE_STEP_BLKS = 32
MOE_EXTRA = LANES


def _sigmoid(x):
    return 1.0 / (1.0 + jnp.exp(-x))


def _silu(x):
    return x * _sigmoid(x)


def _rms(x, g, axis=-1):
    return x * lax.rsqrt(jnp.mean(x * x, axis=axis, keepdims=True) + EPS) * g


def _ada_norm(hv, g, shift, scale):
    return _rms(hv, g) * (1.0 + scale) + shift


def _rope(x, cos, sin_signed, half):
    w = x.shape[-1]
    lane = lax.broadcasted_iota(jnp.int32, x.shape, x.ndim - 1)
    first = (lane % (2 * half)) < half
    partner = jnp.where(first, pltpu.roll(x, w - half, x.ndim - 1), pltpu.roll(x, half, x.ndim - 1))
    return x * cos + partner * sin_signed


def _dot(a, b):
    return jnp.dot(a, b, preferred_element_type=F32)


def _dot_hi(a, b):
    return jnp.dot(a, b, preferred_element_type=F32, precision=lax.Precision.HIGHEST)


def _dot_split3(x, m01):
    x1 = x.astype(BF16)
    r1 = x - x1.astype(F32)
    x2 = r1.astype(BF16)
    x3 = (r1 - x2.astype(F32)).astype(BF16)
    return _dot(x1, m01) + _dot(x2, m01) + _dot(x3, m01)


def _layer_spec(a, i):
    return pl.BlockSpec((None,) + a.shape[1:], lambda *_: (i,) + (0,) * (a.ndim - 1))


def _cparams(*sem):
    return pltpu.CompilerParams(dimension_semantics=sem, vmem_limit_bytes=VMEM_LIMIT_BYTES)


def _mod_kernel(c_ref, w_ref, b_ref, o_ref):
    o_ref[...] = _dot_hi(_silu(c_ref[...]), w_ref[...]) + b_ref[...]


def _modulation(c_rows, w_mod, b_mod):
    depth, d, n = w_mod.shape
    r = c_rows.shape[0]
    tn = 1536
    return pl.pallas_call(
        _mod_kernel,
        grid=(depth, n // tn),
        in_specs=[pl.BlockSpec((r, d), lambda l, j: (0, 0)),
                  pl.BlockSpec((None, d, tn), lambda l, j: (l, 0, j)),
                  pl.BlockSpec((None, 1, tn), lambda l, j: (l, 0, j))],
        out_specs=pl.BlockSpec((None, r, tn), lambda l, j: (l, 0, j)),
        out_shape=jax.ShapeDtypeStruct((depth, r, n), F32),
        compiler_params=_cparams("arbitrary", "arbitrary"),
        name="modulation",
    )(c_rows, w_mod, b_mod.reshape(depth, 1, n))


def _inproj_kernel(*refs, n_ctx_tiles, ctx_row, d, nb, tm, moe_geom):
    if moe_geom is None:
        h_ref, refs = refs[0], refs[1:]
    else:
        (inv_ref, ys_hbm, meta_ref, h1_ref, modp_ref), refs = refs[:5], refs[5:]
        h2_ref, ybuf, sem = refs[-3:]
        refs = refs[:-3]
    (mod_ref, ng_ref, wa_ref, wb_ref, cda_ref, sda_ref, cq_ref, ck_ref, sm_ref, qg_ref, kvg_ref, wuq_ref, wukv_ref,
     qdat_ref, kda_ref, vdat_ref, zt_ref, xbct_ref, dtt_ref, q4t_ref, k4_ref, vmt_ref) = refs
    ti, bp = pl.program_id(0), pl.program_id(1)
    if moe_geom is None:
        hs = [h_ref[bb] for bb in range(nb)]
    else:
        hs, drain = _moe_combine_rows(inv_ref, ys_hbm, meta_ref, h1_ref, modp_ref, ybuf, sem, nb=nb, tm=tm, d=d,
                                      n_ctx_tiles=n_ctx_tiles, ctx_row=ctx_row, **moe_geom)
        for bb in range(nb):
            h2_ref[bb] = hs[bb]
    cda, sda, sm = cda_ref[...], sda_ref[...], sm_ref[...]
    cos_q = jnp.concatenate([cq_ref[...]] * MLA_HEADS, axis=1)
    sin_q = jnp.concatenate([sm] * MLA_HEADS, axis=1)
    q = SSD_CHUNK
    accs = []
    for bb in range(nb):
        row = jnp.where(ti < n_ctx_tiles, ctx_row, bp * nb + bb)
        shift = mod_ref[pl.ds(row, 1), pl.ds(0, d)]
        scale = mod_ref[pl.ds(row, 1), pl.ds(d, d)]
        u = _ada_norm(hs[bb], ng_ref[...], shift, scale).astype(BF16)
        accs.append((_dot(u, wa_ref[...]), _dot(u, wb_ref[...])))
    for bb in range(nb):
        acc, tail = accs[bb]
        qdat_ref[bb] = (_rope(acc[:, 0:DA_QW], cda, sda, DA_QK // 2) * DA_QSCALE).T.astype(BF16)
        kda_ref[bb] = _rope(acc[:, DA_QW:2 * DA_QW], cda, sda, DA_QK // 2).astype(BF16)
        vdat_ref[bb] = acc[:, 2 * DA_QW:DA_IN].T.astype(BF16)
        z_t = acc[:, C_Z:C_XBC].T
        xbc_t = acc[:, C_XBC:MAIN_COLS].T
        misc = tail[:, T_MISC:TAIL_COLS]
        dt_t = misc.T[DT_LANE0:DT_LANE0 + 2 * SSD_HEADS]
        for c in range(tm // q):
            zt_ref[bb, c] = z_t[:, c * q:(c + 1) * q].astype(BF16)
            xbct_ref[bb, c] = xbc_t[:, c * q:(c + 1) * q]
            dtt_ref[bb, c] = dt_t[:, c * q:(c + 1) * q]
        cqn = _rms(tail[:, 0:T_CKV], qg_ref[...]).astype(BF16)
        qm_t = (_rope(_dot(cqn, wuq_ref[...]), cos_q, sin_q, MLA_ROPE // 2) * MLA_QSCALE).T
        ckvn = _rms(tail[:, T_CKV:T_MISC], kvg_ref[...]).astype(BF16)
        kv = _dot(ckvn, wukv_ref[...])
        kr = _rope(misc, ck_ref[...], sm, MLA_ROPE // 2)
        km = kv[:, :MLA_QPAD] + jnp.concatenate([kr] * MLA_HEADS, axis=1)
        for hh in range(MLA_HEADS):
            q4t_ref[bb, hh] = qm_t[hh * MLA_HEAD_PAD:(hh + 1) * MLA_HEAD_PAD].astype(BF16)
            k4_ref[bb, hh] = km[:, hh * MLA_HEAD_PAD:(hh + 1) * MLA_HEAD_PAD].astype(BF16)
        vmt_ref[bb] = kv[:, MLA_QPAD:].T.astype(BF16)
    if moe_geom is not None:
        drain()


def _inproj(h, mod, ng, w_in_p, tabs, qg, kvg, wuq_p, wukv_p, *, layer, tm, n_ctx, nb, moe=None):
    bsz, t, d = (h if moe is None else moe[3]).shape
    nt, q = t // tm, SSD_CHUNK
    cpt = tm // q
    tok = lambda w: pl.BlockSpec((nb, tm, w), lambda ti, b, *_: (b, ti, 0))
    tab = lambda w: pl.BlockSpec((tm, w), lambda ti, b, *_: (ti, 0))
    chunked = lambda f: pl.BlockSpec((nb, cpt, f, q), lambda ti, b, *_: (b, ti, 0, 0))
    lay = lambda a: _layer_spec(a, layer)
    cda, sda, cq, ck, sm = tabs
    sds = jax.ShapeDtypeStruct
    tok_t = lambda w: pl.BlockSpec((nb, w, tm), lambda ti, b, *_: (b, 0, ti))
    out_specs = [tok_t(DA_QW), tok(DA_QW), tok_t(DA_WIDTH),
                 chunked(SSD_INNER), chunked(SSD_CONV_DIM), chunked(2 * SSD_HEADS),
                 pl.BlockSpec((nb, MLA_HEADS, MLA_HEAD_PAD, tm), lambda ti, b, *_: (b, 0, 0, ti)),
                 pl.BlockSpec((nb, MLA_HEADS, tm, MLA_HEAD_PAD), lambda ti, b, *_: (b, 0, ti, 0)), tok_t(MLA_WIDTH)]
    out_shape = [sds((bsz, DA_QW, t), BF16), sds((bsz, t, DA_QW), BF16), sds((bsz, DA_WIDTH, t), BF16),
                 sds((bsz, t // q, SSD_INNER, q), BF16), sds((bsz, t // q, SSD_CONV_DIM, q), F32),
                 sds((bsz, t // q, 2 * SSD_HEADS, q), F32),
                 sds((bsz, MLA_HEADS, MLA_HEAD_PAD, t), BF16), sds((bsz, MLA_HEADS, t, MLA_HEAD_PAD), BF16),
                 sds((bsz, MLA_WIDTH, t), BF16)]
    w_main, w_tail = w_in_p
    common_specs = [lay(mod), lay(ng), lay(w_main), lay(w_tail), tab(DA_QW), tab(DA_QW), tab(LANES), tab(LANES),
                    tab(LANES), lay(qg), lay(kvg), lay(wuq_p), lay(wukv_p)]
    common_args = (mod, ng, w_main, w_tail, cda, sda, cq, ck, sm, qg, kvg, wuq_p, wukv_p)
    geom = dict(n_ctx_tiles=n_ctx // tm, ctx_row=bsz, d=d, nb=nb, tm=tm)
    if moe is None:
        return pl.pallas_call(
            functools.partial(_inproj_kernel, moe_geom=None, **geom),
            grid=(nt, bsz // nb),
            in_specs=[tok(d)] + common_specs,
            out_specs=out_specs,
            out_shape=out_shape,
            compiler_params=_cparams("arbitrary", "arbitrary"),
            name="inproj",
        )(h, *common_args)
    ys_em, inv, meta, h1, s_loc = moe
    grid_spec = pltpu.PrefetchScalarGridSpec(
        num_scalar_prefetch=1,
        grid=(nt, bsz // nb),
        in_specs=[pl.BlockSpec(memory_space=pl.ANY),
                  pl.BlockSpec((nb, None, tm, LANES), lambda ti, b, inv: (b, ti, 0, 0)), tok(d),
                  _layer_spec(mod, layer - 1)] + common_specs,
        out_specs=out_specs + [tok(d)],
        scratch_shapes=[pltpu.VMEM((2, nb * s_loc, d), BF16), pltpu.SemaphoreType.DMA((2,))],
    )
    moe_geom = dict(n_lb=s_loc // MOE_BLK, nt=nt, s_loc=s_loc)
    return pl.pallas_call(
        functools.partial(_inproj_kernel, moe_geom=moe_geom, **geom),
        grid_spec=grid_spec,
        out_shape=out_shape + [sds((bsz, t, d), F32)],
        compiler_params=_cparams("arbitrary", "arbitrary"),
        name="combine_inproj",
    )(inv, ys_em, meta, h1, mod, *common_args)


V_AUG = DA_V + SUBLANES_BF16


def _scores_pass(k_at, qtm, nk, s_scr):
    kc = KEY_CHUNK if nk % KEY_CHUNK == 0 else LANES
    m = None
    for c0 in range(0, nk, kc):
        s_c = _dot(k_at(c0, c0 + kc), qtm)
        s_scr[c0:c0 + kc, :] = s_c
        part = jnp.max(s_c, axis=0, keepdims=True)
        m = part if m is None else jnp.maximum(m, part)
    return m


def _pv_pass(vaug_at, nk, s_scr, m):
    kc = KEY_CHUNK if nk % KEY_CHUNK == 0 else LANES
    acc = None
    for c0 in range(0, nk, kc):
        e = jnp.exp2(s_scr[c0:c0 + kc, :] - m).astype(BF16)
        inc = _dot(vaug_at(c0, c0 + kc), e)
        acc = inc if acc is None else acc + inc
    return acc


def _attend_heads(k_of, qt_of, vaug_of, nk, scr, n_sub):
    m = _scores_pass(k_of(0), qt_of(0), nk, scr[0])
    outs = []
    for j in range(n_sub):
        if j + 1 < n_sub:
            m_next = _scores_pass(k_of(j + 1), qt_of(j + 1), nk, scr[(j + 1) % 2])
        acc = _pv_pass(vaug_of(j), nk, scr[j % 2], m)
        outs.append(acc[:DA_V] * (1.0 / acc[DA_V:DA_V + 1]))
        if j + 1 < n_sub:
            m = m_next
    return outs


def _fill_vaug(vt_ref, vaug_scr, n_heads):
    t = vt_ref.shape[1]
    for hh in range(n_heads):
        vaug_scr[hh, 0:DA_V, :] = vt_ref[hh * DA_V:(hh + 1) * DA_V, :]
        vaug_scr[hh, DA_V:V_AUG, :] = jnp.ones((V_AUG - DA_V, t), BF16)


def _da_attn_kernel(lam_ref, g_ref, qt_ref, k_ref, vt_ref, o_ref, s0_scr, s1_scr, vaug_scr, *, n_ctx, n_ctx_tiles,
                    lam_init, tq):
    qi = pl.program_id(1)

    @pl.when(qi == 0)
    def _():
        _fill_vaug(vt_ref, vaug_scr, DA_HEADS)

    lv = lam_ref[...]
    lam = (jnp.exp(jnp.sum(lv[0:1] * lv[1:2], axis=-1, keepdims=True))
           - jnp.exp(jnp.sum(lv[2:3] * lv[3:4], axis=-1, keepdims=True)) + lam_init)
    row_q = lax.broadcasted_iota(jnp.int32, (DA_QW, 1), 0)

    def attend(nk):
        qt = qt_ref[...]
        qt_of = lambda j: qt * jnp.where(row_q // DA_QK == j, 1.0, 0.0).astype(BF16)
        k_of = lambda j: (lambda c0, c1: k_ref[c0:c1, :])
        vaug_of = lambda j: (lambda c0, c1: vaug_scr[j // 2, :, c0:c1])
        outs = _attend_heads(k_of, qt_of, vaug_of, nk, (s0_scr, s1_scr), 2 * DA_HEADS)
        heads = []
        for hh in range(DA_HEADS):
            o = outs[2 * hh] - lam * outs[2 * hh + 1]
            heads.append(o * lax.rsqrt(jnp.mean(o * o, axis=0, keepdims=True) + EPS))
        o_t = jnp.concatenate(heads, axis=0) * g_ref[...] * (1.0 - lam_init)
        o_ref[...] = o_t.T.astype(BF16)

    @pl.when(qi < n_ctx_tiles)
    def _():
        attend(n_ctx)

    @pl.when(qi >= n_ctx_tiles)
    def _():
        attend(k_ref.shape[0])


def _da_attention(qt, k, vt, lam_vec, g_col, *, layer, lam_init, tq, n_ctx):
    bsz, _, t = qt.shape
    kern = functools.partial(_da_attn_kernel, n_ctx=n_ctx, n_ctx_tiles=n_ctx // tq, lam_init=lam_init, tq=tq)
    return pl.pallas_call(
        kern,
        grid=(bsz, t // tq),
        in_specs=[_layer_spec(lam_vec, layer), _layer_spec(g_col, layer),
                  pl.BlockSpec((None, DA_QW, tq), lambda b, i: (b, 0, i)),
                  pl.BlockSpec((None, t, DA_QW), lambda b, i: (b, 0, 0)),
                  pl.BlockSpec((None, DA_WIDTH, t), lambda b, i: (b, 0, 0))],
        out_specs=pl.BlockSpec((None, tq, DA_WIDTH), lambda b, i: (b, i, 0)),
        out_shape=jax.ShapeDtypeStruct((bsz, t, DA_WIDTH), BF16),
        scratch_shapes=[pltpu.VMEM((t, tq), F32), pltpu.VMEM((t, tq), F32), pltpu.VMEM((DA_HEADS, V_AUG, t), BF16)],
        compiler_params=_cparams("arbitrary", "arbitrary"),
        name="da_attention",
    )(lam_vec, g_col, qt, k, vt)


def _mla_attn_kernel(qt_ref, k_ref, vt_ref, o_ref, s0_scr, s1_scr, vaug_scr, *, n_ctx, n_ctx_tiles, tq):
    qi = pl.program_id(1)

    @pl.when(qi == 0)
    def _():
        _fill_vaug(vt_ref, vaug_scr, MLA_HEADS)

    def attend(nk):
        k_of = lambda hh: (lambda c0, c1: k_ref[hh, c0:c1, :])
        vaug_of = lambda hh: (lambda c0, c1: vaug_scr[hh, :, c0:c1])
        outs = _attend_heads(k_of, lambda hh: qt_ref[hh], vaug_of, nk, (s0_scr, s1_scr), MLA_HEADS)
        o_ref[...] = jnp.concatenate(outs, axis=0).T.astype(BF16)

    @pl.when(qi < n_ctx_tiles)
    def _():
        attend(n_ctx)

    @pl.when(qi >= n_ctx_tiles)
    def _():
        attend(k_ref.shape[1])


def _mla_attention(q4t, k4, vt, *, tq, n_ctx):
    bsz, _, _, t = q4t.shape
    kern = functools.partial(_mla_attn_kernel, n_ctx=n_ctx, n_ctx_tiles=n_ctx // tq, tq=tq)
    return pl.pallas_call(
        kern,
        grid=(bsz, t // tq),
        in_specs=[pl.BlockSpec((None, MLA_HEADS, MLA_HEAD_PAD, tq), lambda b, i: (b, 0, 0, i)),
                  pl.BlockSpec((None, MLA_HEADS, t, MLA_HEAD_PAD), lambda b, i: (b, 0, 0, 0)),
                  pl.BlockSpec((None, MLA_WIDTH, t), lambda b, i: (b, 0, 0))],
        out_specs=pl.BlockSpec((None, tq, MLA_WIDTH), lambda b, i: (b, i, 0)),
        out_shape=jax.ShapeDtypeStruct((bsz, t, MLA_WIDTH), BF16),
        scratch_shapes=[pltpu.VMEM((t, tq), F32), pltpu.VMEM((t, tq), F32), pltpu.VMEM((MLA_HEADS, V_AUG, t), BF16)],
        compiler_params=_cparams("arbitrary", "arbitrary"),
        name="mla_attention",
    )(q4t, k4, vt)


def _ssd_kernel(z_ref, xbc_ref, dt_ref, cw_ref, cb_ref, dtb_ref, alog_ref, dsk_ref, ng_ref,
                o_ref, xc_scr, y_scr, hf_scr, hb_scr, *, nc, nc0):
    q = SSD_CHUNK
    hp = SSD_HEADS * SSD_P
    lane_x = lax.broadcasted_iota(jnp.int32, (SSD_CONV_DIM, q), 1)

    def conv_body(c, carry):
        xc = xbc_ref[c]
        keep_prev = jnp.where((c == 0) | (c == nc0), 0.0, 1.0)
        keep_next = jnp.where((c == nc0 - 1) | (c == nc - 1), 0.0, 1.0)
        xp = xbc_ref[jnp.maximum(c - 1, 0)] * keep_prev
        xn = xbc_ref[jnp.minimum(c + 1, nc - 1)] * keep_next
        prev = pltpu.roll(jnp.where(lane_x == q - 1, xp, xc), 1, 1)
        nxt = pltpu.roll(jnp.where(lane_x == 0, xn, xc), q - 1, 1)
        a = _silu(cw_ref[0] * prev + cw_ref[1] * xc + cw_ref[2] * nxt + cb_ref[...])
        xc_scr[c] = a
        y_scr[c] = dsk_ref[...] * a[:hp]
        return carry

    lax.fori_loop(0, nc, conv_body, 0)
    hf_scr[...] = jnp.zeros_like(hf_scr)
    hb_scr[...] = jnp.zeros_like(hb_scr)

    sub = lax.broadcasted_iota(jnp.int32, (q, q), 0)
    lan = lax.broadcasted_iota(jnp.int32, (q, q), 1)
    lane_n = lax.broadcasted_iota(jnp.int32, (1, SSD_GN), 1)
    pad_rows = jnp.zeros((q - SSD_HEADS, q), F32)

    per_g = SSD_HEADS // SSD_GROUPS
    rows_g = hp // SSD_GROUPS
    gmask = [lane_n // SSD_STATE == g for g in range(SSD_GROUPS)]
    tris = (sub <= lan, sub >= lan)
    tri01 = [jnp.where(tr, 1.0, 0.0).astype(BF16) for tr in tris]
    ends = (q - 1, 0)
    h_scrs = (hf_scr, hb_scr)
    dirs = (0, 1)

    lanes = 2 if nc % 2 == 0 else 1

    def prep(c, d):
        xc = xc_scr[c]
        hs = slice(SSD_HEADS * d, SSD_HEADS * (d + 1))
        dtl = dt_ref[c][hs] + dtb_ref[hs]
        dt = jnp.maximum(dtl, 0.0) + jnp.log(1.0 + jnp.exp(-jnp.abs(dtl)))
        dta = dt * (-jnp.exp(alog_ref[hs]))
        cum_pad = _dot_split3(jnp.concatenate([dta, pad_rows], axis=0), tri01[d])
        btok = xc[hp:hp + SSD_GN].T
        ct16 = xc[hp + SSD_GN:].astype(BF16)
        gts = [_dot(jnp.where(gmask[g], btok, 0.0).astype(BF16), ct16) for g in range(SSD_GROUPS)]
        return dict(c=c, xt=xc[:hp], dt=dt, cum_pad=cum_pad, btok16=btok.astype(BF16), ct16=ct16, gts=gts)

    def head_work(p, d):
        cum_row = p["cum_pad"][:SSD_HEADS]
        cum_col = p["cum_pad"].T
        ydiag, xws, decs, ear = [], [], [], []
        for hh in range(SSD_HEADS):
            ar = cum_row[hh:hh + 1, :]
            ac = cum_col[:, hh:hh + 1]
            sct = (p["gts"][hh // per_g] * jnp.where(tris[d], jnp.exp(ar - ac), 0.0)).astype(BF16)
            xdt = p["xt"][hh * SSD_P:(hh + 1) * SSD_P] * p["dt"][hh:hh + 1, :]
            ydiag.append(_dot(xdt.astype(BF16), sct))
            a_end = ar[:, ends[d]:ends[d] + 1]
            xws.append(xdt * jnp.exp(a_end - ar))
            decs.append(jnp.exp(a_end))
            ear.append(jnp.exp(ar))
        return ydiag, xws, decs, ear

    def state_incs(p, xws):
        return [jnp.where(gmask[g], _dot(jnp.concatenate(xws[g * per_g:(g + 1) * per_g], axis=0).astype(BF16),
                                         p["btok16"]), 0.0) for g in range(SSD_GROUPS)]

    def carried_out(p, hm):
        return [_dot(hm[g * rows_g:(g + 1) * rows_g].astype(BF16), p["ct16"]) for g in range(SSD_GROUPS)]

    def step(s, carry):
        pos = [s * lanes + u for u in range(lanes)]
        cs = [(p_, jnp.where(p_ < nc0, nc0 - 1 - p_, nc - 1 - (p_ - nc0))) for p_ in pos]
        pre = [[prep(cs[u][d], d) for d in dirs] for u in range(lanes)]
        hm = [h_scrs[d][...] for d in dirs]
        yo = [carried_out(pre[0][d], hm[d]) for d in dirs]
        hw = [head_work(pre[0][d], d) for d in dirs]
        for u in range(lanes):
            incs = [state_incs(pre[u][d], hw[d][1]) for d in dirs]
            hw_next = [head_work(pre[u + 1][d], d) for d in dirs] if u + 1 < lanes else None
            for d in dirs:
                ydiag, _, decs, ear = hw[d]
                y = [ydiag[hh] + yo[d][hh // per_g][(hh % per_g) * SSD_P:(hh % per_g + 1) * SSD_P] * ear[hh]
                     for hh in range(SSD_HEADS)]
                y_scr[pre[u][d]["c"]] += jnp.concatenate(y, axis=0)
                hm[d] = jnp.concatenate(
                    [hm[d][hh * SSD_P:(hh + 1) * SSD_P] * decs[hh]
                     + incs[d][hh // per_g][(hh % per_g) * SSD_P:(hh % per_g + 1) * SSD_P]
                     for hh in range(SSD_HEADS)], axis=0)
            if u + 1 < lanes:
                yo = [carried_out(pre[u + 1][d], hm[d]) for d in dirs]
                hw = hw_next
        for d in dirs:
            h_scrs[d][...] = hm[d]
        return carry

    lax.fori_loop(0, nc // lanes, step, 0)

    def fin_body(c, carry):
        gated = y_scr[c] * _silu(z_ref[c].astype(F32))
        o_ref[c] = _rms(gated, ng_ref[...], axis=0).astype(BF16)
        return carry

    lax.fori_loop(0, nc, fin_body, 0)


def _ssd_mixer(z_t, xbc_t, dt_t, cw, cb, dtb, alog, dsk, ng, *, layer, nc0):
    bsz, nc, _, q = xbc_t.shape
    hp = SSD_HEADS * SSD_P
    per_b = lambda f: pl.BlockSpec((None, nc, f, q), lambda b: (b, 0, 0, 0))
    full = lambda a: _layer_spec(a, layer)
    kern = functools.partial(_ssd_kernel, nc=nc, nc0=nc0)
    return pl.pallas_call(
        kern,
        grid=(bsz,),
        in_specs=[per_b(hp), per_b(SSD_CONV_DIM), per_b(2 * SSD_HEADS),
                  full(cw), full(cb), full(dtb), full(alog), full(dsk), full(ng)],
        out_specs=per_b(hp),
        out_shape=jax.ShapeDtypeStruct((bsz, nc, hp, q), BF16),
        scratch_shapes=[pltpu.VMEM((nc, SSD_CONV_DIM, q), F32), pltpu.VMEM((nc, hp, q), F32),
                        pltpu.VMEM((hp, SSD_GN), F32), pltpu.VMEM((hp, SSD_GN), F32)],
        compiler_params=_cparams("arbitrary"),
        name="ssd_mixer",
    )(z_t, xbc_t, dt_t, cw, cb, dtb, alog, dsk, ng)


def _route(logits_t, bias_col):
    aff = _sigmoid(logits_t)
    sel = aff + bias_col
    rows = [sel[e:e + 1, :] for e in range(N_EXPERTS)]
    gscore = []
    for g in range(N_GROUPS):
        a, b, c, d = rows[PER_GROUP * g:PER_GROUP * (g + 1)]
        hi1, lo1, hi2, lo2 = jnp.maximum(a, b), jnp.minimum(a, b), jnp.maximum(c, d), jnp.minimum(c, d)
        gscore.append(jnp.maximum(hi1, hi2) + jnp.maximum(jnp.minimum(hi1, hi2), jnp.maximum(lo1, lo2)))
    best = jnp.zeros_like(gscore[0], dtype=jnp.int32)
    cur = gscore[0]
    for g in range(1, N_GROUPS):
        better = gscore[g] > cur
        best = jnp.where(better, g, best)
        cur = jnp.where(better, gscore[g], cur)
    eidx = lax.broadcasted_iota(jnp.int32, sel.shape, 0)
    masked = jnp.where(eidx // PER_GROUP == best, sel, -jnp.inf)
    m1 = jnp.max(masked, axis=0, keepdims=True)
    idx1 = jnp.min(jnp.where(masked == m1, eidx, N_EXPERTS), axis=0, keepdims=True)
    masked2 = jnp.where(eidx == idx1, -jnp.inf, masked)
    m2 = jnp.max(masked2, axis=0, keepdims=True)
    idx2 = jnp.min(jnp.where(masked2 == m2, eidx, N_EXPERTS), axis=0, keepdims=True)
    oh1, oh2 = eidx == idx1, eidx == idx2
    w1 = jnp.sum(jnp.where(oh1, aff, 0.0), axis=0, keepdims=True)
    w2 = jnp.sum(jnp.where(oh2, aff, 0.0), axis=0, keepdims=True)
    den = w1 + w2
    return oh1, oh2, w1 / den, w2 / den


def _split_hi_lo(x):
    hi = x.astype(BF16)
    return hi, (x - hi.astype(F32)).astype(BF16)


def _outproj_kernel(oda_ref, ossdt_ref, omla_ref, h_ref, mod_ref, ng_ref, wo_ref, rw_ref, rb_ref,
                    h1_ref, xs_ref, meta_ref, cnt_ref, *, n_ctx_tiles, ctx_row, d, tm, s_loc, nb, tile0):
    ti, bp = pl.program_id(0) + tile0, pl.program_id(1)
    tiles = range(nb)
    r_i = lax.broadcasted_iota(jnp.int32, (tm, tm), 0)
    c_i = lax.broadcasted_iota(jnp.int32, (tm, tm), 1)
    before = jnp.where(r_i < c_i, 1.0, 0.0).astype(BF16)
    row_e = lax.broadcasted_iota(jnp.int32, (N_EXPERTS, LANES), 0)
    r_s = lax.broadcasted_iota(jnp.int32, (s_loc, tm), 0).astype(F32)
    lane_e = lax.broadcasted_iota(jnp.int32, (s_loc, MOE_EXTRA), 1)
    row_m = lax.broadcasted_iota(jnp.int32, (LANES, tm), 0)
    rows = [jnp.where(ti < n_ctx_tiles, ctx_row, bp * nb + bb) for bb in tiles]
    mod_at = lambda bb, k: mod_ref[pl.ds(rows[bb], 1), pl.ds(k * d, d)]
    ossd = [jnp.concatenate([ossdt_ref[bb, c].astype(F32).T for c in range(tm // SSD_CHUNK)], axis=0).astype(BF16)
            for bb in tiles]
    mix = [_dot(oda_ref[bb], wo_ref[0:DA_WIDTH]) + _dot(ossd[bb], wo_ref[DA_WIDTH:DA_WIDTH + SSD_INNER])
           + _dot(omla_ref[bb], wo_ref[DA_WIDTH + SSD_INNER:]) for bb in tiles]
    h1 = [h_ref[bb] + mod_at(bb, 2) * mix[bb] for bb in tiles]
    for bb in tiles:
        h1_ref[bb] = h1[bb]
    u = [_ada_norm(h1[bb], ng_ref[...], mod_at(bb, 3), mod_at(bb, 4)) for bb in tiles]
    u16 = [x.astype(BF16) for x in u]
    rw_hi, rw_lo = _split_hi_lo(rw_ref[...])
    u_lo = [(u[bb] - u16[bb].astype(F32)).astype(BF16) for bb in tiles]
    logits = [_dot(u16[bb], rw_hi) + _dot(u_lo[bb], rw_hi) + _dot(u16[bb], rw_lo) for bb in tiles]
    routed = [_route(logits[bb].T[:N_EXPERTS], rb_ref[...]) for bb in tiles]
    cnt = [jnp.where(r[0], 1.0, 0.0) + jnp.where(r[1], 1.0, 0.0) for r in routed]
    rank = [_dot(cnt[bb].astype(BF16), before) for bb in tiles]
    for bb in tiles:
        oh1, oh2, w1, w2 = routed[bb]
        tot = jnp.sum(cnt[bb], axis=1, keepdims=True)
        ptot = jnp.floor((tot + (MOE_BLK - 1)) * (1.0 / MOE_BLK)) * MOE_BLK
        ptot_b = jnp.broadcast_to(ptot, (N_EXPERTS, LANES))
        cnt_ref[bb] = ptot_b
        run = jnp.zeros((1, LANES), F32)
        off = jnp.zeros((N_EXPERTS, LANES), F32)
        for e in range(1, N_EXPERTS):
            run = run + ptot_b[e - 1:e]
            off = jnp.where(row_e == e, run, off)
        slot = off[:, 0:1] + rank[bb]
        dest1 = jnp.sum(jnp.where(oh1, slot, 0.0), axis=0, keepdims=True)
        dest2 = jnp.sum(jnp.where(oh2, slot, 0.0), axis=0, keepdims=True)
        routed[bb] = (r_s == dest1, r_s == dest2, w1, w2)
        meta_ref[bb] = jnp.where(row_m == 0, dest1, jnp.where(row_m == 1, dest2, 0.0)).T
    perm = [jnp.where(routed[bb][0] | routed[bb][1], 1.0, 0.0).astype(BF16) for bb in tiles]
    xs = [_dot(perm[bb], u16[bb]) for bb in tiles]
    for bb in tiles:
        p1, p2, w1, w2 = routed[bb]
        xs_ref[bb, :, 0:d] = xs[bb].astype(BF16)
        wslot = jnp.sum(jnp.where(p1, w1, 0.0) + jnp.where(p2, w2, 0.0), axis=1, keepdims=True)
        w_hi = wslot.astype(BF16).astype(F32)
        xs_ref[bb, :, d:] = jnp.where(lane_e == 0, w_hi, jnp.where(lane_e == 1, wslot - w_hi, 0.0)).astype(BF16)


def _outproj_dispatch(oda, ossd_t, omla, h, mod, ng, wo, rw, rb, *, layer, tm, n_ctx, s_loc, nb, tile0):
    bsz, t, d = h.shape
    nt, q = t // tm - tile0, SSD_CHUNK
    tok_in = lambda w: pl.BlockSpec((nb, tm, w), lambda g, b: (b, g + tile0, 0))
    tok = lambda w: pl.BlockSpec((nb, tm, w), lambda g, b: (b, g, 0))
    full = lambda a: pl.BlockSpec(a.shape, lambda g, b: (0,) * a.ndim)
    tile = lambda r, w: pl.BlockSpec((nb, None, r, w), lambda g, b: (b, g, 0, 0))
    lay = lambda a: _layer_spec(a, layer)
    kern = functools.partial(_outproj_kernel, n_ctx_tiles=n_ctx // tm, ctx_row=bsz, d=d, tm=tm, s_loc=s_loc, nb=nb,
                             tile0=tile0)
    return pl.pallas_call(
        kern,
        grid=(nt, bsz // nb),
        in_specs=[tok_in(DA_WIDTH), pl.BlockSpec((nb, tm // q, SSD_INNER, q), lambda g, b: (b, g + tile0, 0, 0)),
                  tok_in(MLA_WIDTH), tok_in(d), lay(mod), lay(ng), lay(wo), full(rw), full(rb)],
        out_specs=[tok(d), tile(s_loc, d + MOE_EXTRA), tile(tm, LANES), tile(N_EXPERTS, LANES)],
        out_shape=[jax.ShapeDtypeStruct((bsz, nt * tm, d), F32),
                   jax.ShapeDtypeStruct((bsz, nt, s_loc, d + MOE_EXTRA), BF16),
                   jax.ShapeDtypeStruct((bsz, nt, tm, LANES), F32),
                   jax.ShapeDtypeStruct((bsz, nt, N_EXPERTS, LANES), F32)],
        compiler_params=_cparams("arbitrary", "arbitrary"),
        name="outproj_dispatch",
    )(oda, ossd_t, omla, h, mod, ng, wo, rw, rb)


def _expert_kernel(se_ref, bi_ref, sv_ref, sn_ref, xs_hbm, wg_ref, wu_ref, wd_ref, y_ref, xbuf, sem, wg16, wu16, wd16,
                   *, d):
    s = pl.program_id(0)
    last = pl.num_programs(0) - 1
    slot = s % 2
    real = sv_ref[s] > 0

    def start_all(step_, slot_):
        for j in range(MOE_STEP_BLKS):
            blk = bi_ref[step_ * MOE_STEP_BLKS + j]
            pltpu.make_async_copy(xs_hbm.at[pl.ds(pl.multiple_of(blk * MOE_BLK, MOE_BLK), MOE_BLK), :],
                                  xbuf.at[slot_, pl.ds(j * MOE_BLK, MOE_BLK), :], sem.at[slot_]).start(priority=j % 2)

    def wait_all(slot_):
        for j in range(MOE_STEP_BLKS):
            pltpu.make_async_copy(xs_hbm.at[pl.ds(0, MOE_BLK), :], xbuf.at[slot_, pl.ds(j * MOE_BLK, MOE_BLK), :],
                                  sem.at[slot_]).wait()

    @pl.when(jnp.logical_and(s == 0, real))
    def _():
        start_all(s, slot)

    nxt = jnp.minimum(s + 1, last)

    @pl.when(jnp.logical_and(s < last, sv_ref[nxt] > 0))
    def _():
        start_all(nxt, 1 - slot)

    @pl.when(jnp.logical_and(real, sn_ref[s] > 0))
    def _():
        wg16[...] = wg_ref[...].astype(BF16)
        wu16[...] = wu_ref[...].astype(BF16)
        wd16[...] = wd_ref[...].astype(BF16)

    @pl.when(real)
    def _():
        wait_all(slot)
        xm = xbuf[slot, :, 0:d]
        wx = xbuf[slot, :, d:]
        wrow = wx[:, 0:1].astype(F32) + wx[:, 1:2].astype(F32)
        he = _silu(_dot(xm, wg16[...])) * _dot(xm, wu16[...])
        y_ref[...] = (_dot(he.astype(BF16), wd16[...]) * wrow).astype(BF16)

    @pl.when(jnp.logical_not(real))
    def _():
        y_ref[...] = jnp.zeros_like(y_ref)


def _expert_ffn(xs2d, wg, wu, wd, step_e, blk_ids, step_valid, step_new, *, layer, n_steps):
    d = wg.shape[2]
    rows = MOE_STEP_BLKS * MOE_BLK
    wspec = lambda a: pl.BlockSpec((None, None) + a.shape[2:], lambda s, se, bi, sv, sn: (layer, se[s], 0, 0))
    grid_spec = pltpu.PrefetchScalarGridSpec(
        num_scalar_prefetch=4,
        grid=(n_steps,),
        in_specs=[pl.BlockSpec(memory_space=pl.ANY), wspec(wg), wspec(wu), wspec(wd)],
        out_specs=pl.BlockSpec((rows, d), lambda s, se, bi, sv, sn: (s, 0)),
        scratch_shapes=[pltpu.VMEM((2, rows, d + MOE_EXTRA), BF16), pltpu.SemaphoreType.DMA((2,)),
                        pltpu.VMEM(wg.shape[2:], BF16), pltpu.VMEM(wu.shape[2:], BF16),
                        pltpu.VMEM(wd.shape[2:], BF16)],
    )
    return pl.pallas_call(
        functools.partial(_expert_kernel, d=d),
        grid_spec=grid_spec,
        out_shape=jax.ShapeDtypeStruct((n_steps * rows, d), BF16),
        compiler_params=_cparams("arbitrary"),
        name="expert_ffn",
    )(step_e, blk_ids, step_valid, step_new, xs2d, wg, wu, wd)


def _moe_combine_rows(inv_ref, ys_hbm, meta_ref, h1_ref, mod_ref, ybuf, sem, *, n_lb, nt, n_ctx_tiles, ctx_row, d, tm,
                      s_loc, nb):
    ti, bp = pl.program_id(0), pl.program_id(1)
    nbp = pl.num_programs(1)
    step = ti * nbp + bp
    last = pl.num_programs(0) * nbp - 1
    slot = step % 2

    def block_copy(ti_, bp_, k, slot_):
        bb, j = divmod(k, n_lb)
        blk = inv_ref[((bp_ * nb + bb) * nt + ti_) * n_lb + j]
        return pltpu.make_async_copy(ys_hbm.at[pl.ds(pl.multiple_of(blk * MOE_BLK, MOE_BLK), MOE_BLK), :],
                                     ybuf.at[slot_, pl.ds(k * MOE_BLK, MOE_BLK), :], sem.at[slot_])

    def start_all(ti_, bp_, slot_):
        for k in range(nb * n_lb):
            block_copy(ti_, bp_, k, slot_).start(priority=k % 2)

    def wait_all(slot_):
        for k in range(nb * n_lb):
            pltpu.make_async_copy(ys_hbm.at[pl.ds(0, MOE_BLK), :], ybuf.at[slot_, pl.ds(k * MOE_BLK, MOE_BLK), :],
                                  sem.at[slot_]).wait()

    @pl.when(step == 0)
    def _():
        start_all(ti, bp, slot)

    wrap = bp + 1 == nbp
    ti_n = jnp.where(step == last, ti, jnp.where(wrap, ti + 1, ti))
    bp_n = jnp.where(step == last, bp, jnp.where(wrap, 0, bp + 1))
    start_all(ti_n, bp_n, 1 - slot)
    wait_all(slot)

    tiles = range(nb)
    lane_s = lax.broadcasted_iota(jnp.int32, (tm, s_loc), 1).astype(F32)
    metas = [meta_ref[bb] for bb in tiles]
    pts = [jnp.where((lane_s == m[:, 0:1]) | (lane_s == m[:, 1:2]), 1.0, 0.0).astype(BF16) for m in metas]
    y = [_dot(pts[bb], ybuf[slot, bb * s_loc:(bb + 1) * s_loc, :]) for bb in tiles]
    h2 = []
    for bb in tiles:
        row = jnp.where(ti < n_ctx_tiles, ctx_row, bp * nb + bb)
        h2.append(h1_ref[bb] + mod_ref[pl.ds(row, 1), pl.ds(5 * d, d)] * y[bb])

    def drain():
        @pl.when(step == last)
        def _():
            wait_all(1 - slot)

    return h2, drain


def _final_combine_kernel(inv_ref, ys_hbm, meta_ref, h1_ref, mod_ref, fg_ref, o_ref, ybuf, sem, *, nb, **geom):
    h2, drain = _moe_combine_rows(inv_ref, ys_hbm, meta_ref, h1_ref, mod_ref, ybuf, sem, nb=nb, **geom)
    for bb in range(nb):
        o_ref[bb] = _rms(h2[bb], fg_ref[...])
    drain()


def _final_combine(ys_em, inv, meta, h1, mod, fg, *, layer, tm, s_loc, nb):
    bsz, t, d = h1.shape
    nt = t // tm
    n_lb = s_loc // MOE_BLK
    full = lambda a: pl.BlockSpec(a.shape, lambda g, b, inv: (0,) * a.ndim)
    grid_spec = pltpu.PrefetchScalarGridSpec(
        num_scalar_prefetch=1,
        grid=(nt, bsz // nb),
        in_specs=[pl.BlockSpec(memory_space=pl.ANY),
                  pl.BlockSpec((nb, None, tm, LANES), lambda g, b, inv: (b, g, 0, 0)),
                  pl.BlockSpec((nb, tm, d), lambda g, b, inv: (b, g, 0)),
                  _layer_spec(mod, layer), full(fg)],
        out_specs=pl.BlockSpec((nb, tm, d), lambda g, b, inv: (b, g, 0)),
        scratch_shapes=[pltpu.VMEM((2, nb * s_loc, d), BF16), pltpu.SemaphoreType.DMA((2,))],
    )
    kern = functools.partial(_final_combine_kernel, n_lb=n_lb, nt=nt, n_ctx_tiles=0, ctx_row=bsz, d=d, tm=tm,
                             s_loc=s_loc, nb=nb)
    return pl.pallas_call(
        kern,
        grid_spec=grid_spec,
        out_shape=jax.ShapeDtypeStruct((bsz, t, d), F32),
        compiler_params=_cparams("arbitrary", "arbitrary"),
        name="moe_combine",
    )(inv, ys_em, meta, h1, mod, fg)


def _rope_tables(length, dim):
    rows = length // GRID_W
    row = jnp.repeat(jnp.arange(rows), GRID_W).astype(F32)
    col = jnp.tile(jnp.arange(GRID_W), rows).astype(F32)
    n_freq = dim // 4
    inv_freq = ROPE_THETA ** (-jnp.arange(n_freq, dtype=F32) / n_freq)
    ang = jnp.concatenate([row[:, None] * inv_freq, col[:, None] * inv_freq], axis=-1)
    return jnp.cos(ang), jnp.sin(ang)


def _table_set(n_ctx, n_lat):
    def lanes(cos, sin, lane0, width, reps, outside_cos):
        c = jnp.concatenate([cos, cos], axis=-1)
        s = jnp.concatenate([-sin, sin], axis=-1)
        grp_c = jnp.full((n_lat, width), outside_cos, F32).at[:, lane0:lane0 + c.shape[1]].set(c)
        grp_s = jnp.zeros((n_lat, width), F32).at[:, lane0:lane0 + s.shape[1]].set(s)
        ctx_c = jnp.full((n_ctx, width), outside_cos, F32).at[:, lane0:lane0 + c.shape[1]].set(1.0)
        ctx_s = jnp.zeros((n_ctx, width), F32)
        return (jnp.tile(jnp.concatenate([ctx_c, grp_c], axis=0), (1, reps)),
                jnp.tile(jnp.concatenate([ctx_s, grp_s], axis=0), (1, reps)))

    dcos, dsin = _rope_tables(n_lat, DA_QK)
    cda, sda = lanes(dcos, dsin, 0, DA_QK, DA_QW // DA_QK, 1.0)
    mcos, msin = _rope_tables(n_lat, MLA_ROPE)
    cq, sm = lanes(mcos, msin, KR_LANE0, LANES, 1, 1.0)
    ck, _ = lanes(mcos, msin, KR_LANE0, LANES, 1, 0.0)
    return cda, sda, cq, ck, sm


def _pack_w_in(w_in):
    depth, d, _ = w_in.shape
    o_mla = DA_IN + SSD_IN
    dt = w_in[..., MAIN_COLS:o_mla]
    cq_ckv = w_in[..., o_mla:o_mla + MLA_Q_RANK + MLA_KV_RANK]
    kr = w_in[..., o_mla + MLA_Q_RANK + MLA_KV_RANK:]
    zeros = lambda n: jnp.zeros((depth, d, n), w_in.dtype)
    tail = jnp.concatenate([cq_ckv, zeros(KR_LANE0), kr, dt, zeros(LANES - DT_LANE0 - 2 * SSD_HEADS)], axis=-1)
    return w_in[..., :MAIN_COLS].astype(BF16), tail.astype(BF16)


def _pack_w_uq(w_uq):
    depth, r, _ = w_uq.shape
    w = w_uq.reshape(depth, r, MLA_HEADS, MLA_NOPE + MLA_ROPE)
    w = jnp.pad(w, ((0, 0), (0, 0), (0, 0), (0, MLA_HEAD_PAD - MLA_NOPE - MLA_ROPE)))
    return w.reshape(depth, r, MLA_QPAD).astype(BF16)


def _pack_w_ukv(w_ukv):
    depth, r, _ = w_ukv.shape
    w = w_ukv.reshape(depth, r, MLA_HEADS, MLA_NOPE + MLA_V)
    kn = jnp.pad(w[..., :MLA_NOPE], ((0, 0), (0, 0), (0, 0), (0, MLA_HEAD_PAD - MLA_NOPE)))
    return jnp.concatenate([kn.reshape(depth, r, MLA_QPAD), w[..., MLA_NOPE:].reshape(depth, r, MLA_WIDTH)],
                           axis=-1).astype(BF16)


def _dispatch_tables(pcnt, n_lb, n_steps):
    ntt = pcnt.shape[0]
    nb = (pcnt / MOE_BLK).astype(jnp.int32)
    lo = jnp.cumsum(nb, axis=1) - nb
    n_e = jnp.sum(nb, axis=0)
    p_e = (n_e + MOE_STEP_BLKS - 1) // MOE_STEP_BLKS * MOE_STEP_BLKS
    ends = jnp.cumsum(p_e)
    base = (ends - p_e)[None, :] + jnp.cumsum(nb, axis=0) - nb
    lb = jnp.arange(n_lb, dtype=jnp.int32)
    owner = (lb[None, :, None] >= lo[:, None, :]) & (lb[None, :, None] < (lo + nb)[:, None, :])
    used = jnp.any(owner, axis=-1)
    pos = jnp.sum(jnp.where(owner, (base - lo)[:, None, :], 0), axis=-1) + lb[None, :]
    inv = jnp.where(used, pos, 0).astype(jnp.int32)
    p = jnp.arange(n_steps * MOE_STEP_BLKS, dtype=jnp.int32)
    e_p = jnp.sum(p[:, None] >= ends[None, :], axis=1)
    oh_e = jnp.minimum(e_p, N_EXPERTS - 1)[:, None] == jnp.arange(N_EXPERTS)[None, :]
    r = p - jnp.sum(jnp.where(oh_e, (ends - p_e)[None, :], 0), axis=1)
    real = (e_p < N_EXPERTS) & (r < jnp.sum(jnp.where(oh_e, n_e[None, :], 0), axis=1))
    cum = jnp.cumsum(nb, axis=0)
    pick = lambda tab: jnp.dot(oh_e.astype(F32), tab.T.astype(F32), precision=lax.Precision.HIGHEST)
    t_p = jnp.minimum(jnp.sum(pick(cum) <= r[:, None].astype(F32), axis=1), ntt - 1)
    first = jnp.arange(ntt, dtype=jnp.int32)[:, None] * n_lb + lo - (cum - nb)
    src = jnp.sum(jnp.where(t_p[:, None] == jnp.arange(ntt)[None, :], pick(first), 0.0), axis=1).astype(jnp.int32) + r
    fwd = jnp.where(real, src, 0)
    starts = jnp.arange(n_steps, dtype=jnp.int32) * MOE_STEP_BLKS
    step_e = jnp.minimum(jnp.sum(starts[:, None] >= ends[None, :], axis=1), N_EXPERTS - 1).astype(jnp.int32)
    step_valid = (starts < ends[-1]).astype(jnp.int32)
    step_new = jnp.concatenate([jnp.ones((1,), jnp.int32), (step_e[1:] != step_e[:-1]).astype(jnp.int32)])
    return inv.reshape(-1), fwd, step_e, step_valid, step_new


def kernel(x, c, ctx, c_ctx, norm_mix_g, norm_ffn_g, w_mod, b_mod, w_in, w_out, da_lambda, da_subln_g, ssd_conv_w, ssd_conv_b, ssd_dt_bias, ssd_a_log, ssd_d, ssd_norm_g, mla_q_norm_g, mla_kv_norm_g, mla_w_uq, mla_w_ukv, router_w, router_bias, exp_w_gate, exp_w_up, exp_w_down, final_norm_g):
    bsz, n_lat, d = x.shape
    n_ctx = ctx.shape[1]
    t = n_ctx + n_lat
    tm = min(256, n_ctx)
    assert n_ctx % tm == 0 and n_lat % tm == 0 and n_ctx % SSD_CHUNK == 0 and n_lat % SSD_CHUNK == 0
    nt = t // tm
    s_loc = 2 * tm + 2 * LANES
    n_lb = s_loc // MOE_BLK
    nc, nc0 = t // SSD_CHUNK, n_ctx // SSD_CHUNK
    nb = 2 if bsz % 2 == 0 else 1
    nb_wide = 4 if bsz % 4 == 0 else nb
    q = SSD_CHUNK

    r_pad = -(-(bsz + 1) // 8) * 8
    c_rows = jnp.concatenate([c, c_ctx[None, :], jnp.zeros((r_pad - bsz - 1, d), F32)], axis=0)
    mod = _modulation(c_rows, w_mod, b_mod)

    w_in_p = _pack_w_in(w_in)
    w_uq_p = _pack_w_uq(mla_w_uq)
    w_ukv_p = _pack_w_ukv(mla_w_ukv)
    w_out16 = w_out.astype(BF16)
    tabs = _table_set(n_ctx, n_lat)
    rw_pad = jnp.pad(router_w, ((0, 0), (0, LANES - N_EXPERTS)))
    rb = router_bias.reshape(N_EXPERTS, 1)
    lane_b = lambda v: jnp.broadcast_to(v[..., None], v.shape + (q,))
    dsk_rows = jnp.repeat(ssd_d, SSD_P, axis=-1)
    ng_mix, ng_ffn = norm_mix_g[:, None, :], norm_ffn_g[:, None, :]
    qg, kvg = mla_q_norm_g[:, None, :], mla_kv_norm_g[:, None, :]
    sub_g = jnp.tile(da_subln_g, (1, DA_HEADS))[:, :, None]
    ssd_par = (lane_b(ssd_conv_w), lane_b(ssd_conv_b), lane_b(ssd_dt_bias.reshape(DEPTH, -1)),
               lane_b(ssd_a_log.reshape(DEPTH, -1)), lane_b(dsk_rows), lane_b(ssd_norm_g))

    h = jnp.concatenate([ctx, x], axis=1)
    moe = None
    for i in range(DEPTH):
        lam_init = 0.8 - 0.6 * math.exp(-0.3 * i)
        if moe is None:
            qda_t, kda, vda_t, z_t, xbc_t, dt_t, q4t, k4, vm_t = _inproj(
                h, mod, ng_mix, w_in_p, tabs, qg, kvg, w_uq_p, w_ukv_p, layer=i, tm=tm, n_ctx=n_ctx, nb=nb_wide)
        else:
            qda_t, kda, vda_t, z_t, xbc_t, dt_t, q4t, k4, vm_t, h = _inproj(
                None, mod, ng_mix, w_in_p, tabs, qg, kvg, w_uq_p, w_ukv_p, layer=i, tm=tm, n_ctx=n_ctx, nb=nb, moe=moe)
        o_da = _da_attention(qda_t, kda, vda_t, da_lambda, sub_g, layer=i, lam_init=lam_init, tq=tm, n_ctx=n_ctx)
        o_mla = _mla_attention(q4t, k4, vm_t, tq=tm, n_ctx=n_ctx)
        o_ssd_t = _ssd_mixer(z_t, xbc_t, dt_t, *ssd_par, layer=i, nc0=nc0)
        tile0 = n_ctx // tm if i == DEPTH - 1 else 0
        ntl = (nt - tile0) * bsz
        n_steps = -(-(ntl * n_lb + N_EXPERTS * (MOE_STEP_BLKS - 1)) // MOE_STEP_BLKS)
        h1, xs, meta, cnt = _outproj_dispatch(o_da, o_ssd_t, o_mla, h, mod, ng_ffn, w_out16, rw_pad, rb, layer=i, tm=tm,
                                              n_ctx=n_ctx, s_loc=s_loc, nb=nb_wide, tile0=tile0)
        inv, fwd, step_e, step_valid, step_new = _dispatch_tables(cnt[:, :, :, 0].reshape(ntl, N_EXPERTS), n_lb, n_steps)
        ys_em = _expert_ffn(xs.reshape(ntl * s_loc, d + MOE_EXTRA), exp_w_gate, exp_w_up, exp_w_down, step_e, fwd,
                            step_valid, step_new, layer=i, n_steps=n_steps)
        moe = (ys_em, inv, meta, h1, s_loc)
    return _final_combine(ys_em, inv, meta, h1, mod, final_norm_g[None, :], layer=DEPTH - 1, tm=tm, s_loc=s_loc, nb=nb)
```

```python
import functools
import math

import jax
import jax.numpy as jnp
from jax import lax
from jax.experimental import pallas as pl
from jax.experimental.pallas import tpu as pltpu

F32 = jnp.float32
BF16 = jnp.bfloat16

DEPTH = 4
GRID_W = 64
EPS = 1e-6
ROPE_THETA = 10000.0
DA_HEADS, DA_QK = 4, 32
DA_V = 2 * DA_QK
DA_WIDTH = DA_HEADS * DA_V
DA_QW = DA_HEADS * 2 * DA_QK
DA_IN = 2 * DA_QW + DA_WIDTH
SSD_HEADS, SSD_P, SSD_GROUPS, SSD_STATE, SSD_CHUNK = 8, 64, 2, 64, 128
SSD_INNER = SSD_HEADS * SSD_P
SSD_GN = SSD_GROUPS * SSD_STATE
SSD_CONV_DIM = SSD_INNER + 2 * SSD_GN
SSD_IN = SSD_INNER + SSD_CONV_DIM + 2 * SSD_HEADS
MLA_HEADS, MLA_Q_RANK, MLA_KV_RANK, MLA_NOPE, MLA_ROPE, MLA_V = 4, 256, 128, 64, 32, 64
MLA_WIDTH = MLA_HEADS * MLA_V
MLA_IN = MLA_Q_RANK + MLA_KV_RANK + MLA_ROPE
MLA_SCALE = (MLA_NOPE + MLA_ROPE) ** -0.5
N_EXPERTS, N_GROUPS, D_EXPERT = 16, 4, 512
PER_GROUP = N_EXPERTS // N_GROUPS

LANES = 128
SUBLANES_BF16 = 16
VMEM_LIMIT_BYTES = 56 * 1024 * 1024

MLA_HEAD_PAD = LANES
MLA_QPAD = MLA_HEADS * MLA_HEAD_PAD
KR_LANE0 = MLA_NOPE
DT_LANE0 = MLA_NOPE + MLA_ROPE
MAIN_COLS = DA_IN + SSD_INNER + SSD_CONV_DIM
C_Z = DA_IN
C_XBC = C_Z + SSD_INNER
TAIL_COLS = MLA_Q_RANK + MLA_KV_RANK + LANES
T_CKV = MLA_Q_RANK
T_MISC = T_CKV + MLA_KV_RANK
LOG2E = math.log2(math.e)
DA_QSCALE = DA_QK ** -0.5 * LOG2E
MLA_QSCALE = MLA_SCALE * LOG2E

KEY_CHUNK = 256
MOE_BLK = SUBLANES_BF16
MOE_STEP_BLKS = 32
MOE_EXTRA = LANES


def _sigmoid(x):
    return 1.0 / (1.0 + jnp.exp(-x))


def _silu(x):
    return x * _sigmoid(x)


def _rms(x, g, axis=-1):
    return x * lax.rsqrt(jnp.mean(x * x, axis=axis, keepdims=True) + EPS) * g


def _ada_norm(hv, g, shift, scale):
    return _rms(hv, g) * (1.0 + scale) + shift


def _rope(x, cos, sin_signed, half):
    w = x.shape[-1]
    lane = lax.broadcasted_iota(jnp.int32, x.shape, x.ndim - 1)
    first = (lane % (2 * half)) < half
    partner = jnp.where(first, pltpu.roll(x, w - half, x.ndim - 1), pltpu.roll(x, half, x.ndim - 1))
    return x * cos + partner * sin_signed


def _dot(a, b):
    return jnp.dot(a, b, preferred_element_type=F32)


def _dot_hi(a, b):
    return jnp.dot(a, b, preferred_element_type=F32, precision=lax.Precision.HIGHEST)


def _dot_split3(x, m01):
    x1 = x.astype(BF16)
    r1 = x - x1.astype(F32)
    x2 = r1.astype(BF16)
    x3 = (r1 - x2.astype(F32)).astype(BF16)
    return _dot(x1, m01) + _dot(x2, m01) + _dot(x3, m01)


def _layer_spec(a, i):
    return pl.BlockSpec((None,) + a.shape[1:], lambda *_: (i,) + (0,) * (a.ndim - 1))


def _cparams(*sem):
    return pltpu.CompilerParams(dimension_semantics=sem, vmem_limit_bytes=VMEM_LIMIT_BYTES)


def _mod_kernel(c_ref, w_ref, b_ref, o_ref):
    o_ref[...] = _dot_hi(_silu(c_ref[...]), w_ref[...]) + b_ref[...]


def _modulation(c_rows, w_mod, b_mod):
    depth, d, n = w_mod.shape
    r = c_rows.shape[0]
    tn = 1536
    return pl.pallas_call(
        _mod_kernel,
        grid=(depth, n // tn),
        in_specs=[pl.BlockSpec((r, d), lambda l, j: (0, 0)),
                  pl.BlockSpec((None, d, tn), lambda l, j: (l, 0, j)),
                  pl.BlockSpec((None, 1, tn), lambda l, j: (l, 0, j))],
        out_specs=pl.BlockSpec((None, r, tn), lambda l, j: (l, 0, j)),
        out_shape=jax.ShapeDtypeStruct((depth, r, n), F32),
        compiler_params=_cparams("arbitrary", "arbitrary"),
        name="modulation",
    )(c_rows, w_mod, b_mod.reshape(depth, 1, n))


def _inproj_kernel(*refs, n_ctx_tiles, ctx_row, d, nb, tm, moe_geom):
    if moe_geom is None:
        h_ref, refs = refs[0], refs[1:]
    else:
        (inv_ref, ys_hbm, meta_ref, h1_ref, modp_ref), refs = refs[:5], refs[5:]
        h2_ref, ybuf, sem = refs[-3:]
        refs = refs[:-3]
    (mod_ref, ng_ref, wa_ref, wb_ref, cda_ref, sda_ref, cq_ref, ck_ref, sm_ref, qg_ref, kvg_ref, wuq_ref, wukv_ref,
     qdat_ref, kda_ref, vdat_ref, zt_ref, xbct_ref, dtt_ref, q4t_ref, k4_ref, vmt_ref) = refs
    ti, bp = pl.program_id(0), pl.program_id(1)
    if moe_geom is None:
        hs = [h_ref[bb] for bb in range(nb)]
    else:
        hs, drain = _moe_combine_rows(inv_ref, ys_hbm, meta_ref, h1_ref, modp_ref, ybuf, sem, nb=nb, tm=tm, d=d,
                                      n_ctx_tiles=n_ctx_tiles, ctx_row=ctx_row, **moe_geom)
        for bb in range(nb):
            h2_ref[bb] = hs[bb]
    cda, sda, sm = cda_ref[...], sda_ref[...], sm_ref[...]
    cos_q = jnp.concatenate([cq_ref[...]] * MLA_HEADS, axis=1)
    sin_q = jnp.concatenate([sm] * MLA_HEADS, axis=1)
    q = SSD_CHUNK
    accs = []
    for bb in range(nb):
        row = jnp.where(ti < n_ctx_tiles, ctx_row, bp * nb + bb)
        shift = mod_ref[pl.ds(row, 1), pl.ds(0, d)]
        scale = mod_ref[pl.ds(row, 1), pl.ds(d, d)]
        u = _ada_norm(hs[bb], ng_ref[...], shift, scale).astype(BF16)
        accs.append((_dot(u, wa_ref[...]), _dot(u, wb_ref[...])))
    for bb in range(nb):
        acc, tail = accs[bb]
        qdat_ref[bb] = (_rope(acc[:, 0:DA_QW], cda, sda, DA_QK // 2) * DA_QSCALE).T.astype(BF16)
        kda_ref[bb] = _rope(acc[:, DA_QW:2 * DA_QW], cda, sda, DA_QK // 2).astype(BF16)
        vdat_ref[bb] = acc[:, 2 * DA_QW:DA_IN].T.astype(BF16)
        z_t = acc[:, C_Z:C_XBC].T
        xbc_t = acc[:, C_XBC:MAIN_COLS].T
        misc = tail[:, T_MISC:TAIL_COLS]
        dt_t = misc.T[DT_LANE0:DT_LANE0 + 2 * SSD_HEADS]
        for c in range(tm // q):
            zt_ref[bb, c] = z_t[:, c * q:(c + 1) * q].astype(BF16)
            xbct_ref[bb, c] = xbc_t[:, c * q:(c + 1) * q]
            dtt_ref[bb, c] = dt_t[:, c * q:(c + 1) * q]
        cqn = _rms(tail[:, 0:T_CKV], qg_ref[...]).astype(BF16)
        qm_t = (_rope(_dot(cqn, wuq_ref[...]), cos_q, sin_q, MLA_ROPE // 2) * MLA_QSCALE).T
        ckvn = _rms(tail[:, T_CKV:T_MISC], kvg_ref[...]).astype(BF16)
        kv = _dot(ckvn, wukv_ref[...])
        kr = _rope(misc, ck_ref[...], sm, MLA_ROPE // 2)
        km = kv[:, :MLA_QPAD] + jnp.concatenate([kr] * MLA_HEADS, axis=1)
        for hh in range(MLA_HEADS):
            q4t_ref[bb, hh] = qm_t[hh * MLA_HEAD_PAD:(hh + 1) * MLA_HEAD_PAD].astype(BF16)
            k4_ref[bb, hh] = km[:, hh * MLA_HEAD_PAD:(hh + 1) * MLA_HEAD_PAD].astype(BF16)
        vmt_ref[bb] = kv[:, MLA_QPAD:].T.astype(BF16)
    if moe_geom is not None:
        drain()


def _inproj(h, mod, ng, w_in_p, tabs, qg, kvg, wuq_p, wukv_p, *, layer, tm, n_ctx, nb, moe=None):
    bsz, t, d = (h if moe is None else moe[3]).shape
    nt, q = t // tm, SSD_CHUNK
    cpt = tm // q
    tok = lambda w: pl.BlockSpec((nb, tm, w), lambda ti, b, *_: (b, ti, 0))
    tab = lambda w: pl.BlockSpec((tm, w), lambda ti, b, *_: (ti, 0))
    chunked = lambda f: pl.BlockSpec((nb, cpt, f, q), lambda ti, b, *_: (b, ti, 0, 0))
    lay = lambda a: _layer_spec(a, layer)
    cda, sda, cq, ck, sm = tabs
    sds = jax.ShapeDtypeStruct
    tok_t = lambda w: pl.BlockSpec((nb, w, tm), lambda ti, b, *_: (b, 0, ti))
    out_specs = [tok_t(DA_QW), tok(DA_QW), tok_t(DA_WIDTH),
                 chunked(SSD_INNER), chunked(SSD_CONV_DIM), chunked(2 * SSD_HEADS),
                 pl.BlockSpec((nb, MLA_HEADS, MLA_HEAD_PAD, tm), lambda ti, b, *_: (b, 0, 0, ti)),
                 pl.BlockSpec((nb, MLA_HEADS, tm, MLA_HEAD_PAD), lambda ti, b, *_: (b, 0, ti, 0)), tok_t(MLA_WIDTH)]
    out_shape = [sds((bsz, DA_QW, t), BF16), sds((bsz, t, DA_QW), BF16), sds((bsz, DA_WIDTH, t), BF16),
                 sds((bsz, t // q, SSD_INNER, q), BF16), sds((bsz, t // q, SSD_CONV_DIM, q), F32),
                 sds((bsz, t // q, 2 * SSD_HEADS, q), F32),
                 sds((bsz, MLA_HEADS, MLA_HEAD_PAD, t), BF16), sds((bsz, MLA_HEADS, t, MLA_HEAD_PAD), BF16),
                 sds((bsz, MLA_WIDTH, t), BF16)]
    w_main, w_tail = w_in_p
    common_specs = [lay(mod), lay(ng), lay(w_main), lay(w_tail), tab(DA_QW), tab(DA_QW), tab(LANES), tab(LANES),
                    tab(LANES), lay(qg), lay(kvg), lay(wuq_p), lay(wukv_p)]
    common_args = (mod, ng, w_main, w_tail, cda, sda, cq, ck, sm, qg, kvg, wuq_p, wukv_p)
    geom = dict(n_ctx_tiles=n_ctx // tm, ctx_row=bsz, d=d, nb=nb, tm=tm)
    if moe is None:
        return pl.pallas_call(
            functools.partial(_inproj_kernel, moe_geom=None, **geom),
            grid=(nt, bsz // nb),
            in_specs=[tok(d)] + common_specs,
            out_specs=out_specs,
            out_shape=out_shape,
            compiler_params=_cparams("arbitrary", "arbitrary"),
            name="inproj",
        )(h, *common_args)
    ys_em, inv, meta, h1, s_loc = moe
    grid_spec = pltpu.PrefetchScalarGridSpec(
        num_scalar_prefetch=1,
        grid=(nt, bsz // nb),
        in_specs=[pl.BlockSpec(memory_space=pl.ANY),
                  pl.BlockSpec((nb, None, tm, LANES), lambda ti, b, inv: (b, ti, 0, 0)), tok(d),
                  _layer_spec(mod, layer - 1)] + common_specs,
        out_specs=out_specs + [tok(d)],
        scratch_shapes=[pltpu.VMEM((2, nb * s_loc, d), BF16), pltpu.SemaphoreType.DMA((2,))],
    )
    moe_geom = dict(n_lb=s_loc // MOE_BLK, nt=nt, s_loc=s_loc)
    return pl.pallas_call(
        functools.partial(_inproj_kernel, moe_geom=moe_geom, **geom),
        grid_spec=grid_spec,
        out_shape=out_shape + [sds((bsz, t, d), F32)],
        compiler_params=_cparams("arbitrary", "arbitrary"),
        name="combine_inproj",
    )(inv, ys_em, meta, h1, mod, *common_args)


V_AUG = DA_V + SUBLANES_BF16


def _scores_pass(k_at, qtm, nk, s_scr):
    kc = KEY_CHUNK if nk % KEY_CHUNK == 0 else LANES
    m = None
    for c0 in range(0, nk, kc):
        s_c = _dot(k_at(c0, c0 + kc), qtm)
        s_scr[c0:c0 + kc, :] = s_c
        part = jnp.max(s_c, axis=0, keepdims=True)
        m = part if m is None else jnp.maximum(m, part)
    return m


def _pv_pass(vaug_at, nk, s_scr, m):
    kc = KEY_CHUNK if nk % KEY_CHUNK == 0 else LANES
    acc = None
    for c0 in range(0, nk, kc):
        e = jnp.exp2(s_scr[c0:c0 + kc, :] - m).astype(BF16)
        inc = _dot(vaug_at(c0, c0 + kc), e)
        acc = inc if acc is None else acc + inc
    return acc


def _attend_heads(k_of, qt_of, vaug_of, nk, scr, n_sub):
    m = _scores_pass(k_of(0), qt_of(0), nk, scr[0])
    outs = []
    for j in range(n_sub):
        if j + 1 < n_sub:
            m_next = _scores_pass(k_of(j + 1), qt_of(j + 1), nk, scr[(j + 1) % 2])
        acc = _pv_pass(vaug_of(j), nk, scr[j % 2], m)
        outs.append(acc[:DA_V] * (1.0 / acc[DA_V:DA_V + 1]))
        if j + 1 < n_sub:
            m = m_next
    return outs


def _fill_vaug(vt_ref, vaug_scr, n_heads):
    t = vt_ref.shape[1]
    for hh in range(n_heads):
        vaug_scr[hh, 0:DA_V, :] = vt_ref[hh * DA_V:(hh + 1) * DA_V, :]
        vaug_scr[hh, DA_V:V_AUG, :] = jnp.ones((V_AUG - DA_V, t), BF16)


def _da_attn_kernel(lam_ref, g_ref, qt_ref, k_ref, vt_ref, o_ref, s0_scr, s1_scr, vaug_scr, *, n_ctx, n_ctx_tiles,
                    lam_init, tq):
    qi = pl.program_id(1)

    @pl.when(qi == 0)
    def _():
        _fill_vaug(vt_ref, vaug_scr, DA_HEADS)

    lv = lam_ref[...]
    lam = (jnp.exp(jnp.sum(lv[0:1] * lv[1:2], axis=-1, keepdims=True))
           - jnp.exp(jnp.sum(lv[2:3] * lv[3:4], axis=-1, keepdims=True)) + lam_init)
    row_q = lax.broadcasted_iota(jnp.int32, (DA_QW, 1), 0)

    def attend(nk):
        qt = qt_ref[...]
        qt_of = lambda j: qt * jnp.where(row_q // DA_QK == j, 1.0, 0.0).astype(BF16)
        k_of = lambda j: (lambda c0, c1: k_ref[c0:c1, :])
        vaug_of = lambda j: (lambda c0, c1: vaug_scr[j // 2, :, c0:c1])
        outs = _attend_heads(k_of, qt_of, vaug_of, nk, (s0_scr, s1_scr), 2 * DA_HEADS)
        heads = []
        for hh in range(DA_HEADS):
            o = outs[2 * hh] - lam * outs[2 * hh + 1]
            heads.append(o * lax.rsqrt(jnp.mean(o * o, axis=0, keepdims=True) + EPS))
        o_t = jnp.concatenate(heads, axis=0) * g_ref[...] * (1.0 - lam_init)
        o_ref[...] = o_t.T.astype(BF16)

    @pl.when(qi < n_ctx_tiles)
    def _():
        attend(n_ctx)

    @pl.when(qi >= n_ctx_tiles)
    def _():
        attend(k_ref.shape[0])


def _da_attention(qt, k, vt, lam_vec, g_col, *, layer, lam_init, tq, n_ctx):
    bsz, _, t = qt.shape
    kern = functools.partial(_da_attn_kernel, n_ctx=n_ctx, n_ctx_tiles=n_ctx // tq, lam_init=lam_init, tq=tq)
    return pl.pallas_call(
        kern,
        grid=(bsz, t // tq),
        in_specs=[_layer_spec(lam_vec, layer), _layer_spec(g_col, layer),
                  pl.BlockSpec((None, DA_QW, tq), lambda b, i: (b, 0, i)),
                  pl.BlockSpec((None, t, DA_QW), lambda b, i: (b, 0, 0)),
                  pl.BlockSpec((None, DA_WIDTH, t), lambda b, i: (b, 0, 0))],
        out_specs=pl.BlockSpec((None, tq, DA_WIDTH), lambda b, i: (b, i, 0)),
        out_shape=jax.ShapeDtypeStruct((bsz, t, DA_WIDTH), BF16),
        scratch_shapes=[pltpu.VMEM((t, tq), F32), pltpu.VMEM((t, tq), F32), pltpu.VMEM((DA_HEADS, V_AUG, t), BF16)],
        compiler_params=_cparams("arbitrary", "arbitrary"),
        name="da_attention",
    )(lam_vec, g_col, qt, k, vt)


def _mla_attn_kernel(qt_ref, k_ref, vt_ref, o_ref, s0_scr, s1_scr, vaug_scr, *, n_ctx, n_ctx_tiles, tq):
    qi = pl.program_id(1)

    @pl.when(qi == 0)
    def _():
        _fill_vaug(vt_ref, vaug_scr, MLA_HEADS)

    def attend(nk):
        k_of = lambda hh: (lambda c0, c1: k_ref[hh, c0:c1, :])
        vaug_of = lambda hh: (lambda c0, c1: vaug_scr[hh, :, c0:c1])
        outs = _attend_heads(k_of, lambda hh: qt_ref[hh], vaug_of, nk, (s0_scr, s1_scr), MLA_HEADS)
        o_ref[...] = jnp.concatenate(outs, axis=0).T.astype(BF16)

    @pl.when(qi < n_ctx_tiles)
    def _():
        attend(n_ctx)

    @pl.when(qi >= n_ctx_tiles)
    def _():
        attend(k_ref.shape[1])


def _mla_attention(q4t, k4, vt, *, tq, n_ctx):
    bsz, _, _, t = q4t.shape
    kern = functools.partial(_mla_attn_kernel, n_ctx=n_ctx, n_ctx_tiles=n_ctx // tq, tq=tq)
    return pl.pallas_call(
        kern,
        grid=(bsz, t // tq),
        in_specs=[pl.BlockSpec((None, MLA_HEADS, MLA_HEAD_PAD, tq), lambda b, i: (b, 0, 0, i)),
                  pl.BlockSpec((None, MLA_HEADS, t, MLA_HEAD_PAD), lambda b, i: (b, 0, 0, 0)),
                  pl.BlockSpec((None, MLA_WIDTH, t), lambda b, i: (b, 0, 0))],
        out_specs=pl.BlockSpec((None, tq, MLA_WIDTH), lambda b, i: (b, i, 0)),
        out_shape=jax.ShapeDtypeStruct((bsz, t, MLA_WIDTH), BF16),
        scratch_shapes=[pltpu.VMEM((t, tq), F32), pltpu.VMEM((t, tq), F32), pltpu.VMEM((MLA_HEADS, V_AUG, t), BF16)],
        compiler_params=_cparams("arbitrary", "arbitrary"),
        name="mla_attention",
    )(q4t, k4, vt)


def _ssd_kernel(z_ref, xbc_ref, dt_ref, cw_ref, cb_ref, dtb_ref, alog_ref, dsk_ref, ng_ref,
                o_ref, xc_scr, y_scr, hf_scr, hb_scr, *, nc, nc0):
    q = SSD_CHUNK
    hp = SSD_HEADS * SSD_P
    lane_x = lax.broadcasted_iota(jnp.int32, (SSD_CONV_DIM, q), 1)

    lanes = 2 if nc % 2 == 0 else 1

    def conv_body(s, carry):
        for c in [s * lanes + u for u in range(lanes)]:
            xc = xbc_ref[c]
            keep_prev = jnp.where((c == 0) | (c == nc0), 0.0, 1.0)
            keep_next = jnp.where((c == nc0 - 1) | (c == nc - 1), 0.0, 1.0)
            xp = xbc_ref[jnp.maximum(c - 1, 0)] * keep_prev
            xn = xbc_ref[jnp.minimum(c + 1, nc - 1)] * keep_next
            prev = pltpu.roll(jnp.where(lane_x == q - 1, xp, xc), 1, 1)
            nxt = pltpu.roll(jnp.where(lane_x == 0, xn, xc), q - 1, 1)
            a = _silu(cw_ref[0] * prev + cw_ref[1] * xc + cw_ref[2] * nxt + cb_ref[...])
            xc_scr[c] = a
            y_scr[c] = dsk_ref[...] * a[:hp]
        return carry

    lax.fori_loop(0, nc // lanes, conv_body, 0)
    hf_scr[...] = jnp.zeros_like(hf_scr)
    hb_scr[...] = jnp.zeros_like(hb_scr)

    sub = lax.broadcasted_iota(jnp.int32, (q, q), 0)
    lan = lax.broadcasted_iota(jnp.int32, (q, q), 1)
    lane_n = lax.broadcasted_iota(jnp.int32, (1, SSD_GN), 1)
    pad_rows = jnp.zeros((q - SSD_HEADS, q), F32)

    per_g = SSD_HEADS // SSD_GROUPS
    rows_g = hp // SSD_GROUPS
    gmask = [lane_n // SSD_STATE == g for g in range(SSD_GROUPS)]
    tris = (sub <= lan, sub >= lan)
    tri01 = [jnp.where(tr, 1.0, 0.0).astype(BF16) for tr in tris]
    ends = (q - 1, 0)
    h_scrs = (hf_scr, hb_scr)
    dirs = (0, 1)

    def prep(c, d):
        xc = xc_scr[c]
        hs = slice(SSD_HEADS * d, SSD_HEADS * (d + 1))
        dtl = dt_ref[c][hs] + dtb_ref[hs]
        dt = jnp.maximum(dtl, 0.0) + jnp.log(1.0 + jnp.exp(-jnp.abs(dtl)))
        dta = dt * (-jnp.exp(alog_ref[hs]))
        cum_pad = _dot_split3(jnp.concatenate([dta, pad_rows], axis=0), tri01[d])
        btok = xc[hp:hp + SSD_GN].T
        ct16 = xc[hp + SSD_GN:].astype(BF16)
        gts = [_dot(jnp.where(gmask[g], btok, 0.0).astype(BF16), ct16) for g in range(SSD_GROUPS)]
        return dict(c=c, xt=xc[:hp], dt=dt, cum_pad=cum_pad, btok16=btok.astype(BF16), ct16=ct16, gts=gts)

    def head_work(p, d):
        cum_row = p["cum_pad"][:SSD_HEADS]
        cum_col = p["cum_pad"].T
        ydiag, xws, decs, ear = [], [], [], []
        for hh in range(SSD_HEADS):
            ar = cum_row[hh:hh + 1, :]
            ac = cum_col[:, hh:hh + 1]
            sct = (p["gts"][hh // per_g] * jnp.where(tris[d], jnp.exp(ar - ac), 0.0)).astype(BF16)
            xdt = p["xt"][hh * SSD_P:(hh + 1) * SSD_P] * p["dt"][hh:hh + 1, :]
            ydiag.append(_dot(xdt.astype(BF16), sct))
            a_end = ar[:, ends[d]:ends[d] + 1]
            xws.append(xdt * jnp.exp(a_end - ar))
            decs.append(jnp.exp(a_end))
            ear.append(jnp.exp(ar))
        return ydiag, xws, decs, ear

    def state_incs(p, xws):
        return [jnp.where(gmask[g], _dot(jnp.concatenate(xws[g * per_g:(g + 1) * per_g], axis=0).astype(BF16),
                                         p["btok16"]), 0.0) for g in range(SSD_GROUPS)]

    def carried_out(p, hm):
        return [_dot(hm[g * rows_g:(g + 1) * rows_g].astype(BF16), p["ct16"]) for g in range(SSD_GROUPS)]

    def step(s, carry):
        pos = [s * lanes + u for u in range(lanes)]
        cs = [(p_, jnp.where(p_ < nc0, nc0 - 1 - p_, nc - 1 - (p_ - nc0))) for p_ in pos]
        pre = [[prep(cs[u][d], d) for d in dirs] for u in range(lanes)]
        hm = [h_scrs[d][...] for d in dirs]
        yo = [carried_out(pre[0][d], hm[d]) for d in dirs]
        hw = [head_work(pre[0][d], d) for d in dirs]
        for u in range(lanes):
            incs = [state_incs(pre[u][d], hw[d][1]) for d in dirs]
            hw_next = [head_work(pre[u + 1][d], d) for d in dirs] if u + 1 < lanes else None
            for d in dirs:
                ydiag, _, decs, ear = hw[d]
                y = [ydiag[hh] + yo[d][hh // per_g][(hh % per_g) * SSD_P:(hh % per_g + 1) * SSD_P] * ear[hh]
                     for hh in range(SSD_HEADS)]
                y_scr[pre[u][d]["c"]] += jnp.concatenate(y, axis=0)
                hm[d] = jnp.concatenate(
                    [hm[d][hh * SSD_P:(hh + 1) * SSD_P] * decs[hh]
                     + incs[d][hh // per_g][(hh % per_g) * SSD_P:(hh % per_g + 1) * SSD_P]
                     for hh in range(SSD_HEADS)], axis=0)
            if u + 1 < lanes:
                yo = [carried_out(pre[u + 1][d], hm[d]) for d in dirs]
                hw = hw_next
        for d in dirs:
            h_scrs[d][...] = hm[d]
        return carry

    lax.fori_loop(0, nc // lanes, step, 0)

    def fin_body(s, carry):
        for c in [s * lanes + u for u in range(lanes)]:
            gated = y_scr[c] * _silu(z_ref[c].astype(F32))
            o_ref[c] = _rms(gated, ng_ref[...], axis=0).astype(BF16)
        return carry

    lax.fori_loop(0, nc // lanes, fin_body, 0)


def _ssd_mixer(z_t, xbc_t, dt_t, cw, cb, dtb, alog, dsk, ng, *, layer, nc0):
    bsz, nc, _, q = xbc_t.shape
    hp = SSD_HEADS * SSD_P
    per_b = lambda f: pl.BlockSpec((None, nc, f, q), lambda b: (b, 0, 0, 0))
    full = lambda a: _layer_spec(a, layer)
    kern = functools.partial(_ssd_kernel, nc=nc, nc0=nc0)
    return pl.pallas_call(
        kern,
        grid=(bsz,),
        in_specs=[per_b(hp), per_b(SSD_CONV_DIM), per_b(2 * SSD_HEADS),
                  full(cw), full(cb), full(dtb), full(alog), full(dsk), full(ng)],
        out_specs=per_b(hp),
        out_shape=jax.ShapeDtypeStruct((bsz, nc, hp, q), BF16),
        scratch_shapes=[pltpu.VMEM((nc, SSD_CONV_DIM, q), F32), pltpu.VMEM((nc, hp, q), F32),
                        pltpu.VMEM((hp, SSD_GN), F32), pltpu.VMEM((hp, SSD_GN), F32)],
        compiler_params=_cparams("arbitrary"),
        name="ssd_mixer",
    )(z_t, xbc_t, dt_t, cw, cb, dtb, alog, dsk, ng)


def _route(logits_t, bias_col):
    aff = _sigmoid(logits_t)
    sel = aff + bias_col
    rows = [sel[e:e + 1, :] for e in range(N_EXPERTS)]
    gscore = []
    for g in range(N_GROUPS):
        a, b, c, d = rows[PER_GROUP * g:PER_GROUP * (g + 1)]
        hi1, lo1, hi2, lo2 = jnp.maximum(a, b), jnp.minimum(a, b), jnp.maximum(c, d), jnp.minimum(c, d)
        gscore.append(jnp.maximum(hi1, hi2) + jnp.maximum(jnp.minimum(hi1, hi2), jnp.maximum(lo1, lo2)))
    best = jnp.zeros_like(gscore[0], dtype=jnp.int32)
    cur = gscore[0]
    for g in range(1, N_GROUPS):
        better = gscore[g] > cur
        best = jnp.where(better, g, best)
        cur = jnp.where(better, gscore[g], cur)
    eidx = lax.broadcasted_iota(jnp.int32, sel.shape, 0)
    masked = jnp.where(eidx // PER_GROUP == best, sel, -jnp.inf)
    m1 = jnp.max(masked, axis=0, keepdims=True)
    idx1 = jnp.min(jnp.where(masked == m1, eidx, N_EXPERTS), axis=0, keepdims=True)
    masked2 = jnp.where(eidx == idx1, -jnp.inf, masked)
    m2 = jnp.max(masked2, axis=0, keepdims=True)
    idx2 = jnp.min(jnp.where(masked2 == m2, eidx, N_EXPERTS), axis=0, keepdims=True)
    oh1, oh2 = eidx == idx1, eidx == idx2
    w1 = jnp.sum(jnp.where(oh1, aff, 0.0), axis=0, keepdims=True)
    w2 = jnp.sum(jnp.where(oh2, aff, 0.0), axis=0, keepdims=True)
    den = w1 + w2
    return oh1, oh2, w1 / den, w2 / den


def _split_hi_lo(x):
    hi = x.astype(BF16)
    return hi, (x - hi.astype(F32)).astype(BF16)


def _outproj_kernel(oda_ref, ossdt_ref, omla_ref, h_ref, mod_ref, ng_ref, wo_ref, rw_ref, rb_ref,
                    h1_ref, xs_ref, meta_ref, cnt_ref, *, n_ctx_tiles, ctx_row, d, tm, s_loc, nb, tile0):
    ti, bp = pl.program_id(0) + tile0, pl.program_id(1)
    tiles = range(nb)
    r_i = lax.broadcasted_iota(jnp.int32, (tm, tm), 0)
    c_i = lax.broadcasted_iota(jnp.int32, (tm, tm), 1)
    before = jnp.where(r_i < c_i, 1.0, 0.0).astype(BF16)
    row_e = lax.broadcasted_iota(jnp.int32, (N_EXPERTS, LANES), 0)
    r_s = lax.broadcasted_iota(jnp.int32, (s_loc, tm), 0).astype(F32)
    lane_e = lax.broadcasted_iota(jnp.int32, (s_loc, MOE_EXTRA), 1)
    row_m = lax.broadcasted_iota(jnp.int32, (LANES, tm), 0)
    rows = [jnp.where(ti < n_ctx_tiles, ctx_row, bp * nb + bb) for bb in tiles]
    mod_at = lambda bb, k: mod_ref[pl.ds(rows[bb], 1), pl.ds(k * d, d)]
    ossd = [jnp.concatenate([ossdt_ref[bb, c].astype(F32).T for c in range(tm // SSD_CHUNK)], axis=0).astype(BF16)
            for bb in tiles]
    mix = [_dot(oda_ref[bb], wo_ref[0:DA_WIDTH]) + _dot(ossd[bb], wo_ref[DA_WIDTH:DA_WIDTH + SSD_INNER])
           + _dot(omla_ref[bb], wo_ref[DA_WIDTH + SSD_INNER:]) for bb in tiles]
    h1 = [h_ref[bb] + mod_at(bb, 2) * mix[bb] for bb in tiles]
    for bb in tiles:
        h1_ref[bb] = h1[bb]
    u = [_ada_norm(h1[bb], ng_ref[...], mod_at(bb, 3), mod_at(bb, 4)) for bb in tiles]
    u16 = [x.astype(BF16) for x in u]
    rw_hi, rw_lo = _split_hi_lo(rw_ref[...])
    u_lo = [(u[bb] - u16[bb].astype(F32)).astype(BF16) for bb in tiles]
    logits = [_dot(u16[bb], rw_hi) + _dot(u_lo[bb], rw_hi) + _dot(u16[bb], rw_lo) for bb in tiles]
    routed = [_route(logits[bb].T[:N_EXPERTS], rb_ref[...]) for bb in tiles]
    cnt = [jnp.where(r[0], 1.0, 0.0) + jnp.where(r[1], 1.0, 0.0) for r in routed]
    rank = [_dot(cnt[bb].astype(BF16), before) for bb in tiles]
    for bb in tiles:
        oh1, oh2, w1, w2 = routed[bb]
        tot = jnp.sum(cnt[bb], axis=1, keepdims=True)
        ptot = jnp.floor((tot + (MOE_BLK - 1)) * (1.0 / MOE_BLK)) * MOE_BLK
        ptot_b = jnp.broadcast_to(ptot, (N_EXPERTS, LANES))
        cnt_ref[bb] = ptot_b
        run = jnp.zeros((1, LANES), F32)
        off = jnp.zeros((N_EXPERTS, LANES), F32)
        for e in range(1, N_EXPERTS):
            run = run + ptot_b[e - 1:e]
            off = jnp.where(row_e == e, run, off)
        slot = off[:, 0:1] + rank[bb]
        dest1 = jnp.sum(jnp.where(oh1, slot, 0.0), axis=0, keepdims=True)
        dest2 = jnp.sum(jnp.where(oh2, slot, 0.0), axis=0, keepdims=True)
        routed[bb] = (r_s == dest1, r_s == dest2, w1, w2)
        meta_ref[bb] = jnp.where(row_m == 0, dest1, jnp.where(row_m == 1, dest2, 0.0)).T
    perm = [jnp.where(routed[bb][0] | routed[bb][1], 1.0, 0.0).astype(BF16) for bb in tiles]
    xs = [_dot(perm[bb], u16[bb]) for bb in tiles]
    for bb in tiles:
        p1, p2, w1, w2 = routed[bb]
        xs_ref[bb, :, 0:d] = xs[bb].astype(BF16)
        wslot = jnp.sum(jnp.where(p1, w1, 0.0) + jnp.where(p2, w2, 0.0), axis=1, keepdims=True)
        w_hi = wslot.astype(BF16).astype(F32)
        xs_ref[bb, :, d:] = jnp.where(lane_e == 0, w_hi, jnp.where(lane_e == 1, wslot - w_hi, 0.0)).astype(BF16)


def _outproj_dispatch(oda, ossd_t, omla, h, mod, ng, wo, rw, rb, *, layer, tm, n_ctx, s_loc, nb, tile0):
    bsz, t, d = h.shape
    nt, q = t // tm - tile0, SSD_CHUNK
    tok_in = lambda w: pl.BlockSpec((nb, tm, w), lambda g, b: (b, g + tile0, 0))
    tok = lambda w: pl.BlockSpec((nb, tm, w), lambda g, b: (b, g, 0))
    full = lambda a: pl.BlockSpec(a.shape, lambda g, b: (0,) * a.ndim)
    tile = lambda r, w: pl.BlockSpec((nb, None, r, w), lambda g, b: (b, g, 0, 0))
    lay = lambda a: _layer_spec(a, layer)
    kern = functools.partial(_outproj_kernel, n_ctx_tiles=n_ctx // tm, ctx_row=bsz, d=d, tm=tm, s_loc=s_loc, nb=nb,
                             tile0=tile0)
    return pl.pallas_call(
        kern,
        grid=(nt, bsz // nb),
        in_specs=[tok_in(DA_WIDTH), pl.BlockSpec((nb, tm // q, SSD_INNER, q), lambda g, b: (b, g + tile0, 0, 0)),
                  tok_in(MLA_WIDTH), tok_in(d), lay(mod), lay(ng), lay(wo), full(rw), full(rb)],
        out_specs=[tok(d), tile(s_loc, d + MOE_EXTRA), tile(tm, LANES), tile(N_EXPERTS, LANES)],
        out_shape=[jax.ShapeDtypeStruct((bsz, nt * tm, d), F32),
                   jax.ShapeDtypeStruct((bsz, nt, s_loc, d + MOE_EXTRA), BF16),
                   jax.ShapeDtypeStruct((bsz, nt, tm, LANES), F32),
                   jax.ShapeDtypeStruct((bsz, nt, N_EXPERTS, LANES), F32)],
        compiler_params=_cparams("arbitrary", "arbitrary"),
        name="outproj_dispatch",
    )(oda, ossd_t, omla, h, mod, ng, wo, rw, rb)


def _expert_kernel(se_ref, bi_ref, sv_ref, sn_ref, xs_hbm, wg_ref, wu_ref, wd_ref, y_ref, xbuf, sem, wg16, wu16, wd16,
                   *, d):
    s = pl.program_id(0)
    last = pl.num_programs(0) - 1
    slot = s % 2
    real = sv_ref[s] > 0

    def start_all(step_, slot_):
        for j in range(MOE_STEP_BLKS):
            blk = bi_ref[step_ * MOE_STEP_BLKS + j]
            pltpu.make_async_copy(xs_hbm.at[pl.ds(pl.multiple_of(blk * MOE_BLK, MOE_BLK), MOE_BLK), :],
                                  xbuf.at[slot_, pl.ds(j * MOE_BLK, MOE_BLK), :], sem.at[slot_]).start(priority=j % 2)

    def wait_all(slot_):
        for j in range(MOE_STEP_BLKS):
            pltpu.make_async_copy(xs_hbm.at[pl.ds(0, MOE_BLK), :], xbuf.at[slot_, pl.ds(j * MOE_BLK, MOE_BLK), :],
                                  sem.at[slot_]).wait()

    @pl.when(jnp.logical_and(s == 0, real))
    def _():
        start_all(s, slot)

    nxt = jnp.minimum(s + 1, last)

    @pl.when(jnp.logical_and(s < last, sv_ref[nxt] > 0))
    def _():
        start_all(nxt, 1 - slot)

    @pl.when(jnp.logical_and(real, sn_ref[s] > 0))
    def _():
        wg16[...] = wg_ref[...].astype(BF16)
        wu16[...] = wu_ref[...].astype(BF16)
        wd16[...] = wd_ref[...].astype(BF16)

    @pl.when(real)
    def _():
        wait_all(slot)
        xm = xbuf[slot, :, 0:d]
        wx = xbuf[slot, :, d:]
        wrow = wx[:, 0:1].astype(F32) + wx[:, 1:2].astype(F32)
        he = _silu(_dot(xm, wg16[...])) * _dot(xm, wu16[...])
        y_ref[...] = (_dot(he.astype(BF16), wd16[...]) * wrow).astype(BF16)

    @pl.when(jnp.logical_not(real))
    def _():
        y_ref[...] = jnp.zeros_like(y_ref)


def _expert_ffn(xs2d, wg, wu, wd, step_e, blk_ids, step_valid, step_new, *, layer, n_steps):
    d = wg.shape[2]
    rows = MOE_STEP_BLKS * MOE_BLK
    wspec = lambda a: pl.BlockSpec((None, None) + a.shape[2:], lambda s, se, bi, sv, sn: (layer, se[s], 0, 0))
    grid_spec = pltpu.PrefetchScalarGridSpec(
        num_scalar_prefetch=4,
        grid=(n_steps,),
        in_specs=[pl.BlockSpec(memory_space=pl.ANY), wspec(wg), wspec(wu), wspec(wd)],
        out_specs=pl.BlockSpec((rows, d), lambda s, se, bi, sv, sn: (s, 0)),
        scratch_shapes=[pltpu.VMEM((2, rows, d + MOE_EXTRA), BF16), pltpu.SemaphoreType.DMA((2,)),
                        pltpu.VMEM(wg.shape[2:], BF16), pltpu.VMEM(wu.shape[2:], BF16),
                        pltpu.VMEM(wd.shape[2:], BF16)],
    )
    return pl.pallas_call(
        functools.partial(_expert_kernel, d=d),
        grid_spec=grid_spec,
        out_shape=jax.ShapeDtypeStruct((n_steps * rows, d), BF16),
        compiler_params=_cparams("arbitrary"),
        name="expert_ffn",
    )(step_e, blk_ids, step_valid, step_new, xs2d, wg, wu, wd)


def _moe_combine_rows(inv_ref, ys_hbm, meta_ref, h1_ref, mod_ref, ybuf, sem, *, n_lb, nt, n_ctx_tiles, ctx_row, d, tm,
                      s_loc, nb):
    ti, bp = pl.program_id(0), pl.program_id(1)
    nbp = pl.num_programs(1)
    step = ti * nbp + bp
    last = pl.num_programs(0) * nbp - 1
    slot = step % 2

    def block_copy(ti_, bp_, k, slot_):
        bb, j = divmod(k, n_lb)
        blk = inv_ref[((bp_ * nb + bb) * nt + ti_) * n_lb + j]
        return pltpu.make_async_copy(ys_hbm.at[pl.ds(pl.multiple_of(blk * MOE_BLK, MOE_BLK), MOE_BLK), :],
                                     ybuf.at[slot_, pl.ds(k * MOE_BLK, MOE_BLK), :], sem.at[slot_])

    def start_all(ti_, bp_, slot_):
        for k in range(nb * n_lb):
            block_copy(ti_, bp_, k, slot_).start(priority=k % 2)

    def wait_all(slot_):
        for k in range(nb * n_lb):
            pltpu.make_async_copy(ys_hbm.at[pl.ds(0, MOE_BLK), :], ybuf.at[slot_, pl.ds(k * MOE_BLK, MOE_BLK), :],
                                  sem.at[slot_]).wait()

    @pl.when(step == 0)
    def _():
        start_all(ti, bp, slot)

    wrap = bp + 1 == nbp
    ti_n = jnp.where(step == last, ti, jnp.where(wrap, ti + 1, ti))
    bp_n = jnp.where(step == last, bp, jnp.where(wrap, 0, bp + 1))
    start_all(ti_n, bp_n, 1 - slot)
    wait_all(slot)

    tiles = range(nb)
    lane_s = lax.broadcasted_iota(jnp.int32, (tm, s_loc), 1).astype(F32)
    metas = [meta_ref[bb] for bb in tiles]
    pts = [jnp.where((lane_s == m[:, 0:1]) | (lane_s == m[:, 1:2]), 1.0, 0.0).astype(BF16) for m in metas]
    y = [_dot(pts[bb], ybuf[slot, bb * s_loc:(bb + 1) * s_loc, :]) for bb in tiles]
    h2 = []
    for bb in tiles:
        row = jnp.where(ti < n_ctx_tiles, ctx_row, bp * nb + bb)
        h2.append(h1_ref[bb] + mod_ref[pl.ds(row, 1), pl.ds(5 * d, d)] * y[bb])

    def drain():
        @pl.when(step == last)
        def _():
            wait_all(1 - slot)

    return h2, drain


def _final_combine_kernel(inv_ref, ys_hbm, meta_ref, h1_ref, mod_ref, fg_ref, o_ref, ybuf, sem, *, nb, **geom):
    h2, drain = _moe_combine_rows(inv_ref, ys_hbm, meta_ref, h1_ref, mod_ref, ybuf, sem, nb=nb, **geom)
    for bb in range(nb):
        o_ref[bb] = _rms(h2[bb], fg_ref[...])
    drain()


def _final_combine(ys_em, inv, meta, h1, mod, fg, *, layer, tm, s_loc, nb):
    bsz, t, d = h1.shape
    nt = t // tm
    n_lb = s_loc // MOE_BLK
    full = lambda a: pl.BlockSpec(a.shape, lambda g, b, inv: (0,) * a.ndim)
    grid_spec = pltpu.PrefetchScalarGridSpec(
        num_scalar_prefetch=1,
        grid=(nt, bsz // nb),
        in_specs=[pl.BlockSpec(memory_space=pl.ANY),
                  pl.BlockSpec((nb, None, tm, LANES), lambda g, b, inv: (b, g, 0, 0)),
                  pl.BlockSpec((nb, tm, d), lambda g, b, inv: (b, g, 0)),
                  _layer_spec(mod, layer), full(fg)],
        out_specs=pl.BlockSpec((nb, tm, d), lambda g, b, inv: (b, g, 0)),
        scratch_shapes=[pltpu.VMEM((2, nb * s_loc, d), BF16), pltpu.SemaphoreType.DMA((2,))],
    )
    kern = functools.partial(_final_combine_kernel, n_lb=n_lb, nt=nt, n_ctx_tiles=0, ctx_row=bsz, d=d, tm=tm,
                             s_loc=s_loc, nb=nb)
    return pl.pallas_call(
        kern,
        grid_spec=grid_spec,
        out_shape=jax.ShapeDtypeStruct((bsz, t, d), F32),
        compiler_params=_cparams("arbitrary", "arbitrary"),
        name="moe_combine",
    )(inv, ys_em, meta, h1, mod, fg)


def _rope_tables(length, dim):
    rows = length // GRID_W
    row = jnp.repeat(jnp.arange(rows), GRID_W).astype(F32)
    col = jnp.tile(jnp.arange(GRID_W), rows).astype(F32)
    n_freq = dim // 4
    inv_freq = ROPE_THETA ** (-jnp.arange(n_freq, dtype=F32) / n_freq)
    ang = jnp.concatenate([row[:, None] * inv_freq, col[:, None] * inv_freq], axis=-1)
    return jnp.cos(ang), jnp.sin(ang)


def _table_set(n_ctx, n_lat):
    def lanes(cos, sin, lane0, width, reps, outside_cos):
        c = jnp.concatenate([cos, cos], axis=-1)
        s = jnp.concatenate([-sin, sin], axis=-1)
        grp_c = jnp.full((n_lat, width), outside_cos, F32).at[:, lane0:lane0 + c.shape[1]].set(c)
        grp_s = jnp.zeros((n_lat, width), F32).at[:, lane0:lane0 + s.shape[1]].set(s)
        ctx_c = jnp.full((n_ctx, width), outside_cos, F32).at[:, lane0:lane0 + c.shape[1]].set(1.0)
        ctx_s = jnp.zeros((n_ctx, width), F32)
        return (jnp.tile(jnp.concatenate([ctx_c, grp_c], axis=0), (1, reps)),
                jnp.tile(jnp.concatenate([ctx_s, grp_s], axis=0), (1, reps)))

    dcos, dsin = _rope_tables(n_lat, DA_QK)
    cda, sda = lanes(dcos, dsin, 0, DA_QK, DA_QW // DA_QK, 1.0)
    mcos, msin = _rope_tables(n_lat, MLA_ROPE)
    cq, sm = lanes(mcos, msin, KR_LANE0, LANES, 1, 1.0)
    ck, _ = lanes(mcos, msin, KR_LANE0, LANES, 1, 0.0)
    return cda, sda, cq, ck, sm


def _pack_w_in(w_in):
    depth, d, _ = w_in.shape
    o_mla = DA_IN + SSD_IN
    dt = w_in[..., MAIN_COLS:o_mla]
    cq_ckv = w_in[..., o_mla:o_mla + MLA_Q_RANK + MLA_KV_RANK]
    kr = w_in[..., o_mla + MLA_Q_RANK + MLA_KV_RANK:]
    zeros = lambda n: jnp.zeros((depth, d, n), w_in.dtype)
    tail = jnp.concatenate([cq_ckv, zeros(KR_LANE0), kr, dt, zeros(LANES - DT_LANE0 - 2 * SSD_HEADS)], axis=-1)
    return w_in[..., :MAIN_COLS].astype(BF16), tail.astype(BF16)


def _pack_w_uq(w_uq):
    depth, r, _ = w_uq.shape
    w = w_uq.reshape(depth, r, MLA_HEADS, MLA_NOPE + MLA_ROPE)
    w = jnp.pad(w, ((0, 0), (0, 0), (0, 0), (0, MLA_HEAD_PAD - MLA_NOPE - MLA_ROPE)))
    return w.reshape(depth, r, MLA_QPAD).astype(BF16)


def _pack_w_ukv(w_ukv):
    depth, r, _ = w_ukv.shape
    w = w_ukv.reshape(depth, r, MLA_HEADS, MLA_NOPE + MLA_V)
    kn = jnp.pad(w[..., :MLA_NOPE], ((0, 0), (0, 0), (0, 0), (0, MLA_HEAD_PAD - MLA_NOPE)))
    return jnp.concatenate([kn.reshape(depth, r, MLA_QPAD), w[..., MLA_NOPE:].reshape(depth, r, MLA_WIDTH)],
                           axis=-1).astype(BF16)


def _dispatch_tables(pcnt, n_lb, n_steps):
    ntt = pcnt.shape[0]
    nb = (pcnt / MOE_BLK).astype(jnp.int32)
    lo = jnp.cumsum(nb, axis=1) - nb
    n_e = jnp.sum(nb, axis=0)
    p_e = (n_e + MOE_STEP_BLKS - 1) // MOE_STEP_BLKS * MOE_STEP_BLKS
    ends = jnp.cumsum(p_e)
    base = (ends - p_e)[None, :] + jnp.cumsum(nb, axis=0) - nb
    lb = jnp.arange(n_lb, dtype=jnp.int32)
    owner = (lb[None, :, None] >= lo[:, None, :]) & (lb[None, :, None] < (lo + nb)[:, None, :])
    used = jnp.any(owner, axis=-1)
    pos = jnp.sum(jnp.where(owner, (base - lo)[:, None, :], 0), axis=-1) + lb[None, :]
    inv = jnp.where(used, pos, 0).astype(jnp.int32)
    p = jnp.arange(n_steps * MOE_STEP_BLKS, dtype=jnp.int32)
    e_p = jnp.sum(p[:, None] >= ends[None, :], axis=1)
    oh_e = jnp.minimum(e_p, N_EXPERTS - 1)[:, None] == jnp.arange(N_EXPERTS)[None, :]
    r = p - jnp.sum(jnp.where(oh_e, (ends - p_e)[None, :], 0), axis=1)
    real = (e_p < N_EXPERTS) & (r < jnp.sum(jnp.where(oh_e, n_e[None, :], 0), axis=1))
    cum = jnp.cumsum(nb, axis=0)
    pick = lambda tab: jnp.dot(oh_e.astype(F32), tab.T.astype(F32), precision=lax.Precision.HIGHEST)
    t_p = jnp.minimum(jnp.sum(pick(cum) <= r[:, None].astype(F32), axis=1), ntt - 1)
    first = jnp.arange(ntt, dtype=jnp.int32)[:, None] * n_lb + lo - (cum - nb)
    src = jnp.sum(jnp.where(t_p[:, None] == jnp.arange(ntt)[None, :], pick(first), 0.0), axis=1).astype(jnp.int32) + r
    fwd = jnp.where(real, src, 0)
    starts = jnp.arange(n_steps, dtype=jnp.int32) * MOE_STEP_BLKS
    step_e = jnp.minimum(jnp.sum(starts[:, None] >= ends[None, :], axis=1), N_EXPERTS - 1).astype(jnp.int32)
    step_valid = (starts < ends[-1]).astype(jnp.int32)
    step_new = jnp.concatenate([jnp.ones((1,), jnp.int32), (step_e[1:] != step_e[:-1]).astype(jnp.int32)])
    return inv.reshape(-1), fwd, step_e, step_valid, step_new


def kernel(x, c, ctx, c_ctx, norm_mix_g, norm_ffn_g, w_mod, b_mod, w_in, w_out, da_lambda, da_subln_g, ssd_conv_w, ssd_conv_b, ssd_dt_bias, ssd_a_log, ssd_d, ssd_norm_g, mla_q_norm_g, mla_kv_norm_g, mla_w_uq, mla_w_ukv, router_w, router_bias, exp_w_gate, exp_w_up, exp_w_down, final_norm_g):
    bsz, n_lat, d = x.shape
    n_ctx = ctx.shape[1]
    t = n_ctx + n_lat
    tm = min(256, n_ctx)
    assert n_ctx % tm == 0 and n_lat % tm == 0 and n_ctx % SSD_CHUNK == 0 and n_lat % SSD_CHUNK == 0
    nt = t // tm
    s_loc = 2 * tm + 2 * LANES
    n_lb = s_loc // MOE_BLK
    nc, nc0 = t // SSD_CHUNK, n_ctx // SSD_CHUNK
    nb = 2 if bsz % 2 == 0 else 1
    nb_wide = 4 if bsz % 4 == 0 else nb
    q = SSD_CHUNK

    r_pad = -(-(bsz + 1) // 8) * 8
    c_rows = jnp.concatenate([c, c_ctx[None, :], jnp.zeros((r_pad - bsz - 1, d), F32)], axis=0)
    mod = _modulation(c_rows, w_mod, b_mod)

    w_in_p = _pack_w_in(w_in)
    w_uq_p = _pack_w_uq(mla_w_uq)
    w_ukv_p = _pack_w_ukv(mla_w_ukv)
    w_out16 = w_out.astype(BF16)
    tabs = _table_set(n_ctx, n_lat)
    rw_pad = jnp.pad(router_w, ((0, 0), (0, LANES - N_EXPERTS)))
    rb = router_bias.reshape(N_EXPERTS, 1)
    lane_b = lambda v: jnp.broadcast_to(v[..., None], v.shape + (q,))
    dsk_rows = jnp.repeat(ssd_d, SSD_P, axis=-1)
    ng_mix, ng_ffn = norm_mix_g[:, None, :], norm_ffn_g[:, None, :]
    qg, kvg = mla_q_norm_g[:, None, :], mla_kv_norm_g[:, None, :]
    sub_g = jnp.tile(da_subln_g, (1, DA_HEADS))[:, :, None]
    ssd_par = (lane_b(ssd_conv_w), lane_b(ssd_conv_b), lane_b(ssd_dt_bias.reshape(DEPTH, -1)),
               lane_b(ssd_a_log.reshape(DEPTH, -1)), lane_b(dsk_rows), lane_b(ssd_norm_g))

    h = jnp.concatenate([ctx, x], axis=1)
    moe = None
    for i in range(DEPTH):
        lam_init = 0.8 - 0.6 * math.exp(-0.3 * i)
        if moe is None:
            qda_t, kda, vda_t, z_t, xbc_t, dt_t, q4t, k4, vm_t = _inproj(
                h, mod, ng_mix, w_in_p, tabs, qg, kvg, w_uq_p, w_ukv_p, layer=i, tm=tm, n_ctx=n_ctx, nb=nb_wide)
        else:
            qda_t, kda, vda_t, z_t, xbc_t, dt_t, q4t, k4, vm_t, h = _inproj(
                None, mod, ng_mix, w_in_p, tabs, qg, kvg, w_uq_p, w_ukv_p, layer=i, tm=tm, n_ctx=n_ctx, nb=nb, moe=moe)
        o_da = _da_attention(qda_t, kda, vda_t, da_lambda, sub_g, layer=i, lam_init=lam_init, tq=tm, n_ctx=n_ctx)
        o_mla = _mla_attention(q4t, k4, vm_t, tq=tm, n_ctx=n_ctx)
        o_ssd_t = _ssd_mixer(z_t, xbc_t, dt_t, *ssd_par, layer=i, nc0=nc0)
        tile0 = n_ctx // tm if i == DEPTH - 1 else 0
        ntl = (nt - tile0) * bsz
        n_steps = -(-(ntl * n_lb + N_EXPERTS * (MOE_STEP_BLKS - 1)) // MOE_STEP_BLKS)
        h1, xs, meta, cnt = _outproj_dispatch(o_da, o_ssd_t, o_mla, h, mod, ng_ffn, w_out16, rw_pad, rb, layer=i, tm=tm,
                                              n_ctx=n_ctx, s_loc=s_loc, nb=nb_wide, tile0=tile0)
        inv, fwd, step_e, step_valid, step_new = _dispatch_tables(cnt[:, :, :, 0].reshape(ntl, N_EXPERTS), n_lb, n_steps)
        ys_em = _expert_ffn(xs.reshape(ntl * s_loc, d + MOE_EXTRA), exp_w_gate, exp_w_up, exp_w_down, step_e, fwd,
                            step_valid, step_new, layer=i, n_steps=n_steps)
        moe = (ys_em, inv, meta, h1, s_loc)
    return _final_combine(ys_em, inv, meta, h1, mod, final_norm_g[None, :], layer=DEPTH - 1, tm=tm, s_loc=s_loc, nb=nb)
```
